```python
import math
import jax, jax.numpy as jnp
from jax import lax
import numpy as np

D_MODEL = 1024
BATCH = 4
SEQ = 4096
DEPTH = 4
DEC_BATCH = 128
DEC_SEQ = 1
PAST_LEN = 8192
PAGE_SIZE = 128

DN_HEADS = 4
DN_DK = 128
DN_DV = 128
CONV_W = 4
DN_CHUNK = 64
DN_QKV = 2 * DN_HEADS * DN_DK + DN_HEADS * DN_DV
GLA_HEADS = 4
GLA_DK = 64
GLA_DV = 128
GLA_RANK = 16
GLA_TAU = 16.0
GLA_CHUNK = 64
SWA_HEADS = 8
SWA_KV = 2
SWA_HD = 64
WINDOW = 128
N_BRANCH = 3
BRANCH_W = 512
N_GROUPS = 4
EXP_PER_GROUP = 8
N_EXPERTS = N_GROUPS * EXP_PER_GROUP
TOP_K = 2
D_EXPERT = 256
EPS = 1e-6

kernel_name = 'hybrid_gdn_gla_swa_hmoe_step'

F32 = jnp.float32


def _proj_sizes():
    return (DN_QKV, DN_HEADS * DN_DV, DN_HEADS, DN_HEADS,
            GLA_HEADS * GLA_DK, GLA_HEADS * GLA_DK, GLA_HEADS * GLA_DV, GLA_HEADS * GLA_DV, GLA_RANK,
            SWA_HEADS * SWA_HD, SWA_KV * SWA_HD, SWA_KV * SWA_HD,
            N_BRANCH * D_MODEL)


def _split_cols(p):
    idx = [int(v) for v in np.cumsum(_proj_sizes())[:-1]]
    return jnp.split(p, idx, axis=-1)


def _rms(x, g):
    xf = x.astype(F32)
    y = xf * lax.rsqrt(jnp.mean(xf * xf, -1, keepdims=True) + EPS)
    return (y * g.astype(F32)).astype(x.dtype)


def _l2(x):
    xf = x.astype(F32)
    return (xf * lax.rsqrt(jnp.sum(xf * xf, -1, keepdims=True) + EPS)).astype(x.dtype)


def _alibi_slopes():
    return jnp.exp2(-8.0 * jnp.arange(1, SWA_HEADS + 1, dtype=F32) / SWA_HEADS)


def _short_conv(x, buf, w):
    L = x.shape[1]
    xp = jnp.concatenate([buf, x], 1)
    y = xp[:, 0:L] * w[0]
    for j in range(1, CONV_W):
        y = y + xp[:, j:j + L] * w[j]
    return jax.nn.silu(y), xp[:, -(CONV_W - 1):]


def _to_chunks(a, C, n, pad):
    a = a.astype(F32)
    a = jnp.pad(a, [(0, 0), (0, pad)] + [(0, 0)] * (a.ndim - 2))
    a = a.reshape((a.shape[0], n, C) + a.shape[2:])
    return jnp.moveaxis(a, 1, 0)


def _from_chunks(o, L):
    o = jnp.moveaxis(o, 0, 1)
    return o.reshape((o.shape[0], -1) + o.shape[3:])[:, :L]


def _gated_delta(q, k, v, beta, g, S0):
    L = k.shape[1]
    DV = v.shape[-1]
    C = min(DN_CHUNK, L)
    n = -(-L // C)
    pad = n * C - L
    xs = tuple(_to_chunks(a, C, n, pad) for a in (q, k, v, beta, g))
    incl = jnp.tril(jnp.ones((C, C), bool))
    strict = jnp.tril(jnp.ones((C, C), bool), -1)

    def step(S, inp):
        qi, ki, vi, bi, gi = inp
        gam = jnp.moveaxis(jnp.cumsum(gi, 1), 1, 2)
        diff = gam[..., :, None] - gam[..., None, :]
        dec = jnp.where(incl, jnp.exp(jnp.where(incl, diff, 0.0)), 0.0)
        qh, kh, vh = (jnp.moveaxis(a, 1, 2) for a in (qi, ki, vi))
        bh = jnp.moveaxis(bi, 1, 2)[..., None]
        kb = kh * bh
        A = jnp.where(strict, jnp.einsum('bhtd,bhsd->bhts', kb, kh) * dec, 0.0)
        rhs = jnp.concatenate([vh * bh, kb * jnp.exp(gam)[..., None]], -1)
        sol = lax.linalg.triangular_solve(A, rhs, left_side=True, lower=True, unit_diagonal=True)
        u = sol[..., :DV] - jnp.einsum('bhtk,bhkv->bhtv', sol[..., DV:], S)
        o = (jnp.einsum('bhtk,bhkv->bhtv', qh * jnp.exp(gam)[..., None], S)
             + jnp.einsum('bhts,bhsv->bhtv', jnp.einsum('bhtd,bhsd->bhts', qh, kh) * dec, u))
        gl = gam[..., -1:]
        S = S * jnp.exp(gl)[..., None] + jnp.einsum('bhsk,bhsv->bhkv', kh * jnp.exp(gl - gam)[..., None], u)
        return S, jnp.moveaxis(o, 1, 2)

    S, o = lax.scan(step, S0.astype(F32), xs)
    return _from_chunks(o, L), S


def _gla(q, k, v, lg, S0):
    L = k.shape[1]
    C = min(GLA_CHUNK, L)
    n = -(-L // C)
    pad = n * C - L
    xs = tuple(_to_chunks(a, C, n, pad) for a in (q, k, v, lg))
    m5 = jnp.tril(jnp.ones((C, C), bool))[None, :, :, None, None]

    def step(S, inp):
        qi, ki, vi, gi = inp
        gam = jnp.cumsum(gi, 1)
        diff = gam[:, :, None] - gam[:, None]
        dec = jnp.where(m5, jnp.exp(jnp.where(m5, diff, 0.0)), 0.0)
        att = jnp.einsum('btshd,bshd->bhts', qi[:, :, None] * dec, ki)
        o = (jnp.einsum('bthd,bhdv->bthv', qi * jnp.exp(gam), S)
             + jnp.einsum('bhts,bshv->bthv', att, vi))
        S = S * jnp.exp(gam[:, -1])[..., None] + jnp.einsum('bshd,bshv->bhdv', ki * jnp.exp(gam[:, -1:] - gam), vi)
        return S, o

    S, o = lax.scan(step, S0.astype(F32), xs)
    return _from_chunks(o, L), S


def _attend(q, k, v, dist, valid, sinks, slopes):
    KV, G = q.shape[-3], q.shape[-2]
    s = jnp.einsum('...qkgd,...skd->...kgqs', q, k, preferred_element_type=F32)
    s = s - slopes.reshape(KV, G, 1, 1) * dist[..., None, None, :, :].astype(F32)
    s = jnp.where(valid[..., None, None, :, :], s, -jnp.inf)
    snk = sinks.astype(F32).reshape(KV, G, 1, 1)
    m = jnp.maximum(jnp.max(s, -1, keepdims=True), snk)
    p = jnp.exp(s - m)
    p = p / (jnp.sum(p, -1, keepdims=True) + jnp.exp(snk - m))
    return jnp.einsum('...kgqs,...skd->...qkgd', p.astype(v.dtype), v)


def _swa_prompt(q, k, v, sinks, slopes):
    B, L, H, hd = q.shape
    W = WINDOW
    n = L // W
    qb = q.reshape(B, n, W, SWA_KV, H // SWA_KV, hd)

    def blocks(a):
        ap = jnp.pad(a, [(0, 0), (W, 0), (0, 0), (0, 0)])
        prev = ap[:, :L].reshape(B, n, W, SWA_KV, hd)
        cur = a.reshape(B, n, W, SWA_KV, hd)
        return jnp.concatenate([prev, cur], 2)

    qi = jnp.arange(W)[:, None]
    sj = jnp.arange(2 * W)[None, :]
    dist = W + qi - sj
    kpos = jnp.arange(n)[:, None, None] * W - W + sj[None]
    valid = (dist >= 0) & (dist <= W) & (kpos >= 0)
    o = _attend(qb, blocks(k), blocks(v), jnp.broadcast_to(dist, valid.shape), valid, sinks, slopes)
    return o.reshape(B, L, H * hd)


def _swa_sample(q, k, v, k_buf, v_buf, sinks, slopes):
    B, L, H, hd = q.shape
    Lb = k_buf.shape[1]
    kk = jnp.concatenate([k_buf.astype(k.dtype), k], 1)
    vv = jnp.concatenate([v_buf.astype(v.dtype), v], 1)
    dist = Lb + jnp.arange(L)[:, None] - jnp.arange(Lb + L)[None, :]
    valid = (dist >= 0) & (dist <= WINDOW)
    o = _attend(q.reshape(B, L, SWA_KV, H // SWA_KV, hd), kk, vv, dist, valid, sinks, slopes)
    return o.reshape(B, L, H * hd), kk[:, -Lb:], vv[:, -Lb:]


def _moe(h, w):
    B, L, D = h.shape
    t = h.reshape(-1, D)
    lc = (t @ w['rc_w'] + w['rc_b']).astype(F32)
    pc = jax.nn.softmax(lc, -1)
    grp = jnp.argmax(lc, -1)
    pg = jnp.max(pc, -1, keepdims=True)
    le = (t @ w['re_w'] + w['re_b']).astype(F32).reshape(-1, N_GROUPS, EXP_PER_GROUP)
    le = jnp.einsum('tge,tg->te', le, jax.nn.one_hot(grp, N_GROUPS, dtype=F32))
    top_v, top_i = lax.top_k(jax.nn.softmax(le, -1), TOP_K)
    wts = pg * top_v / jnp.sum(top_v, -1, keepdims=True)
    eid = grp[:, None] * EXP_PER_GROUP + top_i
    comb = jnp.sum(wts[..., None] * jax.nn.one_hot(eid, N_EXPERTS, dtype=F32), 1).astype(h.dtype)
    y = jnp.zeros_like(t)
    for e in range(N_EXPERTS):
        he = jax.nn.silu(t @ w['w1'][e]) * (t @ w['w3'][e])
        y = y + comb[:, e:e + 1] * (he @ w['w2'][e])
    return y.reshape(B, L, D)


def _layer(x, c, w, slopes, dn_S, dn_buf, gla_S, k_buf, v_buf):
    B, L, D = x.shape
    prompt = dn_S is None
    mod = jax.nn.silu(c) @ w['ada_w'] + w['ada_b']
    sh1, sc1, gt1, sh2, sc2, gt2 = (m[:, None, :] for m in jnp.split(mod, 6, axis=-1))
    h = _rms(x, w['ln1']) * (1 + sc1) + sh1
    (a_qkv, a_z, a_b, a_a, b_q, b_k, b_v, b_r, b_lr,
     c_q, c_k, c_v, gate_logits) = _split_cols(h @ w['w_in'])
    if prompt:
        dn_buf = jnp.zeros((B, CONV_W - 1, DN_QKV), x.dtype)
        dn_S = jnp.zeros((B, DN_HEADS, DN_DK, DN_DV), F32)
        gla_S = jnp.zeros((B, GLA_HEADS, GLA_DK, GLA_DV), F32)
    qkv, new_buf = _short_conv(a_qkv, dn_buf.astype(x.dtype), w['dn_conv'])
    hk = DN_HEADS * DN_DK
    aq = _l2(qkv[..., :hk].reshape(B, L, DN_HEADS, DN_DK)) * DN_DK ** -0.5
    ak = _l2(qkv[..., hk:2 * hk].reshape(B, L, DN_HEADS, DN_DK))
    av = qkv[..., 2 * hk:].reshape(B, L, DN_HEADS, DN_DV)
    beta = jax.nn.sigmoid(a_b.astype(F32))
    g = -jnp.exp(w['dn_a_log'].astype(F32)) * jax.nn.softplus(a_a.astype(F32) + w['dn_dt_bias'].astype(F32))
    o_a, S_a = _gated_delta(aq, ak, av, beta, g, dn_S)
    o_a = _rms(o_a.astype(x.dtype), w['dn_onorm']) * jax.nn.silu(a_z.reshape(B, L, DN_HEADS, DN_DV))
    bq = (b_q * GLA_DK ** -0.5).reshape(B, L, GLA_HEADS, GLA_DK)
    bk = b_k.reshape(B, L, GLA_HEADS, GLA_DK)
    bv = b_v.reshape(B, L, GLA_HEADS, GLA_DV)
    lg = (jax.nn.log_sigmoid((b_lr @ w['gla_w2'] + w['gla_b']).astype(F32)) / GLA_TAU).reshape(B, L, GLA_HEADS, GLA_DK)
    o_b, S_b = _gla(bq, bk, bv, lg, gla_S)
    o_b = _rms(o_b.astype(x.dtype), w['gla_onorm']) * jax.nn.silu(b_r.reshape(B, L, GLA_HEADS, GLA_DV))
    cq = _rms(c_q.reshape(B, L, SWA_HEADS, SWA_HD), w['qn']) * SWA_HD ** -0.5
    ck = _rms(c_k.reshape(B, L, SWA_KV, SWA_HD), w['kn'])
    cv = c_v.reshape(B, L, SWA_KV, SWA_HD)
    if prompt:
        o_c = _swa_prompt(cq, ck, cv, w['sinks'], slopes)
        new_k, new_v = ck[:, -WINDOW:], cv[:, -WINDOW:]
    else:
        o_c, new_k, new_v = _swa_sample(cq, ck, cv, k_buf, v_buf, w['sinks'], slopes)
    br = jnp.stack([o_a.reshape(B, L, BRANCH_W), o_b.reshape(B, L, BRANCH_W), o_c.reshape(B, L, BRANCH_W)], 2)
    up = jnp.einsum('blnc,ncd->blnd', br, w['w_branch'])
    gates = jax.nn.sigmoid(gate_logits.reshape(B, L, N_BRANCH, D))
    x = x + gt1 * (jnp.sum(gates * up, 2) @ w['w_out'])
    h2 = _rms(x, w['ln2']) * (1 + sc2) + sh2
    x = x + gt2 * _moe(h2, w)
    return x, (S_a.astype(x.dtype), new_buf, S_b.astype(x.dtype), new_k, new_v)


def setup_inputs(seed: int = 0) -> dict:
    key = jax.random.key(seed)
    ks = iter(jax.random.split(key, 48))

    def nrm(shape, s):
        return jax.random.normal(next(ks), shape, F32) * s

    D = D_MODEL
    win = min(WINDOW, PAST_LEN)
    P = int(sum(_proj_sizes()))
    dt = jnp.exp(jax.random.uniform(next(ks), (DEPTH, DN_HEADS), F32, math.log(1e-3), math.log(1e-1)))
    dt_bias = dt + jnp.log(-jnp.expm1(-dt))
    a_log = jnp.log(jax.random.uniform(next(ks), (DEPTH, DN_HEADS), F32, 1.0, 16.0))
    return {
        'x_prompt': nrm((BATCH, SEQ, D), 1.0),
        'x_sample': nrm((DEC_BATCH, DEC_SEQ, D), 1.0),
        'c_prompt': nrm((BATCH, D), 1.0),
        'c_sample': nrm((DEC_BATCH, D), 1.0),
        'state_dn': nrm((DEPTH, DEC_BATCH, DN_HEADS, DN_DK, DN_DV), 0.1),
        'state_dn_conv': nrm((DEPTH, DEC_BATCH, CONV_W - 1, DN_QKV), 1.0),
        'state_gla': nrm((DEPTH, DEC_BATCH, GLA_HEADS, GLA_DK, GLA_DV), 0.1),
        'cache_swa_k': nrm((DEPTH, DEC_BATCH, win, SWA_KV, SWA_HD), 1.0),
        'cache_swa_v': nrm((DEPTH, DEC_BATCH, win, SWA_KV, SWA_HD), 1.0),
        'ln1_g': 1.0 + nrm((DEPTH, D), 0.02),
        'ln2_g': 1.0 + nrm((DEPTH, D), 0.02),
        'ada_w': nrm((DEPTH, D, 6 * D), 0.5 * D ** -0.5),
        'ada_b': nrm((DEPTH, 6 * D), 0.02),
        'w_in': nrm((DEPTH, D, P), D ** -0.5),
        'dn_conv_w': nrm((DEPTH, CONV_W, DN_QKV), CONV_W ** -0.5),
        'dn_a_log': a_log,
        'dn_dt_bias': dt_bias,
        'dn_onorm_g': 1.0 + nrm((DEPTH, DN_DV), 0.02),
        'gla_w2': nrm((DEPTH, GLA_RANK, GLA_HEADS * GLA_DK), GLA_RANK ** -0.5),
        'gla_b': nrm((DEPTH, GLA_HEADS * GLA_DK), 0.1),
        'gla_onorm_g': 1.0 + nrm((DEPTH, GLA_DV), 0.02),
        'swa_qnorm_g': 1.0 + nrm((DEPTH, SWA_HD), 0.02),
        'swa_knorm_g': 1.0 + nrm((DEPTH, SWA_HD), 0.02),
        'swa_sinks': nrm((DEPTH, SWA_HEADS), 0.5),
        'w_branch': nrm((DEPTH, N_BRANCH, BRANCH_W, D), BRANCH_W ** -0.5),
        'w_out': nrm((DEPTH, D, D), D ** -0.5),
        'rc_w': nrm((DEPTH, D, N_GROUPS), D ** -0.5),
        'rc_b': nrm((DEPTH, N_GROUPS), 0.01),
        're_w': nrm((DEPTH, D, N_EXPERTS), D ** -0.5),
        're_b': nrm((DEPTH, N_EXPERTS), 0.01),
        'w1': nrm((DEPTH, N_EXPERTS, D, D_EXPERT), D ** -0.5),
        'w3': nrm((DEPTH, N_EXPERTS, D, D_EXPERT), D ** -0.5),
        'w2': nrm((DEPTH, N_EXPERTS, D_EXPERT, D), D_EXPERT ** -0.5),
    }


def reference(x_prompt, x_sample, c_prompt, c_sample, state_dn, state_dn_conv, state_gla,
              cache_swa_k, cache_swa_v, ln1_g, ln2_g, ada_w, ada_b, w_in, dn_conv_w, dn_a_log,
              dn_dt_bias, dn_onorm_g, gla_w2, gla_b, gla_onorm_g, swa_qnorm_g, swa_knorm_g,
              swa_sinks, w_branch, w_out, rc_w, rc_b, re_w, re_b, w1, w3, w2):
    slopes = _alibi_slopes()
    yp, ys = x_prompt, x_sample
    st_p, st_s = [], []
    for l in range(DEPTH):
        w = {'ln1': ln1_g[l], 'ln2': ln2_g[l], 'ada_w': ada_w[l], 'ada_b': ada_b[l],
             'w_in': w_in[l], 'dn_conv': dn_conv_w[l], 'dn_a_log': dn_a_log[l],
             'dn_dt_bias': dn_dt_bias[l], 'dn_onorm': dn_onorm_g[l], 'gla_w2': gla_w2[l],
             'gla_b': gla_b[l], 'gla_onorm': gla_onorm_g[l], 'qn': swa_qnorm_g[l],
             'kn': swa_knorm_g[l], 'sinks': swa_sinks[l], 'w_branch': w_branch[l],
             'w_out': w_out[l], 'rc_w': rc_w[l], 'rc_b': rc_b[l], 're_w': re_w[l],
             're_b': re_b[l], 'w1': w1[l], 'w3': w3[l], 'w2': w2[l]}
        yp, sp = _layer(yp, c_prompt, w, slopes, None, None, None, None, None)
        ys, ss = _layer(ys, c_sample, w, slopes, state_dn[l], state_dn_conv[l], state_gla[l],
                        cache_swa_k[l], cache_swa_v[l])
        st_p.append(sp)
        st_s.append(ss)
    dn_p, conv_p, gla_p, k_p, v_p = [jnp.stack(z) for z in zip(*st_p)]
    dn_s, conv_s, gla_s, k_s, v_s = [jnp.stack(z) for z in zip(*st_s)]
    return (yp, ys, dn_p, dn_s, conv_p, conv_s, gla_p, gla_s, k_p, k_s, v_p, v_s)
```

```python
import functools

import jax
import jax.numpy as jnp
from jax import lax
from jax.experimental import pallas as pl
from jax.experimental.pallas import tpu as pltpu

F32 = jnp.float32
BF16 = jnp.bfloat16
HIGHEST = lax.Precision.HIGHEST

D_MODEL = 1024
DN_HEADS, DN_DK, DN_DV, CONV_W = 4, 128, 128, 4
DN_QKV = 2 * DN_HEADS * DN_DK + DN_HEADS * DN_DV
GLA_HEADS, GLA_DK, GLA_DV, GLA_RANK, GLA_TAU = 4, 64, 128, 16, 16.0
SWA_HEADS, SWA_KV, SWA_HD, WINDOW = 8, 2, 64, 128
N_BRANCH, BRANCH_W = 3, 512
N_GROUPS, EXP_PER_GROUP, TOP_K, D_EXPERT = 4, 8, 2, 256
N_EXPERTS = N_GROUPS * EXP_PER_GROUP
EPS = 1e-6

LANES = 128
CHUNK = 128
SUB = 16
VMEM_LIMIT = 56 * 1024 * 1024

C_GATE, C_AQKV, C_AZ, C_BV, C_BR, C_CQ = 0, 3072, 4608, 5120, 5632, 6144
C_BQ, C_BK, C_CK, C_CV, C_SM, P_PAD = 6656, 6912, 7168, 7296, 7424, 7680
SM_BETA, SM_G, SM_LR = 0, DN_HEADS, 2 * DN_HEADS
R_EXP = N_GROUPS


def _cparams(*sem):
    return pltpu.CompilerParams(dimension_semantics=sem, vmem_limit_bytes=VMEM_LIMIT)


def _dot(a, b, precision=None):
    return jnp.dot(a, b, preferred_element_type=F32, precision=precision)


def _dot_nt(a, b, precision=None):
    return lax.dot_general(a, b, (((1,), (1,)), ((), ())), preferred_element_type=F32, precision=precision)


def _dot_tn(a, b, precision=None):
    return lax.dot_general(a, b, (((0,), (0,)), ((), ())), preferred_element_type=F32, precision=precision)


def _silu(x):
    return x * jax.nn.sigmoid(x)


def _iota(shape, dim):
    return lax.broadcasted_iota(jnp.int32, shape, dim)


def _ada_kernel(c_ref, w_ref, b_ref, o_ref):
    c = _silu(c_ref[...]).astype(BF16)
    o_ref[0] = _dot(c, w_ref[0]) + b_ref[0]


def _ada(c_all, ada_w, ada_b):
    depth, d, n = ada_w.shape
    rows = c_all.shape[0]
    tn = 1536
    return pl.pallas_call(
        _ada_kernel,
        grid=(depth, n // tn),
        in_specs=[pl.BlockSpec((rows, d), lambda l, j: (0, 0)),
                  pl.BlockSpec((1, d, tn), lambda l, j: (l, 0, j)),
                  pl.BlockSpec((1, 1, tn), lambda l, j: (l, 0, j))],
        out_specs=pl.BlockSpec((1, rows, tn), lambda l, j: (l, 0, j)),
        out_shape=jax.ShapeDtypeStruct((depth, rows, n), F32),
        compiler_params=_cparams("parallel", "parallel"),
    )(c_all, ada_w, ada_b.reshape(depth, 1, n))


def _mod_spec(mod, tm, tokens_per_row):
    _, r, d = mod.shape
    per = tokens_per_row // tm
    return pl.BlockSpec((1, r, d), lambda i, *_: (i // per, 0, 0))


def _in_proj_kernel(x_ref, sc_ref, sh_ref, g_ref, w_ref, o_ref, h_ref):
    @pl.when(pl.program_id(1) == 0)
    def _():
        x = x_ref[...]
        y = x * lax.rsqrt(jnp.mean(x * x, -1, keepdims=True) + EPS) * g_ref[...]
        h_ref[...] = (y * (1.0 + sc_ref[0]) + sh_ref[0]).astype(BF16)

    o_ref[...] = _dot(h_ref[...], w_ref[...])


def _in_proj(x, sc, sh, g, w, tm, tokens_per_row):
    t, d = x.shape
    n = w.shape[1]
    tn = 1536
    return pl.pallas_call(
        _in_proj_kernel,
        grid=(t // tm, n // tn),
        in_specs=[pl.BlockSpec((tm, d), lambda i, j: (i, 0)),
                  _mod_spec(sc, tm, tokens_per_row), _mod_spec(sh, tm, tokens_per_row),
                  pl.BlockSpec((1, d), lambda i, j: (0, 0)),
                  pl.BlockSpec((d, tn), lambda i, j: (0, j))],
        out_specs=pl.BlockSpec((tm, tn), lambda i, j: (i, j)),
        out_shape=jax.ShapeDtypeStruct((t, n), F32),
        scratch_shapes=[pltpu.VMEM((tm, d), BF16)],
        compiler_params=_cparams("parallel", "arbitrary"),
    )(x, sc, sh, g, w)


def _strict_lower_inverse_minus_eye(a):
    c = a.shape[0]
    row, col = _iota((c, c), 0), _iota((c, c), 1)
    diag_blk = (row // SUB) == (col // SUB)
    ad = jnp.where(diag_blk, a, 0.0)
    ao = a - ad
    n = -ad
    p = n
    steps = SUB.bit_length() - 2
    for _ in range(steps):
        p = _dot(p, p)
        n = n + p + _dot(n, p)
    bm = -(ao + _dot(n, ao))
    m = bm
    q = bm
    steps = (c // SUB).bit_length() - 2
    for _ in range(steps):
        q = _dot(q, q)
        m = m + q + _dot(m, q)
    return m + n + _dot(m, n)


def _dn_kernel(qkv_ref, z_ref, sm_ref, cw_ref, alog_ref, dtb_ref, on_ref,
               o_ref, s_out_ref, s_ref, xp_ref):
    c_idx = pl.program_id(1)
    c = CHUNK
    pad = 8

    @pl.when(c_idx == 0)
    def _():
        s_ref[...] = jnp.zeros_like(s_ref)
        xp_ref[0:pad, :] = jnp.zeros((pad, DN_QKV), F32)

    xp_ref[pad:pad + c, :] = qkv_ref[...]
    acc = xp_ref[pad - 3:pad - 3 + c, :] * cw_ref[0:1, :]
    for j in range(1, CONV_W):
        acc = acc + xp_ref[pad - 3 + j:pad - 3 + j + c, :] * cw_ref[j:j + 1, :]
    y = _silu(acc)
    xp_ref[pad - 3:pad, :] = xp_ref[pad + c - 3:pad + c, :]

    sm = sm_ref[...]
    beta_all = jax.nn.sigmoid(sm)
    g_all = -jnp.exp(alog_ref[...]) * jax.nn.softplus(sm + dtb_ref[...])
    row, col = _iota((c, c), 0), _iota((c, c), 1)
    incl = row >= col
    strict = row > col
    gam_all = _dot(incl.astype(F32), g_all, HIGHEST)
    gam_t = gam_all.T
    hk = DN_HEADS * DN_DK
    for h in range(DN_HEADS):
        q = y[:, h * DN_DK:(h + 1) * DN_DK]
        k = y[:, hk + h * DN_DK:hk + (h + 1) * DN_DK]
        v = y[:, 2 * hk + h * DN_DV:2 * hk + (h + 1) * DN_DV]
        q = q * lax.rsqrt(jnp.sum(q * q, -1, keepdims=True) + EPS) * DN_DK ** -0.5
        k = k * lax.rsqrt(jnp.sum(k * k, -1, keepdims=True) + EPS)
        beta = beta_all[:, SM_BETA + h:SM_BETA + h + 1]
        gam = gam_all[:, SM_G + h:SM_G + h + 1]
        gam_row = gam_t[SM_G + h:SM_G + h + 1, :]
        dec = jnp.where(incl, jnp.exp(jnp.where(incl, gam - gam_row, 0.0)), 0.0)
        eg = jnp.exp(gam)
        gl = gam[c - 1:c, :]
        kb = k * beta
        kbf = k.astype(BF16)
        a = jnp.where(strict, _dot_nt(kb.astype(BF16), kbf) * dec, 0.0)
        w = _strict_lower_inverse_minus_eye(a)
        rhs = jnp.concatenate([v * beta, kb * eg], -1)
        sol = rhs + _dot(w, rhs)
        s = s_ref[h]
        sbf = s.astype(BF16)
        u = sol[:, :DN_DV] - _dot(sol[:, DN_DV:].astype(BF16), sbf)
        ubf = u.astype(BF16)
        qk = _dot_nt(q.astype(BF16), kbf) * dec
        o = _dot((q * eg).astype(BF16), sbf) + _dot(qk.astype(BF16), ubf)
        kd = k * jnp.exp(gl - gam)
        s_ref[h] = s * jnp.exp(gl) + _dot_tn(kd.astype(BF16), ubf)
        o = o * lax.rsqrt(jnp.mean(o * o, -1, keepdims=True) + EPS) * on_ref[...]
        o_ref[:, h * DN_DV:(h + 1) * DN_DV] = (o * _silu(z_ref[:, h * DN_DV:(h + 1) * DN_DV])).astype(BF16)

    @pl.when(c_idx == pl.num_programs(1) - 1)
    def _():
        s_out_ref[0] = s_ref[...]


def _dn_prompt(p, batch, seq, conv_w, alog_row, dtb_row, onorm):
    c = CHUNK
    nc = seq // c
    return pl.pallas_call(
        _dn_kernel,
        grid=(batch, nc),
        in_specs=[pl.BlockSpec((c, DN_QKV), lambda b, i: (b * nc + i, C_AQKV // DN_QKV)),
                  pl.BlockSpec((c, BRANCH_W), lambda b, i: (b * nc + i, C_AZ // BRANCH_W)),
                  pl.BlockSpec((c, LANES), lambda b, i: (b * nc + i, C_SM // LANES)),
                  pl.BlockSpec((CONV_W, DN_QKV), lambda b, i: (0, 0)),
                  pl.BlockSpec((1, LANES), lambda b, i: (0, 0)),
                  pl.BlockSpec((1, LANES), lambda b, i: (0, 0)),
                  pl.BlockSpec((1, DN_DV), lambda b, i: (0, 0))],
        out_specs=[pl.BlockSpec((c, BRANCH_W), lambda b, i: (b * nc + i, 0)),
                   pl.BlockSpec((1, DN_HEADS, DN_DK, DN_DV), lambda b, i: (b, 0, 0, 0))],
        out_shape=[jax.ShapeDtypeStruct((batch * seq, BRANCH_W), BF16),
                   jax.ShapeDtypeStruct((batch, DN_HEADS, DN_DK, DN_DV), F32)],
        scratch_shapes=[pltpu.VMEM((DN_HEADS, DN_DK, DN_DV), F32),
                        pltpu.VMEM((c + 8, DN_QKV), F32)],
        compiler_params=_cparams("parallel", "arbitrary"),
    )(p, p, p, conv_w, alog_row, dtb_row, onorm)


EXP_CAP = 80.0


def _gla_kernel(q_ref, k_ref, v_ref, r_ref, sm_ref, w2_ref, b_ref, on_ref,
                o_ref, s_out_ref, st_ref):
    c_idx = pl.program_id(1)
    c = CHUNK
    hw = GLA_HEADS * GLA_DK

    @pl.when(c_idx == 0)
    def _():
        st_ref[...] = jnp.zeros_like(st_ref)

    lg = jax.nn.log_sigmoid(_dot(sm_ref[...].astype(BF16), w2_ref[...]) + b_ref[...]) / GLA_TAU
    row, col = _iota((c, c), 0), _iota((c, c), 1)
    gam = _dot((row >= col).astype(F32), lg, HIGHEST)
    gl = gam[c - 1:c, :]
    qs = q_ref[...] * GLA_DK ** -0.5
    k = k_ref[...]
    qg = (qs * jnp.exp(gam)).astype(BF16)
    kd = (k * jnp.exp(gl - gam)).astype(BF16)
    egl = jnp.exp(gl)

    att_rows = [[] for _ in range(GLA_HEADS)]
    for i in range(c // SUB):
        lo, hi = i * SUB, (i + 1) * SUB
        ref_pt = gam[lo - 1:lo, :] if i > 0 else jnp.zeros((1, hw), F32)
        qi = (qs[lo:hi] * jnp.exp(gam[lo:hi] - ref_pt)).astype(BF16)
        ki = (k[:hi] * jnp.exp(jnp.minimum(ref_pt - gam[:hi], EXP_CAP))).astype(BF16)
        keep = (_iota((SUB, hi), 0) + lo) >= _iota((SUB, hi), 1)
        for h in range(GLA_HEADS):
            sl = slice(h * GLA_DK, (h + 1) * GLA_DK)
            att = jnp.where(keep, _dot_nt(qi[:, sl], ki[:, sl]), 0.0)
            att_rows[h].append(_dot(att.astype(BF16), v_ref[:hi, h * GLA_DV:(h + 1) * GLA_DV].astype(BF16)))

    for h in range(GLA_HEADS):
        sl = slice(h * GLA_DK, (h + 1) * GLA_DK)
        vh = v_ref[:, h * GLA_DV:(h + 1) * GLA_DV].astype(BF16)
        st = st_ref[h]
        o = _dot_nt(qg[:, sl], st.astype(BF16)) + jnp.concatenate(att_rows[h], 0)
        st_ref[h] = st * egl[:, sl] + _dot_tn(vh, kd[:, sl])
        o = o * lax.rsqrt(jnp.mean(o * o, -1, keepdims=True) + EPS) * on_ref[...]
        o_ref[:, h * GLA_DV:(h + 1) * GLA_DV] = (o * _silu(r_ref[:, h * GLA_DV:(h + 1) * GLA_DV])).astype(BF16)

    @pl.when(c_idx == pl.num_programs(1) - 1)
    def _():
        s_out_ref[0] = st_ref[...]


def _gla_prompt(p, batch, seq, w2pad, gla_b, onorm):
    c = CHUNK
    nc = seq // c
    hw = GLA_HEADS * GLA_DK
    return pl.pallas_call(
        _gla_kernel,
        grid=(batch, nc),
        in_specs=[pl.BlockSpec((c, hw), lambda b, i: (b * nc + i, C_BQ // hw)),
                  pl.BlockSpec((c, hw), lambda b, i: (b * nc + i, C_BK // hw)),
                  pl.BlockSpec((c, BRANCH_W), lambda b, i: (b * nc + i, C_BV // BRANCH_W)),
                  pl.BlockSpec((c, BRANCH_W), lambda b, i: (b * nc + i, C_BR // BRANCH_W)),
                  pl.BlockSpec((c, LANES), lambda b, i: (b * nc + i, C_SM // LANES)),
                  pl.BlockSpec((LANES, hw), lambda b, i: (0, 0)),
                  pl.BlockSpec((1, hw), lambda b, i: (0, 0)),
                  pl.BlockSpec((1, GLA_DV), lambda b, i: (0, 0))],
        out_specs=[pl.BlockSpec((c, BRANCH_W), lambda b, i: (b * nc + i, 0)),
                   pl.BlockSpec((1, GLA_HEADS, GLA_DV, GLA_DK), lambda b, i: (b, 0, 0, 0))],
        out_shape=[jax.ShapeDtypeStruct((batch * seq, BRANCH_W), BF16),
                   jax.ShapeDtypeStruct((batch, GLA_HEADS, GLA_DV, GLA_DK), F32)],
        scratch_shapes=[pltpu.VMEM((GLA_HEADS, GLA_DV, GLA_DK), F32)],
        compiler_params=_cparams("parallel", "arbitrary"),
    )(p, p, p, p, p, w2pad, gla_b, onorm)


def _half_rms(x, g2):
    lane = _iota(x.shape, 1)
    first = lane < SWA_HD
    sq = x * x
    s0 = jnp.sum(jnp.where(first, sq, 0.0), -1, keepdims=True)
    s1 = jnp.sum(jnp.where(first, 0.0, sq), -1, keepdims=True)
    ms = jnp.where(first, s0, s1) * (1.0 / SWA_HD)
    return x * lax.rsqrt(ms + EPS) * g2


def _swa_kernel(q_ref, kp_ref, kc_ref, vp_ref, vc_ref, qn_ref, kn_ref, snk_ref,
                o_ref, knew_ref):
    i = pl.program_id(1)
    w = WINDOW
    kc = _half_rms(kc_ref[...], kn_ref[...])
    kk = jnp.concatenate([_half_rms(kp_ref[...], kn_ref[...]), kc], 0).astype(BF16)
    vv = jnp.concatenate([vp_ref[...], vc_ref[...]], 0).astype(BF16)
    t = _iota((w, 2 * w), 0)
    j = _iota((w, 2 * w), 1)
    dist = w + t - j
    valid = (dist >= 0) & (dist <= w) & ((j >= w) | (i > 0))
    distf = dist.astype(F32)
    g = SWA_HEADS // SWA_KV
    for h in range(SWA_HEADS):
        kv = h // g
        q = q_ref[:, h * SWA_HD:(h + 1) * SWA_HD]
        q = q * lax.rsqrt(jnp.mean(q * q, -1, keepdims=True) + EPS) * qn_ref[...] * SWA_HD ** -0.5
        s = _dot_nt(q.astype(BF16), kk[:, kv * SWA_HD:(kv + 1) * SWA_HD])
        s = s - (2.0 ** (-8.0 * (h + 1) / SWA_HEADS)) * distf
        s = jnp.where(valid, s, -jnp.inf)
        snk = snk_ref[:, h:h + 1]
        m = jnp.maximum(jnp.max(s, -1, keepdims=True), snk)
        pr = jnp.exp(s - m)
        pr = pr / (jnp.sum(pr, -1, keepdims=True) + jnp.exp(snk - m))
        o = _dot(pr.astype(BF16), vv[:, kv * SWA_HD:(kv + 1) * SWA_HD])
        o_ref[:, h * SWA_HD:(h + 1) * SWA_HD] = o.astype(BF16)

    @pl.when(i == pl.num_programs(1) - 1)
    def _():
        knew_ref[0] = kc


def _swa_prompt(p, batch, seq, qn, kn2, sinks):
    w = WINDOW
    nw = seq // w
    kvw = SWA_KV * SWA_HD
    return pl.pallas_call(
        _swa_kernel,
        grid=(batch, nw),
        in_specs=[pl.BlockSpec((w, BRANCH_W), lambda b, i: (b * nw + i, C_CQ // BRANCH_W)),
                  pl.BlockSpec((w, kvw), lambda b, i: (b * nw + jnp.maximum(i - 1, 0), C_CK // kvw)),
                  pl.BlockSpec((w, kvw), lambda b, i: (b * nw + i, C_CK // kvw)),
                  pl.BlockSpec((w, kvw), lambda b, i: (b * nw + jnp.maximum(i - 1, 0), C_CV // kvw)),
                  pl.BlockSpec((w, kvw), lambda b, i: (b * nw + i, C_CV // kvw)),
                  pl.BlockSpec((1, SWA_HD), lambda b, i: (0, 0)),
                  pl.BlockSpec((1, kvw), lambda b, i: (0, 0)),
                  pl.BlockSpec((1, SWA_HEADS), lambda b, i: (0, 0))],
        out_specs=[pl.BlockSpec((w, BRANCH_W), lambda b, i: (b * nw + i, 0)),
                   pl.BlockSpec((1, w, kvw), lambda b, i: (b, 0, 0))],
        out_shape=[jax.ShapeDtypeStruct((batch * seq, BRANCH_W), BF16),
                   jax.ShapeDtypeStruct((batch, w, kvw), F32)],
        compiler_params=_cparams("parallel", "arbitrary"),
    )(p, p, p, p, p, qn, kn2, sinks)


DEC_TILE = 8


def _dec_kernel(qkv_ref, z_ref, sm_ref, bq_ref, bk_ref, bv_ref, br_ref, cq_ref, ck_ref, cv_ref,
                sdn_ref, buf_ref, sgl_ref, kc_ref, vc_ref,
                cw_ref, alog_ref, dtb_ref, dnon_ref, w2_ref, glb_ref, glon_ref,
                qn_ref, kn_ref, snk_ref, slp_ref,
                oa_ref, ob_ref, oc_ref, sdn_out, buf_out, sgl_out, kc_out, vc_out):
    bt = DEC_TILE
    x = qkv_ref[...]
    buf = buf_ref[...]
    acc = x * cw_ref[CONV_W - 1:CONV_W, :]
    for j in range(CONV_W - 1):
        acc = acc + buf[:, j * DN_QKV:(j + 1) * DN_QKV] * cw_ref[j:j + 1, :]
    y = _silu(acc)
    buf_out[:, 0:(CONV_W - 2) * DN_QKV] = buf[:, DN_QKV:]
    buf_out[:, (CONV_W - 2) * DN_QKV:] = x

    sm = sm_ref[...]
    beta_all = jax.nn.sigmoid(sm)
    g_all = -jnp.exp(alog_ref[...]) * jax.nn.softplus(sm + dtb_ref[...])
    eye = (_iota((LANES, LANES), 0) == _iota((LANES, LANES), 1)).astype(F32)
    hk = DN_HEADS * DN_DK
    for h in range(DN_HEADS):
        q = y[:, h * DN_DK:(h + 1) * DN_DK]
        k = y[:, hk + h * DN_DK:hk + (h + 1) * DN_DK]
        v = y[:, 2 * hk + h * DN_DV:2 * hk + (h + 1) * DN_DV]
        q = q * lax.rsqrt(jnp.sum(q * q, -1, keepdims=True) + EPS) * DN_DK ** -0.5
        k = k * lax.rsqrt(jnp.sum(k * k, -1, keepdims=True) + EPS)
        beta = beta_all[:, SM_BETA + h:SM_BETA + h + 1]
        eg = jnp.exp(g_all[:, SM_G + h:SM_G + h + 1])
        kb = k * beta
        lhs = jnp.concatenate([kb * eg, q * eg], 0).astype(BF16)
        k_t = _dot_nt(eye, k)
        qk = jnp.sum(q * k, -1, keepdims=True)
        vb = v * beta
        o_rows = []
        for b in range(bt):
            s = sdn_ref[b, h]
            r = _dot(lhs, s.astype(BF16))
            u = vb[b:b + 1] - r[b:b + 1]
            o_rows.append(r[bt + b:bt + b + 1] + qk[b:b + 1] * u)
            sdn_out[b, h] = s * eg[b:b + 1] + k_t[:, b:b + 1] * u
        o = jnp.concatenate(o_rows, 0)
        o = o * lax.rsqrt(jnp.mean(o * o, -1, keepdims=True) + EPS) * dnon_ref[...]
        oa_ref[:, h * DN_DV:(h + 1) * DN_DV] = (o * _silu(z_ref[:, h * DN_DV:(h + 1) * DN_DV])).astype(BF16)

    lg = jax.nn.log_sigmoid(_dot(sm.astype(BF16), w2_ref[...]) + glb_ref[...]) / GLA_TAU
    elg = jnp.exp(lg)
    bq = bq_ref[...] * GLA_DK ** -0.5
    bk = bk_ref[...]
    qg = bq * elg
    lane = _iota((bt, LANES), 1)
    first = lane < GLA_DK
    rows_first = _iota((LANES, GLA_DV), 0) < GLA_DK
    for j in range(GLA_HEADS // 2):
        sl = slice(j * LANES, (j + 1) * LANES)
        qgj = qg[:, sl]
        lhs = jnp.concatenate([jnp.where(first, qgj, 0.0), jnp.where(first, 0.0, qgj)], 0).astype(BF16)
        cols = _dot_nt(eye, jnp.concatenate([elg[:, sl], bk[:, sl]], 0), HIGHEST)
        prod = bq[:, sl] * bk[:, sl]
        qk0 = jnp.sum(jnp.where(first, prod, 0.0), -1, keepdims=True)
        qk1 = jnp.sum(jnp.where(first, 0.0, prod), -1, keepdims=True)
        v0 = bv_ref[:, (2 * j) * GLA_DV:(2 * j + 1) * GLA_DV]
        v1 = bv_ref[:, (2 * j + 1) * GLA_DV:(2 * j + 2) * GLA_DV]
        o0, o1 = [], []
        for b in range(bt):
            s = sgl_ref[b, j]
            r = _dot(lhs, s.astype(BF16))
            o0.append(r[b:b + 1] + qk0[b:b + 1] * v0[b:b + 1])
            o1.append(r[bt + b:bt + b + 1] + qk1[b:b + 1] * v1[b:b + 1])
            vsel = jnp.where(rows_first, v0[b:b + 1], v1[b:b + 1])
            sgl_out[b, j] = s * cols[:, b:b + 1] + cols[:, bt + b:bt + b + 1] * vsel
        for hh, rows in ((2 * j, o0), (2 * j + 1, o1)):
            o = jnp.concatenate(rows, 0)
            o = o * lax.rsqrt(jnp.mean(o * o, -1, keepdims=True) + EPS) * glon_ref[...]
            ob_ref[:, hh * GLA_DV:(hh + 1) * GLA_DV] = (o * _silu(br_ref[:, hh * GLA_DV:(hh + 1) * GLA_DV])).astype(BF16)

    g = SWA_HEADS // SWA_KV
    nr = bt * SWA_HEADS
    cq = cq_ref[...]
    cq = cq * lax.rsqrt(jnp.mean(cq * cq, -1, keepdims=True) + EPS) * qn_ref[...] * SWA_HD ** -0.5
    head = _iota((nr, 2 * SWA_HD), 0) % SWA_HEADS
    in_half = (head // g) == (_iota((nr, 2 * SWA_HD), 1) // SWA_HD)
    qx = jnp.where(in_half, jnp.concatenate([cq, cq], -1), 0.0)
    knew = _half_rms(ck_ref[...], kn_ref[...])
    vnew = cv_ref[...]
    s_c, kn_rows, vn_rows = [], [], []
    for b in range(bt):
        s_c.append(_dot_nt(qx[b * SWA_HEADS:(b + 1) * SWA_HEADS].astype(BF16), kc_ref[b].astype(BF16)))
        kn_rows.append(jnp.broadcast_to(knew[b:b + 1], (SWA_HEADS, 2 * SWA_HD)))
        vn_rows.append(jnp.broadcast_to(vnew[b:b + 1], (SWA_HEADS, 2 * SWA_HD)))
    s_c = jnp.concatenate(s_c, 0)
    kn_x = jnp.concatenate(kn_rows, 0)
    vn_x = jnp.concatenate(vn_rows, 0)
    slopes = slp_ref[...]
    snk = snk_ref[...]
    dist = (WINDOW - _iota((nr, WINDOW), 1)).astype(F32)
    s_c = s_c - slopes * dist
    s_n = jnp.sum(qx * kn_x, -1, keepdims=True)
    m = jnp.maximum(jnp.maximum(jnp.max(s_c, -1, keepdims=True), s_n), snk)
    p_c = jnp.exp(s_c - m)
    p_n = jnp.exp(s_n - m)
    den = jnp.sum(p_c, -1, keepdims=True) + p_n + jnp.exp(snk - m)
    p_c = p_c / den
    p_n = p_n / den
    half_sel = (_iota((nr, SWA_HD), 0) % SWA_HEADS) < g
    for b in range(bt):
        rs = slice(b * SWA_HEADS, (b + 1) * SWA_HEADS)
        r = _dot(p_c[rs].astype(BF16), vc_ref[b].astype(BF16)) + p_n[rs] * vn_x[rs]
        oc_ref[rs, :] = jnp.where(half_sel[rs], r[:, :SWA_HD], r[:, SWA_HD:]).astype(BF16)
        kc_out[b, 0:WINDOW - 1, :] = kc_ref[b, 1:WINDOW, :]
        kc_out[b, WINDOW - 1:WINDOW, :] = knew[b:b + 1]
        vc_out[b, 0:WINDOW - 1, :] = vc_ref[b, 1:WINDOW, :]
        vc_out[b, WINDOW - 1:WINDOW, :] = vnew[b:b + 1]


def _dec_mixers(p, cq_r, sdn, buf, sgl2, kc, vc, conv_w, alog_row, dtb_row, dn_on, w2pad, gla_b, gla_on,
                qn, kn2, snk_col, slp_col):
    n = p.shape[0]
    bt = DEC_TILE
    hw = GLA_HEADS * GLA_DK
    kvw = SWA_KV * SWA_HD
    nr = bt * SWA_HEADS

    def col(width, off):
        return pl.BlockSpec((bt, width), lambda i: (i, off // width))

    def full(shape):
        return pl.BlockSpec(shape, lambda i: tuple(0 for _ in shape))

    def lead(shape):
        return pl.BlockSpec((bt,) + shape, lambda i: (i,) + tuple(0 for _ in shape))

    return pl.pallas_call(
        _dec_kernel,
        grid=(n // bt,),
        in_specs=[col(DN_QKV, C_AQKV), col(BRANCH_W, C_AZ), col(LANES, C_SM), col(hw, C_BQ), col(hw, C_BK),
                  col(BRANCH_W, C_BV), col(BRANCH_W, C_BR),
                  pl.BlockSpec((nr, SWA_HD), lambda i: (i, 0)),
                  col(kvw, C_CK), col(kvw, C_CV),
                  lead((DN_HEADS, DN_DK, DN_DV)),
                  pl.BlockSpec((bt, (CONV_W - 1) * DN_QKV), lambda i: (i, 0)),
                  lead((GLA_HEADS // 2, 2 * GLA_DK, GLA_DV)),
                  lead((WINDOW, kvw)), lead((WINDOW, kvw)),
                  full((CONV_W, DN_QKV)), full((1, LANES)), full((1, LANES)), full((1, DN_DV)),
                  full((LANES, hw)), full((1, hw)), full((1, GLA_DV)),
                  full((1, SWA_HD)), full((1, kvw)), full((nr, 1)), full((nr, 1))],
        out_specs=[pl.BlockSpec((bt, BRANCH_W), lambda i: (i, 0)),
                   pl.BlockSpec((bt, BRANCH_W), lambda i: (i, 0)),
                   pl.BlockSpec((nr, SWA_HD), lambda i: (i, 0)),
                   lead((DN_HEADS, DN_DK, DN_DV)),
                   pl.BlockSpec((bt, (CONV_W - 1) * DN_QKV), lambda i: (i, 0)),
                   lead((GLA_HEADS // 2, 2 * GLA_DK, GLA_DV)),
                   lead((WINDOW, kvw)), lead((WINDOW, kvw))],
        out_shape=[jax.ShapeDtypeStruct((n, BRANCH_W), BF16),
                   jax.ShapeDtypeStruct((n, BRANCH_W), BF16),
                   jax.ShapeDtypeStruct((n * SWA_HEADS, SWA_HD), BF16),
                   jax.ShapeDtypeStruct(sdn.shape, F32),
                   jax.ShapeDtypeStruct(buf.shape, F32),
                   jax.ShapeDtypeStruct(sgl2.shape, F32),
                   jax.ShapeDtypeStruct(kc.shape, F32),
                   jax.ShapeDtypeStruct(vc.shape, F32)],
        compiler_params=_cparams("parallel"),
    )(p, p, p, p, p, p, p, cq_r, p, p, sdn, buf, sgl2, kc, vc,
      conv_w, alog_row, dtb_row, dn_on, w2pad, gla_b, gla_on, qn, kn2, snk_col, slp_col)


def _merge_kernel(ba_ref, bb_ref, bc_ref, gate_ref, x_ref, gt_ref, sc_ref, sh_ref, ln_ref,
                  wb_ref, wo_ref, wr_ref, rb_ref, x1_ref, h2_ref, comb_ref):
    mix = None
    for n, br in enumerate((ba_ref, bb_ref, bc_ref)):
        up = _dot(br[...], wb_ref[n])
        term = jax.nn.sigmoid(gate_ref[:, n * D_MODEL:(n + 1) * D_MODEL]) * up
        mix = term if mix is None else mix + term
    x1 = x_ref[...] + gt_ref[0] * _dot(mix.astype(BF16), wo_ref[...])
    x1_ref[...] = x1
    h2 = x1 * lax.rsqrt(jnp.mean(x1 * x1, -1, keepdims=True) + EPS) * ln_ref[...]
    h2 = h2 * (1.0 + sc_ref[0]) + sh_ref[0]
    h2_ref[...] = h2.astype(BF16)

    logits = _dot(h2, wr_ref[...], HIGHEST) + rb_ref[...]
    lane = _iota(logits.shape, 1).astype(F32)
    big = float(LANES)
    lc = jnp.where(lane < N_GROUPS, logits, -jnp.inf)
    mc = jnp.max(lc, -1, keepdims=True)
    pg = 1.0 / jnp.sum(jnp.exp(lc - mc), -1, keepdims=True)
    grp = jnp.min(jnp.where(lc == mc, lane, big), -1, keepdims=True)
    lo = R_EXP + grp * EXP_PER_GROUP
    emask = (lane >= lo) & (lane < lo + EXP_PER_GROUP)
    le = jnp.where(emask, logits, -jnp.inf)
    pe = jnp.exp(le - jnp.max(le, -1, keepdims=True))
    pe = pe / jnp.sum(pe, -1, keepdims=True)
    v1 = jnp.max(pe, -1, keepdims=True)
    i1 = jnp.min(jnp.where(emask & (pe == v1), lane, big), -1, keepdims=True)
    pe2 = jnp.where(emask & (lane != i1), pe, -1.0)
    v2 = jnp.max(pe2, -1, keepdims=True)
    i2 = jnp.min(jnp.where(pe2 == v2, lane, big), -1, keepdims=True)
    tot = v1 + v2
    comb_ref[...] = jnp.where(lane == i1, pg * v1 / tot, 0.0) + jnp.where(lane == i2, pg * v2 / tot, 0.0)


def _merge(ba, bb, bc, p, x, gt, sc, sh, ln, wb, wo, wr, rb, tm, tokens_per_row):
    t, d = x.shape
    tok = lambda width: pl.BlockSpec((tm, width), lambda i: (i, 0))
    full = lambda shape: pl.BlockSpec(shape, lambda i: tuple(0 for _ in shape))
    return pl.pallas_call(
        _merge_kernel,
        grid=(t // tm,),
        in_specs=[tok(BRANCH_W), tok(BRANCH_W), tok(BRANCH_W),
                  pl.BlockSpec((tm, N_BRANCH * d), lambda i: (i, C_GATE)),
                  tok(d),
                  _mod_spec(gt, tm, tokens_per_row), _mod_spec(sc, tm, tokens_per_row),
                  _mod_spec(sh, tm, tokens_per_row),
                  full((1, d)), full((N_BRANCH, BRANCH_W, d)), full((d, d)), full((d, LANES)), full((1, LANES))],
        out_specs=[tok(d), tok(d), tok(LANES)],
        out_shape=[jax.ShapeDtypeStruct((t, d), F32),
                   jax.ShapeDtypeStruct((t, d), BF16),
                   jax.ShapeDtypeStruct((t, LANES), F32)],
        compiler_params=_cparams("parallel"),
    )(ba, bb, bc, p, x, gt, sc, sh, ln, wb, wo, wr, rb)


def _moe_kernel(h_ref, comb_ref, x1_ref, gt_ref, w1_ref, w3_ref, w2_ref, o_ref, acc_ref):
    e = pl.program_id(1)

    @pl.when(e == 0)
    def _():
        acc_ref[...] = jnp.zeros_like(acc_ref)

    h = h_ref[...]
    he = _silu(_dot(h, w1_ref[0])) * _dot(h, w3_ref[0])
    ye = _dot(he.astype(BF16), w2_ref[0])
    comb = comb_ref[...]
    lane = _iota(comb.shape, 1)
    ce = jnp.sum(jnp.where(lane == e + R_EXP, comb, 0.0), -1, keepdims=True)
    acc_ref[...] += ce * ye

    @pl.when(e == pl.num_programs(1) - 1)
    def _():
        o_ref[...] = x1_ref[...] + gt_ref[0] * acc_ref[...]


def _moe(h2, comb, x1, gt, w1, w3, w2, tm, tokens_per_row):
    t, d = x1.shape
    ne, _, de = w1.shape
    return pl.pallas_call(
        _moe_kernel,
        grid=(t // tm, ne),
        in_specs=[pl.BlockSpec((tm, d), lambda i, e: (i, 0)),
                  pl.BlockSpec((tm, LANES), lambda i, e: (i, 0)),
                  pl.BlockSpec((tm, d), lambda i, e: (i, 0)),
                  _mod_spec(gt, tm, tokens_per_row),
                  pl.BlockSpec((1, d, de), lambda i, e: (e, 0, 0)),
                  pl.BlockSpec((1, d, de), lambda i, e: (e, 0, 0)),
                  pl.BlockSpec((1, de, d), lambda i, e: (e, 0, 0))],
        out_specs=pl.BlockSpec((tm, d), lambda i, e: (i, 0)),
        out_shape=jax.ShapeDtypeStruct((t, d), F32),
        scratch_shapes=[pltpu.VMEM((tm, d), F32)],
        compiler_params=_cparams("parallel", "arbitrary"),
    )(h2, comb, x1, gt, w1, w3, w2)


def _permute_w_in(w):
    sizes = (DN_QKV, DN_HEADS * DN_DV, DN_HEADS, DN_HEADS,
             GLA_HEADS * GLA_DK, GLA_HEADS * GLA_DK, GLA_HEADS * GLA_DV, GLA_HEADS * GLA_DV, GLA_RANK,
             SWA_HEADS * SWA_HD, SWA_KV * SWA_HD, SWA_KV * SWA_HD, N_BRANCH * D_MODEL)
    offs = [0]
    for s in sizes:
        offs.append(offs[-1] + s)
    seg = lambda i: w[..., offs[i]:offs[i + 1]]
    (a_qkv, a_z, a_b, a_a, b_q, b_k, b_v, b_r, b_lr, c_q, c_k, c_v, gate) = (seg(i) for i in range(len(sizes)))
    small = jnp.concatenate([a_b, a_a, b_lr], -1)
    fill = jnp.zeros(w.shape[:-1] + (P_PAD - C_SM - small.shape[-1],), w.dtype)
    return jnp.concatenate([gate, a_qkv, a_z, b_v, b_r, c_q, b_q, b_k, c_k, c_v, small, fill], -1)


def _lane_row(v, off):
    depth, n = v.shape
    return jnp.zeros((depth, 1, LANES), F32).at[:, 0, off:off + n].set(v.astype(F32))


def kernel(x_prompt, x_sample, c_prompt, c_sample, state_dn, state_dn_conv, state_gla, cache_swa_k, cache_swa_v, ln1_g, ln2_g, ada_w, ada_b, w_in, dn_conv_w, dn_a_log, dn_dt_bias, dn_onorm_g, gla_w2, gla_b, gla_onorm_g, swa_qnorm_g, swa_knorm_g, swa_sinks, w_branch, w_out, rc_w, rc_b, re_w, re_b, w1, w3, w2):
    batch, seq, d = x_prompt.shape
    nb = x_sample.shape[0]
    depth = w_in.shape[0]
    assert x_sample.shape[1] == 1 and d == D_MODEL and seq % CHUNK == 0 and nb % DEC_TILE == 0
    kvw = SWA_KV * SWA_HD

    w_in_p = _permute_w_in(w_in).astype(BF16)
    ada_w_b = ada_w.astype(BF16)
    wb_b, wo_b = w_branch.astype(BF16), w_out.astype(BF16)
    w1_b, w3_b, w2_b = w1.astype(BF16), w3.astype(BF16), w2.astype(BF16)
    wr = jnp.concatenate([rc_w, re_w, jnp.zeros((depth, d, LANES - N_GROUPS - N_EXPERTS), F32)], -1)
    rb = jnp.concatenate([rc_b, re_b, jnp.zeros((depth, LANES - N_GROUPS - N_EXPERTS), F32)], -1)[:, None, :]
    alog_row = _lane_row(dn_a_log, SM_G)
    dtb_row = _lane_row(dn_dt_bias, SM_G)
    w2pad = jnp.zeros((depth, LANES, GLA_HEADS * GLA_DK), F32).at[:, SM_LR:SM_LR + GLA_RANK].set(gla_w2).astype(BF16)
    kn2 = jnp.concatenate([swa_knorm_g] * SWA_KV, -1)[:, None, :]
    slopes = jnp.exp2(-8.0 * jnp.arange(1, SWA_HEADS + 1, dtype=F32) / SWA_HEADS)
    slp_col = jnp.tile(slopes, DEC_TILE)[:, None]

    pad_rows = (-(batch + nb)) % 8
    c_all = jnp.concatenate([c_prompt, c_sample, jnp.zeros((pad_rows, d), F32)], 0)
    mod = _ada(c_all, ada_w_b, ada_b)

    xp = x_prompt.reshape(batch * seq, d)
    xs = x_sample.reshape(nb, d)
    tm_p = 1024 if seq % 1024 == 0 else CHUNK
    tm_m = 256 if seq % 256 == 0 else CHUNK

    st_p, st_s = [], []
    for l in range(depth):
        mp = [m[:, None, :] for m in jnp.split(mod[l, :batch], 6, -1)]
        ms = [m[None] for m in jnp.split(mod[l, batch:batch + nb], 6, -1)]
        ln1, ln2 = ln1_g[l][None], ln2_g[l][None]
        conv_w = dn_conv_w[l]
        dn_on, gla_on = dn_onorm_g[l][None], gla_onorm_g[l][None]
        glb = gla_b[l][None]
        qn = swa_qnorm_g[l][None]
        snk = swa_sinks[l][None]

        pp = _in_proj(xp, mp[1], mp[0], ln1, w_in_p[l], tm_p, seq)
        ba, dn_s = _dn_prompt(pp, batch, seq, conv_w, alog_row[l], dtb_row[l], dn_on)
        bb, gla_st = _gla_prompt(pp, batch, seq, w2pad[l], glb, gla_on)
        bc, k_new = _swa_prompt(pp, batch, seq, qn, kn2[l], snk)
        x1, h2, comb = _merge(ba, bb, bc, pp, xp, mp[2], mp[4], mp[3], ln2, wb_b[l], wo_b[l], wr[l], rb[l],
                              tm_m, seq)
        xp = _moe(h2, comb, x1, mp[5], w1_b[l], w3_b[l], w2_b[l], tm_p, seq)
        pp3 = pp.reshape(batch, seq, P_PAD)
        st_p.append((dn_s,
                     pp3[:, seq - (CONV_W - 1):, C_AQKV:C_AQKV + DN_QKV],
                     jnp.swapaxes(gla_st, -1, -2),
                     k_new.reshape(batch, WINDOW, SWA_KV, SWA_HD),
                     pp3[:, seq - WINDOW:, C_CV:C_CV + kvw].reshape(batch, WINDOW, SWA_KV, SWA_HD)))

        ps = _in_proj(xs, ms[1], ms[0], ln1, w_in_p[l], nb, nb)
        cq_r = ps[:, C_CQ:C_CQ + SWA_HEADS * SWA_HD].reshape(nb * SWA_HEADS, SWA_HD)
        oa, ob, oc_r, sdn_n, buf_n, sgl_n, kc_n, vc_n = _dec_mixers(
            ps, cq_r, state_dn[l], state_dn_conv[l].reshape(nb, (CONV_W - 1) * DN_QKV),
            state_gla[l].reshape(nb, GLA_HEADS // 2, 2 * GLA_DK, GLA_DV),
            cache_swa_k[l].reshape(nb, WINDOW, kvw), cache_swa_v[l].reshape(nb, WINDOW, kvw),
            conv_w, alog_row[l], dtb_row[l], dn_on, w2pad[l], glb, gla_on, qn, kn2[l],
            jnp.tile(swa_sinks[l], DEC_TILE)[:, None], slp_col)
        oc = oc_r.reshape(nb, SWA_HEADS * SWA_HD)
        x1, h2, comb = _merge(oa, ob, oc, ps, xs, ms[2], ms[4], ms[3], ln2, wb_b[l], wo_b[l], wr[l], rb[l],
                              nb, nb)
        xs = _moe(h2, comb, x1, ms[5], w1_b[l], w3_b[l], w2_b[l], nb, nb)
        st_s.append((sdn_n,
                     buf_n.reshape(nb, CONV_W - 1, DN_QKV),
                     sgl_n.reshape(nb, GLA_HEADS, GLA_DK, GLA_DV),
                     kc_n.reshape(nb, WINDOW, SWA_KV, SWA_HD),
                     vc_n.reshape(nb, WINDOW, SWA_KV, SWA_HD)))

    dn_p, conv_p, gla_p, k_p, v_p = [jnp.stack(z) for z in zip(*st_p)]
    dn_s, conv_s, gla_s, k_s, v_s = [jnp.stack(z) for z in zip(*st_s)]
    return (xp.reshape(batch, seq, d), xs.reshape(nb, 1, d), dn_p, dn_s, conv_p, conv_s, gla_p, gla_s,
            k_p, k_s, v_p, v_s)
```

```python
import functools

import jax
import jax.numpy as jnp
from jax import lax
from jax.experimental import pallas as pl
from jax.experimental.pallas import tpu as pltpu

F32 = jnp.float32
BF16 = jnp.bfloat16
HIGHEST = lax.Precision.HIGHEST

D_MODEL = 1024
DN_HEADS, DN_DK, DN_DV, CONV_W = 4, 128, 128, 4
DN_QKV = 2 * DN_HEADS * DN_DK + DN_HEADS * DN_DV
GLA_HEADS, GLA_DK, GLA_DV, GLA_RANK, GLA_TAU = 4, 64, 128, 16, 16.0
SWA_HEADS, SWA_KV, SWA_HD, WINDOW = 8, 2, 64, 128
N_BRANCH, BRANCH_W = 3, 512
N_GROUPS, EXP_PER_GROUP, TOP_K, D_EXPERT = 4, 8, 2, 256
N_EXPERTS = N_GROUPS * EXP_PER_GROUP
EPS = 1e-6

LANES = 128
CHUNK = 128
SUB = 16
VMEM_LIMIT = 56 * 1024 * 1024

C_GATE, C_AQKV, C_AZ, C_BV, C_BR, C_CQ = 0, 3072, 4608, 5120, 5632, 6144
C_BQ, C_BK, C_CK, C_CV, C_SM, P_PAD = 6656, 6912, 7168, 7296, 7424, 7680
SM_BETA, SM_G, SM_LR = 0, DN_HEADS, 2 * DN_HEADS
R_EXP = N_GROUPS


def _cparams(*sem):
    return pltpu.CompilerParams(dimension_semantics=sem, vmem_limit_bytes=VMEM_LIMIT)


def _dot(a, b, precision=None):
    return jnp.dot(a, b, preferred_element_type=F32, precision=precision)


def _dot_nt(a, b, precision=None):
    return lax.dot_general(a, b, (((1,), (1,)), ((), ())), preferred_element_type=F32, precision=precision)


def _dot_tn(a, b, precision=None):
    return lax.dot_general(a, b, (((0,), (0,)), ((), ())), preferred_element_type=F32, precision=precision)


def _silu(x):
    return x * jax.nn.sigmoid(x)


def _iota(shape, dim):
    return lax.broadcasted_iota(jnp.int32, shape, dim)


def _ada_kernel(c_ref, w_ref, b_ref, o_ref):
    c = _silu(c_ref[...]).astype(BF16)
    o_ref[0] = _dot(c, w_ref[0]) + b_ref[0]


def _ada(c_all, ada_w, ada_b):
    depth, d, n = ada_w.shape
    rows = c_all.shape[0]
    tn = 1536
    return pl.pallas_call(
        _ada_kernel,
        grid=(depth, n // tn),
        in_specs=[pl.BlockSpec((rows, d), lambda l, j: (0, 0)),
                  pl.BlockSpec((1, d, tn), lambda l, j: (l, 0, j)),
                  pl.BlockSpec((1, 1, tn), lambda l, j: (l, 0, j))],
        out_specs=pl.BlockSpec((1, rows, tn), lambda l, j: (l, 0, j)),
        out_shape=jax.ShapeDtypeStruct((depth, rows, n), F32),
        compiler_params=_cparams("parallel", "parallel"),
    )(c_all, ada_w, ada_b.reshape(depth, 1, n))


def _mod_spec(mod, tm, tokens_per_row):
    _, r, d = mod.shape
    per = tokens_per_row // tm
    return pl.BlockSpec((1, r, d), lambda i, *_: (i // per, 0, 0))


def _in_proj_kernel(x_ref, sc_ref, sh_ref, g_ref, w_ref, o_ref, h_ref):
    @pl.when(pl.program_id(1) == 0)
    def _():
        x = x_ref[...]
        y = x * lax.rsqrt(jnp.mean(x * x, -1, keepdims=True) + EPS) * g_ref[...]
        h_ref[...] = (y * (1.0 + sc_ref[0]) + sh_ref[0]).astype(BF16)

    o_ref[...] = _dot(h_ref[...], w_ref[...])


def _in_proj(x, sc, sh, g, w, tm, tokens_per_row):
    t, d = x.shape
    n = w.shape[1]
    tn = 1536
    return pl.pallas_call(
        _in_proj_kernel,
        grid=(t // tm, n // tn),
        in_specs=[pl.BlockSpec((tm, d), lambda i, j: (i, 0)),
                  _mod_spec(sc, tm, tokens_per_row), _mod_spec(sh, tm, tokens_per_row),
                  pl.BlockSpec((1, d), lambda i, j: (0, 0)),
                  pl.BlockSpec((d, tn), lambda i, j: (0, j))],
        out_specs=pl.BlockSpec((tm, tn), lambda i, j: (i, j)),
        out_shape=jax.ShapeDtypeStruct((t, n), F32),
        scratch_shapes=[pltpu.VMEM((tm, d), BF16)],
        compiler_params=_cparams("parallel", "arbitrary"),
    )(x, sc, sh, g, w)


def _strict_lower_inverse_minus_eye(a):
    c = a.shape[0]
    row, col = _iota((c, c), 0), _iota((c, c), 1)
    diag_blk = (row // SUB) == (col // SUB)
    ad = jnp.where(diag_blk, a, 0.0)
    ao = a - ad
    n = -ad
    p = n
    steps = SUB.bit_length() - 2
    for _ in range(steps):
        p = _dot(p, p)
        n = n + p + _dot(n, p)
    bm = -(ao + _dot(n, ao))
    m = bm
    q = bm
    steps = (c // SUB).bit_length() - 2
    for _ in range(steps):
        q = _dot(q, q)
        m = m + q + _dot(m, q)
    return m + n + _dot(m, n)


def _strict_lower_inverse_minus_eye_multi(a_list):
    c = a_list[0].shape[0]
    diag_blk = (_iota((c, c), 0) // SUB) == (_iota((c, c), 1) // SUB)
    idx = range(len(a_list))
    ad = [jnp.where(diag_blk, a, 0.0) for a in a_list]
    ao = [a_list[i] - ad[i] for i in idx]
    n = [-x for x in ad]
    p = n
    for _ in range(SUB.bit_length() - 2):
        p = [_dot(x, x) for x in p]
        np_ = [_dot(n[i], p[i]) for i in idx]
        n = [n[i] + p[i] + np_[i] for i in idx]
    nao = [_dot(n[i], ao[i]) for i in idx]
    m = [-(ao[i] + nao[i]) for i in idx]
    q = m
    for _ in range((c // SUB).bit_length() - 2):
        q = [_dot(x, x) for x in q]
        mq = [_dot(m[i], q[i]) for i in idx]
        m = [m[i] + q[i] + mq[i] for i in idx]
    mn = [_dot(m[i], n[i]) for i in idx]
    return [m[i] + n[i] + mn[i] for i in idx]


def _dn_kernel(qkv_ref, z_ref, sm_ref, cw_ref, alog_ref, dtb_ref, on_ref,
               o_ref, s_out_ref, s_ref, xp_ref):
    c_idx = pl.program_id(1)
    c = CHUNK
    pad = 8

    @pl.when(c_idx == 0)
    def _():
        s_ref[...] = jnp.zeros_like(s_ref)
        xp_ref[0:pad, :] = jnp.zeros((pad, DN_QKV), F32)

    xp_ref[pad:pad + c, :] = qkv_ref[...]
    acc = xp_ref[pad - 3:pad - 3 + c, :] * cw_ref[0:1, :]
    for j in range(1, CONV_W):
        acc = acc + xp_ref[pad - 3 + j:pad - 3 + j + c, :] * cw_ref[j:j + 1, :]
    y = _silu(acc)
    xp_ref[pad - 3:pad, :] = xp_ref[pad + c - 3:pad + c, :]

    sm = sm_ref[...]
    beta_all = jax.nn.sigmoid(sm)
    g_all = -jnp.exp(alog_ref[...]) * jax.nn.softplus(sm + dtb_ref[...])
    row, col = _iota((c, c), 0), _iota((c, c), 1)
    incl = row >= col
    strict = row > col
    gam_all = _dot(incl.astype(F32), g_all, HIGHEST)
    gam_t = gam_all.T
    hk = DN_HEADS * DN_DK
    for h in range(DN_HEADS):
        q = y[:, h * DN_DK:(h + 1) * DN_DK]
        k = y[:, hk + h * DN_DK:hk + (h + 1) * DN_DK]
        v = y[:, 2 * hk + h * DN_DV:2 * hk + (h + 1) * DN_DV]
        q = q * lax.rsqrt(jnp.sum(q * q, -1, keepdims=True) + EPS) * DN_DK ** -0.5
        k = k * lax.rsqrt(jnp.sum(k * k, -1, keepdims=True) + EPS)
        beta = beta_all[:, SM_BETA + h:SM_BETA + h + 1]
        gam = gam_all[:, SM_G + h:SM_G + h + 1]
        gam_row = gam_t[SM_G + h:SM_G + h + 1, :]
        dec = jnp.where(incl, jnp.exp(jnp.where(incl, gam - gam_row, 0.0)), 0.0)
        eg = jnp.exp(gam)
        gl = gam[c - 1:c, :]
        kb = k * beta
        kbf = k.astype(BF16)
        a = jnp.where(strict, _dot_nt(kb.astype(BF16), kbf) * dec, 0.0)
        w = _strict_lower_inverse_minus_eye(a)
        rhs = jnp.concatenate([v * beta, kb * eg], -1)
        sol = rhs + _dot(w, rhs)
        s = s_ref[h]
        sbf = s.astype(BF16)
        u = sol[:, :DN_DV] - _dot(sol[:, DN_DV:].astype(BF16), sbf)
        ubf = u.astype(BF16)
        qk = _dot_nt(q.astype(BF16), kbf) * dec
        o = _dot((q * eg).astype(BF16), sbf) + _dot(qk.astype(BF16), ubf)
        kd = k * jnp.exp(gl - gam)
        s_ref[h] = s * jnp.exp(gl) + _dot_tn(kd.astype(BF16), ubf)
        o = o * lax.rsqrt(jnp.mean(o * o, -1, keepdims=True) + EPS) * on_ref[...]
        o_ref[:, h * DN_DV:(h + 1) * DN_DV] = (o * _silu(z_ref[:, h * DN_DV:(h + 1) * DN_DV])).astype(BF16)

    @pl.when(c_idx == pl.num_programs(1) - 1)
    def _():
        s_out_ref[0] = s_ref[...]


def _dn_prompt(p, batch, seq, conv_w, alog_row, dtb_row, onorm):
    c = CHUNK
    nc = seq // c
    return pl.pallas_call(
        _dn_kernel,
        grid=(batch, nc),
        in_specs=[pl.BlockSpec((c, DN_QKV), lambda b, i: (b * nc + i, C_AQKV // DN_QKV)),
                  pl.BlockSpec((c, BRANCH_W), lambda b, i: (b * nc + i, C_AZ // BRANCH_W)),
                  pl.BlockSpec((c, LANES), lambda b, i: (b * nc + i, C_SM // LANES)),
                  pl.BlockSpec((CONV_W, DN_QKV), lambda b, i: (0, 0)),
                  pl.BlockSpec((1, LANES), lambda b, i: (0, 0)),
                  pl.BlockSpec((1, LANES), lambda b, i: (0, 0)),
                  pl.BlockSpec((1, DN_DV), lambda b, i: (0, 0))],
        out_specs=[pl.BlockSpec((c, BRANCH_W), lambda b, i: (b * nc + i, 0)),
                   pl.BlockSpec((1, DN_HEADS, DN_DK, DN_DV), lambda b, i: (b, 0, 0, 0))],
        out_shape=[jax.ShapeDtypeStruct((batch * seq, BRANCH_W), BF16),
                   jax.ShapeDtypeStruct((batch, DN_HEADS, DN_DK, DN_DV), F32)],
        scratch_shapes=[pltpu.VMEM((DN_HEADS, DN_DK, DN_DV), F32),
                        pltpu.VMEM((c + 8, DN_QKV), F32)],
        compiler_params=_cparams("parallel", "arbitrary"),
    )(p, p, p, conv_w, alog_row, dtb_row, onorm)


EXP_CAP = 80.0


def _gla_kernel(q_ref, k_ref, v_ref, r_ref, sm_ref, w2_ref, b_ref, on_ref,
                o_ref, s_out_ref, st_ref):
    c_idx = pl.program_id(1)
    c = CHUNK
    hw = GLA_HEADS * GLA_DK

    @pl.when(c_idx == 0)
    def _():
        st_ref[...] = jnp.zeros_like(st_ref)

    lg = jax.nn.log_sigmoid(_dot(sm_ref[...].astype(BF16), w2_ref[...]) + b_ref[...]) / GLA_TAU
    row, col = _iota((c, c), 0), _iota((c, c), 1)
    gam = _dot((row >= col).astype(F32), lg, HIGHEST)
    gl = gam[c - 1:c, :]
    qs = q_ref[...] * GLA_DK ** -0.5
    k = k_ref[...]
    qg = (qs * jnp.exp(gam)).astype(BF16)
    kd = (k * jnp.exp(gl - gam)).astype(BF16)
    egl = jnp.exp(gl)

    att_rows = [[] for _ in range(GLA_HEADS)]
    for i in range(c // SUB):
        lo, hi = i * SUB, (i + 1) * SUB
        ref_pt = gam[lo - 1:lo, :] if i > 0 else jnp.zeros((1, hw), F32)
        qi = (qs[lo:hi] * jnp.exp(gam[lo:hi] - ref_pt)).astype(BF16)
        ki = (k[:hi] * jnp.exp(jnp.minimum(ref_pt - gam[:hi], EXP_CAP))).astype(BF16)
        keep = (_iota((SUB, hi), 0) + lo) >= _iota((SUB, hi), 1)
        for h in range(GLA_HEADS):
            sl = slice(h * GLA_DK, (h + 1) * GLA_DK)
            att = jnp.where(keep, _dot_nt(qi[:, sl], ki[:, sl]), 0.0)
            att_rows[h].append(_dot(att.astype(BF16), v_ref[:hi, h * GLA_DV:(h + 1) * GLA_DV].astype(BF16)))

    for h in range(GLA_HEADS):
        sl = slice(h * GLA_DK, (h + 1) * GLA_DK)
        vh = v_ref[:, h * GLA_DV:(h + 1) * GLA_DV].astype(BF16)
        st = st_ref[h]
        o = _dot_nt(qg[:, sl], st.astype(BF16)) + jnp.concatenate(att_rows[h], 0)
        st_ref[h] = st * egl[:, sl] + _dot_tn(vh, kd[:, sl])
        o = o * lax.rsqrt(jnp.mean(o * o, -1, keepdims=True) + EPS) * on_ref[...]
        o_ref[:, h * GLA_DV:(h + 1) * GLA_DV] = (o * _silu(r_ref[:, h * GLA_DV:(h + 1) * GLA_DV])).astype(BF16)

    @pl.when(c_idx == pl.num_programs(1) - 1)
    def _():
        s_out_ref[0] = st_ref[...]


def _gla_prompt(p, batch, seq, w2pad, gla_b, onorm):
    c = CHUNK
    nc = seq // c
    hw = GLA_HEADS * GLA_DK
    return pl.pallas_call(
        _gla_kernel,
        grid=(batch, nc),
        in_specs=[pl.BlockSpec((c, hw), lambda b, i: (b * nc + i, C_BQ // hw)),
                  pl.BlockSpec((c, hw), lambda b, i: (b * nc + i, C_BK // hw)),
                  pl.BlockSpec((c, BRANCH_W), lambda b, i: (b * nc + i, C_BV // BRANCH_W)),
                  pl.BlockSpec((c, BRANCH_W), lambda b, i: (b * nc + i, C_BR // BRANCH_W)),
                  pl.BlockSpec((c, LANES), lambda b, i: (b * nc + i, C_SM // LANES)),
                  pl.BlockSpec((LANES, hw), lambda b, i: (0, 0)),
                  pl.BlockSpec((1, hw), lambda b, i: (0, 0)),
                  pl.BlockSpec((1, GLA_DV), lambda b, i: (0, 0))],
        out_specs=[pl.BlockSpec((c, BRANCH_W), lambda b, i: (b * nc + i, 0)),
                   pl.BlockSpec((1, GLA_HEADS, GLA_DV, GLA_DK), lambda b, i: (b, 0, 0, 0))],
        out_shape=[jax.ShapeDtypeStruct((batch * seq, BRANCH_W), BF16),
                   jax.ShapeDtypeStruct((batch, GLA_HEADS, GLA_DV, GLA_DK), F32)],
        scratch_shapes=[pltpu.VMEM((GLA_HEADS, GLA_DV, GLA_DK), F32)],
        compiler_params=_cparams("parallel", "arbitrary"),
    )(p, p, p, p, p, w2pad, gla_b, onorm)


def _half_rms(x, g2):
    lane = _iota(x.shape, 1)
    first = lane < SWA_HD
    sq = x * x
    s0 = jnp.sum(jnp.where(first, sq, 0.0), -1, keepdims=True)
    s1 = jnp.sum(jnp.where(first, 0.0, sq), -1, keepdims=True)
    ms = jnp.where(first, s0, s1) * (1.0 / SWA_HD)
    return x * lax.rsqrt(ms + EPS) * g2


def _swa_kernel(q_ref, kp_ref, kc_ref, vp_ref, vc_ref, qn_ref, kn_ref, snk_ref,
                o_ref, knew_ref):
    i = pl.program_id(1)
    w = WINDOW
    kc = _half_rms(kc_ref[...], kn_ref[...])
    kk = jnp.concatenate([_half_rms(kp_ref[...], kn_ref[...]), kc], 0).astype(BF16)
    vv = jnp.concatenate([vp_ref[...], vc_ref[...]], 0).astype(BF16)
    t = _iota((w, 2 * w), 0)
    j = _iota((w, 2 * w), 1)
    dist = w + t - j
    valid = (dist >= 0) & (dist <= w) & ((j >= w) | (i > 0))
    distf = dist.astype(F32)
    g = SWA_HEADS // SWA_KV
    for h in range(SWA_HEADS):
        kv = h // g
        q = q_ref[:, h * SWA_HD:(h + 1) * SWA_HD]
        q = q * lax.rsqrt(jnp.mean(q * q, -1, keepdims=True) + EPS) * qn_ref[...] * SWA_HD ** -0.5
        s = _dot_nt(q.astype(BF16), kk[:, kv * SWA_HD:(kv + 1) * SWA_HD])
        s = s - (2.0 ** (-8.0 * (h + 1) / SWA_HEADS)) * distf
        s = jnp.where(valid, s, -jnp.inf)
        snk = snk_ref[:, h:h + 1]
        m = jnp.maximum(jnp.max(s, -1, keepdims=True), snk)
        pr = jnp.exp(s - m)
        pr = pr / (jnp.sum(pr, -1, keepdims=True) + jnp.exp(snk - m))
        o = _dot(pr.astype(BF16), vv[:, kv * SWA_HD:(kv + 1) * SWA_HD])
        o_ref[:, h * SWA_HD:(h + 1) * SWA_HD] = o.astype(BF16)

    @pl.when(i == pl.num_programs(1) - 1)
    def _():
        knew_ref[0] = kc


def _swa_prompt(p, batch, seq, qn, kn2, sinks):
    w = WINDOW
    nw = seq // w
    kvw = SWA_KV * SWA_HD
    return pl.pallas_call(
        _swa_kernel,
        grid=(batch, nw),
        in_specs=[pl.BlockSpec((w, BRANCH_W), lambda b, i: (b * nw + i, C_CQ // BRANCH_W)),
                  pl.BlockSpec((w, kvw), lambda b, i: (b * nw + jnp.maximum(i - 1, 0), C_CK // kvw)),
                  pl.BlockSpec((w, kvw), lambda b, i: (b * nw + i, C_CK // kvw)),
                  pl.BlockSpec((w, kvw), lambda b, i: (b * nw + jnp.maximum(i - 1, 0), C_CV // kvw)),
                  pl.BlockSpec((w, kvw), lambda b, i: (b * nw + i, C_CV // kvw)),
                  pl.BlockSpec((1, SWA_HD), lambda b, i: (0, 0)),
                  pl.BlockSpec((1, kvw), lambda b, i: (0, 0)),
                  pl.BlockSpec((1, SWA_HEADS), lambda b, i: (0, 0))],
        out_specs=[pl.BlockSpec((w, BRANCH_W), lambda b, i: (b * nw + i, 0)),
                   pl.BlockSpec((1, w, kvw), lambda b, i: (b, 0, 0))],
        out_shape=[jax.ShapeDtypeStruct((batch * seq, BRANCH_W), BF16),
                   jax.ShapeDtypeStruct((batch, w, kvw), F32)],
        compiler_params=_cparams("parallel", "arbitrary"),
    )(p, p, p, p, p, qn, kn2, sinks)


def _dn_mb_kernel(qkv_ref, z_ref, sm_ref, cw_ref, alog_ref, dtb_ref, on_ref,
                  o_ref, s_out_ref, s_ref, xp_ref, y_ref):
    c_idx = pl.program_id(0)
    nb = qkv_ref.shape[0]
    c = CHUNK
    pad = 8

    @pl.when(c_idx == 0)
    def _():
        s_ref[...] = jnp.zeros_like(s_ref)
        xp_ref[:, 0:pad, :] = jnp.zeros((nb, pad, DN_QKV), F32)

    row, col = _iota((c, c), 0), _iota((c, c), 1)
    incl = row >= col
    strict = row > col
    incl_f = incl.astype(F32)
    hk = DN_HEADS * DN_DK
    gam_all, gam_t, beta_all = [], [], []
    for b in range(nb):
        xp_ref[b, pad:pad + c, :] = qkv_ref[b]
        acc = xp_ref[b, pad - 3:pad - 3 + c, :] * cw_ref[0:1, :]
        for j in range(1, CONV_W):
            acc = acc + xp_ref[b, pad - 3 + j:pad - 3 + j + c, :] * cw_ref[j:j + 1, :]
        y_ref[b] = _silu(acc)
        xp_ref[b, pad - 3:pad, :] = xp_ref[b, pad + c - 3:pad + c, :]
        sm = sm_ref[b]
        beta_all.append(jax.nn.sigmoid(sm))
        g_all = -jnp.exp(alog_ref[...]) * jax.nn.softplus(sm + dtb_ref[...])
        gam_all.append(_dot(incl_f, g_all, HIGHEST))
        gam_t.append(gam_all[b].T)

    chains = [(b, h) for b in range(nb) for h in range(DN_HEADS)]
    n = len(chains)
    q, k, dec, eg, gl, gam, kb, rhs, a = ([None] * n for _ in range(9))
    for i, (b, h) in enumerate(chains):
        qi = y_ref[b, :, h * DN_DK:(h + 1) * DN_DK]
        ki = y_ref[b, :, hk + h * DN_DK:hk + (h + 1) * DN_DK]
        vi = y_ref[b, :, 2 * hk + h * DN_DV:2 * hk + (h + 1) * DN_DV]
        q[i] = qi * lax.rsqrt(jnp.sum(qi * qi, -1, keepdims=True) + EPS) * DN_DK ** -0.5
        k[i] = ki * lax.rsqrt(jnp.sum(ki * ki, -1, keepdims=True) + EPS)
        beta = beta_all[b][:, SM_BETA + h:SM_BETA + h + 1]
        gam[i] = gam_all[b][:, SM_G + h:SM_G + h + 1]
        gam_row = gam_t[b][SM_G + h:SM_G + h + 1, :]
        dec[i] = jnp.where(incl, jnp.exp(jnp.minimum(gam[i] - gam_row, 0.0)), 0.0)
        eg[i] = jnp.exp(gam[i])
        gl[i] = gam[i][c - 1:c, :]
        kb[i] = k[i] * beta
        rhs[i] = jnp.concatenate([vi * beta, kb[i] * eg[i]], -1)
    kbf = [x.astype(BF16) for x in k]
    kk = [_dot_nt(kb[i].astype(BF16), kbf[i]) for i in range(n)]
    qk = [_dot_nt(q[i].astype(BF16), kbf[i]) for i in range(n)]
    a = [jnp.where(strict, kk[i] * dec[i], 0.0) for i in range(n)]
    w = _strict_lower_inverse_minus_eye_multi(a)
    sol = [rhs[i] + _dot(w[i], rhs[i]) for i in range(n)]
    s = [s_ref[b, h] for (b, h) in chains]
    sbf = [x.astype(BF16) for x in s]
    u = [sol[i][:, :DN_DV] - _dot(sol[i][:, DN_DV:].astype(BF16), sbf[i]) for i in range(n)]
    ubf = [x.astype(BF16) for x in u]
    o_s = [_dot((q[i] * eg[i]).astype(BF16), sbf[i]) for i in range(n)]
    o_u = [_dot((qk[i] * dec[i]).astype(BF16), ubf[i]) for i in range(n)]
    ds = [_dot_tn((k[i] * jnp.exp(gl[i] - gam[i])).astype(BF16), ubf[i]) for i in range(n)]
    for i, (b, h) in enumerate(chains):
        s_ref[b, h] = s[i] * jnp.exp(gl[i]) + ds[i]
        o = o_s[i] + o_u[i]
        o = o * lax.rsqrt(jnp.mean(o * o, -1, keepdims=True) + EPS) * on_ref[...]
        o_ref[b, :, h * DN_DV:(h + 1) * DN_DV] = (
            o * _silu(z_ref[b, :, h * DN_DV:(h + 1) * DN_DV])).astype(BF16)

    @pl.when(c_idx == pl.num_programs(0) - 1)
    def _():
        s_out_ref[...] = s_ref[...]


def _dn_prompt_mb(p3, conv_w, alog_row, dtb_row, onorm):
    batch, seq, _ = p3.shape
    c = CHUNK
    full = lambda shape: pl.BlockSpec(shape, lambda i: tuple(0 for _ in shape))
    return pl.pallas_call(
        _dn_mb_kernel,
        grid=(seq // c,),
        in_specs=[pl.BlockSpec((batch, c, DN_QKV), lambda i: (0, i, C_AQKV // DN_QKV)),
                  pl.BlockSpec((batch, c, BRANCH_W), lambda i: (0, i, C_AZ // BRANCH_W)),
                  pl.BlockSpec((batch, c, LANES), lambda i: (0, i, C_SM // LANES)),
                  full((CONV_W, DN_QKV)), full((1, LANES)), full((1, LANES)), full((1, DN_DV))],
        out_specs=[pl.BlockSpec((batch, c, BRANCH_W), lambda i: (0, i, 0)),
                   full((batch, DN_HEADS, DN_DK, DN_DV))],
        out_shape=[jax.ShapeDtypeStruct((batch, seq, BRANCH_W), BF16),
                   jax.ShapeDtypeStruct((batch, DN_HEADS, DN_DK, DN_DV), F32)],
        scratch_shapes=[pltpu.VMEM((batch, DN_HEADS, DN_DK, DN_DV), F32),
                        pltpu.VMEM((batch, c + 8, DN_QKV), F32),
                        pltpu.VMEM((batch, c, DN_QKV), F32)],
        compiler_params=_cparams("arbitrary"),
    )(p3, p3, p3, conv_w, alog_row, dtb_row, onorm)


def _gla_mb_kernel(q_ref, k_ref, v_ref, r_ref, sm_ref, w2_ref, b_ref, on_ref,
                   o_ref, s_out_ref, st_ref):
    c_idx = pl.program_id(0)
    nb = q_ref.shape[0]
    c = CHUNK
    hw = GLA_HEADS * GLA_DK
    hv = GLA_HEADS * GLA_DV

    @pl.when(c_idx == 0)
    def _():
        st_ref[...] = jnp.zeros_like(st_ref)

    incl_f = (_iota((c, c), 0) >= _iota((c, c), 1)).astype(F32)
    blk = (_iota((hv, hw), 0) // GLA_DV) == (_iota((hv, hw), 1) // GLA_DK)
    qsel = (_iota((GLA_HEADS * SUB, hw), 0) // SUB) == (_iota((GLA_HEADS * SUB, hw), 1) // GLA_DK)
    nbr = range(nb)
    lg = [jax.nn.log_sigmoid(_dot(sm_ref[b].astype(BF16), w2_ref[...]) + b_ref[...]) / GLA_TAU for b in nbr]
    gam = [_dot(incl_f, lg[b], HIGHEST) for b in nbr]
    gl = [gam[b][c - 1:c, :] for b in nbr]
    qs = [q_ref[b] * GLA_DK ** -0.5 for b in nbr]
    k = [k_ref[b] for b in nbr]
    qg = [(qs[b] * jnp.exp(gam[b])).astype(BF16) for b in nbr]
    kd = [(k[b] * jnp.exp(gl[b] - gam[b])).astype(BF16) for b in nbr]
    vbf = [v_ref[b].astype(BF16) for b in nbr]
    st = [st_ref[b] for b in nbr]
    o_inter = [_dot_nt(qg[b], st[b].astype(BF16)) for b in nbr]
    dst = [_dot_tn(vbf[b], kd[b]) for b in nbr]
    for b in nbr:
        st_ref[b] = st[b] * jnp.exp(gl[b]) + jnp.where(blk, dst[b], 0.0)

    res = [[] for _ in nbr]
    for i in range(c // SUB):
        lo, hi = i * SUB, (i + 1) * SUB
        keep = (_iota((GLA_HEADS * SUB, hi), 0) % SUB + lo) >= _iota((GLA_HEADS * SUB, hi), 1)
        qm, ki = [], []
        for b in nbr:
            ref_pt = gam[b][lo - 1:lo, :] if i > 0 else jnp.zeros((1, hw), F32)
            qi = qs[b][lo:hi] * jnp.exp(gam[b][lo:hi] - ref_pt)
            qm.append(jnp.where(qsel, jnp.concatenate([qi] * GLA_HEADS, 0), 0.0).astype(BF16))
            ki.append((k[b][:hi] * jnp.exp(jnp.minimum(ref_pt - gam[b][:hi], EXP_CAP))).astype(BF16))
        att = [_dot_nt(qm[b], ki[b]) for b in nbr]
        att = [jnp.where(keep, att[b], 0.0).astype(BF16) for b in nbr]
        for b in nbr:
            res[b].append(_dot(att[b], vbf[b][:hi]))
    for b in nbr:
        for h in range(GLA_HEADS):
            vs = slice(h * GLA_DV, (h + 1) * GLA_DV)
            o = o_inter[b][:, vs] + jnp.concatenate([r[h * SUB:(h + 1) * SUB, vs] for r in res[b]], 0)
            o = o * lax.rsqrt(jnp.mean(o * o, -1, keepdims=True) + EPS) * on_ref[...]
            o_ref[b, :, vs] = (o * _silu(r_ref[b, :, vs])).astype(BF16)

    @pl.when(c_idx == pl.num_programs(0) - 1)
    def _():
        for b in range(nb):
            for h in range(GLA_HEADS):
                s_out_ref[b, h] = st_ref[b, h * GLA_DV:(h + 1) * GLA_DV, h * GLA_DK:(h + 1) * GLA_DK]


def _gla_prompt_mb(p3, w2pad, gla_b, onorm):
    batch, seq, _ = p3.shape
    c = CHUNK
    hw = GLA_HEADS * GLA_DK
    hv = GLA_HEADS * GLA_DV
    full = lambda shape: pl.BlockSpec(shape, lambda i: tuple(0 for _ in shape))
    return pl.pallas_call(
        _gla_mb_kernel,
        grid=(seq // c,),
        in_specs=[pl.BlockSpec((batch, c, hw), lambda i: (0, i, C_BQ // hw)),
                  pl.BlockSpec((batch, c, hw), lambda i: (0, i, C_BK // hw)),
                  pl.BlockSpec((batch, c, hv), lambda i: (0, i, C_BV // hv)),
                  pl.BlockSpec((batch, c, hv), lambda i: (0, i, C_BR // hv)),
                  pl.BlockSpec((batch, c, LANES), lambda i: (0, i, C_SM // LANES)),
                  full((LANES, hw)), full((1, hw)), full((1, GLA_DV))],
        out_specs=[pl.BlockSpec((batch, c, hv), lambda i: (0, i, 0)),
                   full((batch, GLA_HEADS, GLA_DV, GLA_DK))],
        out_shape=[jax.ShapeDtypeStruct((batch, seq, hv), BF16),
                   jax.ShapeDtypeStruct((batch, GLA_HEADS, GLA_DV, GLA_DK), F32)],
        scratch_shapes=[pltpu.VMEM((batch, hv, hw), F32)],
        compiler_params=_cparams("arbitrary"),
    )(p3, p3, p3, p3, p3, w2pad, gla_b, onorm)


def _swa_mb_kernel(q_ref, kp_ref, kc_ref, vp_ref, vc_ref, qn_ref, kn_ref, snk_ref,
                   o_ref, knew_ref):
    i = pl.program_id(0)
    nb = q_ref.shape[0]
    w = WINDOW
    g = SWA_HEADS // SWA_KV
    nr = g * w
    first = _iota((w, 2 * SWA_HD), 1) < SWA_HD
    first2 = _iota((2 * w, 2 * SWA_HD), 1) < SWA_HD
    t = _iota((nr, 2 * w), 0) % w
    j = _iota((nr, 2 * w), 1)
    dist = w + t - j
    valid = (dist >= 0) & (dist <= w) & ((j >= w) | (i > 0))
    distf = dist.astype(F32)
    hrow = _iota((nr, 1), 0) // w
    slope, snk = [], []
    for kv in range(SWA_KV):
        sl = jnp.zeros((nr, 1), F32)
        sk = jnp.zeros((nr, 1), F32)
        for hh in range(g):
            h = kv * g + hh
            sl = jnp.where(hrow == hh, 2.0 ** (-8.0 * (h + 1) / SWA_HEADS), sl)
            sk = jnp.where(hrow == hh, snk_ref[:, h:h + 1], sk)
        slope.append(sl * distf)
        snk.append(sk)

    units = [(b, kv) for b in range(nb) for kv in range(SWA_KV)]
    kcs, k2, v2, qx = [], [], [], []
    for b in range(nb):
        kc = _half_rms(kc_ref[b], kn_ref[...])
        kcs.append(kc)
        kk = jnp.concatenate([_half_rms(kp_ref[b], kn_ref[...]), kc], 0)
        vv = jnp.concatenate([vp_ref[b], vc_ref[b]], 0)
        kk_sw = pltpu.roll(kk, SWA_HD, axis=1)
        vv_sw = pltpu.roll(vv, SWA_HD, axis=1)
        k2 += [jnp.where(first2, kk, kk_sw).astype(BF16), jnp.where(first2, kk_sw, kk).astype(BF16)]
        v2 += [jnp.where(first2, vv, vv_sw).astype(BF16), jnp.where(first2, vv_sw, vv).astype(BF16)]
        for kv in range(SWA_KV):
            rows = []
            for jj in range(g // 2):
                grp = kv * (g // 2) + jj
                qg = _half_rms(q_ref[b, :, grp * 2 * SWA_HD:(grp + 1) * 2 * SWA_HD], qn_ref[...]) * SWA_HD ** -0.5
                rows += [jnp.where(first, qg, 0.0), jnp.where(first, 0.0, qg)]
            qx.append(jnp.concatenate(rows, 0).astype(BF16))
    nu = range(len(units))
    s = [_dot_nt(qx[u], k2[u]) for u in nu]
    s = [jnp.where(valid, s[u] - slope[units[u][1]], -jnp.inf) for u in nu]
    m = [jnp.maximum(jnp.max(s[u], -1, keepdims=True), snk[units[u][1]]) for u in nu]
    pr = [jnp.exp(s[u] - m[u]) for u in nu]
    den = [jnp.sum(pr[u], -1, keepdims=True) + jnp.exp(snk[units[u][1]] - m[u]) for u in nu]
    pr = [(pr[u] * (1.0 / den[u])).astype(BF16) for u in nu]
    o = [_dot(pr[u], v2[u]) for u in nu]
    for u, (b, kv) in enumerate(units):
        for jj in range(g // 2):
            grp = kv * (g // 2) + jj
            o_ref[b, :, grp * 2 * SWA_HD:(grp + 1) * 2 * SWA_HD] = jnp.where(
                first, o[u][(2 * jj) * w:(2 * jj + 1) * w], o[u][(2 * jj + 1) * w:(2 * jj + 2) * w]).astype(BF16)

    @pl.when(i == pl.num_programs(0) - 1)
    def _():
        for b in range(nb):
            knew_ref[b] = kcs[b]


def _swa_prompt_mb(p3, qn2, kn2, sinks):
    batch, seq, _ = p3.shape
    w = WINDOW
    kvw = SWA_KV * SWA_HD
    full = lambda shape: pl.BlockSpec(shape, lambda i: tuple(0 for _ in shape))
    prev = lambda i: jnp.maximum(i - 1, 0)
    return pl.pallas_call(
        _swa_mb_kernel,
        grid=(seq // w,),
        in_specs=[pl.BlockSpec((batch, w, BRANCH_W), lambda i: (0, i, C_CQ // BRANCH_W)),
                  pl.BlockSpec((batch, w, kvw), lambda i: (0, prev(i), C_CK // kvw)),
                  pl.BlockSpec((batch, w, kvw), lambda i: (0, i, C_CK // kvw)),
                  pl.BlockSpec((batch, w, kvw), lambda i: (0, prev(i), C_CV // kvw)),
                  pl.BlockSpec((batch, w, kvw), lambda i: (0, i, C_CV // kvw)),
                  full((1, kvw)), full((1, kvw)), full((1, SWA_HEADS))],
        out_specs=[pl.BlockSpec((batch, w, BRANCH_W), lambda i: (0, i, 0)),
                   full((batch, w, kvw))],
        out_shape=[jax.ShapeDtypeStruct((batch, seq, BRANCH_W), BF16),
                   jax.ShapeDtypeStruct((batch, w, kvw), F32)],
        compiler_params=_cparams("arbitrary"),
    )(p3, p3, p3, p3, p3, qn2, kn2, sinks)


DEC_TILE = 8


def _dec_kernel(qkv_ref, z_ref, sm_ref, bq_ref, bk_ref, bv_ref, br_ref, cq_ref, ck_ref, cv_ref,
                sdn_ref, buf_ref, sgl_ref, kc_ref, vc_ref,
                cw_ref, alog_ref, dtb_ref, dnon_ref, w2_ref, glb_ref, glon_ref,
                qn_ref, kn_ref, snk_ref, slp_ref,
                oa_ref, ob_ref, oc_ref, sdn_out, buf_out, sgl_out, kc_out, vc_out):
    bt = DEC_TILE
    x = qkv_ref[...]
    buf = buf_ref[...]
    acc = x * cw_ref[CONV_W - 1:CONV_W, :]
    for j in range(CONV_W - 1):
        acc = acc + buf[:, j * DN_QKV:(j + 1) * DN_QKV] * cw_ref[j:j + 1, :]
    y = _silu(acc)
    buf_out[:, 0:(CONV_W - 2) * DN_QKV] = buf[:, DN_QKV:]
    buf_out[:, (CONV_W - 2) * DN_QKV:] = x

    sm = sm_ref[...]
    beta_all = jax.nn.sigmoid(sm)
    g_all = -jnp.exp(alog_ref[...]) * jax.nn.softplus(sm + dtb_ref[...])
    eye = (_iota((LANES, LANES), 0) == _iota((LANES, LANES), 1)).astype(F32)
    hk = DN_HEADS * DN_DK
    for h in range(DN_HEADS):
        q = y[:, h * DN_DK:(h + 1) * DN_DK]
        k = y[:, hk + h * DN_DK:hk + (h + 1) * DN_DK]
        v = y[:, 2 * hk + h * DN_DV:2 * hk + (h + 1) * DN_DV]
        q = q * lax.rsqrt(jnp.sum(q * q, -1, keepdims=True) + EPS) * DN_DK ** -0.5
        k = k * lax.rsqrt(jnp.sum(k * k, -1, keepdims=True) + EPS)
        beta = beta_all[:, SM_BETA + h:SM_BETA + h + 1]
        eg = jnp.exp(g_all[:, SM_G + h:SM_G + h + 1])
        kb = k * beta
        lhs = jnp.concatenate([kb * eg, q * eg], 0).astype(BF16)
        k_t = _dot_nt(eye, k)
        qk = jnp.sum(q * k, -1, keepdims=True)
        vb = v * beta
        o_rows = []
        for b in range(bt):
            s = sdn_ref[b, h]
            r = _dot(lhs, s.astype(BF16))
            u = vb[b:b + 1] - r[b:b + 1]
            o_rows.append(r[bt + b:bt + b + 1] + qk[b:b + 1] * u)
            sdn_out[b, h] = s * eg[b:b + 1] + k_t[:, b:b + 1] * u
        o = jnp.concatenate(o_rows, 0)
        o = o * lax.rsqrt(jnp.mean(o * o, -1, keepdims=True) + EPS) * dnon_ref[...]
        oa_ref[:, h * DN_DV:(h + 1) * DN_DV] = (o * _silu(z_ref[:, h * DN_DV:(h + 1) * DN_DV])).astype(BF16)

    lg = jax.nn.log_sigmoid(_dot(sm.astype(BF16), w2_ref[...]) + glb_ref[...]) / GLA_TAU
    elg = jnp.exp(lg)
    bq = bq_ref[...] * GLA_DK ** -0.5
    bk = bk_ref[...]
    qg = bq * elg
    lane = _iota((bt, LANES), 1)
    first = lane < GLA_DK
    rows_first = _iota((LANES, GLA_DV), 0) < GLA_DK
    for j in range(GLA_HEADS // 2):
        sl = slice(j * LANES, (j + 1) * LANES)
        qgj = qg[:, sl]
        lhs = jnp.concatenate([jnp.where(first, qgj, 0.0), jnp.where(first, 0.0, qgj)], 0).astype(BF16)
        cols = _dot_nt(eye, jnp.concatenate([elg[:, sl], bk[:, sl]], 0), HIGHEST)
        prod = bq[:, sl] * bk[:, sl]
        qk0 = jnp.sum(jnp.where(first, prod, 0.0), -1, keepdims=True)
        qk1 = jnp.sum(jnp.where(first, 0.0, prod), -1, keepdims=True)
        v0 = bv_ref[:, (2 * j) * GLA_DV:(2 * j + 1) * GLA_DV]
        v1 = bv_ref[:, (2 * j + 1) * GLA_DV:(2 * j + 2) * GLA_DV]
        o0, o1 = [], []
        for b in range(bt):
            s = sgl_ref[b, j]
            r = _dot(lhs, s.astype(BF16))
            o0.append(r[b:b + 1] + qk0[b:b + 1] * v0[b:b + 1])
            o1.append(r[bt + b:bt + b + 1] + qk1[b:b + 1] * v1[b:b + 1])
            vsel = jnp.where(rows_first, v0[b:b + 1], v1[b:b + 1])
            sgl_out[b, j] = s * cols[:, b:b + 1] + cols[:, bt + b:bt + b + 1] * vsel
        for hh, rows in ((2 * j, o0), (2 * j + 1, o1)):
            o = jnp.concatenate(rows, 0)
            o = o * lax.rsqrt(jnp.mean(o * o, -1, keepdims=True) + EPS) * glon_ref[...]
            ob_ref[:, hh * GLA_DV:(hh + 1) * GLA_DV] = (o * _silu(br_ref[:, hh * GLA_DV:(hh + 1) * GLA_DV])).astype(BF16)

    g = SWA_HEADS // SWA_KV
    nr = bt * SWA_HEADS
    cq = cq_ref[...]
    cq = cq * lax.rsqrt(jnp.mean(cq * cq, -1, keepdims=True) + EPS) * qn_ref[...] * SWA_HD ** -0.5
    head = _iota((nr, 2 * SWA_HD), 0) % SWA_HEADS
    in_half = (head // g) == (_iota((nr, 2 * SWA_HD), 1) // SWA_HD)
    qx = jnp.where(in_half, jnp.concatenate([cq, cq], -1), 0.0)
    knew = _half_rms(ck_ref[...], kn_ref[...])
    vnew = cv_ref[...]
    s_c, kn_rows, vn_rows = [], [], []
    for b in range(bt):
        s_c.append(_dot_nt(qx[b * SWA_HEADS:(b + 1) * SWA_HEADS].astype(BF16), kc_ref[b].astype(BF16)))
        kn_rows.append(jnp.broadcast_to(knew[b:b + 1], (SWA_HEADS, 2 * SWA_HD)))
        vn_rows.append(jnp.broadcast_to(vnew[b:b + 1], (SWA_HEADS, 2 * SWA_HD)))
    s_c = jnp.concatenate(s_c, 0)
    kn_x = jnp.concatenate(kn_rows, 0)
    vn_x = jnp.concatenate(vn_rows, 0)
    slopes = slp_ref[...]
    snk = snk_ref[...]
    dist = (WINDOW - _iota((nr, WINDOW), 1)).astype(F32)
    s_c = s_c - slopes * dist
    s_n = jnp.sum(qx * kn_x, -1, keepdims=True)
    m = jnp.maximum(jnp.maximum(jnp.max(s_c, -1, keepdims=True), s_n), snk)
    p_c = jnp.exp(s_c - m)
    p_n = jnp.exp(s_n - m)
    den = jnp.sum(p_c, -1, keepdims=True) + p_n + jnp.exp(snk - m)
    p_c = p_c / den
    p_n = p_n / den
    half_sel = (_iota((nr, SWA_HD), 0) % SWA_HEADS) < g
    for b in range(bt):
        rs = slice(b * SWA_HEADS, (b + 1) * SWA_HEADS)
        r = _dot(p_c[rs].astype(BF16), vc_ref[b].astype(BF16)) + p_n[rs] * vn_x[rs]
        oc_ref[rs, :] = jnp.where(half_sel[rs], r[:, :SWA_HD], r[:, SWA_HD:]).astype(BF16)
        kc_out[b, 0:WINDOW - 1, :] = kc_ref[b, 1:WINDOW, :]
        kc_out[b, WINDOW - 1:WINDOW, :] = knew[b:b + 1]
        vc_out[b, 0:WINDOW - 1, :] = vc_ref[b, 1:WINDOW, :]
        vc_out[b, WINDOW - 1:WINDOW, :] = vnew[b:b + 1]


def _dec_mixers(p, cq_r, sdn, buf, sgl2, kc, vc, conv_w, alog_row, dtb_row, dn_on, w2pad, gla_b, gla_on,
                qn, kn2, snk_col, slp_col):
    n = p.shape[0]
    bt = DEC_TILE
    hw = GLA_HEADS * GLA_DK
    kvw = SWA_KV * SWA_HD
    nr = bt * SWA_HEADS

    def col(width, off):
        return pl.BlockSpec((bt, width), lambda i: (i, off // width))

    def full(shape):
        return pl.BlockSpec(shape, lambda i: tuple(0 for _ in shape))

    def lead(shape):
        return pl.BlockSpec((bt,) + shape, lambda i: (i,) + tuple(0 for _ in shape))

    return pl.pallas_call(
        _dec_kernel,
        grid=(n // bt,),
        in_specs=[col(DN_QKV, C_AQKV), col(BRANCH_W, C_AZ), col(LANES, C_SM), col(hw, C_BQ), col(hw, C_BK),
                  col(BRANCH_W, C_BV), col(BRANCH_W, C_BR),
                  pl.BlockSpec((nr, SWA_HD), lambda i: (i, 0)),
                  col(kvw, C_CK), col(kvw, C_CV),
                  lead((DN_HEADS, DN_DK, DN_DV)),
                  pl.BlockSpec((bt, (CONV_W - 1) * DN_QKV), lambda i: (i, 0)),
                  lead((GLA_HEADS // 2, 2 * GLA_DK, GLA_DV)),
                  lead((WINDOW, kvw)), lead((WINDOW, kvw)),
                  full((CONV_W, DN_QKV)), full((1, LANES)), full((1, LANES)), full((1, DN_DV)),
                  full((LANES, hw)), full((1, hw)), full((1, GLA_DV)),
                  full((1, SWA_HD)), full((1, kvw)), full((nr, 1)), full((nr, 1))],
        out_specs=[pl.BlockSpec((bt, BRANCH_W), lambda i: (i, 0)),
                   pl.BlockSpec((bt, BRANCH_W), lambda i: (i, 0)),
                   pl.BlockSpec((nr, SWA_HD), lambda i: (i, 0)),
                   lead((DN_HEADS, DN_DK, DN_DV)),
                   pl.BlockSpec((bt, (CONV_W - 1) * DN_QKV), lambda i: (i, 0)),
                   lead((GLA_HEADS // 2, 2 * GLA_DK, GLA_DV)),
                   lead((WINDOW, kvw)), lead((WINDOW, kvw))],
        out_shape=[jax.ShapeDtypeStruct((n, BRANCH_W), BF16),
                   jax.ShapeDtypeStruct((n, BRANCH_W), BF16),
                   jax.ShapeDtypeStruct((n * SWA_HEADS, SWA_HD), BF16),
                   jax.ShapeDtypeStruct(sdn.shape, F32),
                   jax.ShapeDtypeStruct(buf.shape, F32),
                   jax.ShapeDtypeStruct(sgl2.shape, F32),
                   jax.ShapeDtypeStruct(kc.shape, F32),
                   jax.ShapeDtypeStruct(vc.shape, F32)],
        compiler_params=_cparams("parallel"),
    )(p, p, p, p, p, p, p, cq_r, p, p, sdn, buf, sgl2, kc, vc,
      conv_w, alog_row, dtb_row, dn_on, w2pad, gla_b, gla_on, qn, kn2, snk_col, slp_col)


def _merge_kernel(ba_ref, bb_ref, bc_ref, gate_ref, x_ref, gt_ref, sc_ref, sh_ref, ln_ref,
                  wb_ref, wo_ref, wr_ref, rb_ref, x1_ref, h2_ref, comb_ref):
    mix = None
    for n, br in enumerate((ba_ref, bb_ref, bc_ref)):
        up = _dot(br[...], wb_ref[n])
        term = jax.nn.sigmoid(gate_ref[:, n * D_MODEL:(n + 1) * D_MODEL]) * up
        mix = term if mix is None else mix + term
    x1 = x_ref[...] + gt_ref[0] * _dot(mix.astype(BF16), wo_ref[...])
    x1_ref[...] = x1
    h2 = x1 * lax.rsqrt(jnp.mean(x1 * x1, -1, keepdims=True) + EPS) * ln_ref[...]
    h2 = h2 * (1.0 + sc_ref[0]) + sh_ref[0]
    h2_ref[...] = h2.astype(BF16)

    logits = _dot(h2, wr_ref[...], HIGHEST) + rb_ref[...]
    lane = _iota(logits.shape, 1).astype(F32)
    big = float(LANES)
    lc = jnp.where(lane < N_GROUPS, logits, -jnp.inf)
    mc = jnp.max(lc, -1, keepdims=True)
    pg = 1.0 / jnp.sum(jnp.exp(lc - mc), -1, keepdims=True)
    grp = jnp.min(jnp.where(lc == mc, lane, big), -1, keepdims=True)
    lo = R_EXP + grp * EXP_PER_GROUP
    emask = (lane >= lo) & (lane < lo + EXP_PER_GROUP)
    le = jnp.where(emask, logits, -jnp.inf)
    pe = jnp.exp(le - jnp.max(le, -1, keepdims=True))
    pe = pe / jnp.sum(pe, -1, keepdims=True)
    v1 = jnp.max(pe, -1, keepdims=True)
    i1 = jnp.min(jnp.where(emask & (pe == v1), lane, big), -1, keepdims=True)
    pe2 = jnp.where(emask & (lane != i1), pe, -1.0)
    v2 = jnp.max(pe2, -1, keepdims=True)
    i2 = jnp.min(jnp.where(pe2 == v2, lane, big), -1, keepdims=True)
    tot = v1 + v2
    comb_ref[...] = jnp.where(lane == i1, pg * v1 / tot, 0.0) + jnp.where(lane == i2, pg * v2 / tot, 0.0)


def _merge(ba, bb, bc, p, x, gt, sc, sh, ln, wb, wo, wr, rb, tm, tokens_per_row):
    t, d = x.shape
    tok = lambda width: pl.BlockSpec((tm, width), lambda i: (i, 0))
    full = lambda shape: pl.BlockSpec(shape, lambda i: tuple(0 for _ in shape))
    return pl.pallas_call(
        _merge_kernel,
        grid=(t // tm,),
        in_specs=[tok(BRANCH_W), tok(BRANCH_W), tok(BRANCH_W),
                  pl.BlockSpec((tm, N_BRANCH * d), lambda i: (i, C_GATE)),
                  tok(d),
                  _mod_spec(gt, tm, tokens_per_row), _mod_spec(sc, tm, tokens_per_row),
                  _mod_spec(sh, tm, tokens_per_row),
                  full((1, d)), full((N_BRANCH, BRANCH_W, d)), full((d, d)), full((d, LANES)), full((1, LANES))],
        out_specs=[tok(d), tok(d), tok(LANES)],
        out_shape=[jax.ShapeDtypeStruct((t, d), F32),
                   jax.ShapeDtypeStruct((t, d), BF16),
                   jax.ShapeDtypeStruct((t, LANES), F32)],
        compiler_params=_cparams("parallel"),
    )(ba, bb, bc, p, x, gt, sc, sh, ln, wb, wo, wr, rb)


def _moe_kernel(h_ref, comb_ref, x1_ref, gt_ref, w1_ref, w3_ref, w2_ref, o_ref, acc_ref):
    e = pl.program_id(1)

    @pl.when(e == 0)
    def _():
        acc_ref[...] = jnp.zeros_like(acc_ref)

    h = h_ref[...]
    he = _silu(_dot(h, w1_ref[0])) * _dot(h, w3_ref[0])
    ye = _dot(he.astype(BF16), w2_ref[0])
    comb = comb_ref[...]
    lane = _iota(comb.shape, 1)
    ce = jnp.sum(jnp.where(lane == e + R_EXP, comb, 0.0), -1, keepdims=True)
    acc_ref[...] += ce * ye

    @pl.when(e == pl.num_programs(1) - 1)
    def _():
        o_ref[...] = x1_ref[...] + gt_ref[0] * acc_ref[...]


def _moe(h2, comb, x1, gt, w1, w3, w2, tm, tokens_per_row):
    t, d = x1.shape
    ne, _, de = w1.shape
    return pl.pallas_call(
        _moe_kernel,
        grid=(t // tm, ne),
        in_specs=[pl.BlockSpec((tm, d), lambda i, e: (i, 0)),
                  pl.BlockSpec((tm, LANES), lambda i, e: (i, 0)),
                  pl.BlockSpec((tm, d), lambda i, e: (i, 0)),
                  _mod_spec(gt, tm, tokens_per_row),
                  pl.BlockSpec((1, d, de), lambda i, e: (e, 0, 0)),
                  pl.BlockSpec((1, d, de), lambda i, e: (e, 0, 0)),
                  pl.BlockSpec((1, de, d), lambda i, e: (e, 0, 0))],
        out_specs=pl.BlockSpec((tm, d), lambda i, e: (i, 0)),
        out_shape=jax.ShapeDtypeStruct((t, d), F32),
        scratch_shapes=[pltpu.VMEM((tm, d), F32)],
        compiler_params=_cparams("parallel", "arbitrary"),
    )(h2, comb, x1, gt, w1, w3, w2)


def _permute_w_in(w):
    sizes = (DN_QKV, DN_HEADS * DN_DV, DN_HEADS, DN_HEADS,
             GLA_HEADS * GLA_DK, GLA_HEADS * GLA_DK, GLA_HEADS * GLA_DV, GLA_HEADS * GLA_DV, GLA_RANK,
             SWA_HEADS * SWA_HD, SWA_KV * SWA_HD, SWA_KV * SWA_HD, N_BRANCH * D_MODEL)
    offs = [0]
    for s in sizes:
        offs.append(offs[-1] + s)
    seg = lambda i: w[..., offs[i]:offs[i + 1]]
    (a_qkv, a_z, a_b, a_a, b_q, b_k, b_v, b_r, b_lr, c_q, c_k, c_v, gate) = (seg(i) for i in range(len(sizes)))
    small = jnp.concatenate([a_b, a_a, b_lr], -1)
    fill = jnp.zeros(w.shape[:-1] + (P_PAD - C_SM - small.shape[-1],), w.dtype)
    return jnp.concatenate([gate, a_qkv, a_z, b_v, b_r, c_q, b_q, b_k, c_k, c_v, small, fill], -1)


def _lane_row(v, off):
    depth, n = v.shape
    return jnp.zeros((depth, 1, LANES), F32).at[:, 0, off:off + n].set(v.astype(F32))


def kernel(x_prompt, x_sample, c_prompt, c_sample, state_dn, state_dn_conv, state_gla, cache_swa_k, cache_swa_v, ln1_g, ln2_g, ada_w, ada_b, w_in, dn_conv_w, dn_a_log, dn_dt_bias, dn_onorm_g, gla_w2, gla_b, gla_onorm_g, swa_qnorm_g, swa_knorm_g, swa_sinks, w_branch, w_out, rc_w, rc_b, re_w, re_b, w1, w3, w2):
    batch, seq, d = x_prompt.shape
    nb = x_sample.shape[0]
    depth = w_in.shape[0]
    assert x_sample.shape[1] == 1 and d == D_MODEL and seq % CHUNK == 0 and nb % DEC_TILE == 0
    kvw = SWA_KV * SWA_HD

    w_in_p = _permute_w_in(w_in).astype(BF16)
    ada_w_b = ada_w.astype(BF16)
    wb_b, wo_b = w_branch.astype(BF16), w_out.astype(BF16)
    w1_b, w3_b, w2_b = w1.astype(BF16), w3.astype(BF16), w2.astype(BF16)
    wr = jnp.concatenate([rc_w, re_w, jnp.zeros((depth, d, LANES - N_GROUPS - N_EXPERTS), F32)], -1)
    rb = jnp.concatenate([rc_b, re_b, jnp.zeros((depth, LANES - N_GROUPS - N_EXPERTS), F32)], -1)[:, None, :]
    alog_row = _lane_row(dn_a_log, SM_G)
    dtb_row = _lane_row(dn_dt_bias, SM_G)
    w2pad = jnp.zeros((depth, LANES, GLA_HEADS * GLA_DK), F32).at[:, SM_LR:SM_LR + GLA_RANK].set(gla_w2).astype(BF16)
    kn2 = jnp.concatenate([swa_knorm_g] * SWA_KV, -1)[:, None, :]
    slopes = jnp.exp2(-8.0 * jnp.arange(1, SWA_HEADS + 1, dtype=F32) / SWA_HEADS)
    slp_col = jnp.tile(slopes, DEC_TILE)[:, None]

    pad_rows = (-(batch + nb)) % 8
    c_all = jnp.concatenate([c_prompt, c_sample, jnp.zeros((pad_rows, d), F32)], 0)
    mod = _ada(c_all, ada_w_b, ada_b)

    xp = x_prompt.reshape(batch * seq, d)
    xs = x_sample.reshape(nb, d)
    tm_p = 1024 if seq % 1024 == 0 else CHUNK
    tm_m = 256 if seq % 256 == 0 else CHUNK

    st_p, st_s = [], []
    for l in range(depth):
        mp = [m[:, None, :] for m in jnp.split(mod[l, :batch], 6, -1)]
        ms = [m[None] for m in jnp.split(mod[l, batch:batch + nb], 6, -1)]
        ln1, ln2 = ln1_g[l][None], ln2_g[l][None]
        conv_w = dn_conv_w[l]
        dn_on, gla_on = dn_onorm_g[l][None], gla_onorm_g[l][None]
        glb = gla_b[l][None]
        qn = swa_qnorm_g[l][None]
        snk = swa_sinks[l][None]

        pp = _in_proj(xp, mp[1], mp[0], ln1, w_in_p[l], tm_p, seq)
        pp3 = pp.reshape(batch, seq, P_PAD)
        ba, dn_s = _dn_prompt_mb(pp3, conv_w, alog_row[l], dtb_row[l], dn_on)
        bb, gla_st = _gla_prompt_mb(pp3, w2pad[l], glb, gla_on)
        bc, k_new = _swa_prompt_mb(pp3, jnp.concatenate([qn, qn], -1), kn2[l], snk)
        ba, bb, bc = (z.reshape(batch * seq, BRANCH_W) for z in (ba, bb, bc))
        x1, h2, comb = _merge(ba, bb, bc, pp, xp, mp[2], mp[4], mp[3], ln2, wb_b[l], wo_b[l], wr[l], rb[l],
                              tm_m, seq)
        xp = _moe(h2, comb, x1, mp[5], w1_b[l], w3_b[l], w2_b[l], tm_p, seq)
        pp3 = pp.reshape(batch, seq, P_PAD)
        st_p.append((dn_s,
                     pp3[:, seq - (CONV_W - 1):, C_AQKV:C_AQKV + DN_QKV],
                     jnp.swapaxes(gla_st, -1, -2),
                     k_new.reshape(batch, WINDOW, SWA_KV, SWA_HD),
                     pp3[:, seq - WINDOW:, C_CV:C_CV + kvw].reshape(batch, WINDOW, SWA_KV, SWA_HD)))

        ps = _in_proj(xs, ms[1], ms[0], ln1, w_in_p[l], nb, nb)
        cq_r = ps[:, C_CQ:C_CQ + SWA_HEADS * SWA_HD].reshape(nb * SWA_HEADS, SWA_HD)
        oa, ob, oc_r, sdn_n, buf_n, sgl_n, kc_n, vc_n = _dec_mixers(
            ps, cq_r, state_dn[l], state_dn_conv[l].reshape(nb, (CONV_W - 1) * DN_QKV),
            state_gla[l].reshape(nb, GLA_HEADS // 2, 2 * GLA_DK, GLA_DV),
            cache_swa_k[l].reshape(nb, WINDOW, kvw), cache_swa_v[l].reshape(nb, WINDOW, kvw),
            conv_w, alog_row[l], dtb_row[l], dn_on, w2pad[l], glb, gla_on, qn, kn2[l],
            jnp.tile(swa_sinks[l], DEC_TILE)[:, None], slp_col)
        oc = oc_r.reshape(nb, SWA_HEADS * SWA_HD)
        x1, h2, comb = _merge(oa, ob, oc, ps, xs, ms[2], ms[4], ms[3], ln2, wb_b[l], wo_b[l], wr[l], rb[l],
                              nb, nb)
        xs = _moe(h2, comb, x1, ms[5], w1_b[l], w3_b[l], w2_b[l], nb, nb)
        st_s.append((sdn_n,
                     buf_n.reshape(nb, CONV_W - 1, DN_QKV),
                     sgl_n.reshape(nb, GLA_HEADS, GLA_DK, GLA_DV),
                     kc_n.reshape(nb, WINDOW, SWA_KV, SWA_HD),
                     vc_n.reshape(nb, WINDOW, SWA_KV, SWA_HD)))

    dn_p, conv_p, gla_p, k_p, v_p = [jnp.stack(z) for z in zip(*st_p)]
    dn_s, conv_s, gla_s, k_s, v_s = [jnp.stack(z) for z in zip(*st_s)]
    return (xp.reshape(batch, seq, d), xs.reshape(nb, 1, d), dn_p, dn_s, conv_p, conv_s, gla_p, gla_s,
            k_p, k_s, v_p, v_s)
```

```python
import functools

import jax
import jax.numpy as jnp
from jax import lax
from jax.experimental import pallas as pl
from jax.experimental.pallas import tpu as pltpu

F32 = jnp.float32
BF16 = jnp.bfloat16
HIGHEST = lax.Precision.HIGHEST

D_MODEL = 1024
DN_HEADS, DN_DK, DN_DV, CONV_W = 4, 128, 128, 4
DN_QKV = 2 * DN_HEADS * DN_DK + DN_HEADS * DN_DV
GLA_HEADS, GLA_DK, GLA_DV, GLA_RANK, GLA_TAU = 4, 64, 128, 16, 16.0
SWA_HEADS, SWA_KV, SWA_HD, WINDOW = 8, 2, 64, 128
N_BRANCH, BRANCH_W = 3, 512
N_GROUPS, EXP_PER_GROUP, TOP_K, D_EXPERT = 4, 8, 2, 256
N_EXPERTS = N_GROUPS * EXP_PER_GROUP
EPS = 1e-6

LANES = 128
CHUNK = 128
SUB = 16
VMEM_LIMIT = 56 * 1024 * 1024

C_GATE, C_AQKV, C_AZ, C_BV, C_BR, C_CQ = 0, 3072, 4608, 5120, 5632, 6144
C_BQ, C_BK, C_CK, C_CV, C_SM, P_PAD = 6656, 6912, 7168, 7296, 7424, 7680
SM_BETA, SM_G, SM_LR = 0, DN_HEADS, 2 * DN_HEADS
R_EXP = N_GROUPS


def _cparams(*sem):
    return pltpu.CompilerParams(dimension_semantics=sem, vmem_limit_bytes=VMEM_LIMIT)


def _dot(a, b, precision=None):
    return jnp.dot(a, b, preferred_element_type=F32, precision=precision)


def _dot_nt(a, b, precision=None):
    return lax.dot_general(a, b, (((1,), (1,)), ((), ())), preferred_element_type=F32, precision=precision)


def _dot_tn(a, b, precision=None):
    return lax.dot_general(a, b, (((0,), (0,)), ((), ())), preferred_element_type=F32, precision=precision)


def _silu(x):
    return x * jax.nn.sigmoid(x)


def _iota(shape, dim):
    return lax.broadcasted_iota(jnp.int32, shape, dim)


def _ada_kernel(c_ref, w_ref, b_ref, o_ref):
    c = _silu(c_ref[...]).astype(BF16)
    o_ref[0] = _dot(c, w_ref[0]) + b_ref[0]


def _ada(c_all, ada_w, ada_b):
    depth, d, n = ada_w.shape
    rows = c_all.shape[0]
    tn = 1536
    return pl.pallas_call(
        _ada_kernel,
        grid=(depth, n // tn),
        in_specs=[pl.BlockSpec((rows, d), lambda l, j: (0, 0)),
                  pl.BlockSpec((1, d, tn), lambda l, j: (l, 0, j)),
                  pl.BlockSpec((1, 1, tn), lambda l, j: (l, 0, j))],
        out_specs=pl.BlockSpec((1, rows, tn), lambda l, j: (l, 0, j)),
        out_shape=jax.ShapeDtypeStruct((depth, rows, n), F32),
        compiler_params=_cparams("parallel", "parallel"),
    )(c_all, ada_w, ada_b.reshape(depth, 1, n))


def _mod_spec(mod, tm, tokens_per_row):
    _, r, d = mod.shape
    assert tokens_per_row % tm == 0
    per = tokens_per_row // tm
    return pl.BlockSpec((1, r, d), lambda i, *_: (i // per, 0, 0))


def _in_proj_kernel(x_ref, sc_ref, sh_ref, g_ref, w_ref, o_ref, h_ref):
    @pl.when(pl.program_id(1) == 0)
    def _():
        x = x_ref[...]
        y = x * lax.rsqrt(jnp.mean(x * x, -1, keepdims=True) + EPS) * g_ref[...]
        h_ref[...] = (y * (1.0 + sc_ref[0]) + sh_ref[0]).astype(BF16)

    o_ref[...] = _dot(h_ref[...], w_ref[...])


def _in_proj(x, sc, sh, g, w, tm, tokens_per_row):
    t, d = x.shape
    n = w.shape[1]
    tn = 1536
    return pl.pallas_call(
        _in_proj_kernel,
        grid=(t // tm, n // tn),
        in_specs=[pl.BlockSpec((tm, d), lambda i, j: (i, 0)),
                  _mod_spec(sc, tm, tokens_per_row), _mod_spec(sh, tm, tokens_per_row),
                  pl.BlockSpec((1, d), lambda i, j: (0, 0)),
                  pl.BlockSpec((d, tn), lambda i, j: (0, j))],
        out_specs=pl.BlockSpec((tm, tn), lambda i, j: (i, j)),
        out_shape=jax.ShapeDtypeStruct((t, n), F32),
        scratch_shapes=[pltpu.VMEM((tm, d), BF16)],
        compiler_params=_cparams("parallel", "arbitrary"),
    )(x, sc, sh, g, w)


def _strict_lower_inverse_minus_eye(a):
    c = a.shape[0]
    row, col = _iota((c, c), 0), _iota((c, c), 1)
    diag_blk = (row // SUB) == (col // SUB)
    ad = jnp.where(diag_blk, a, 0.0)
    ao = a - ad
    n = -ad
    p = n
    steps = SUB.bit_length() - 2
    for _ in range(steps):
        p = _dot(p, p)
        n = n + p + _dot(n, p)
    bm = -(ao + _dot(n, ao))
    m = bm
    q = bm
    steps = (c // SUB).bit_length() - 2
    for _ in range(steps):
        q = _dot(q, q)
        m = m + q + _dot(m, q)
    return m + n + _dot(m, n)


def _strict_lower_inverse_minus_eye_multi(a_list):
    c = a_list[0].shape[0]
    diag_blk = (_iota((c, c), 0) // SUB) == (_iota((c, c), 1) // SUB)
    idx = range(len(a_list))
    ad = [jnp.where(diag_blk, a, 0.0) for a in a_list]
    ao = [a_list[i] - ad[i] for i in idx]
    n = [-x for x in ad]
    p = n
    for _ in range(SUB.bit_length() - 2):
        p = [_dot(x, x) for x in p]
        np_ = [_dot(n[i], p[i]) for i in idx]
        n = [n[i] + p[i] + np_[i] for i in idx]
    nao = [_dot(n[i], ao[i]) for i in idx]
    m = [-(ao[i] + nao[i]) for i in idx]
    q = m
    for _ in range((c // SUB).bit_length() - 2):
        q = [_dot(x, x) for x in q]
        mq = [_dot(m[i], q[i]) for i in idx]
        m = [m[i] + q[i] + mq[i] for i in idx]
    mn = [_dot(m[i], n[i]) for i in idx]
    return [m[i] + n[i] + mn[i] for i in idx]


def _dn_kernel(qkv_ref, z_ref, sm_ref, cw_ref, alog_ref, dtb_ref, on_ref,
               o_ref, s_out_ref, s_ref, xp_ref):
    c_idx = pl.program_id(1)
    c = CHUNK
    pad = 8

    @pl.when(c_idx == 0)
    def _():
        s_ref[...] = jnp.zeros_like(s_ref)
        xp_ref[0:pad, :] = jnp.zeros((pad, DN_QKV), F32)

    xp_ref[pad:pad + c, :] = qkv_ref[...]
    acc = xp_ref[pad - 3:pad - 3 + c, :] * cw_ref[0:1, :]
    for j in range(1, CONV_W):
        acc = acc + xp_ref[pad - 3 + j:pad - 3 + j + c, :] * cw_ref[j:j + 1, :]
    y = _silu(acc)
    xp_ref[pad - 3:pad, :] = xp_ref[pad + c - 3:pad + c, :]

    sm = sm_ref[...]
    beta_all = jax.nn.sigmoid(sm)
    g_all = -jnp.exp(alog_ref[...]) * jax.nn.softplus(sm + dtb_ref[...])
    row, col = _iota((c, c), 0), _iota((c, c), 1)
    incl = row >= col
    strict = row > col
    gam_all = _dot(incl.astype(F32), g_all, HIGHEST)
    gam_t = gam_all.T
    hk = DN_HEADS * DN_DK
    for h in range(DN_HEADS):
        q = y[:, h * DN_DK:(h + 1) * DN_DK]
        k = y[:, hk + h * DN_DK:hk + (h + 1) * DN_DK]
        v = y[:, 2 * hk + h * DN_DV:2 * hk + (h + 1) * DN_DV]
        q = q * lax.rsqrt(jnp.sum(q * q, -1, keepdims=True) + EPS) * DN_DK ** -0.5
        k = k * lax.rsqrt(jnp.sum(k * k, -1, keepdims=True) + EPS)
        beta = beta_all[:, SM_BETA + h:SM_BETA + h + 1]
        gam = gam_all[:, SM_G + h:SM_G + h + 1]
        gam_row = gam_t[SM_G + h:SM_G + h + 1, :]
        dec = jnp.where(incl, jnp.exp(jnp.where(incl, gam - gam_row, 0.0)), 0.0)
        eg = jnp.exp(gam)
        gl = gam[c - 1:c, :]
        kb = k * beta
        kbf = k.astype(BF16)
        a = jnp.where(strict, _dot_nt(kb.astype(BF16), kbf) * dec, 0.0)
        w = _strict_lower_inverse_minus_eye(a)
        rhs = jnp.concatenate([v * beta, kb * eg], -1)
        sol = rhs + _dot(w, rhs)
        s = s_ref[h]
        sbf = s.astype(BF16)
        u = sol[:, :DN_DV] - _dot(sol[:, DN_DV:].astype(BF16), sbf)
        ubf = u.astype(BF16)
        qk = _dot_nt(q.astype(BF16), kbf) * dec
        o = _dot((q * eg).astype(BF16), sbf) + _dot(qk.astype(BF16), ubf)
        kd = k * jnp.exp(gl - gam)
        s_ref[h] = s * jnp.exp(gl) + _dot_tn(kd.astype(BF16), ubf)
        o = o * lax.rsqrt(jnp.mean(o * o, -1, keepdims=True) + EPS) * on_ref[...]
        o_ref[:, h * DN_DV:(h + 1) * DN_DV] = (o * _silu(z_ref[:, h * DN_DV:(h + 1) * DN_DV])).astype(BF16)

    @pl.when(c_idx == pl.num_programs(1) - 1)
    def _():
        s_out_ref[0] = s_ref[...]


def _dn_prompt(p, batch, seq, conv_w, alog_row, dtb_row, onorm):
    c = CHUNK
    nc = seq // c
    return pl.pallas_call(
        _dn_kernel,
        grid=(batch, nc),
        in_specs=[pl.BlockSpec((c, DN_QKV), lambda b, i: (b * nc + i, C_AQKV // DN_QKV)),
                  pl.BlockSpec((c, BRANCH_W), lambda b, i: (b * nc + i, C_AZ // BRANCH_W)),
                  pl.BlockSpec((c, LANES), lambda b, i: (b * nc + i, C_SM // LANES)),
                  pl.BlockSpec((CONV_W, DN_QKV), lambda b, i: (0, 0)),
                  pl.BlockSpec((1, LANES), lambda b, i: (0, 0)),
                  pl.BlockSpec((1, LANES), lambda b, i: (0, 0)),
                  pl.BlockSpec((1, DN_DV), lambda b, i: (0, 0))],
        out_specs=[pl.BlockSpec((c, BRANCH_W), lambda b, i: (b * nc + i, 0)),
                   pl.BlockSpec((1, DN_HEADS, DN_DK, DN_DV), lambda b, i: (b, 0, 0, 0))],
        out_shape=[jax.ShapeDtypeStruct((batch * seq, BRANCH_W), BF16),
                   jax.ShapeDtypeStruct((batch, DN_HEADS, DN_DK, DN_DV), F32)],
        scratch_shapes=[pltpu.VMEM((DN_HEADS, DN_DK, DN_DV), F32),
                        pltpu.VMEM((c + 8, DN_QKV), F32)],
        compiler_params=_cparams("parallel", "arbitrary"),
    )(p, p, p, conv_w, alog_row, dtb_row, onorm)


EXP_CAP = 80.0


def _gla_kernel(q_ref, k_ref, v_ref, r_ref, sm_ref, w2_ref, b_ref, on_ref,
                o_ref, s_out_ref, st_ref):
    c_idx = pl.program_id(1)
    c = CHUNK
    hw = GLA_HEADS * GLA_DK

    @pl.when(c_idx == 0)
    def _():
        st_ref[...] = jnp.zeros_like(st_ref)

    lg = jax.nn.log_sigmoid(_dot(sm_ref[...].astype(BF16), w2_ref[...]) + b_ref[...]) / GLA_TAU
    row, col = _iota((c, c), 0), _iota((c, c), 1)
    gam = _dot((row >= col).astype(F32), lg, HIGHEST)
    gl = gam[c - 1:c, :]
    qs = q_ref[...] * GLA_DK ** -0.5
    k = k_ref[...]
    qg = (qs * jnp.exp(gam)).astype(BF16)
    kd = (k * jnp.exp(gl - gam)).astype(BF16)
    egl = jnp.exp(gl)

    att_rows = [[] for _ in range(GLA_HEADS)]
    for i in range(c // SUB):
        lo, hi = i * SUB, (i + 1) * SUB
        ref_pt = gam[lo - 1:lo, :] if i > 0 else jnp.zeros((1, hw), F32)
        qi = (qs[lo:hi] * jnp.exp(gam[lo:hi] - ref_pt)).astype(BF16)
        ki = (k[:hi] * jnp.exp(jnp.minimum(ref_pt - gam[:hi], EXP_CAP))).astype(BF16)
        keep = (_iota((SUB, hi), 0) + lo) >= _iota((SUB, hi), 1)
        for h in range(GLA_HEADS):
            sl = slice(h * GLA_DK, (h + 1) * GLA_DK)
            att = jnp.where(keep, _dot_nt(qi[:, sl], ki[:, sl]), 0.0)
            att_rows[h].append(_dot(att.astype(BF16), v_ref[:hi, h * GLA_DV:(h + 1) * GLA_DV].astype(BF16)))

    for h in range(GLA_HEADS):
        sl = slice(h * GLA_DK, (h + 1) * GLA_DK)
        vh = v_ref[:, h * GLA_DV:(h + 1) * GLA_DV].astype(BF16)
        st = st_ref[h]
        o = _dot_nt(qg[:, sl], st.astype(BF16)) + jnp.concatenate(att_rows[h], 0)
        st_ref[h] = st * egl[:, sl] + _dot_tn(vh, kd[:, sl])
        o = o * lax.rsqrt(jnp.mean(o * o, -1, keepdims=True) + EPS) * on_ref[...]
        o_ref[:, h * GLA_DV:(h + 1) * GLA_DV] = (o * _silu(r_ref[:, h * GLA_DV:(h + 1) * GLA_DV])).astype(BF16)

    @pl.when(c_idx == pl.num_programs(1) - 1)
    def _():
        s_out_ref[0] = st_ref[...]


def _gla_prompt(p, batch, seq, w2pad, gla_b, onorm):
    c = CHUNK
    nc = seq // c
    hw = GLA_HEADS * GLA_DK
    return pl.pallas_call(
        _gla_kernel,
        grid=(batch, nc),
        in_specs=[pl.BlockSpec((c, hw), lambda b, i: (b * nc + i, C_BQ // hw)),
                  pl.BlockSpec((c, hw), lambda b, i: (b * nc + i, C_BK // hw)),
                  pl.BlockSpec((c, BRANCH_W), lambda b, i: (b * nc + i, C_BV // BRANCH_W)),
                  pl.BlockSpec((c, BRANCH_W), lambda b, i: (b * nc + i, C_BR // BRANCH_W)),
                  pl.BlockSpec((c, LANES), lambda b, i: (b * nc + i, C_SM // LANES)),
                  pl.BlockSpec((LANES, hw), lambda b, i: (0, 0)),
                  pl.BlockSpec((1, hw), lambda b, i: (0, 0)),
                  pl.BlockSpec((1, GLA_DV), lambda b, i: (0, 0))],
        out_specs=[pl.BlockSpec((c, BRANCH_W), lambda b, i: (b * nc + i, 0)),
                   pl.BlockSpec((1, GLA_HEADS, GLA_DV, GLA_DK), lambda b, i: (b, 0, 0, 0))],
        out_shape=[jax.ShapeDtypeStruct((batch * seq, BRANCH_W), BF16),
                   jax.ShapeDtypeStruct((batch, GLA_HEADS, GLA_DV, GLA_DK), F32)],
        scratch_shapes=[pltpu.VMEM((GLA_HEADS, GLA_DV, GLA_DK), F32)],
        compiler_params=_cparams("parallel", "arbitrary"),
    )(p, p, p, p, p, w2pad, gla_b, onorm)


def _half_rms(x, g2):
    lane = _iota(x.shape, 1)
    first = lane < SWA_HD
    sq = x * x
    s0 = jnp.sum(jnp.where(first, sq, 0.0), -1, keepdims=True)
    s1 = jnp.sum(jnp.where(first, 0.0, sq), -1, keepdims=True)
    ms = jnp.where(first, s0, s1) * (1.0 / SWA_HD)
    return x * lax.rsqrt(ms + EPS) * g2


def _swa_kernel(q_ref, kp_ref, kc_ref, vp_ref, vc_ref, qn_ref, kn_ref, snk_ref,
                o_ref, knew_ref):
    i = pl.program_id(1)
    w = WINDOW
    kc = _half_rms(kc_ref[...], kn_ref[...])
    kk = jnp.concatenate([_half_rms(kp_ref[...], kn_ref[...]), kc], 0).astype(BF16)
    vv = jnp.concatenate([vp_ref[...], vc_ref[...]], 0).astype(BF16)
    t = _iota((w, 2 * w), 0)
    j = _iota((w, 2 * w), 1)
    dist = w + t - j
    valid = (dist >= 0) & (dist <= w) & ((j >= w) | (i > 0))
    distf = dist.astype(F32)
    g = SWA_HEADS // SWA_KV
    for h in range(SWA_HEADS):
        kv = h // g
        q = q_ref[:, h * SWA_HD:(h + 1) * SWA_HD]
        q = q * lax.rsqrt(jnp.mean(q * q, -1, keepdims=True) + EPS) * qn_ref[...] * SWA_HD ** -0.5
        s = _dot_nt(q.astype(BF16), kk[:, kv * SWA_HD:(kv + 1) * SWA_HD])
        s = s - (2.0 ** (-8.0 * (h + 1) / SWA_HEADS)) * distf
        s = jnp.where(valid, s, -jnp.inf)
        snk = snk_ref[:, h:h + 1]
        m = jnp.maximum(jnp.max(s, -1, keepdims=True), snk)
        pr = jnp.exp(s - m)
        pr = pr / (jnp.sum(pr, -1, keepdims=True) + jnp.exp(snk - m))
        o = _dot(pr.astype(BF16), vv[:, kv * SWA_HD:(kv + 1) * SWA_HD])
        o_ref[:, h * SWA_HD:(h + 1) * SWA_HD] = o.astype(BF16)

    @pl.when(i == pl.num_programs(1) - 1)
    def _():
        knew_ref[0] = kc


def _swa_prompt(p, batch, seq, qn, kn2, sinks):
    w = WINDOW
    nw = seq // w
    kvw = SWA_KV * SWA_HD
    return pl.pallas_call(
        _swa_kernel,
        grid=(batch, nw),
        in_specs=[pl.BlockSpec((w, BRANCH_W), lambda b, i: (b * nw + i, C_CQ // BRANCH_W)),
                  pl.BlockSpec((w, kvw), lambda b, i: (b * nw + jnp.maximum(i - 1, 0), C_CK // kvw)),
                  pl.BlockSpec((w, kvw), lambda b, i: (b * nw + i, C_CK // kvw)),
                  pl.BlockSpec((w, kvw), lambda b, i: (b * nw + jnp.maximum(i - 1, 0), C_CV // kvw)),
                  pl.BlockSpec((w, kvw), lambda b, i: (b * nw + i, C_CV // kvw)),
                  pl.BlockSpec((1, SWA_HD), lambda b, i: (0, 0)),
                  pl.BlockSpec((1, kvw), lambda b, i: (0, 0)),
                  pl.BlockSpec((1, SWA_HEADS), lambda b, i: (0, 0))],
        out_specs=[pl.BlockSpec((w, BRANCH_W), lambda b, i: (b * nw + i, 0)),
                   pl.BlockSpec((1, w, kvw), lambda b, i: (b, 0, 0))],
        out_shape=[jax.ShapeDtypeStruct((batch * seq, BRANCH_W), BF16),
                   jax.ShapeDtypeStruct((batch, w, kvw), F32)],
        compiler_params=_cparams("parallel", "arbitrary"),
    )(p, p, p, p, p, qn, kn2, sinks)


def _dn_mb_kernel(qkv_ref, z_ref, sm_ref, cw_ref, alog_ref, dtb_ref, on_ref,
                  o_ref, s_out_ref, s_ref, xp_ref, y_ref):
    c_idx = pl.program_id(0)
    nb = qkv_ref.shape[0]
    c = CHUNK
    pad = 8

    @pl.when(c_idx == 0)
    def _():
        s_ref[...] = jnp.zeros_like(s_ref)
        xp_ref[:, 0:pad, :] = jnp.zeros((nb, pad, DN_QKV), F32)

    row, col = _iota((c, c), 0), _iota((c, c), 1)
    incl = row >= col
    strict = row > col
    incl_f = incl.astype(F32)
    hk = DN_HEADS * DN_DK
    gam_all, gam_t, beta_all = [], [], []
    for b in range(nb):
        xp_ref[b, pad:pad + c, :] = qkv_ref[b]
        acc = xp_ref[b, pad - 3:pad - 3 + c, :] * cw_ref[0:1, :]
        for j in range(1, CONV_W):
            acc = acc + xp_ref[b, pad - 3 + j:pad - 3 + j + c, :] * cw_ref[j:j + 1, :]
        y_ref[b] = _silu(acc)
        xp_ref[b, pad - 3:pad, :] = xp_ref[b, pad + c - 3:pad + c, :]
        sm = sm_ref[b]
        beta_all.append(jax.nn.sigmoid(sm))
        g_all = -jnp.exp(alog_ref[...]) * jax.nn.softplus(sm + dtb_ref[...])
        gam_all.append(_dot(incl_f, g_all, HIGHEST))
        gam_t.append(gam_all[b].T)

    chains = [(b, h) for b in range(nb) for h in range(DN_HEADS)]
    n = len(chains)
    q, k, dec, eg, gl, gam, kb, rhs, a = ([None] * n for _ in range(9))
    for i, (b, h) in enumerate(chains):
        qi = y_ref[b, :, h * DN_DK:(h + 1) * DN_DK]
        ki = y_ref[b, :, hk + h * DN_DK:hk + (h + 1) * DN_DK]
        vi = y_ref[b, :, 2 * hk + h * DN_DV:2 * hk + (h + 1) * DN_DV]
        q[i] = qi * lax.rsqrt(jnp.sum(qi * qi, -1, keepdims=True) + EPS) * DN_DK ** -0.5
        k[i] = ki * lax.rsqrt(jnp.sum(ki * ki, -1, keepdims=True) + EPS)
        beta = beta_all[b][:, SM_BETA + h:SM_BETA + h + 1]
        gam[i] = gam_all[b][:, SM_G + h:SM_G + h + 1]
        gam_row = gam_t[b][SM_G + h:SM_G + h + 1, :]
        dec[i] = jnp.where(incl, jnp.exp(jnp.minimum(gam[i] - gam_row, 0.0)), 0.0)
        eg[i] = jnp.exp(gam[i])
        gl[i] = gam[i][c - 1:c, :]
        kb[i] = k[i] * beta
        rhs[i] = jnp.concatenate([vi * beta, kb[i] * eg[i]], -1)
    kbf = [x.astype(BF16) for x in k]
    kk = [_dot_nt(kb[i].astype(BF16), kbf[i]) for i in range(n)]
    qk = [_dot_nt(q[i].astype(BF16), kbf[i]) for i in range(n)]
    a = [jnp.where(strict, kk[i] * dec[i], 0.0) for i in range(n)]
    w = _strict_lower_inverse_minus_eye_multi(a)
    sol = [rhs[i] + _dot(w[i], rhs[i]) for i in range(n)]
    s = [s_ref[b, h] for (b, h) in chains]
    sbf = [x.astype(BF16) for x in s]
    u = [sol[i][:, :DN_DV] - _dot(sol[i][:, DN_DV:].astype(BF16), sbf[i]) for i in range(n)]
    ubf = [x.astype(BF16) for x in u]
    o_s = [_dot((q[i] * eg[i]).astype(BF16), sbf[i]) for i in range(n)]
    o_u = [_dot((qk[i] * dec[i]).astype(BF16), ubf[i]) for i in range(n)]
    ds = [_dot_tn((k[i] * jnp.exp(gl[i] - gam[i])).astype(BF16), ubf[i]) for i in range(n)]
    for i, (b, h) in enumerate(chains):
        s_ref[b, h] = s[i] * jnp.exp(gl[i]) + ds[i]
        o = o_s[i] + o_u[i]
        o = o * lax.rsqrt(jnp.mean(o * o, -1, keepdims=True) + EPS) * on_ref[...]
        o_ref[b, :, h * DN_DV:(h + 1) * DN_DV] = (
            o * _silu(z_ref[b, :, h * DN_DV:(h + 1) * DN_DV])).astype(BF16)

    @pl.when(c_idx == pl.num_programs(0) - 1)
    def _():
        s_out_ref[...] = s_ref[...]


def _dn_prompt_mb(p3, conv_w, alog_row, dtb_row, onorm):
    batch, seq, _ = p3.shape
    c = CHUNK
    full = lambda shape: pl.BlockSpec(shape, lambda i: tuple(0 for _ in shape))
    return pl.pallas_call(
        _dn_mb_kernel,
        grid=(seq // c,),
        in_specs=[pl.BlockSpec((batch, c, DN_QKV), lambda i: (0, i, C_AQKV // DN_QKV)),
                  pl.BlockSpec((batch, c, BRANCH_W), lambda i: (0, i, C_AZ // BRANCH_W)),
                  pl.BlockSpec((batch, c, LANES), lambda i: (0, i, C_SM // LANES)),
                  full((CONV_W, DN_QKV)), full((1, LANES)), full((1, LANES)), full((1, DN_DV))],
        out_specs=[pl.BlockSpec((batch, c, BRANCH_W), lambda i: (0, i, 0)),
                   full((batch, DN_HEADS, DN_DK, DN_DV))],
        out_shape=[jax.ShapeDtypeStruct((batch, seq, BRANCH_W), BF16),
                   jax.ShapeDtypeStruct((batch, DN_HEADS, DN_DK, DN_DV), F32)],
        scratch_shapes=[pltpu.VMEM((batch, DN_HEADS, DN_DK, DN_DV), F32),
                        pltpu.VMEM((batch, c + 8, DN_QKV), F32),
                        pltpu.VMEM((batch, c, DN_QKV), F32)],
        compiler_params=_cparams("arbitrary"),
    )(p3, p3, p3, conv_w, alog_row, dtb_row, onorm)


def _gla_mb_kernel(q_ref, k_ref, v_ref, r_ref, sm_ref, w2_ref, b_ref, on_ref,
                   o_ref, s_out_ref, st_ref):
    c_idx = pl.program_id(0)
    nb = q_ref.shape[0]
    c = CHUNK
    hw = GLA_HEADS * GLA_DK
    hv = GLA_HEADS * GLA_DV

    @pl.when(c_idx == 0)
    def _():
        st_ref[...] = jnp.zeros_like(st_ref)

    incl_f = (_iota((c, c), 0) >= _iota((c, c), 1)).astype(F32)
    blk = (_iota((hv, hw), 0) // GLA_DV) == (_iota((hv, hw), 1) // GLA_DK)
    qsel = (_iota((GLA_HEADS * SUB, hw), 0) // SUB) == (_iota((GLA_HEADS * SUB, hw), 1) // GLA_DK)
    nbr = range(nb)
    lg = [jax.nn.log_sigmoid(_dot(sm_ref[b].astype(BF16), w2_ref[...]) + b_ref[...]) / GLA_TAU for b in nbr]
    gam = [_dot(incl_f, lg[b], HIGHEST) for b in nbr]
    gl = [gam[b][c - 1:c, :] for b in nbr]
    qs = [q_ref[b] * GLA_DK ** -0.5 for b in nbr]
    k = [k_ref[b] for b in nbr]
    qg = [(qs[b] * jnp.exp(gam[b])).astype(BF16) for b in nbr]
    kd = [(k[b] * jnp.exp(gl[b] - gam[b])).astype(BF16) for b in nbr]
    vbf = [v_ref[b].astype(BF16) for b in nbr]
    st = [st_ref[b] for b in nbr]
    o_inter = [_dot_nt(qg[b], st[b].astype(BF16)) for b in nbr]
    dst = [_dot_tn(vbf[b], kd[b]) for b in nbr]
    for b in nbr:
        st_ref[b] = st[b] * jnp.exp(gl[b]) + jnp.where(blk, dst[b], 0.0)

    res = [[] for _ in nbr]
    for i in range(c // SUB):
        lo, hi = i * SUB, (i + 1) * SUB
        keep = (_iota((GLA_HEADS * SUB, hi), 0) % SUB + lo) >= _iota((GLA_HEADS * SUB, hi), 1)
        qm, ki = [], []
        for b in nbr:
            ref_pt = gam[b][lo - 1:lo, :] if i > 0 else jnp.zeros((1, hw), F32)
            qi = qs[b][lo:hi] * jnp.exp(gam[b][lo:hi] - ref_pt)
            qm.append(jnp.where(qsel, jnp.concatenate([qi] * GLA_HEADS, 0), 0.0).astype(BF16))
            ki.append((k[b][:hi] * jnp.exp(jnp.minimum(ref_pt - gam[b][:hi], EXP_CAP))).astype(BF16))
        att = [_dot_nt(qm[b], ki[b]) for b in nbr]
        att = [jnp.where(keep, att[b], 0.0).astype(BF16) for b in nbr]
        for b in nbr:
            res[b].append(_dot(att[b], vbf[b][:hi]))
    for b in nbr:
        for h in range(GLA_HEADS):
            vs = slice(h * GLA_DV, (h + 1) * GLA_DV)
            o = o_inter[b][:, vs] + jnp.concatenate([r[h * SUB:(h + 1) * SUB, vs] for r in res[b]], 0)
            o = o * lax.rsqrt(jnp.mean(o * o, -1, keepdims=True) + EPS) * on_ref[...]
            o_ref[b, :, vs] = (o * _silu(r_ref[b, :, vs])).astype(BF16)

    @pl.when(c_idx == pl.num_programs(0) - 1)
    def _():
        for b in range(nb):
            for h in range(GLA_HEADS):
                s_out_ref[b, h] = st_ref[b, h * GLA_DV:(h + 1) * GLA_DV, h * GLA_DK:(h + 1) * GLA_DK]


def _gla_prompt_mb(p3, w2pad, gla_b, onorm):
    batch, seq, _ = p3.shape
    c = CHUNK
    hw = GLA_HEADS * GLA_DK
    hv = GLA_HEADS * GLA_DV
    full = lambda shape: pl.BlockSpec(shape, lambda i: tuple(0 for _ in shape))
    return pl.pallas_call(
        _gla_mb_kernel,
        grid=(seq // c,),
        in_specs=[pl.BlockSpec((batch, c, hw), lambda i: (0, i, C_BQ // hw)),
                  pl.BlockSpec((batch, c, hw), lambda i: (0, i, C_BK // hw)),
                  pl.BlockSpec((batch, c, hv), lambda i: (0, i, C_BV // hv)),
                  pl.BlockSpec((batch, c, hv), lambda i: (0, i, C_BR // hv)),
                  pl.BlockSpec((batch, c, LANES), lambda i: (0, i, C_SM // LANES)),
                  full((LANES, hw)), full((1, hw)), full((1, GLA_DV))],
        out_specs=[pl.BlockSpec((batch, c, hv), lambda i: (0, i, 0)),
                   full((batch, GLA_HEADS, GLA_DV, GLA_DK))],
        out_shape=[jax.ShapeDtypeStruct((batch, seq, hv), BF16),
                   jax.ShapeDtypeStruct((batch, GLA_HEADS, GLA_DV, GLA_DK), F32)],
        scratch_shapes=[pltpu.VMEM((batch, hv, hw), F32)],
        compiler_params=_cparams("arbitrary"),
    )(p3, p3, p3, p3, p3, w2pad, gla_b, onorm)


def _swa_mb_kernel(q_ref, kp_ref, kc_ref, vp_ref, vc_ref, qn_ref, kn_ref, snk_ref,
                   o_ref, knew_ref):
    i = pl.program_id(0)
    nb = q_ref.shape[0]
    w = WINDOW
    g = SWA_HEADS // SWA_KV
    nr = g * w
    first = _iota((w, 2 * SWA_HD), 1) < SWA_HD
    first2 = _iota((2 * w, 2 * SWA_HD), 1) < SWA_HD
    t = _iota((nr, 2 * w), 0) % w
    j = _iota((nr, 2 * w), 1)
    dist = w + t - j
    valid = (dist >= 0) & (dist <= w) & ((j >= w) | (i > 0))
    distf = dist.astype(F32)
    hrow = _iota((nr, 1), 0) // w
    slope, snk = [], []
    for kv in range(SWA_KV):
        sl = jnp.zeros((nr, 1), F32)
        sk = jnp.zeros((nr, 1), F32)
        for hh in range(g):
            h = kv * g + hh
            sl = jnp.where(hrow == hh, 2.0 ** (-8.0 * (h + 1) / SWA_HEADS), sl)
            sk = jnp.where(hrow == hh, snk_ref[:, h:h + 1], sk)
        slope.append(sl * distf)
        snk.append(sk)

    units = [(b, kv) for b in range(nb) for kv in range(SWA_KV)]
    kcs, k2, v2, qx = [], [], [], []
    for b in range(nb):
        kc = _half_rms(kc_ref[b], kn_ref[...])
        kcs.append(kc)
        kk = jnp.concatenate([_half_rms(kp_ref[b], kn_ref[...]), kc], 0)
        vv = jnp.concatenate([vp_ref[b], vc_ref[b]], 0)
        kk_sw = pltpu.roll(kk, SWA_HD, axis=1)
        vv_sw = pltpu.roll(vv, SWA_HD, axis=1)
        k2 += [jnp.where(first2, kk, kk_sw).astype(BF16), jnp.where(first2, kk_sw, kk).astype(BF16)]
        v2 += [jnp.where(first2, vv, vv_sw).astype(BF16), jnp.where(first2, vv_sw, vv).astype(BF16)]
        for kv in range(SWA_KV):
            rows = []
            for jj in range(g // 2):
                grp = kv * (g // 2) + jj
                qg = _half_rms(q_ref[b, :, grp * 2 * SWA_HD:(grp + 1) * 2 * SWA_HD], qn_ref[...]) * SWA_HD ** -0.5
                rows += [jnp.where(first, qg, 0.0), jnp.where(first, 0.0, qg)]
            qx.append(jnp.concatenate(rows, 0).astype(BF16))
    nu = range(len(units))
    s = [_dot_nt(qx[u], k2[u]) for u in nu]
    s = [jnp.where(valid, s[u] - slope[units[u][1]], -jnp.inf) for u in nu]
    m = [jnp.maximum(jnp.max(s[u], -1, keepdims=True), snk[units[u][1]]) for u in nu]
    pr = [jnp.exp(s[u] - m[u]) for u in nu]
    den = [jnp.sum(pr[u], -1, keepdims=True) + jnp.exp(snk[units[u][1]] - m[u]) for u in nu]
    pr = [(pr[u] * (1.0 / den[u])).astype(BF16) for u in nu]
    o = [_dot(pr[u], v2[u]) for u in nu]
    for u, (b, kv) in enumerate(units):
        for jj in range(g // 2):
            grp = kv * (g // 2) + jj
            o_ref[b, :, grp * 2 * SWA_HD:(grp + 1) * 2 * SWA_HD] = jnp.where(
                first, o[u][(2 * jj) * w:(2 * jj + 1) * w], o[u][(2 * jj + 1) * w:(2 * jj + 2) * w]).astype(BF16)

    @pl.when(i == pl.num_programs(0) - 1)
    def _():
        for b in range(nb):
            knew_ref[b] = kcs[b]


def _swa_prompt_mb(p3, qn2, kn2, sinks):
    batch, seq, _ = p3.shape
    w = WINDOW
    kvw = SWA_KV * SWA_HD
    full = lambda shape: pl.BlockSpec(shape, lambda i: tuple(0 for _ in shape))
    prev = lambda i: jnp.maximum(i - 1, 0)
    return pl.pallas_call(
        _swa_mb_kernel,
        grid=(seq // w,),
        in_specs=[pl.BlockSpec((batch, w, BRANCH_W), lambda i: (0, i, C_CQ // BRANCH_W)),
                  pl.BlockSpec((batch, w, kvw), lambda i: (0, prev(i), C_CK // kvw)),
                  pl.BlockSpec((batch, w, kvw), lambda i: (0, i, C_CK // kvw)),
                  pl.BlockSpec((batch, w, kvw), lambda i: (0, prev(i), C_CV // kvw)),
                  pl.BlockSpec((batch, w, kvw), lambda i: (0, i, C_CV // kvw)),
                  full((1, kvw)), full((1, kvw)), full((1, SWA_HEADS))],
        out_specs=[pl.BlockSpec((batch, w, BRANCH_W), lambda i: (0, i, 0)),
                   full((batch, w, kvw))],
        out_shape=[jax.ShapeDtypeStruct((batch, seq, BRANCH_W), BF16),
                   jax.ShapeDtypeStruct((batch, w, kvw), F32)],
        compiler_params=_cparams("arbitrary"),
    )(p3, p3, p3, p3, p3, qn2, kn2, sinks)


DEC_TILE = 8


def _dec_kernel(qkv_ref, z_ref, sm_ref, bq_ref, bk_ref, bv_ref, br_ref, cq_ref, ck_ref, cv_ref,
                sdn_ref, buf_ref, sgl_ref, kc_ref, vc_ref,
                cw_ref, alog_ref, dtb_ref, dnon_ref, w2_ref, glb_ref, glon_ref,
                qn_ref, kn_ref, snk_ref, slp_ref,
                oa_ref, ob_ref, oc_ref, sdn_out, buf_out, sgl_out, kc_out, vc_out):
    bt = DEC_TILE
    x = qkv_ref[...]
    buf = buf_ref[...]
    acc = x * cw_ref[CONV_W - 1:CONV_W, :]
    for j in range(CONV_W - 1):
        acc = acc + buf[:, j * DN_QKV:(j + 1) * DN_QKV] * cw_ref[j:j + 1, :]
    y = _silu(acc)
    buf_out[:, 0:(CONV_W - 2) * DN_QKV] = buf[:, DN_QKV:]
    buf_out[:, (CONV_W - 2) * DN_QKV:] = x

    sm = sm_ref[...]
    beta_all = jax.nn.sigmoid(sm)
    g_all = -jnp.exp(alog_ref[...]) * jax.nn.softplus(sm + dtb_ref[...])
    eye = (_iota((LANES, LANES), 0) == _iota((LANES, LANES), 1)).astype(F32)
    hk = DN_HEADS * DN_DK
    for h in range(DN_HEADS):
        q = y[:, h * DN_DK:(h + 1) * DN_DK]
        k = y[:, hk + h * DN_DK:hk + (h + 1) * DN_DK]
        v = y[:, 2 * hk + h * DN_DV:2 * hk + (h + 1) * DN_DV]
        q = q * lax.rsqrt(jnp.sum(q * q, -1, keepdims=True) + EPS) * DN_DK ** -0.5
        k = k * lax.rsqrt(jnp.sum(k * k, -1, keepdims=True) + EPS)
        beta = beta_all[:, SM_BETA + h:SM_BETA + h + 1]
        eg = jnp.exp(g_all[:, SM_G + h:SM_G + h + 1])
        kb = k * beta
        lhs = jnp.concatenate([kb * eg, q * eg], 0).astype(BF16)
        k_t = _dot_nt(eye, k)
        qk = jnp.sum(q * k, -1, keepdims=True)
        vb = v * beta
        o_rows = []
        for b in range(bt):
            s = sdn_ref[b, h]
            r = _dot(lhs, s.astype(BF16))
            u = vb[b:b + 1] - r[b:b + 1]
            o_rows.append(r[bt + b:bt + b + 1] + qk[b:b + 1] * u)
            sdn_out[b, h] = s * eg[b:b + 1] + k_t[:, b:b + 1] * u
        o = jnp.concatenate(o_rows, 0)
        o = o * lax.rsqrt(jnp.mean(o * o, -1, keepdims=True) + EPS) * dnon_ref[...]
        oa_ref[:, h * DN_DV:(h + 1) * DN_DV] = (o * _silu(z_ref[:, h * DN_DV:(h + 1) * DN_DV])).astype(BF16)

    lg = jax.nn.log_sigmoid(_dot(sm.astype(BF16), w2_ref[...]) + glb_ref[...]) / GLA_TAU
    elg = jnp.exp(lg)
    bq = bq_ref[...] * GLA_DK ** -0.5
    bk = bk_ref[...]
    qg = bq * elg
    lane = _iota((bt, LANES), 1)
    first = lane < GLA_DK
    rows_first = _iota((LANES, GLA_DV), 0) < GLA_DK
    for j in range(GLA_HEADS // 2):
        sl = slice(j * LANES, (j + 1) * LANES)
        qgj = qg[:, sl]
        lhs = jnp.concatenate([jnp.where(first, qgj, 0.0), jnp.where(first, 0.0, qgj)], 0).astype(BF16)
        cols = _dot_nt(eye, jnp.concatenate([elg[:, sl], bk[:, sl]], 0), HIGHEST)
        prod = bq[:, sl] * bk[:, sl]
        qk0 = jnp.sum(jnp.where(first, prod, 0.0), -1, keepdims=True)
        qk1 = jnp.sum(jnp.where(first, 0.0, prod), -1, keepdims=True)
        v0 = bv_ref[:, (2 * j) * GLA_DV:(2 * j + 1) * GLA_DV]
        v1 = bv_ref[:, (2 * j + 1) * GLA_DV:(2 * j + 2) * GLA_DV]
        o0, o1 = [], []
        for b in range(bt):
            s = sgl_ref[b, j]
            r = _dot(lhs, s.astype(BF16))
            o0.append(r[b:b + 1] + qk0[b:b + 1] * v0[b:b + 1])
            o1.append(r[bt + b:bt + b + 1] + qk1[b:b + 1] * v1[b:b + 1])
            vsel = jnp.where(rows_first, v0[b:b + 1], v1[b:b + 1])
            sgl_out[b, j] = s * cols[:, b:b + 1] + cols[:, bt + b:bt + b + 1] * vsel
        for hh, rows in ((2 * j, o0), (2 * j + 1, o1)):
            o = jnp.concatenate(rows, 0)
            o = o * lax.rsqrt(jnp.mean(o * o, -1, keepdims=True) + EPS) * glon_ref[...]
            ob_ref[:, hh * GLA_DV:(hh + 1) * GLA_DV] = (o * _silu(br_ref[:, hh * GLA_DV:(hh + 1) * GLA_DV])).astype(BF16)

    g = SWA_HEADS // SWA_KV
    nr = bt * SWA_HEADS
    cq = cq_ref[...]
    cq = cq * lax.rsqrt(jnp.mean(cq * cq, -1, keepdims=True) + EPS) * qn_ref[...] * SWA_HD ** -0.5
    head = _iota((nr, 2 * SWA_HD), 0) % SWA_HEADS
    in_half = (head // g) == (_iota((nr, 2 * SWA_HD), 1) // SWA_HD)
    qx = jnp.where(in_half, jnp.concatenate([cq, cq], -1), 0.0)
    knew = _half_rms(ck_ref[...], kn_ref[...])
    vnew = cv_ref[...]
    s_c, kn_rows, vn_rows = [], [], []
    for b in range(bt):
        s_c.append(_dot_nt(qx[b * SWA_HEADS:(b + 1) * SWA_HEADS].astype(BF16), kc_ref[b].astype(BF16)))
        kn_rows.append(jnp.broadcast_to(knew[b:b + 1], (SWA_HEADS, 2 * SWA_HD)))
        vn_rows.append(jnp.broadcast_to(vnew[b:b + 1], (SWA_HEADS, 2 * SWA_HD)))
    s_c = jnp.concatenate(s_c, 0)
    kn_x = jnp.concatenate(kn_rows, 0)
    vn_x = jnp.concatenate(vn_rows, 0)
    slopes = slp_ref[...]
    snk = snk_ref[...]
    dist = (WINDOW - _iota((nr, WINDOW), 1)).astype(F32)
    s_c = s_c - slopes * dist
    s_n = jnp.sum(qx * kn_x, -1, keepdims=True)
    m = jnp.maximum(jnp.maximum(jnp.max(s_c, -1, keepdims=True), s_n), snk)
    p_c = jnp.exp(s_c - m)
    p_n = jnp.exp(s_n - m)
    den = jnp.sum(p_c, -1, keepdims=True) + p_n + jnp.exp(snk - m)
    p_c = p_c / den
    p_n = p_n / den
    half_sel = (_iota((nr, SWA_HD), 0) % SWA_HEADS) < g
    for b in range(bt):
        rs = slice(b * SWA_HEADS, (b + 1) * SWA_HEADS)
        r = _dot(p_c[rs].astype(BF16), vc_ref[b].astype(BF16)) + p_n[rs] * vn_x[rs]
        oc_ref[rs, :] = jnp.where(half_sel[rs], r[:, :SWA_HD], r[:, SWA_HD:]).astype(BF16)
        kc_out[b, 0:WINDOW - 1, :] = kc_ref[b, 1:WINDOW, :]
        kc_out[b, WINDOW - 1:WINDOW, :] = knew[b:b + 1]
        vc_out[b, 0:WINDOW - 1, :] = vc_ref[b, 1:WINDOW, :]
        vc_out[b, WINDOW - 1:WINDOW, :] = vnew[b:b + 1]


def _dec_mixers(p, cq_r, sdn, buf, sgl2, kc, vc, conv_w, alog_row, dtb_row, dn_on, w2pad, gla_b, gla_on,
                qn, kn2, snk_col, slp_col):
    n = p.shape[0]
    bt = DEC_TILE
    hw = GLA_HEADS * GLA_DK
    kvw = SWA_KV * SWA_HD
    nr = bt * SWA_HEADS

    def col(width, off):
        return pl.BlockSpec((bt, width), lambda i: (i, off // width))

    def full(shape):
        return pl.BlockSpec(shape, lambda i: tuple(0 for _ in shape))

    def lead(shape):
        return pl.BlockSpec((bt,) + shape, lambda i: (i,) + tuple(0 for _ in shape))

    return pl.pallas_call(
        _dec_kernel,
        grid=(n // bt,),
        in_specs=[col(DN_QKV, C_AQKV), col(BRANCH_W, C_AZ), col(LANES, C_SM), col(hw, C_BQ), col(hw, C_BK),
                  col(BRANCH_W, C_BV), col(BRANCH_W, C_BR),
                  pl.BlockSpec((nr, SWA_HD), lambda i: (i, 0)),
                  col(kvw, C_CK), col(kvw, C_CV),
                  lead((DN_HEADS, DN_DK, DN_DV)),
                  pl.BlockSpec((bt, (CONV_W - 1) * DN_QKV), lambda i: (i, 0)),
                  lead((GLA_HEADS // 2, 2 * GLA_DK, GLA_DV)),
                  lead((WINDOW, kvw)), lead((WINDOW, kvw)),
                  full((CONV_W, DN_QKV)), full((1, LANES)), full((1, LANES)), full((1, DN_DV)),
                  full((LANES, hw)), full((1, hw)), full((1, GLA_DV)),
                  full((1, SWA_HD)), full((1, kvw)), full((nr, 1)), full((nr, 1))],
        out_specs=[pl.BlockSpec((bt, BRANCH_W), lambda i: (i, 0)),
                   pl.BlockSpec((bt, BRANCH_W), lambda i: (i, 0)),
                   pl.BlockSpec((nr, SWA_HD), lambda i: (i, 0)),
                   lead((DN_HEADS, DN_DK, DN_DV)),
                   pl.BlockSpec((bt, (CONV_W - 1) * DN_QKV), lambda i: (i, 0)),
                   lead((GLA_HEADS // 2, 2 * GLA_DK, GLA_DV)),
                   lead((WINDOW, kvw)), lead((WINDOW, kvw))],
        out_shape=[jax.ShapeDtypeStruct((n, BRANCH_W), BF16),
                   jax.ShapeDtypeStruct((n, BRANCH_W), BF16),
                   jax.ShapeDtypeStruct((n * SWA_HEADS, SWA_HD), BF16),
                   jax.ShapeDtypeStruct(sdn.shape, F32),
                   jax.ShapeDtypeStruct(buf.shape, F32),
                   jax.ShapeDtypeStruct(sgl2.shape, F32),
                   jax.ShapeDtypeStruct(kc.shape, F32),
                   jax.ShapeDtypeStruct(vc.shape, F32)],
        compiler_params=_cparams("parallel"),
    )(p, p, p, p, p, p, p, cq_r, p, p, sdn, buf, sgl2, kc, vc,
      conv_w, alog_row, dtb_row, dn_on, w2pad, gla_b, gla_on, qn, kn2, snk_col, slp_col)


def _merge_kernel(ba_ref, bb_ref, bc_ref, gate_ref, x_ref, gt_ref, sc_ref, sh_ref, ln_ref,
                  wb_ref, wo_ref, wr_ref, rb_ref, x1_ref, h2_ref, comb_ref):
    mix = None
    for n, br in enumerate((ba_ref, bb_ref, bc_ref)):
        up = _dot(br[...], wb_ref[n])
        term = jax.nn.sigmoid(gate_ref[:, n * D_MODEL:(n + 1) * D_MODEL]) * up
        mix = term if mix is None else mix + term
    x1 = x_ref[...] + gt_ref[0] * _dot(mix.astype(BF16), wo_ref[...])
    x1_ref[...] = x1
    h2 = x1 * lax.rsqrt(jnp.mean(x1 * x1, -1, keepdims=True) + EPS) * ln_ref[...]
    h2 = h2 * (1.0 + sc_ref[0]) + sh_ref[0]
    h2_ref[...] = h2.astype(BF16)

    logits = _dot(h2, wr_ref[...], HIGHEST) + rb_ref[...]
    lane = _iota(logits.shape, 1).astype(F32)
    big = float(LANES)
    lc = jnp.where(lane < N_GROUPS, logits, -jnp.inf)
    mc = jnp.max(lc, -1, keepdims=True)
    pg = 1.0 / jnp.sum(jnp.exp(lc - mc), -1, keepdims=True)
    grp = jnp.min(jnp.where(lc == mc, lane, big), -1, keepdims=True)
    lo = R_EXP + grp * EXP_PER_GROUP
    emask = (lane >= lo) & (lane < lo + EXP_PER_GROUP)
    le = jnp.where(emask, logits, -jnp.inf)
    pe = jnp.exp(le - jnp.max(le, -1, keepdims=True))
    pe = pe / jnp.sum(pe, -1, keepdims=True)
    v1 = jnp.max(pe, -1, keepdims=True)
    i1 = jnp.min(jnp.where(emask & (pe == v1), lane, big), -1, keepdims=True)
    pe2 = jnp.where(emask & (lane != i1), pe, -1.0)
    v2 = jnp.max(pe2, -1, keepdims=True)
    i2 = jnp.min(jnp.where(pe2 == v2, lane, big), -1, keepdims=True)
    tot = v1 + v2
    comb_ref[...] = jnp.where(lane == 0.0, i1 - R_EXP, jnp.where(lane == 1.0, i2 - R_EXP, jnp.where(
        lane == 2.0, pg * v1 / tot, jnp.where(lane == 3.0, pg * v2 / tot, 0.0))))


def _merge(ba, bb, bc, p, x, gt, sc, sh, ln, wb, wo, wr, rb, tm, tokens_per_row):
    t, d = x.shape
    tok = lambda width: pl.BlockSpec((tm, width), lambda i: (i, 0))
    full = lambda shape: pl.BlockSpec(shape, lambda i: tuple(0 for _ in shape))
    return pl.pallas_call(
        _merge_kernel,
        grid=(t // tm,),
        in_specs=[tok(BRANCH_W), tok(BRANCH_W), tok(BRANCH_W),
                  pl.BlockSpec((tm, N_BRANCH * d), lambda i: (i, C_GATE)),
                  tok(d),
                  _mod_spec(gt, tm, tokens_per_row), _mod_spec(sc, tm, tokens_per_row),
                  _mod_spec(sh, tm, tokens_per_row),
                  full((1, d)), full((N_BRANCH, BRANCH_W, d)), full((d, d)), full((d, LANES)), full((1, LANES))],
        out_specs=[tok(d), tok(d), tok(LANES)],
        out_shape=[jax.ShapeDtypeStruct((t, d), F32),
                   jax.ShapeDtypeStruct((t, d), BF16),
                   jax.ShapeDtypeStruct((t, LANES), F32)],
        compiler_params=_cparams("parallel"),
    )(ba, bb, bc, p, x, gt, sc, sh, ln, wb, wo, wr, rb)


def _moe_kernel(h_ref, comb_ref, x1_ref, gt_ref, w1_ref, w3_ref, w2_ref, o_ref, acc_ref):
    e = pl.program_id(1)

    @pl.when(e == 0)
    def _():
        acc_ref[...] = jnp.zeros_like(acc_ref)

    h = h_ref[...]
    he = _silu(_dot(h, w1_ref[0])) * _dot(h, w3_ref[0])
    ye = _dot(he.astype(BF16), w2_ref[0])
    comb = comb_ref[...]
    ef = e.astype(F32)
    ce = (jnp.where(comb[:, 0:1] == ef, comb[:, 2:3], 0.0) + jnp.where(comb[:, 1:2] == ef, comb[:, 3:4], 0.0))
    acc_ref[...] += ce * ye

    @pl.when(e == pl.num_programs(1) - 1)
    def _():
        o_ref[...] = x1_ref[...] + gt_ref[0] * acc_ref[...]


def _moe(h2, comb, x1, gt, w1, w3, w2, tm, tokens_per_row):
    t, d = x1.shape
    ne, _, de = w1.shape
    return pl.pallas_call(
        _moe_kernel,
        grid=(t // tm, ne),
        in_specs=[pl.BlockSpec((tm, d), lambda i, e: (i, 0)),
                  pl.BlockSpec((tm, LANES), lambda i, e: (i, 0)),
                  pl.BlockSpec((tm, d), lambda i, e: (i, 0)),
                  _mod_spec(gt, tm, tokens_per_row),
                  pl.BlockSpec((1, d, de), lambda i, e: (e, 0, 0)),
                  pl.BlockSpec((1, d, de), lambda i, e: (e, 0, 0)),
                  pl.BlockSpec((1, de, d), lambda i, e: (e, 0, 0))],
        out_specs=pl.BlockSpec((tm, d), lambda i, e: (i, 0)),
        out_shape=jax.ShapeDtypeStruct((t, d), F32),
        scratch_shapes=[pltpu.VMEM((tm, d), F32)],
        compiler_params=_cparams("parallel", "arbitrary"),
    )(h2, comb, x1, gt, w1, w3, w2)


MOE_TK = 512
MOE_TM = 512
ROW_CHUNK = 8
MOE_LB = 2 * MOE_TK + 256
MOE_CH = MOE_LB // ROW_CHUNK
MOE_XW = D_MODEL + LANES
assert 2 * MOE_TK + N_EXPERTS * (ROW_CHUNK - 1) <= MOE_LB - ROW_CHUNK


def _moe_sorted_tiles(t):
    rows = 2 * t + (t // MOE_TK) * N_EXPERTS * (ROW_CHUNK - 1) + N_EXPERTS * (MOE_TM - 1)
    return -(-rows // MOE_TM)


def _plan_kernel(r_ref, lp_ref, dc_ref, te_ref, tot_ref, base_ref):
    ph, t = pl.program_id(0), pl.program_id(1)
    tk = r_ref.shape[0]
    ntp = te_ref.shape[0]
    r = r_ref[...]
    lane = _iota((tk, LANES), 1).astype(F32)
    sel1, sel2 = lane == r[:, 0:1], lane == r[:, 1:2]
    oh = jnp.where(sel1 | sel2, 1.0, 0.0)
    cnt = jnp.sum(oh, 0, keepdims=True)
    c8 = jnp.floor((cnt + (ROW_CHUNK - 1)) * (1.0 / ROW_CHUNK)) * ROW_CHUNK
    upper = (_iota((LANES, LANES), 0) < _iota((LANES, LANES), 1)).astype(F32)

    def excl_cumsum(v):
        return _dot(jnp.broadcast_to(v, (8, LANES)), upper, HIGHEST)[0:1]

    @pl.when(ph == 0)
    def _():
        @pl.when(t == 0)
        def _():
            tot_ref[...] = jnp.zeros_like(tot_ref)

        tot_ref[...] += c8

    @pl.when(ph == 1)
    def _():
        @pl.when(t == 0)
        def _():
            tot = tot_ref[...]
            gp = jnp.floor((tot + (MOE_TM - 1)) * (1.0 / MOE_TM)) * MOE_TM
            off = excl_cumsum(gp)
            base_ref[...] = off
            end = off + gp
            lane_t = _iota((ntp, LANES), 1).astype(F32)
            start = _iota((ntp, 1), 0).astype(F32) * MOE_TM
            te = jnp.sum(jnp.where((lane_t < N_EXPERTS) & (end <= start), 1.0, 0.0), -1, keepdims=True)
            mine = lane_t == te
            filled = jnp.sum(jnp.where(mine, tot + off, 0.0), -1, keepdims=True)
            tv = jnp.clip(filled - start, 0.0, float(MOE_TM))
            n_used = jnp.sum(jnp.where(lane_t == N_EXPERTS - 1, end, 0.0), -1, keepdims=True) * (1.0 / MOE_TM)
            te_ref[...] = jnp.where(lane_t == 0.0, jnp.minimum(te, N_EXPERTS - 1.0),
                                    jnp.where(lane_t == 1.0, tv, jnp.where(lane_t == 2.0, n_used, 0.0))
                                    ).astype(jnp.int32)

        base = base_ref[...]
        lo = excl_cumsum(c8)
        below = (_iota((tk, tk), 0) > _iota((tk, tk), 1)).astype(BF16)
        p = _dot(below, oh.astype(BF16)) + lo
        lp1 = jnp.sum(jnp.where(sel1, p, 0.0), -1, keepdims=True)
        lp2 = jnp.sum(jnp.where(sel2, p, 0.0), -1, keepdims=True)
        lp_ref[...] = jnp.where(lane == 0.0, lp1, jnp.where(lane == 1.0, lp2, jnp.where(lane < 4.0, r, 0.0)))
        lane_c = _iota((MOE_CH, LANES), 1).astype(F32)
        cstart = _iota((MOE_CH, 1), 0).astype(F32) * ROW_CHUNK
        ej = jnp.sum(jnp.where((lane_c < N_EXPERTS) & (lo + c8 <= cstart), 1.0, 0.0), -1, keepdims=True)
        dj = jnp.sum(jnp.where(lane_c == ej, base - lo, 0.0), -1, keepdims=True) + cstart
        nrows = jnp.sum(c8, -1, keepdims=True)
        last = _iota((MOE_CH, 1), 0) == MOE_CH - 1
        dcv = jnp.where(last, nrows * (1.0 / ROW_CHUNK), jnp.where(cstart < nrows, dj, 0.0))
        dc_ref[...] = jnp.broadcast_to(dcv, (MOE_CH, LANES)).astype(jnp.int32)
        base_ref[...] = base + c8


def _moe_plan(route):
    t = route.shape[0]
    nt = t // MOE_TK
    ntp = -(-_moe_sorted_tiles(t) // 8) * 8
    return pl.pallas_call(
        _plan_kernel,
        grid=(2, nt),
        in_specs=[pl.BlockSpec((MOE_TK, LANES), lambda ph, i: (i, 0))],
        out_specs=[pl.BlockSpec((MOE_TK, LANES), lambda ph, i: (i * ph, 0)),
                   pl.BlockSpec((MOE_CH, LANES), lambda ph, i: (i * ph, 0)),
                   pl.BlockSpec((ntp, LANES), lambda ph, i: (0, 0))],
        out_shape=[jax.ShapeDtypeStruct((t, LANES), F32),
                   jax.ShapeDtypeStruct((nt * MOE_CH, LANES), jnp.int32),
                   jax.ShapeDtypeStruct((ntp, LANES), jnp.int32)],
        scratch_shapes=[pltpu.VMEM((1, LANES), F32), pltpu.VMEM((1, LANES), F32)],
        compiler_params=_cparams("arbitrary", "arbitrary"),
    )(route)


def _pick_onehot(lp):
    pos = _iota((lp.shape[0], MOE_LB), 1).astype(F32)
    return pos == lp[:, 0:1], pos == lp[:, 1:2]


def _chunk_loop(n, fn):
    def body(j, carry):
        fn(j)
        return carry

    lax.fori_loop(0, n, body, 0)


def _dispatch_kernel(dc_ref, h_ref, lp_ref, xs_ref, buf_ref, sem, nprev_ref):
    t, nt = pl.program_id(0), pl.num_programs(0)
    slot = t % 2
    tk = h_ref.shape[0]
    lp = lp_ref[...]
    oh1, oh2 = _pick_onehot(lp)
    lane = _iota((tk, LANES), 1)

    def split3(w):
        hi = w.astype(BF16).astype(F32)
        mid = (w - hi).astype(BF16).astype(F32)
        low = w - hi - mid
        return jnp.where(lane == 0, hi, jnp.where(lane == 1, mid, jnp.where(lane == 2, low, 0.0))).astype(BF16)

    b1, b2 = jnp.where(oh1, 1.0, 0.0).astype(BF16), jnp.where(oh2, 1.0, 0.0).astype(BF16)
    buf_ref[slot, :, 0:D_MODEL] = _dot_tn(b1 + b2, h_ref[...])
    buf_ref[slot, :, D_MODEL:] = _dot_tn(b1, split3(lp[:, 2:3])) + _dot_tn(b2, split3(lp[:, 3:4]))

    def chunk_copy(j, s):
        src = buf_ref.at[s, pl.ds(pl.multiple_of(j * ROW_CHUNK, ROW_CHUNK), ROW_CHUNK), :]
        dst = xs_ref.at[pl.ds(pl.multiple_of(dc_ref[0, 0, j], ROW_CHUNK), ROW_CHUNK), :]
        return pltpu.make_async_copy(src, dst, sem.at[s])

    nch = dc_ref[0, 0, MOE_CH - 1]
    _chunk_loop(nch, lambda j: chunk_copy(j, slot).start())

    @pl.when(t > 0)
    def _():
        _chunk_loop(nprev_ref[0], lambda j: chunk_copy(0, 1 - slot).wait())

    nprev_ref[0] = nch

    @pl.when(t == nt - 1)
    def _():
        _chunk_loop(nch, lambda j: chunk_copy(0, slot).wait())


def _moe_dispatch(dc3, h2, lp, n_rows):
    t, d = h2.shape
    nt = t // MOE_TK
    return pl.pallas_call(
        _dispatch_kernel,
        grid=(nt,),
        in_specs=[pl.BlockSpec((1, 1, MOE_CH), lambda i: (i, 0, 0), memory_space=pltpu.SMEM),
                  pl.BlockSpec((MOE_TK, d), lambda i: (i, 0)),
                  pl.BlockSpec((MOE_TK, LANES), lambda i: (i, 0))],
        out_specs=pl.BlockSpec(memory_space=pl.ANY),
        out_shape=jax.ShapeDtypeStruct((n_rows, MOE_XW), F32),
        scratch_shapes=[pltpu.VMEM((2, MOE_LB, MOE_XW), F32), pltpu.SemaphoreType.DMA((2,)),
                        pltpu.SMEM((1,), jnp.int32)],
        compiler_params=_cparams("arbitrary"),
    )(dc3, h2, lp)


def _experts_kernel(te_ref, tv_ref, nu_ref, x_ref, w1_ref, w3_ref, w2_ref, o_ref):
    i = pl.program_id(0)

    @pl.when(i < nu_ref[0])
    def _():
        tm = x_ref.shape[0]
        valid = _iota((tm, 1), 0) < tv_ref[i]
        x = x_ref[...]
        h = jnp.where(valid, x[:, :D_MODEL], 0.0).astype(BF16)
        wv = x[:, D_MODEL:D_MODEL + 1] + x[:, D_MODEL + 1:D_MODEL + 2] + x[:, D_MODEL + 2:D_MODEL + 3]
        he = _silu(_dot(h, w1_ref[0])) * _dot(h, w3_ref[0])
        o_ref[...] = jnp.where(valid, wv, 0.0) * _dot(he.astype(BF16), w2_ref[0])


def _moe_experts(te, tv, nu, xs, w1, w3, w2):
    n_rows = xs.shape[0]
    _, d, de = w1.shape
    cur = lambda i, te, tv, nu: jnp.minimum(i, nu[0] - 1)
    return pl.pallas_call(
        _experts_kernel,
        grid_spec=pltpu.PrefetchScalarGridSpec(
            num_scalar_prefetch=3,
            grid=(n_rows // MOE_TM,),
            in_specs=[pl.BlockSpec((MOE_TM, MOE_XW), lambda i, te, tv, nu: (cur(i, te, tv, nu), 0)),
                      pl.BlockSpec((1, d, de), lambda i, te, tv, nu: (te[cur(i, te, tv, nu)], 0, 0)),
                      pl.BlockSpec((1, d, de), lambda i, te, tv, nu: (te[cur(i, te, tv, nu)], 0, 0)),
                      pl.BlockSpec((1, de, d), lambda i, te, tv, nu: (te[cur(i, te, tv, nu)], 0, 0))],
            out_specs=pl.BlockSpec((MOE_TM, d), lambda i, te, tv, nu: (cur(i, te, tv, nu), 0))),
        out_shape=jax.ShapeDtypeStruct((n_rows, d), F32),
        compiler_params=_cparams("arbitrary"),
    )(te, tv, nu, xs, w1, w3, w2)


def _combine_kernel(dc_ref, dcn_ref, lp_ref, x1_ref, gt_ref, ys_ref, o_ref, buf_ref, sem):
    t, nt = pl.program_id(0), pl.num_programs(0)
    slot = t % 2

    def chunk_copy(tab, j, s):
        src = ys_ref.at[pl.ds(pl.multiple_of(tab[0, 0, j], ROW_CHUNK), ROW_CHUNK), :]
        dst = buf_ref.at[s, pl.ds(pl.multiple_of(j * ROW_CHUNK, ROW_CHUNK), ROW_CHUNK), :]
        return pltpu.make_async_copy(src, dst, sem.at[s])

    nch = dc_ref[0, 0, MOE_CH - 1]

    @pl.when(t == 0)
    def _():
        buf_ref[...] = jnp.zeros_like(buf_ref)
        _chunk_loop(nch, lambda j: chunk_copy(dc_ref, j, slot).start())

    @pl.when(t + 1 < nt)
    def _():
        _chunk_loop(dcn_ref[0, 0, MOE_CH - 1], lambda j: chunk_copy(dcn_ref, j, 1 - slot).start())

    _chunk_loop(nch, lambda j: chunk_copy(dc_ref, 0, slot).wait())
    live = _iota((MOE_LB, 1), 0) < nch * ROW_CHUNK
    local = jnp.where(live, buf_ref[slot], 0.0).astype(BF16)
    oh1, oh2 = _pick_onehot(lp_ref[...])
    perm = jnp.where(oh1 | oh2, 1.0, 0.0).astype(BF16)
    o_ref[...] = x1_ref[...] + gt_ref[0] * _dot(perm, local)


def _moe_combine(dc3, lp, x1, gt, ys, tokens_per_row):
    t, d = x1.shape
    nt = t // MOE_TK
    smem = lambda f: pl.BlockSpec((1, 1, MOE_CH), f, memory_space=pltpu.SMEM)
    return pl.pallas_call(
        _combine_kernel,
        grid=(nt,),
        in_specs=[smem(lambda i: (i, 0, 0)), smem(lambda i: (jnp.minimum(i + 1, nt - 1), 0, 0)),
                  pl.BlockSpec((MOE_TK, LANES), lambda i: (i, 0)),
                  pl.BlockSpec((MOE_TK, d), lambda i: (i, 0)),
                  _mod_spec(gt, MOE_TK, tokens_per_row),
                  pl.BlockSpec(memory_space=pl.ANY)],
        out_specs=pl.BlockSpec((MOE_TK, d), lambda i: (i, 0)),
        out_shape=jax.ShapeDtypeStruct((t, d), F32),
        scratch_shapes=[pltpu.VMEM((2, MOE_LB, d), F32), pltpu.SemaphoreType.DMA((2,))],
        compiler_params=_cparams("arbitrary"),
    )(dc3, dc3, lp, x1, gt, ys)


def _moe_sparse(h2, route, x1, gt, w1, w3, w2, tokens_per_row):
    t = h2.shape[0]
    nt = t // MOE_TK
    lp, dc, tmeta = _moe_plan(route)
    dc3 = dc[:, 0].reshape(nt, 1, MOE_CH)
    n_tiles = _moe_sorted_tiles(t)
    xs = _moe_dispatch(dc3, h2, lp, n_tiles * MOE_TM)
    ys = _moe_experts(tmeta[:n_tiles, 0], tmeta[:n_tiles, 1], tmeta[0:1, 2], xs, w1, w3, w2)
    return _moe_combine(dc3, lp, x1, gt, ys, tokens_per_row)


def _permute_w_in(w):
    sizes = (DN_QKV, DN_HEADS * DN_DV, DN_HEADS, DN_HEADS,
             GLA_HEADS * GLA_DK, GLA_HEADS * GLA_DK, GLA_HEADS * GLA_DV, GLA_HEADS * GLA_DV, GLA_RANK,
             SWA_HEADS * SWA_HD, SWA_KV * SWA_HD, SWA_KV * SWA_HD, N_BRANCH * D_MODEL)
    offs = [0]
    for s in sizes:
        offs.append(offs[-1] + s)
    seg = lambda i: w[..., offs[i]:offs[i + 1]]
    (a_qkv, a_z, a_b, a_a, b_q, b_k, b_v, b_r, b_lr, c_q, c_k, c_v, gate) = (seg(i) for i in range(len(sizes)))
    small = jnp.concatenate([a_b, a_a, b_lr], -1)
    fill = jnp.zeros(w.shape[:-1] + (P_PAD - C_SM - small.shape[-1],), w.dtype)
    return jnp.concatenate([gate, a_qkv, a_z, b_v, b_r, c_q, b_q, b_k, c_k, c_v, small, fill], -1)


def _lane_row(v, off):
    depth, n = v.shape
    return jnp.zeros((depth, 1, LANES), F32).at[:, 0, off:off + n].set(v.astype(F32))


def kernel(x_prompt, x_sample, c_prompt, c_sample, state_dn, state_dn_conv, state_gla, cache_swa_k, cache_swa_v, ln1_g, ln2_g, ada_w, ada_b, w_in, dn_conv_w, dn_a_log, dn_dt_bias, dn_onorm_g, gla_w2, gla_b, gla_onorm_g, swa_qnorm_g, swa_knorm_g, swa_sinks, w_branch, w_out, rc_w, rc_b, re_w, re_b, w1, w3, w2):
    batch, seq, d = x_prompt.shape
    nb = x_sample.shape[0]
    depth = w_in.shape[0]
    assert x_sample.shape[1] == 1 and d == D_MODEL and seq % CHUNK == 0 and nb % DEC_TILE == 0
    kvw = SWA_KV * SWA_HD

    w_in_p = _permute_w_in(w_in).astype(BF16)
    ada_w_b = ada_w.astype(BF16)
    wb_b, wo_b = w_branch.astype(BF16), w_out.astype(BF16)
    w1_b, w3_b, w2_b = w1.astype(BF16), w3.astype(BF16), w2.astype(BF16)
    wr = jnp.concatenate([rc_w, re_w, jnp.zeros((depth, d, LANES - N_GROUPS - N_EXPERTS), F32)], -1)
    rb = jnp.concatenate([rc_b, re_b, jnp.zeros((depth, LANES - N_GROUPS - N_EXPERTS), F32)], -1)[:, None, :]
    alog_row = _lane_row(dn_a_log, SM_G)
    dtb_row = _lane_row(dn_dt_bias, SM_G)
    w2pad = jnp.zeros((depth, LANES, GLA_HEADS * GLA_DK), F32).at[:, SM_LR:SM_LR + GLA_RANK].set(gla_w2).astype(BF16)
    kn2 = jnp.concatenate([swa_knorm_g] * SWA_KV, -1)[:, None, :]
    slopes = jnp.exp2(-8.0 * jnp.arange(1, SWA_HEADS + 1, dtype=F32) / SWA_HEADS)
    slp_col = jnp.tile(slopes, DEC_TILE)[:, None]

    pad_rows = (-(batch + nb)) % 8
    c_all = jnp.concatenate([c_prompt, c_sample, jnp.zeros((pad_rows, d), F32)], 0)
    mod = _ada(c_all, ada_w_b, ada_b)

    xp = x_prompt.reshape(batch * seq, d)
    xs = x_sample.reshape(nb, d)
    tm_p = 1024 if seq % 1024 == 0 else CHUNK
    tm_m = 256 if seq % 256 == 0 else CHUNK

    st_p, st_s = [], []
    for l in range(depth):
        mp = [m[:, None, :] for m in jnp.split(mod[l, :batch], 6, -1)]
        ms = [m[None] for m in jnp.split(mod[l, batch:batch + nb], 6, -1)]
        ln1, ln2 = ln1_g[l][None], ln2_g[l][None]
        conv_w = dn_conv_w[l]
        dn_on, gla_on = dn_onorm_g[l][None], gla_onorm_g[l][None]
        glb = gla_b[l][None]
        qn = swa_qnorm_g[l][None]
        snk = swa_sinks[l][None]

        pp = _in_proj(xp, mp[1], mp[0], ln1, w_in_p[l], tm_p, seq)
        pp3 = pp.reshape(batch, seq, P_PAD)
        ba, dn_s = _dn_prompt_mb(pp3, conv_w, alog_row[l], dtb_row[l], dn_on)
        bb, gla_st = _gla_prompt_mb(pp3, w2pad[l], glb, gla_on)
        bc, k_new = _swa_prompt_mb(pp3, jnp.concatenate([qn, qn], -1), kn2[l], snk)
        ba, bb, bc = (z.reshape(batch * seq, BRANCH_W) for z in (ba, bb, bc))
        x1, h2, comb = _merge(ba, bb, bc, pp, xp, mp[2], mp[4], mp[3], ln2, wb_b[l], wo_b[l], wr[l], rb[l],
                              tm_m, seq)
        xp = _moe_sparse(h2, comb, x1, mp[5], w1_b[l], w3_b[l], w2_b[l], seq)
        pp3 = pp.reshape(batch, seq, P_PAD)
        st_p.append((dn_s,
                     pp3[:, seq - (CONV_W - 1):, C_AQKV:C_AQKV + DN_QKV],
                     jnp.swapaxes(gla_st, -1, -2),
                     k_new.reshape(batch, WINDOW, SWA_KV, SWA_HD),
                     pp3[:, seq - WINDOW:, C_CV:C_CV + kvw].reshape(batch, WINDOW, SWA_KV, SWA_HD)))

        ps = _in_proj(xs, ms[1], ms[0], ln1, w_in_p[l], nb, nb)
        cq_r = ps[:, C_CQ:C_CQ + SWA_HEADS * SWA_HD].reshape(nb * SWA_HEADS, SWA_HD)
        oa, ob, oc_r, sdn_n, buf_n, sgl_n, kc_n, vc_n = _dec_mixers(
            ps, cq_r, state_dn[l], state_dn_conv[l].reshape(nb, (CONV_W - 1) * DN_QKV),
            state_gla[l].reshape(nb, GLA_HEADS // 2, 2 * GLA_DK, GLA_DV),
            cache_swa_k[l].reshape(nb, WINDOW, kvw), cache_swa_v[l].reshape(nb, WINDOW, kvw),
            conv_w, alog_row[l], dtb_row[l], dn_on, w2pad[l], glb, gla_on, qn, kn2[l],
            jnp.tile(swa_sinks[l], DEC_TILE)[:, None], slp_col)
        oc = oc_r.reshape(nb, SWA_HEADS * SWA_HD)
        x1, h2, comb = _merge(oa, ob, oc, ps, xs, ms[2], ms[4], ms[3], ln2, wb_b[l], wo_b[l], wr[l], rb[l],
                              nb, nb)
        xs = _moe(h2, comb, x1, ms[5], w1_b[l], w3_b[l], w2_b[l], nb, nb)
        st_s.append((sdn_n,
                     buf_n.reshape(nb, CONV_W - 1, DN_QKV),
                     sgl_n.reshape(nb, GLA_HEADS, GLA_DK, GLA_DV),
                     kc_n.reshape(nb, WINDOW, SWA_KV, SWA_HD),
                     vc_n.reshape(nb, WINDOW, SWA_KV, SWA_HD)))

    dn_p, conv_p, gla_p, k_p, v_p = [jnp.stack(z) for z in zip(*st_p)]
    dn_s, conv_s, gla_s, k_s, v_s = [jnp.stack(z) for z in zip(*st_s)]
    return (xp.reshape(batch, seq, d), xs.reshape(nb, 1, d), dn_p, dn_s, conv_p, conv_s, gla_p, gla_s,
            k_p, k_s, v_p, v_s)
```

```python
import functools

import jax
import jax.numpy as jnp
from jax import lax
from jax.experimental import pallas as pl
from jax.experimental.pallas import tpu as pltpu

F32 = jnp.float32
BF16 = jnp.bfloat16
HIGHEST = lax.Precision.HIGHEST

D_MODEL = 1024
DN_HEADS, DN_DK, DN_DV, CONV_W = 4, 128, 128, 4
DN_QKV = 2 * DN_HEADS * DN_DK + DN_HEADS * DN_DV
GLA_HEADS, GLA_DK, GLA_DV, GLA_RANK, GLA_TAU = 4, 64, 128, 16, 16.0
SWA_HEADS, SWA_KV, SWA_HD, WINDOW = 8, 2, 64, 128
N_BRANCH, BRANCH_W = 3, 512
N_GROUPS, EXP_PER_GROUP, TOP_K, D_EXPERT = 4, 8, 2, 256
N_EXPERTS = N_GROUPS * EXP_PER_GROUP
EPS = 1e-6

LANES = 128
CHUNK = 128
SUB = 16
VMEM_LIMIT = 56 * 1024 * 1024

C_GATE, C_AQKV, C_AZ, C_BV, C_BR, C_CQ = 0, 3072, 4608, 5120, 5632, 6144
C_BQ, C_BK, C_CK, C_CV, C_SM, P_PAD = 6656, 6912, 7168, 7296, 7424, 7680
SM_BETA, SM_G, SM_LR = 0, DN_HEADS, 2 * DN_HEADS
R_EXP = N_GROUPS


def _cparams(*sem):
    return pltpu.CompilerParams(dimension_semantics=sem, vmem_limit_bytes=VMEM_LIMIT)


def _dot(a, b, precision=None):
    return jnp.dot(a, b, preferred_element_type=F32, precision=precision)


def _dot_nt(a, b, precision=None):
    return lax.dot_general(a, b, (((1,), (1,)), ((), ())), preferred_element_type=F32, precision=precision)


def _dot_tn(a, b, precision=None):
    return lax.dot_general(a, b, (((0,), (0,)), ((), ())), preferred_element_type=F32, precision=precision)


def _silu(x):
    return x * jax.nn.sigmoid(x)


def _iota(shape, dim):
    return lax.broadcasted_iota(jnp.int32, shape, dim)


def _ada_kernel(c_ref, w_ref, b_ref, o_ref):
    c = _silu(c_ref[...]).astype(BF16)
    o_ref[0] = _dot(c, w_ref[0].astype(BF16)) + b_ref[0]


def _ada(c_all, ada_w, ada_b):
    depth, d, n = ada_w.shape
    rows = c_all.shape[0]
    tn = 1536
    return pl.pallas_call(
        _ada_kernel,
        grid=(depth, n // tn),
        in_specs=[pl.BlockSpec((rows, d), lambda l, j: (0, 0)),
                  pl.BlockSpec((1, d, tn), lambda l, j: (l, 0, j)),
                  pl.BlockSpec((1, 1, tn), lambda l, j: (l, 0, j))],
        out_specs=pl.BlockSpec((1, rows, tn), lambda l, j: (l, 0, j)),
        out_shape=jax.ShapeDtypeStruct((depth, rows, n), F32),
        compiler_params=_cparams("parallel", "parallel"),
    )(c_all, ada_w, ada_b.reshape(depth, 1, n))


def _mod_spec(mod, tm, tokens_per_row):
    _, r, d = mod.shape
    assert tokens_per_row % tm == 0
    per = tokens_per_row // tm
    return pl.BlockSpec((1, r, d), lambda i, *_: (i // per, 0, 0))


def _in_proj_kernel(x_ref, sc_ref, sh_ref, g_ref, w_ref, o_ref, h_ref):
    @pl.when(pl.program_id(1) == 0)
    def _():
        x = x_ref[...]
        y = x * lax.rsqrt(jnp.mean(x * x, -1, keepdims=True) + EPS) * g_ref[...]
        h_ref[...] = (y * (1.0 + sc_ref[0]) + sh_ref[0]).astype(BF16)

    o_ref[...] = _dot(h_ref[...], w_ref[...])


def _in_proj(layer, x, sc, sh, g, w, tm, tokens_per_row):
    t, d = x.shape
    n = w.shape[2]
    tn = 1536
    return pl.pallas_call(
        _in_proj_kernel,
        grid=(t // tm, n // tn),
        in_specs=[pl.BlockSpec((tm, d), lambda i, j: (i, 0)),
                  _mod_spec(sc, tm, tokens_per_row), _mod_spec(sh, tm, tokens_per_row),
                  pl.BlockSpec((1, d), lambda i, j: (0, 0)),
                  pl.BlockSpec((None, d, tn), lambda i, j: (layer, 0, j))],
        out_specs=pl.BlockSpec((tm, tn), lambda i, j: (i, j)),
        out_shape=jax.ShapeDtypeStruct((t, n), F32),
        scratch_shapes=[pltpu.VMEM((tm, d), BF16)],
        compiler_params=_cparams("parallel", "arbitrary"),
    )(x, sc, sh, g, w)


def _strict_lower_inverse_minus_eye(a):
    c = a.shape[0]
    row, col = _iota((c, c), 0), _iota((c, c), 1)
    diag_blk = (row // SUB) == (col // SUB)
    ad = jnp.where(diag_blk, a, 0.0)
    ao = a - ad
    n = -ad
    p = n
    steps = SUB.bit_length() - 2
    for _ in range(steps):
        p = _dot(p, p)
        n = n + p + _dot(n, p)
    bm = -(ao + _dot(n, ao))
    m = bm
    q = bm
    steps = (c // SUB).bit_length() - 2
    for _ in range(steps):
        q = _dot(q, q)
        m = m + q + _dot(m, q)
    return m + n + _dot(m, n)


def _strict_lower_inverse_minus_eye_multi(a_list):
    c = a_list[0].shape[0]
    diag_blk = (_iota((c, c), 0) // SUB) == (_iota((c, c), 1) // SUB)
    idx = range(len(a_list))
    ad = [jnp.where(diag_blk, a, 0.0) for a in a_list]
    ao = [a_list[i] - ad[i] for i in idx]
    n = [-x for x in ad]
    p = n
    for _ in range(SUB.bit_length() - 2):
        p = [_dot(x, x) for x in p]
        np_ = [_dot(n[i], p[i]) for i in idx]
        n = [n[i] + p[i] + np_[i] for i in idx]
    nao = [_dot(n[i], ao[i]) for i in idx]
    m = [-(ao[i] + nao[i]) for i in idx]
    q = m
    for _ in range((c // SUB).bit_length() - 2):
        q = [_dot(x, x) for x in q]
        mq = [_dot(m[i], q[i]) for i in idx]
        m = [m[i] + q[i] + mq[i] for i in idx]
    mn = [_dot(m[i], n[i]) for i in idx]
    return [m[i] + n[i] + mn[i] for i in idx]


def _dn_kernel(qkv_ref, z_ref, sm_ref, cw_ref, alog_ref, dtb_ref, on_ref,
               o_ref, s_out_ref, s_ref, xp_ref):
    c_idx = pl.program_id(1)
    c = CHUNK
    pad = 8

    @pl.when(c_idx == 0)
    def _():
        s_ref[...] = jnp.zeros_like(s_ref)
        xp_ref[0:pad, :] = jnp.zeros((pad, DN_QKV), F32)

    xp_ref[pad:pad + c, :] = qkv_ref[...]
    acc = xp_ref[pad - 3:pad - 3 + c, :] * cw_ref[0:1, :]
    for j in range(1, CONV_W):
        acc = acc + xp_ref[pad - 3 + j:pad - 3 + j + c, :] * cw_ref[j:j + 1, :]
    y = _silu(acc)
    xp_ref[pad - 3:pad, :] = xp_ref[pad + c - 3:pad + c, :]

    sm = sm_ref[...]
    beta_all = jax.nn.sigmoid(sm)
    g_all = -jnp.exp(alog_ref[...]) * jax.nn.softplus(sm + dtb_ref[...])
    row, col = _iota((c, c), 0), _iota((c, c), 1)
    incl = row >= col
    strict = row > col
    gam_all = _dot(incl.astype(F32), g_all, HIGHEST)
    gam_t = gam_all.T
    hk = DN_HEADS * DN_DK
    for h in range(DN_HEADS):
        q = y[:, h * DN_DK:(h + 1) * DN_DK]
        k = y[:, hk + h * DN_DK:hk + (h + 1) * DN_DK]
        v = y[:, 2 * hk + h * DN_DV:2 * hk + (h + 1) * DN_DV]
        q = q * lax.rsqrt(jnp.sum(q * q, -1, keepdims=True) + EPS) * DN_DK ** -0.5
        k = k * lax.rsqrt(jnp.sum(k * k, -1, keepdims=True) + EPS)
        beta = beta_all[:, SM_BETA + h:SM_BETA + h + 1]
        gam = gam_all[:, SM_G + h:SM_G + h + 1]
        gam_row = gam_t[SM_G + h:SM_G + h + 1, :]
        dec = jnp.where(incl, jnp.exp(jnp.where(incl, gam - gam_row, 0.0)), 0.0)
        eg = jnp.exp(gam)
        gl = gam[c - 1:c, :]
        kb = k * beta
        kbf = k.astype(BF16)
        a = jnp.where(strict, _dot_nt(kb.astype(BF16), kbf) * dec, 0.0)
        w = _strict_lower_inverse_minus_eye(a)
        rhs = jnp.concatenate([v * beta, kb * eg], -1)
        sol = rhs + _dot(w, rhs)
        s = s_ref[h]
        sbf = s.astype(BF16)
        u = sol[:, :DN_DV] - _dot(sol[:, DN_DV:].astype(BF16), sbf)
        ubf = u.astype(BF16)
        qk = _dot_nt(q.astype(BF16), kbf) * dec
        o = _dot((q * eg).astype(BF16), sbf) + _dot(qk.astype(BF16), ubf)
        kd = k * jnp.exp(gl - gam)
        s_ref[h] = s * jnp.exp(gl) + _dot_tn(kd.astype(BF16), ubf)
        o = o * lax.rsqrt(jnp.mean(o * o, -1, keepdims=True) + EPS) * on_ref[...]
        o_ref[:, h * DN_DV:(h + 1) * DN_DV] = (o * _silu(z_ref[:, h * DN_DV:(h + 1) * DN_DV])).astype(BF16)

    @pl.when(c_idx == pl.num_programs(1) - 1)
    def _():
        s_out_ref[0] = s_ref[...]


def _dn_prompt(p, batch, seq, conv_w, alog_row, dtb_row, onorm):
    c = CHUNK
    nc = seq // c
    return pl.pallas_call(
        _dn_kernel,
        grid=(batch, nc),
        in_specs=[pl.BlockSpec((c, DN_QKV), lambda b, i: (b * nc + i, C_AQKV // DN_QKV)),
                  pl.BlockSpec((c, BRANCH_W), lambda b, i: (b * nc + i, C_AZ // BRANCH_W)),
                  pl.BlockSpec((c, LANES), lambda b, i: (b * nc + i, C_SM // LANES)),
                  pl.BlockSpec((CONV_W, DN_QKV), lambda b, i: (0, 0)),
                  pl.BlockSpec((1, LANES), lambda b, i: (0, 0)),
                  pl.BlockSpec((1, LANES), lambda b, i: (0, 0)),
                  pl.BlockSpec((1, DN_DV), lambda b, i: (0, 0))],
        out_specs=[pl.BlockSpec((c, BRANCH_W), lambda b, i: (b * nc + i, 0)),
                   pl.BlockSpec((1, DN_HEADS, DN_DK, DN_DV), lambda b, i: (b, 0, 0, 0))],
        out_shape=[jax.ShapeDtypeStruct((batch * seq, BRANCH_W), BF16),
                   jax.ShapeDtypeStruct((batch, DN_HEADS, DN_DK, DN_DV), F32)],
        scratch_shapes=[pltpu.VMEM((DN_HEADS, DN_DK, DN_DV), F32),
                        pltpu.VMEM((c + 8, DN_QKV), F32)],
        compiler_params=_cparams("parallel", "arbitrary"),
    )(p, p, p, conv_w, alog_row, dtb_row, onorm)


EXP_CAP = 80.0


def _gla_kernel(q_ref, k_ref, v_ref, r_ref, sm_ref, w2_ref, b_ref, on_ref,
                o_ref, s_out_ref, st_ref):
    c_idx = pl.program_id(1)
    c = CHUNK
    hw = GLA_HEADS * GLA_DK

    @pl.when(c_idx == 0)
    def _():
        st_ref[...] = jnp.zeros_like(st_ref)

    lg = jax.nn.log_sigmoid(_dot(sm_ref[...].astype(BF16), w2_ref[...]) + b_ref[...]) / GLA_TAU
    row, col = _iota((c, c), 0), _iota((c, c), 1)
    gam = _dot((row >= col).astype(F32), lg, HIGHEST)
    gl = gam[c - 1:c, :]
    qs = q_ref[...] * GLA_DK ** -0.5
    k = k_ref[...]
    qg = (qs * jnp.exp(gam)).astype(BF16)
    kd = (k * jnp.exp(gl - gam)).astype(BF16)
    egl = jnp.exp(gl)

    att_rows = [[] for _ in range(GLA_HEADS)]
    for i in range(c // SUB):
        lo, hi = i * SUB, (i + 1) * SUB
        ref_pt = gam[lo - 1:lo, :] if i > 0 else jnp.zeros((1, hw), F32)
        qi = (qs[lo:hi] * jnp.exp(gam[lo:hi] - ref_pt)).astype(BF16)
        ki = (k[:hi] * jnp.exp(jnp.minimum(ref_pt - gam[:hi], EXP_CAP))).astype(BF16)
        keep = (_iota((SUB, hi), 0) + lo) >= _iota((SUB, hi), 1)
        for h in range(GLA_HEADS):
            sl = slice(h * GLA_DK, (h + 1) * GLA_DK)
            att = jnp.where(keep, _dot_nt(qi[:, sl], ki[:, sl]), 0.0)
            att_rows[h].append(_dot(att.astype(BF16), v_ref[:hi, h * GLA_DV:(h + 1) * GLA_DV].astype(BF16)))

    for h in range(GLA_HEADS):
        sl = slice(h * GLA_DK, (h + 1) * GLA_DK)
        vh = v_ref[:, h * GLA_DV:(h + 1) * GLA_DV].astype(BF16)
        st = st_ref[h]
        o = _dot_nt(qg[:, sl], st.astype(BF16)) + jnp.concatenate(att_rows[h], 0)
        st_ref[h] = st * egl[:, sl] + _dot_tn(vh, kd[:, sl])
        o = o * lax.rsqrt(jnp.mean(o * o, -1, keepdims=True) + EPS) * on_ref[...]
        o_ref[:, h * GLA_DV:(h + 1) * GLA_DV] = (o * _silu(r_ref[:, h * GLA_DV:(h + 1) * GLA_DV])).astype(BF16)

    @pl.when(c_idx == pl.num_programs(1) - 1)
    def _():
        s_out_ref[0] = st_ref[...]


def _gla_prompt(p, batch, seq, w2pad, gla_b, onorm):
    c = CHUNK
    nc = seq // c
    hw = GLA_HEADS * GLA_DK
    return pl.pallas_call(
        _gla_kernel,
        grid=(batch, nc),
        in_specs=[pl.BlockSpec((c, hw), lambda b, i: (b * nc + i, C_BQ // hw)),
                  pl.BlockSpec((c, hw), lambda b, i: (b * nc + i, C_BK // hw)),
                  pl.BlockSpec((c, BRANCH_W), lambda b, i: (b * nc + i, C_BV // BRANCH_W)),
                  pl.BlockSpec((c, BRANCH_W), lambda b, i: (b * nc + i, C_BR // BRANCH_W)),
                  pl.BlockSpec((c, LANES), lambda b, i: (b * nc + i, C_SM // LANES)),
                  pl.BlockSpec((LANES, hw), lambda b, i: (0, 0)),
                  pl.BlockSpec((1, hw), lambda b, i: (0, 0)),
                  pl.BlockSpec((1, GLA_DV), lambda b, i: (0, 0))],
        out_specs=[pl.BlockSpec((c, BRANCH_W), lambda b, i: (b * nc + i, 0)),
                   pl.BlockSpec((1, GLA_HEADS, GLA_DV, GLA_DK), lambda b, i: (b, 0, 0, 0))],
        out_shape=[jax.ShapeDtypeStruct((batch * seq, BRANCH_W), BF16),
                   jax.ShapeDtypeStruct((batch, GLA_HEADS, GLA_DV, GLA_DK), F32)],
        scratch_shapes=[pltpu.VMEM((GLA_HEADS, GLA_DV, GLA_DK), F32)],
        compiler_params=_cparams("parallel", "arbitrary"),
    )(p, p, p, p, p, w2pad, gla_b, onorm)


def _half_rms(x, g2):
    lane = _iota(x.shape, 1)
    first = lane < SWA_HD
    sq = x * x
    s0 = jnp.sum(jnp.where(first, sq, 0.0), -1, keepdims=True)
    s1 = jnp.sum(jnp.where(first, 0.0, sq), -1, keepdims=True)
    ms = jnp.where(first, s0, s1) * (1.0 / SWA_HD)
    return x * lax.rsqrt(ms + EPS) * g2


def _swa_kernel(q_ref, kp_ref, kc_ref, vp_ref, vc_ref, qn_ref, kn_ref, snk_ref,
                o_ref, knew_ref):
    i = pl.program_id(1)
    w = WINDOW
    kc = _half_rms(kc_ref[...], kn_ref[...])
    kk = jnp.concatenate([_half_rms(kp_ref[...], kn_ref[...]), kc], 0).astype(BF16)
    vv = jnp.concatenate([vp_ref[...], vc_ref[...]], 0).astype(BF16)
    t = _iota((w, 2 * w), 0)
    j = _iota((w, 2 * w), 1)
    dist = w + t - j
    valid = (dist >= 0) & (dist <= w) & ((j >= w) | (i > 0))
    distf = dist.astype(F32)
    g = SWA_HEADS // SWA_KV
    for h in range(SWA_HEADS):
        kv = h // g
        q = q_ref[:, h * SWA_HD:(h + 1) * SWA_HD]
        q = q * lax.rsqrt(jnp.mean(q * q, -1, keepdims=True) + EPS) * qn_ref[...] * SWA_HD ** -0.5
        s = _dot_nt(q.astype(BF16), kk[:, kv * SWA_HD:(kv + 1) * SWA_HD])
        s = s - (2.0 ** (-8.0 * (h + 1) / SWA_HEADS)) * distf
        s = jnp.where(valid, s, -jnp.inf)
        snk = snk_ref[:, h:h + 1]
        m = jnp.maximum(jnp.max(s, -1, keepdims=True), snk)
        pr = jnp.exp(s - m)
        pr = pr / (jnp.sum(pr, -1, keepdims=True) + jnp.exp(snk - m))
        o = _dot(pr.astype(BF16), vv[:, kv * SWA_HD:(kv + 1) * SWA_HD])
        o_ref[:, h * SWA_HD:(h + 1) * SWA_HD] = o.astype(BF16)

    @pl.when(i == pl.num_programs(1) - 1)
    def _():
        knew_ref[0] = kc


def _swa_prompt(p, batch, seq, qn, kn2, sinks):
    w = WINDOW
    nw = seq // w
    kvw = SWA_KV * SWA_HD
    return pl.pallas_call(
        _swa_kernel,
        grid=(batch, nw),
        in_specs=[pl.BlockSpec((w, BRANCH_W), lambda b, i: (b * nw + i, C_CQ // BRANCH_W)),
                  pl.BlockSpec((w, kvw), lambda b, i: (b * nw + jnp.maximum(i - 1, 0), C_CK // kvw)),
                  pl.BlockSpec((w, kvw), lambda b, i: (b * nw + i, C_CK // kvw)),
                  pl.BlockSpec((w, kvw), lambda b, i: (b * nw + jnp.maximum(i - 1, 0), C_CV // kvw)),
                  pl.BlockSpec((w, kvw), lambda b, i: (b * nw + i, C_CV // kvw)),
                  pl.BlockSpec((1, SWA_HD), lambda b, i: (0, 0)),
                  pl.BlockSpec((1, kvw), lambda b, i: (0, 0)),
                  pl.BlockSpec((1, SWA_HEADS), lambda b, i: (0, 0))],
        out_specs=[pl.BlockSpec((w, BRANCH_W), lambda b, i: (b * nw + i, 0)),
                   pl.BlockSpec((1, w, kvw), lambda b, i: (b, 0, 0))],
        out_shape=[jax.ShapeDtypeStruct((batch * seq, BRANCH_W), BF16),
                   jax.ShapeDtypeStruct((batch, w, kvw), F32)],
        compiler_params=_cparams("parallel", "arbitrary"),
    )(p, p, p, p, p, qn, kn2, sinks)


def _dn_mb_kernel(qkv_ref, z_ref, sm_ref, cw_ref, alog_ref, dtb_ref, on_ref,
                  o_ref, s_out_ref, s_ref, xp_ref, y_ref):
    c_idx = pl.program_id(0)
    nb = qkv_ref.shape[0]
    c = CHUNK
    pad = 8

    @pl.when(c_idx == 0)
    def _():
        s_ref[...] = jnp.zeros_like(s_ref)
        xp_ref[:, 0:pad, :] = jnp.zeros((nb, pad, DN_QKV), F32)

    row, col = _iota((c, c), 0), _iota((c, c), 1)
    incl = row >= col
    strict = row > col
    incl_f = incl.astype(F32)
    hk = DN_HEADS * DN_DK
    gam_all, gam_t, beta_all = [], [], []
    for b in range(nb):
        xp_ref[b, pad:pad + c, :] = qkv_ref[b]
        acc = xp_ref[b, pad - 3:pad - 3 + c, :] * cw_ref[0:1, :]
        for j in range(1, CONV_W):
            acc = acc + xp_ref[b, pad - 3 + j:pad - 3 + j + c, :] * cw_ref[j:j + 1, :]
        y_ref[b] = _silu(acc)
        xp_ref[b, pad - 3:pad, :] = xp_ref[b, pad + c - 3:pad + c, :]
        sm = sm_ref[b]
        beta_all.append(jax.nn.sigmoid(sm))
        g_all = -jnp.exp(alog_ref[...]) * jax.nn.softplus(sm + dtb_ref[...])
        gam_all.append(_dot(incl_f, g_all, HIGHEST))
        gam_t.append(gam_all[b].T)

    chains = [(b, h) for b in range(nb) for h in range(DN_HEADS)]
    n = len(chains)
    q, k, dec, eg, gl, gam, kb, rhs, a = ([None] * n for _ in range(9))
    for i, (b, h) in enumerate(chains):
        qi = y_ref[b, :, h * DN_DK:(h + 1) * DN_DK]
        ki = y_ref[b, :, hk + h * DN_DK:hk + (h + 1) * DN_DK]
        vi = y_ref[b, :, 2 * hk + h * DN_DV:2 * hk + (h + 1) * DN_DV]
        q[i] = qi * lax.rsqrt(jnp.sum(qi * qi, -1, keepdims=True) + EPS) * DN_DK ** -0.5
        k[i] = ki * lax.rsqrt(jnp.sum(ki * ki, -1, keepdims=True) + EPS)
        beta = beta_all[b][:, SM_BETA + h:SM_BETA + h + 1]
        gam[i] = gam_all[b][:, SM_G + h:SM_G + h + 1]
        gam_row = gam_t[b][SM_G + h:SM_G + h + 1, :]
        dec[i] = jnp.where(incl, jnp.exp(jnp.minimum(gam[i] - gam_row, 0.0)), 0.0)
        eg[i] = jnp.exp(gam[i])
        gl[i] = gam[i][c - 1:c, :]
        kb[i] = k[i] * beta
        rhs[i] = jnp.concatenate([vi * beta, kb[i] * eg[i]], -1)
    kbf = [x.astype(BF16) for x in k]
    kk = [_dot_nt(kb[i].astype(BF16), kbf[i]) for i in range(n)]
    qk = [_dot_nt(q[i].astype(BF16), kbf[i]) for i in range(n)]
    a = [jnp.where(strict, kk[i] * dec[i], 0.0) for i in range(n)]
    w = _strict_lower_inverse_minus_eye_multi(a)
    sol = [rhs[i] + _dot(w[i], rhs[i]) for i in range(n)]
    s = [s_ref[b, h] for (b, h) in chains]
    sbf = [x.astype(BF16) for x in s]
    u = [sol[i][:, :DN_DV] - _dot(sol[i][:, DN_DV:].astype(BF16), sbf[i]) for i in range(n)]
    ubf = [x.astype(BF16) for x in u]
    o_s = [_dot((q[i] * eg[i]).astype(BF16), sbf[i]) for i in range(n)]
    o_u = [_dot((qk[i] * dec[i]).astype(BF16), ubf[i]) for i in range(n)]
    ds = [_dot_tn((k[i] * jnp.exp(gl[i] - gam[i])).astype(BF16), ubf[i]) for i in range(n)]
    for i, (b, h) in enumerate(chains):
        s_ref[b, h] = s[i] * jnp.exp(gl[i]) + ds[i]
        o = o_s[i] + o_u[i]
        o = o * lax.rsqrt(jnp.mean(o * o, -1, keepdims=True) + EPS) * on_ref[...]
        o_ref[b, :, h * DN_DV:(h + 1) * DN_DV] = (
            o * _silu(z_ref[b, :, h * DN_DV:(h + 1) * DN_DV])).astype(BF16)

    @pl.when(c_idx == pl.num_programs(0) - 1)
    def _():
        s_out_ref[...] = s_ref[...]


def _dn_prompt_mb(p3, conv_w, alog_row, dtb_row, onorm):
    batch, seq, _ = p3.shape
    c = CHUNK
    full = lambda shape: pl.BlockSpec(shape, lambda i: tuple(0 for _ in shape))
    return pl.pallas_call(
        _dn_mb_kernel,
        grid=(seq // c,),
        in_specs=[pl.BlockSpec((batch, c, DN_QKV), lambda i: (0, i, C_AQKV // DN_QKV)),
                  pl.BlockSpec((batch, c, BRANCH_W), lambda i: (0, i, C_AZ // BRANCH_W)),
                  pl.BlockSpec((batch, c, LANES), lambda i: (0, i, C_SM // LANES)),
                  full((CONV_W, DN_QKV)), full((1, LANES)), full((1, LANES)), full((1, DN_DV))],
        out_specs=[pl.BlockSpec((batch, c, BRANCH_W), lambda i: (0, i, 0)),
                   full((batch, DN_HEADS, DN_DK, DN_DV))],
        out_shape=[jax.ShapeDtypeStruct((batch, seq, BRANCH_W), BF16),
                   jax.ShapeDtypeStruct((batch, DN_HEADS, DN_DK, DN_DV), F32)],
        scratch_shapes=[pltpu.VMEM((batch, DN_HEADS, DN_DK, DN_DV), F32),
                        pltpu.VMEM((batch, c + 8, DN_QKV), F32),
                        pltpu.VMEM((batch, c, DN_QKV), F32)],
        compiler_params=_cparams("arbitrary"),
    )(p3, p3, p3, conv_w, alog_row, dtb_row, onorm)


def _gla_mb_kernel(q_ref, k_ref, v_ref, r_ref, sm_ref, w2_ref, b_ref, on_ref,
                   o_ref, s_out_ref, st_ref):
    c_idx = pl.program_id(0)
    nb = q_ref.shape[0]
    c = CHUNK
    hw = GLA_HEADS * GLA_DK
    hv = GLA_HEADS * GLA_DV

    @pl.when(c_idx == 0)
    def _():
        st_ref[...] = jnp.zeros_like(st_ref)

    incl_f = (_iota((c, c), 0) >= _iota((c, c), 1)).astype(F32)
    blk = (_iota((hv, hw), 0) // GLA_DV) == (_iota((hv, hw), 1) // GLA_DK)
    qsel = (_iota((GLA_HEADS * SUB, hw), 0) // SUB) == (_iota((GLA_HEADS * SUB, hw), 1) // GLA_DK)
    nbr = range(nb)
    lg = [jax.nn.log_sigmoid(_dot(sm_ref[b].astype(BF16), w2_ref[...]) + b_ref[...]) / GLA_TAU for b in nbr]
    gam = [_dot(incl_f, lg[b], HIGHEST) for b in nbr]
    gl = [gam[b][c - 1:c, :] for b in nbr]
    qs = [q_ref[b] * GLA_DK ** -0.5 for b in nbr]
    k = [k_ref[b] for b in nbr]
    qg = [(qs[b] * jnp.exp(gam[b])).astype(BF16) for b in nbr]
    kd = [(k[b] * jnp.exp(gl[b] - gam[b])).astype(BF16) for b in nbr]
    vbf = [v_ref[b].astype(BF16) for b in nbr]
    st = [st_ref[b] for b in nbr]
    o_inter = [_dot_nt(qg[b], st[b].astype(BF16)) for b in nbr]
    dst = [_dot_tn(vbf[b], kd[b]) for b in nbr]
    for b in nbr:
        st_ref[b] = st[b] * jnp.exp(gl[b]) + jnp.where(blk, dst[b], 0.0)

    res = [[] for _ in nbr]
    for i in range(c // SUB):
        lo, hi = i * SUB, (i + 1) * SUB
        keep = (_iota((GLA_HEADS * SUB, hi), 0) % SUB + lo) >= _iota((GLA_HEADS * SUB, hi), 1)
        qm, ki = [], []
        for b in nbr:
            ref_pt = gam[b][lo - 1:lo, :] if i > 0 else jnp.zeros((1, hw), F32)
            qi = qs[b][lo:hi] * jnp.exp(gam[b][lo:hi] - ref_pt)
            qm.append(jnp.where(qsel, jnp.concatenate([qi] * GLA_HEADS, 0), 0.0).astype(BF16))
            ki.append((k[b][:hi] * jnp.exp(jnp.minimum(ref_pt - gam[b][:hi], EXP_CAP))).astype(BF16))
        att = [_dot_nt(qm[b], ki[b]) for b in nbr]
        att = [jnp.where(keep, att[b], 0.0).astype(BF16) for b in nbr]
        for b in nbr:
            res[b].append(_dot(att[b], vbf[b][:hi]))
    for b in nbr:
        for h in range(GLA_HEADS):
            vs = slice(h * GLA_DV, (h + 1) * GLA_DV)
            o = o_inter[b][:, vs] + jnp.concatenate([r[h * SUB:(h + 1) * SUB, vs] for r in res[b]], 0)
            o = o * lax.rsqrt(jnp.mean(o * o, -1, keepdims=True) + EPS) * on_ref[...]
            o_ref[b, :, vs] = (o * _silu(r_ref[b, :, vs])).astype(BF16)

    @pl.when(c_idx == pl.num_programs(0) - 1)
    def _():
        for b in range(nb):
            for h in range(GLA_HEADS):
                s_out_ref[b, h] = st_ref[b, h * GLA_DV:(h + 1) * GLA_DV, h * GLA_DK:(h + 1) * GLA_DK]


def _gla_prompt_mb(p3, w2pad, gla_b, onorm):
    batch, seq, _ = p3.shape
    c = CHUNK
    hw = GLA_HEADS * GLA_DK
    hv = GLA_HEADS * GLA_DV
    full = lambda shape: pl.BlockSpec(shape, lambda i: tuple(0 for _ in shape))
    return pl.pallas_call(
        _gla_mb_kernel,
        grid=(seq // c,),
        in_specs=[pl.BlockSpec((batch, c, hw), lambda i: (0, i, C_BQ // hw)),
                  pl.BlockSpec((batch, c, hw), lambda i: (0, i, C_BK // hw)),
                  pl.BlockSpec((batch, c, hv), lambda i: (0, i, C_BV // hv)),
                  pl.BlockSpec((batch, c, hv), lambda i: (0, i, C_BR // hv)),
                  pl.BlockSpec((batch, c, LANES), lambda i: (0, i, C_SM // LANES)),
                  full((LANES, hw)), full((1, hw)), full((1, GLA_DV))],
        out_specs=[pl.BlockSpec((batch, c, hv), lambda i: (0, i, 0)),
                   full((batch, GLA_HEADS, GLA_DV, GLA_DK))],
        out_shape=[jax.ShapeDtypeStruct((batch, seq, hv), BF16),
                   jax.ShapeDtypeStruct((batch, GLA_HEADS, GLA_DV, GLA_DK), F32)],
        scratch_shapes=[pltpu.VMEM((batch, hv, hw), F32)],
        compiler_params=_cparams("arbitrary"),
    )(p3, p3, p3, p3, p3, w2pad, gla_b, onorm)


def _swa_mb_kernel(q_ref, kp_ref, kc_ref, vp_ref, vc_ref, qn_ref, kn_ref, snk_ref,
                   o_ref, knew_ref):
    i = pl.program_id(0)
    nb = q_ref.shape[0]
    w = WINDOW
    g = SWA_HEADS // SWA_KV
    nr = g * w
    first = _iota((w, 2 * SWA_HD), 1) < SWA_HD
    first2 = _iota((2 * w, 2 * SWA_HD), 1) < SWA_HD
    t = _iota((nr, 2 * w), 0) % w
    j = _iota((nr, 2 * w), 1)
    dist = w + t - j
    valid = (dist >= 0) & (dist <= w) & ((j >= w) | (i > 0))
    distf = dist.astype(F32)
    hrow = _iota((nr, 1), 0) // w
    slope, snk = [], []
    for kv in range(SWA_KV):
        sl = jnp.zeros((nr, 1), F32)
        sk = jnp.zeros((nr, 1), F32)
        for hh in range(g):
            h = kv * g + hh
            sl = jnp.where(hrow == hh, 2.0 ** (-8.0 * (h + 1) / SWA_HEADS), sl)
            sk = jnp.where(hrow == hh, snk_ref[:, h:h + 1], sk)
        slope.append(sl * distf)
        snk.append(sk)

    units = [(b, kv) for b in range(nb) for kv in range(SWA_KV)]
    kcs, k2, v2, qx = [], [], [], []
    for b in range(nb):
        kc = _half_rms(kc_ref[b], kn_ref[...])
        kcs.append(kc)
        kk = jnp.concatenate([_half_rms(kp_ref[b], kn_ref[...]), kc], 0)
        vv = jnp.concatenate([vp_ref[b], vc_ref[b]], 0)
        kk_sw = pltpu.roll(kk, SWA_HD, axis=1)
        vv_sw = pltpu.roll(vv, SWA_HD, axis=1)
        k2 += [jnp.where(first2, kk, kk_sw).astype(BF16), jnp.where(first2, kk_sw, kk).astype(BF16)]
        v2 += [jnp.where(first2, vv, vv_sw).astype(BF16), jnp.where(first2, vv_sw, vv).astype(BF16)]
        for kv in range(SWA_KV):
            rows = []
            for jj in range(g // 2):
                grp = kv * (g // 2) + jj
                qg = _half_rms(q_ref[b, :, grp * 2 * SWA_HD:(grp + 1) * 2 * SWA_HD], qn_ref[...]) * SWA_HD ** -0.5
                rows += [jnp.where(first, qg, 0.0), jnp.where(first, 0.0, qg)]
            qx.append(jnp.concatenate(rows, 0).astype(BF16))
    nu = range(len(units))
    s = [_dot_nt(qx[u], k2[u]) for u in nu]
    s = [jnp.where(valid, s[u] - slope[units[u][1]], -jnp.inf) for u in nu]
    m = [jnp.maximum(jnp.max(s[u], -1, keepdims=True), snk[units[u][1]]) for u in nu]
    pr = [jnp.exp(s[u] - m[u]) for u in nu]
    den = [jnp.sum(pr[u], -1, keepdims=True) + jnp.exp(snk[units[u][1]] - m[u]) for u in nu]
    pr = [(pr[u] * (1.0 / den[u])).astype(BF16) for u in nu]
    o = [_dot(pr[u], v2[u]) for u in nu]
    for u, (b, kv) in enumerate(units):
        for jj in range(g // 2):
            grp = kv * (g // 2) + jj
            o_ref[b, :, grp * 2 * SWA_HD:(grp + 1) * 2 * SWA_HD] = jnp.where(
                first, o[u][(2 * jj) * w:(2 * jj + 1) * w], o[u][(2 * jj + 1) * w:(2 * jj + 2) * w]).astype(BF16)

    @pl.when(i == pl.num_programs(0) - 1)
    def _():
        for b in range(nb):
            knew_ref[b] = kcs[b]


def _swa_prompt_mb(p3, qn2, kn2, sinks):
    batch, seq, _ = p3.shape
    w = WINDOW
    kvw = SWA_KV * SWA_HD
    full = lambda shape: pl.BlockSpec(shape, lambda i: tuple(0 for _ in shape))
    prev = lambda i: jnp.maximum(i - 1, 0)
    return pl.pallas_call(
        _swa_mb_kernel,
        grid=(seq // w,),
        in_specs=[pl.BlockSpec((batch, w, BRANCH_W), lambda i: (0, i, C_CQ // BRANCH_W)),
                  pl.BlockSpec((batch, w, kvw), lambda i: (0, prev(i), C_CK // kvw)),
                  pl.BlockSpec((batch, w, kvw), lambda i: (0, i, C_CK // kvw)),
                  pl.BlockSpec((batch, w, kvw), lambda i: (0, prev(i), C_CV // kvw)),
                  pl.BlockSpec((batch, w, kvw), lambda i: (0, i, C_CV // kvw)),
                  full((1, kvw)), full((1, kvw)), full((1, SWA_HEADS))],
        out_specs=[pl.BlockSpec((batch, w, BRANCH_W), lambda i: (0, i, 0)),
                   full((batch, w, kvw))],
        out_shape=[jax.ShapeDtypeStruct((batch, seq, BRANCH_W), BF16),
                   jax.ShapeDtypeStruct((batch, w, kvw), F32)],
        compiler_params=_cparams("arbitrary"),
    )(p3, p3, p3, p3, p3, qn2, kn2, sinks)


DEC_TILE = 8


def _dec_kernel(qkv_ref, z_ref, sm_ref, bq_ref, bk_ref, bv_ref, br_ref, cq_ref, ck_ref, cv_ref,
                sdn_ref, buf_ref, sgl_ref, kc_ref, vc_ref,
                cw_ref, alog_ref, dtb_ref, dnon_ref, w2_ref, glb_ref, glon_ref,
                qn_ref, kn_ref, snk_ref, slp_ref,
                oa_ref, ob_ref, oc_ref, sdn_out, buf_out, sgl_out, kc_out, vc_out):
    bt = DEC_TILE
    x = qkv_ref[...]
    buf = buf_ref[...]
    acc = x * cw_ref[CONV_W - 1:CONV_W, :]
    for j in range(CONV_W - 1):
        acc = acc + buf[:, j * DN_QKV:(j + 1) * DN_QKV] * cw_ref[j:j + 1, :]
    y = _silu(acc)
    buf_out[:, 0:(CONV_W - 2) * DN_QKV] = buf[:, DN_QKV:]
    buf_out[:, (CONV_W - 2) * DN_QKV:] = x

    sm = sm_ref[...]
    beta_all = jax.nn.sigmoid(sm)
    g_all = -jnp.exp(alog_ref[...]) * jax.nn.softplus(sm + dtb_ref[...])
    eye = (_iota((LANES, LANES), 0) == _iota((LANES, LANES), 1)).astype(F32)
    hk = DN_HEADS * DN_DK
    for h in range(DN_HEADS):
        q = y[:, h * DN_DK:(h + 1) * DN_DK]
        k = y[:, hk + h * DN_DK:hk + (h + 1) * DN_DK]
        v = y[:, 2 * hk + h * DN_DV:2 * hk + (h + 1) * DN_DV]
        q = q * lax.rsqrt(jnp.sum(q * q, -1, keepdims=True) + EPS) * DN_DK ** -0.5
        k = k * lax.rsqrt(jnp.sum(k * k, -1, keepdims=True) + EPS)
        beta = beta_all[:, SM_BETA + h:SM_BETA + h + 1]
        eg = jnp.exp(g_all[:, SM_G + h:SM_G + h + 1])
        kb = k * beta
        lhs = jnp.concatenate([kb * eg, q * eg], 0).astype(BF16)
        k_t = _dot_nt(eye, k)
        qk = jnp.sum(q * k, -1, keepdims=True)
        vb = v * beta
        o_rows = []
        for b in range(bt):
            s = sdn_ref[b, h]
            r = _dot(lhs, s.astype(BF16))
            u = vb[b:b + 1] - r[b:b + 1]
            o_rows.append(r[bt + b:bt + b + 1] + qk[b:b + 1] * u)
            sdn_out[b, h] = s * eg[b:b + 1] + k_t[:, b:b + 1] * u
        o = jnp.concatenate(o_rows, 0)
        o = o * lax.rsqrt(jnp.mean(o * o, -1, keepdims=True) + EPS) * dnon_ref[...]
        oa_ref[:, h * DN_DV:(h + 1) * DN_DV] = (o * _silu(z_ref[:, h * DN_DV:(h + 1) * DN_DV])).astype(BF16)

    lg = jax.nn.log_sigmoid(_dot(sm.astype(BF16), w2_ref[...]) + glb_ref[...]) / GLA_TAU
    elg = jnp.exp(lg)
    bq = bq_ref[...] * GLA_DK ** -0.5
    bk = bk_ref[...]
    qg = bq * elg
    lane = _iota((bt, LANES), 1)
    first = lane < GLA_DK
    rows_first = _iota((LANES, GLA_DV), 0) < GLA_DK
    for j in range(GLA_HEADS // 2):
        sl = slice(j * LANES, (j + 1) * LANES)
        qgj = qg[:, sl]
        lhs = jnp.concatenate([jnp.where(first, qgj, 0.0), jnp.where(first, 0.0, qgj)], 0).astype(BF16)
        cols = _dot_nt(eye, jnp.concatenate([elg[:, sl], bk[:, sl]], 0), HIGHEST)
        prod = bq[:, sl] * bk[:, sl]
        qk0 = jnp.sum(jnp.where(first, prod, 0.0), -1, keepdims=True)
        qk1 = jnp.sum(jnp.where(first, 0.0, prod), -1, keepdims=True)
        v0 = bv_ref[:, (2 * j) * GLA_DV:(2 * j + 1) * GLA_DV]
        v1 = bv_ref[:, (2 * j + 1) * GLA_DV:(2 * j + 2) * GLA_DV]
        o0, o1 = [], []
        for b in range(bt):
            s = sgl_ref[b, j]
            r = _dot(lhs, s.astype(BF16))
            o0.append(r[b:b + 1] + qk0[b:b + 1] * v0[b:b + 1])
            o1.append(r[bt + b:bt + b + 1] + qk1[b:b + 1] * v1[b:b + 1])
            vsel = jnp.where(rows_first, v0[b:b + 1], v1[b:b + 1])
            sgl_out[b, j] = s * cols[:, b:b + 1] + cols[:, bt + b:bt + b + 1] * vsel
        for hh, rows in ((2 * j, o0), (2 * j + 1, o1)):
            o = jnp.concatenate(rows, 0)
            o = o * lax.rsqrt(jnp.mean(o * o, -1, keepdims=True) + EPS) * glon_ref[...]
            ob_ref[:, hh * GLA_DV:(hh + 1) * GLA_DV] = (o * _silu(br_ref[:, hh * GLA_DV:(hh + 1) * GLA_DV])).astype(BF16)

    g = SWA_HEADS // SWA_KV
    nr = bt * SWA_HEADS
    cq = cq_ref[...]
    cq = cq * lax.rsqrt(jnp.mean(cq * cq, -1, keepdims=True) + EPS) * qn_ref[...] * SWA_HD ** -0.5
    head = _iota((nr, 2 * SWA_HD), 0) % SWA_HEADS
    in_half = (head // g) == (_iota((nr, 2 * SWA_HD), 1) // SWA_HD)
    qx = jnp.where(in_half, jnp.concatenate([cq, cq], -1), 0.0)
    knew = _half_rms(ck_ref[...], kn_ref[...])
    vnew = cv_ref[...]
    s_c, kn_rows, vn_rows = [], [], []
    for b in range(bt):
        s_c.append(_dot_nt(qx[b * SWA_HEADS:(b + 1) * SWA_HEADS].astype(BF16), kc_ref[b].astype(BF16)))
        kn_rows.append(jnp.broadcast_to(knew[b:b + 1], (SWA_HEADS, 2 * SWA_HD)))
        vn_rows.append(jnp.broadcast_to(vnew[b:b + 1], (SWA_HEADS, 2 * SWA_HD)))
    s_c = jnp.concatenate(s_c, 0)
    kn_x = jnp.concatenate(kn_rows, 0)
    vn_x = jnp.concatenate(vn_rows, 0)
    slopes = slp_ref[...]
    snk = snk_ref[...]
    dist = (WINDOW - _iota((nr, WINDOW), 1)).astype(F32)
    s_c = s_c - slopes * dist
    s_n = jnp.sum(qx * kn_x, -1, keepdims=True)
    m = jnp.maximum(jnp.maximum(jnp.max(s_c, -1, keepdims=True), s_n), snk)
    p_c = jnp.exp(s_c - m)
    p_n = jnp.exp(s_n - m)
    den = jnp.sum(p_c, -1, keepdims=True) + p_n + jnp.exp(snk - m)
    p_c = p_c / den
    p_n = p_n / den
    half_sel = (_iota((nr, SWA_HD), 0) % SWA_HEADS) < g
    for b in range(bt):
        rs = slice(b * SWA_HEADS, (b + 1) * SWA_HEADS)
        r = _dot(p_c[rs].astype(BF16), vc_ref[b].astype(BF16)) + p_n[rs] * vn_x[rs]
        oc_ref[rs, :] = jnp.where(half_sel[rs], r[:, :SWA_HD], r[:, SWA_HD:]).astype(BF16)
        kc_out[b, 0:WINDOW - 1, :] = kc_ref[b, 1:WINDOW, :]
        kc_out[b, WINDOW - 1:WINDOW, :] = knew[b:b + 1]
        vc_out[b, 0:WINDOW - 1, :] = vc_ref[b, 1:WINDOW, :]
        vc_out[b, WINDOW - 1:WINDOW, :] = vnew[b:b + 1]


def _dec_kernel_aliased(*refs):
    n_in, n_alias = 26, 5
    _dec_kernel(*refs[:n_in], *refs[n_in + n_alias:])


def _dec_mixers(layer, p, cq_r, states, prev_out, conv_w, alog_row, dtb_row, dn_on, w2pad, gla_b, gla_on,
                qn, kn2, snk_col, slp_col):
    n = p.shape[0]
    bt = DEC_TILE
    hw = GLA_HEADS * GLA_DK
    kvw = SWA_KV * SWA_HD
    nr = bt * SWA_HEADS

    def col(width, off):
        return pl.BlockSpec((bt, width), lambda i: (i, off // width))

    def full(shape):
        return pl.BlockSpec(shape, lambda i: tuple(0 for _ in shape))

    def slot(a):
        rest = a.shape[2:]
        return pl.BlockSpec((None, bt) + rest, lambda i: (layer, i) + tuple(0 for _ in rest))

    in_specs = [col(DN_QKV, C_AQKV), col(BRANCH_W, C_AZ), col(LANES, C_SM), col(hw, C_BQ), col(hw, C_BK),
                col(BRANCH_W, C_BV), col(BRANCH_W, C_BR),
                pl.BlockSpec((nr, SWA_HD), lambda i: (i, 0)),
                col(kvw, C_CK), col(kvw, C_CV)]
    in_specs += [slot(a) for a in states]
    in_specs += [full((CONV_W, DN_QKV)), full((1, LANES)), full((1, LANES)), full((1, DN_DV)),
                 full((LANES, hw)), full((1, hw)), full((1, GLA_DV)),
                 full((1, SWA_HD)), full((1, kvw)), full((nr, 1)), full((nr, 1))]
    args = [p, p, p, p, p, p, p, cq_r, p, p, *states,
            conv_w, alog_row, dtb_row, dn_on, w2pad, gla_b, gla_on, qn, kn2, snk_col, slp_col]
    body, aliases = _dec_kernel, {}
    if prev_out is not None:
        body = _dec_kernel_aliased
        aliases = {len(args) + k: 3 + k for k in range(len(prev_out))}
        in_specs += [pl.BlockSpec(memory_space=pl.ANY)] * len(prev_out)
        args += list(prev_out)
    return pl.pallas_call(
        body,
        grid=(n // bt,),
        in_specs=in_specs,
        out_specs=[pl.BlockSpec((bt, BRANCH_W), lambda i: (i, 0)),
                   pl.BlockSpec((bt, BRANCH_W), lambda i: (i, 0)),
                   pl.BlockSpec((nr, SWA_HD), lambda i: (i, 0))] + [slot(a) for a in states],
        out_shape=[jax.ShapeDtypeStruct((n, BRANCH_W), BF16),
                   jax.ShapeDtypeStruct((n, BRANCH_W), BF16),
                   jax.ShapeDtypeStruct((n * SWA_HEADS, SWA_HD), BF16)]
        + [jax.ShapeDtypeStruct(a.shape, F32) for a in states],
        input_output_aliases=aliases,
        compiler_params=_cparams("parallel"),
    )(*args)


def _merge_kernel(ba_ref, bb_ref, bc_ref, gate_ref, x_ref, gt_ref, sc_ref, sh_ref, ln_ref,
                  wb_ref, wo_ref, wr_ref, rb_ref, x1_ref, h2_ref, comb_ref):
    mix = None
    for n, br in enumerate((ba_ref, bb_ref, bc_ref)):
        up = _dot(br[...], wb_ref[n])
        term = jax.nn.sigmoid(gate_ref[:, n * D_MODEL:(n + 1) * D_MODEL]) * up
        mix = term if mix is None else mix + term
    x1 = x_ref[...] + gt_ref[0] * _dot(mix.astype(BF16), wo_ref[...])
    x1_ref[...] = x1
    h2 = x1 * lax.rsqrt(jnp.mean(x1 * x1, -1, keepdims=True) + EPS) * ln_ref[...]
    h2 = h2 * (1.0 + sc_ref[0]) + sh_ref[0]
    h2b = h2.astype(BF16)
    h2_ref[...] = h2b

    h2l = (h2 - h2b.astype(F32)).astype(BF16)
    hi_lo = _dot(h2b, wr_ref[...])
    logits = hi_lo[:, :LANES] + hi_lo[:, LANES:] + _dot(h2l, wr_ref[:, :LANES]) + rb_ref[...]
    lane = _iota(logits.shape, 1).astype(F32)
    big = float(LANES)
    lc = jnp.where(lane < N_GROUPS, logits, -jnp.inf)
    mc = jnp.max(lc, -1, keepdims=True)
    pg = 1.0 / jnp.sum(jnp.exp(lc - mc), -1, keepdims=True)
    grp = jnp.min(jnp.where(lc == mc, lane, big), -1, keepdims=True)
    lo = R_EXP + grp * EXP_PER_GROUP
    emask = (lane >= lo) & (lane < lo + EXP_PER_GROUP)
    le = jnp.where(emask, logits, -jnp.inf)
    pe = jnp.exp(le - jnp.max(le, -1, keepdims=True))
    pe = pe / jnp.sum(pe, -1, keepdims=True)
    v1 = jnp.max(pe, -1, keepdims=True)
    i1 = jnp.min(jnp.where(emask & (pe == v1), lane, big), -1, keepdims=True)
    pe2 = jnp.where(emask & (lane != i1), pe, -1.0)
    v2 = jnp.max(pe2, -1, keepdims=True)
    i2 = jnp.min(jnp.where(pe2 == v2, lane, big), -1, keepdims=True)
    tot = v1 + v2
    comb_ref[...] = jnp.where(lane == 0.0, i1 - R_EXP, jnp.where(lane == 1.0, i2 - R_EXP, jnp.where(
        lane == 2.0, pg * v1 / tot, jnp.where(lane == 3.0, pg * v2 / tot, 0.0))))


def _merge(layer, ba, bb, bc, p, x, gt, sc, sh, ln, wb, wo, wr, rb, tm, tokens_per_row):
    t, d = x.shape
    tok = lambda width: pl.BlockSpec((tm, width), lambda i: (i, 0))
    full = lambda shape: pl.BlockSpec((None,) + shape, lambda i: (layer,) + tuple(0 for _ in shape))
    return pl.pallas_call(
        _merge_kernel,
        grid=(t // tm,),
        in_specs=[tok(BRANCH_W), tok(BRANCH_W), tok(BRANCH_W),
                  pl.BlockSpec((tm, N_BRANCH * d), lambda i: (i, C_GATE)),
                  tok(d),
                  _mod_spec(gt, tm, tokens_per_row), _mod_spec(sc, tm, tokens_per_row),
                  _mod_spec(sh, tm, tokens_per_row),
                  full((1, d)), full((N_BRANCH, BRANCH_W, d)), full((d, d)), full((d, 2 * LANES)),
                  full((1, LANES))],
        out_specs=[tok(d), tok(d), tok(LANES)],
        out_shape=[jax.ShapeDtypeStruct((t, d), F32),
                   jax.ShapeDtypeStruct((t, d), BF16),
                   jax.ShapeDtypeStruct((t, LANES), F32)],
        compiler_params=_cparams("parallel"),
    )(ba, bb, bc, p, x, gt, sc, sh, ln, wb, wo, wr, rb)


def _moe_kernel(h_ref, comb_ref, x1_ref, gt_ref, w1_ref, w3_ref, w2_ref, o_ref, acc_ref):
    e = pl.program_id(1)

    @pl.when(e == 0)
    def _():
        acc_ref[...] = jnp.zeros_like(acc_ref)

    h = h_ref[...]
    he = _silu(_dot(h, w1_ref[0].astype(BF16))) * _dot(h, w3_ref[0].astype(BF16))
    ye = _dot(he.astype(BF16), w2_ref[0].astype(BF16))
    comb = comb_ref[...]
    ef = e.astype(F32)
    ce = (jnp.where(comb[:, 0:1] == ef, comb[:, 2:3], 0.0) + jnp.where(comb[:, 1:2] == ef, comb[:, 3:4], 0.0))
    acc_ref[...] += ce * ye

    @pl.when(e == pl.num_programs(1) - 1)
    def _():
        o_ref[...] = x1_ref[...] + gt_ref[0] * acc_ref[...]


def _moe(layer, h2, comb, x1, gt, w1, w3, w2, tm, tokens_per_row):
    t, d = x1.shape
    _, ne, _, de = w1.shape
    return pl.pallas_call(
        _moe_kernel,
        grid=(t // tm, ne),
        in_specs=[pl.BlockSpec((tm, d), lambda i, e: (i, 0)),
                  pl.BlockSpec((tm, LANES), lambda i, e: (i, 0)),
                  pl.BlockSpec((tm, d), lambda i, e: (i, 0)),
                  _mod_spec(gt, tm, tokens_per_row),
                  pl.BlockSpec((None, 1, d, de), lambda i, e: (layer, e, 0, 0)),
                  pl.BlockSpec((None, 1, d, de), lambda i, e: (layer, e, 0, 0)),
                  pl.BlockSpec((None, 1, de, d), lambda i, e: (layer, e, 0, 0))],
        out_specs=pl.BlockSpec((tm, d), lambda i, e: (i, 0)),
        out_shape=jax.ShapeDtypeStruct((t, d), F32),
        scratch_shapes=[pltpu.VMEM((tm, d), F32)],
        compiler_params=_cparams("parallel", "arbitrary"),
    )(h2, comb, x1, gt, w1, w3, w2)


MOE_TK = 512
MOE_TM = 512
ROW_CHUNK = 8
MOE_LB = 2 * MOE_TK + 256
MOE_CH = MOE_LB // ROW_CHUNK
MOE_XW = D_MODEL + LANES
assert 2 * MOE_TK + N_EXPERTS * (ROW_CHUNK - 1) <= MOE_LB - ROW_CHUNK


def _moe_sorted_tiles(t):
    rows = 2 * t + (t // MOE_TK) * N_EXPERTS * (ROW_CHUNK - 1) + N_EXPERTS * (MOE_TM - 1)
    return -(-rows // MOE_TM)


def _plan_kernel(r_ref, lp_ref, dc_ref, te_ref, tot_ref, base_ref):
    ph, t = pl.program_id(0), pl.program_id(1)
    tk = r_ref.shape[0]
    ntp = te_ref.shape[0]
    r = r_ref[...]
    lane = _iota((tk, LANES), 1).astype(F32)
    sel1, sel2 = lane == r[:, 0:1], lane == r[:, 1:2]
    oh = jnp.where(sel1 | sel2, 1.0, 0.0)
    cnt = jnp.sum(oh, 0, keepdims=True)
    c8 = jnp.floor((cnt + (ROW_CHUNK - 1)) * (1.0 / ROW_CHUNK)) * ROW_CHUNK
    upper = (_iota((LANES, LANES), 0) < _iota((LANES, LANES), 1)).astype(F32)

    def excl_cumsum(v):
        return _dot(jnp.broadcast_to(v, (8, LANES)), upper, HIGHEST)[0:1]

    @pl.when(ph == 0)
    def _():
        @pl.when(t == 0)
        def _():
            tot_ref[...] = jnp.zeros_like(tot_ref)

        tot_ref[...] += c8

    @pl.when(ph == 1)
    def _():
        @pl.when(t == 0)
        def _():
            tot = tot_ref[...]
            gp = jnp.floor((tot + (MOE_TM - 1)) * (1.0 / MOE_TM)) * MOE_TM
            off = excl_cumsum(gp)
            base_ref[...] = off
            end = off + gp
            lane_t = _iota((ntp, LANES), 1).astype(F32)
            start = _iota((ntp, 1), 0).astype(F32) * MOE_TM
            te = jnp.sum(jnp.where((lane_t < N_EXPERTS) & (end <= start), 1.0, 0.0), -1, keepdims=True)
            mine = lane_t == te
            filled = jnp.sum(jnp.where(mine, tot + off, 0.0), -1, keepdims=True)
            tv = jnp.clip(filled - start, 0.0, float(MOE_TM))
            n_used = jnp.sum(jnp.where(lane_t == N_EXPERTS - 1, end, 0.0), -1, keepdims=True) * (1.0 / MOE_TM)
            te_ref[...] = jnp.where(lane_t == 0.0, jnp.minimum(te, N_EXPERTS - 1.0),
                                    jnp.where(lane_t == 1.0, tv, jnp.where(lane_t == 2.0, n_used, 0.0))
                                    ).astype(jnp.int32)

        base = base_ref[...]
        lo = excl_cumsum(c8)
        below = (_iota((tk, tk), 0) > _iota((tk, tk), 1)).astype(BF16)
        p = _dot(below, oh.astype(BF16)) + lo
        lp1 = jnp.sum(jnp.where(sel1, p, 0.0), -1, keepdims=True)
        lp2 = jnp.sum(jnp.where(sel2, p, 0.0), -1, keepdims=True)
        lp_ref[...] = jnp.where(lane == 0.0, lp1, jnp.where(lane == 1.0, lp2, jnp.where(lane < 4.0, r, 0.0)))
        lane_c = _iota((MOE_CH, LANES), 1).astype(F32)
        cstart = _iota((MOE_CH, 1), 0).astype(F32) * ROW_CHUNK
        ej = jnp.sum(jnp.where((lane_c < N_EXPERTS) & (lo + c8 <= cstart), 1.0, 0.0), -1, keepdims=True)
        dj = jnp.sum(jnp.where(lane_c == ej, base - lo, 0.0), -1, keepdims=True) + cstart
        nrows = jnp.sum(c8, -1, keepdims=True)
        last = _iota((MOE_CH, 1), 0) == MOE_CH - 1
        dcv = jnp.where(last, nrows * (1.0 / ROW_CHUNK), jnp.where(cstart < nrows, dj, 0.0))
        dc_ref[...] = jnp.broadcast_to(dcv, (MOE_CH, LANES)).astype(jnp.int32)
        base_ref[...] = base + c8


def _moe_plan(route):
    t = route.shape[0]
    nt = t // MOE_TK
    ntp = -(-_moe_sorted_tiles(t) // 8) * 8
    return pl.pallas_call(
        _plan_kernel,
        grid=(2, nt),
        in_specs=[pl.BlockSpec((MOE_TK, LANES), lambda ph, i: (i, 0))],
        out_specs=[pl.BlockSpec((MOE_TK, LANES), lambda ph, i: (i * ph, 0)),
                   pl.BlockSpec((MOE_CH, LANES), lambda ph, i: (i * ph, 0)),
                   pl.BlockSpec((ntp, LANES), lambda ph, i: (0, 0))],
        out_shape=[jax.ShapeDtypeStruct((t, LANES), F32),
                   jax.ShapeDtypeStruct((nt * MOE_CH, LANES), jnp.int32),
                   jax.ShapeDtypeStruct((ntp, LANES), jnp.int32)],
        scratch_shapes=[pltpu.VMEM((1, LANES), F32), pltpu.VMEM((1, LANES), F32)],
        compiler_params=_cparams("arbitrary", "arbitrary"),
    )(route)


def _pick_onehot(lp):
    pos = _iota((lp.shape[0], MOE_LB), 1).astype(F32)
    return pos == lp[:, 0:1], pos == lp[:, 1:2]


def _chunk_loop(n, fn):
    def body(j, carry):
        fn(j)
        return carry

    lax.fori_loop(0, n, body, 0)


def _dispatch_kernel(dc_ref, h_ref, lp_ref, xs_ref, buf_ref, sem, nprev_ref):
    t, nt = pl.program_id(0), pl.num_programs(0)
    slot = t % 2
    tk = h_ref.shape[0]
    lp = lp_ref[...]
    oh1, oh2 = _pick_onehot(lp)
    lane = _iota((tk, LANES), 1)

    def split3(w):
        hi = w.astype(BF16).astype(F32)
        mid = (w - hi).astype(BF16).astype(F32)
        low = w - hi - mid
        return jnp.where(lane == 0, hi, jnp.where(lane == 1, mid, jnp.where(lane == 2, low, 0.0))).astype(BF16)

    b1, b2 = jnp.where(oh1, 1.0, 0.0).astype(BF16), jnp.where(oh2, 1.0, 0.0).astype(BF16)
    buf_ref[slot, :, 0:D_MODEL] = _dot_tn(b1 + b2, h_ref[...])
    buf_ref[slot, :, D_MODEL:] = _dot_tn(b1, split3(lp[:, 2:3])) + _dot_tn(b2, split3(lp[:, 3:4]))

    def chunk_copy(j, s):
        src = buf_ref.at[s, pl.ds(pl.multiple_of(j * ROW_CHUNK, ROW_CHUNK), ROW_CHUNK), :]
        dst = xs_ref.at[pl.ds(pl.multiple_of(dc_ref[0, 0, j], ROW_CHUNK), ROW_CHUNK), :]
        return pltpu.make_async_copy(src, dst, sem.at[s])

    nch = dc_ref[0, 0, MOE_CH - 1]
    _chunk_loop(nch, lambda j: chunk_copy(j, slot).start())

    @pl.when(t > 0)
    def _():
        _chunk_loop(nprev_ref[0], lambda j: chunk_copy(0, 1 - slot).wait())

    nprev_ref[0] = nch

    @pl.when(t == nt - 1)
    def _():
        _chunk_loop(nch, lambda j: chunk_copy(0, slot).wait())


def _moe_dispatch(dc3, h2, lp, n_rows):
    t, d = h2.shape
    nt = t // MOE_TK
    return pl.pallas_call(
        _dispatch_kernel,
        grid=(nt,),
        in_specs=[pl.BlockSpec((1, 1, MOE_CH), lambda i: (i, 0, 0), memory_space=pltpu.SMEM),
                  pl.BlockSpec((MOE_TK, d), lambda i: (i, 0)),
                  pl.BlockSpec((MOE_TK, LANES), lambda i: (i, 0))],
        out_specs=pl.BlockSpec(memory_space=pl.ANY),
        out_shape=jax.ShapeDtypeStruct((n_rows, MOE_XW), F32),
        scratch_shapes=[pltpu.VMEM((2, MOE_LB, MOE_XW), F32), pltpu.SemaphoreType.DMA((2,)),
                        pltpu.SMEM((1,), jnp.int32)],
        compiler_params=_cparams("arbitrary"),
    )(dc3, h2, lp)


def _experts_kernel(te_ref, tv_ref, nu_ref, x_ref, w1_ref, w3_ref, w2_ref, o_ref, w1b, w3b, w2b):
    i = pl.program_id(0)

    @pl.when(i < nu_ref[0])
    def _():
        @pl.when((i == 0) | (te_ref[i] != te_ref[jnp.maximum(i - 1, 0)]))
        def _():
            w1b[...] = w1_ref[0].astype(BF16)
            w3b[...] = w3_ref[0].astype(BF16)
            w2b[...] = w2_ref[0].astype(BF16)

        tm = x_ref.shape[0]
        valid = _iota((tm, 1), 0) < tv_ref[i]
        x = x_ref[...]
        h = jnp.where(valid, x[:, :D_MODEL], 0.0).astype(BF16)
        wv = x[:, D_MODEL:D_MODEL + 1] + x[:, D_MODEL + 1:D_MODEL + 2] + x[:, D_MODEL + 2:D_MODEL + 3]
        he = _silu(_dot(h, w1b[...])) * _dot(h, w3b[...])
        o_ref[...] = jnp.where(valid, wv, 0.0) * _dot(he.astype(BF16), w2b[...])


def _moe_experts(layer, te, tv, nu, xs, w1, w3, w2):
    n_rows = xs.shape[0]
    _, _, d, de = w1.shape
    cur = lambda i, te, tv, nu: jnp.minimum(i, nu[0] - 1)
    wspec = lambda a, b: pl.BlockSpec((None, 1, a, b), lambda i, te, tv, nu: (layer, te[cur(i, te, tv, nu)], 0, 0))
    return pl.pallas_call(
        _experts_kernel,
        grid_spec=pltpu.PrefetchScalarGridSpec(
            num_scalar_prefetch=3,
            grid=(n_rows // MOE_TM,),
            in_specs=[pl.BlockSpec((MOE_TM, MOE_XW), lambda i, te, tv, nu: (cur(i, te, tv, nu), 0)),
                      wspec(d, de), wspec(d, de), wspec(de, d)],
            out_specs=pl.BlockSpec((MOE_TM, d), lambda i, te, tv, nu: (cur(i, te, tv, nu), 0)),
            scratch_shapes=[pltpu.VMEM((d, de), BF16), pltpu.VMEM((d, de), BF16), pltpu.VMEM((de, d), BF16)]),
        out_shape=jax.ShapeDtypeStruct((n_rows, d), F32),
        compiler_params=_cparams("arbitrary"),
    )(te, tv, nu, xs, w1, w3, w2)


def _combine_kernel(dc_ref, dcn_ref, lp_ref, x1_ref, gt_ref, ys_ref, o_ref, buf_ref, sem):
    t, nt = pl.program_id(0), pl.num_programs(0)
    slot = t % 2

    def chunk_copy(tab, j, s):
        src = ys_ref.at[pl.ds(pl.multiple_of(tab[0, 0, j], ROW_CHUNK), ROW_CHUNK), :]
        dst = buf_ref.at[s, pl.ds(pl.multiple_of(j * ROW_CHUNK, ROW_CHUNK), ROW_CHUNK), :]
        return pltpu.make_async_copy(src, dst, sem.at[s])

    nch = dc_ref[0, 0, MOE_CH - 1]

    @pl.when(t == 0)
    def _():
        buf_ref[...] = jnp.zeros_like(buf_ref)
        _chunk_loop(nch, lambda j: chunk_copy(dc_ref, j, slot).start())

    @pl.when(t + 1 < nt)
    def _():
        _chunk_loop(dcn_ref[0, 0, MOE_CH - 1], lambda j: chunk_copy(dcn_ref, j, 1 - slot).start())

    _chunk_loop(nch, lambda j: chunk_copy(dc_ref, 0, slot).wait())
    live = _iota((MOE_LB, 1), 0) < nch * ROW_CHUNK
    local = jnp.where(live, buf_ref[slot], 0.0).astype(BF16)
    oh1, oh2 = _pick_onehot(lp_ref[...])
    perm = jnp.where(oh1 | oh2, 1.0, 0.0).astype(BF16)
    o_ref[...] = x1_ref[...] + gt_ref[0] * _dot(perm, local)


def _moe_combine(dc3, lp, x1, gt, ys, tokens_per_row):
    t, d = x1.shape
    nt = t // MOE_TK
    smem = lambda f: pl.BlockSpec((1, 1, MOE_CH), f, memory_space=pltpu.SMEM)
    return pl.pallas_call(
        _combine_kernel,
        grid=(nt,),
        in_specs=[smem(lambda i: (i, 0, 0)), smem(lambda i: (jnp.minimum(i + 1, nt - 1), 0, 0)),
                  pl.BlockSpec((MOE_TK, LANES), lambda i: (i, 0)),
                  pl.BlockSpec((MOE_TK, d), lambda i: (i, 0)),
                  _mod_spec(gt, MOE_TK, tokens_per_row),
                  pl.BlockSpec(memory_space=pl.ANY)],
        out_specs=pl.BlockSpec((MOE_TK, d), lambda i: (i, 0)),
        out_shape=jax.ShapeDtypeStruct((t, d), F32),
        scratch_shapes=[pltpu.VMEM((2, MOE_LB, d), F32), pltpu.SemaphoreType.DMA((2,))],
        compiler_params=_cparams("arbitrary"),
    )(dc3, dc3, lp, x1, gt, ys)


def _moe_sparse(layer, h2, route, x1, gt, w1, w3, w2, tokens_per_row):
    t = h2.shape[0]
    nt = t // MOE_TK
    lp, dc, tmeta = _moe_plan(route)
    dc3 = dc[:, 0].reshape(nt, 1, MOE_CH)
    n_tiles = _moe_sorted_tiles(t)
    xs = _moe_dispatch(dc3, h2, lp, n_tiles * MOE_TM)
    ys = _moe_experts(layer, tmeta[:n_tiles, 0], tmeta[:n_tiles, 1], tmeta[0:1, 2], xs, w1, w3, w2)
    return _moe_combine(dc3, lp, x1, gt, ys, tokens_per_row)


def _permute_w_in(w):
    sizes = (DN_QKV, DN_HEADS * DN_DV, DN_HEADS, DN_HEADS,
             GLA_HEADS * GLA_DK, GLA_HEADS * GLA_DK, GLA_HEADS * GLA_DV, GLA_HEADS * GLA_DV, GLA_RANK,
             SWA_HEADS * SWA_HD, SWA_KV * SWA_HD, SWA_KV * SWA_HD, N_BRANCH * D_MODEL)
    offs = [0]
    for s in sizes:
        offs.append(offs[-1] + s)
    seg = lambda i: w[..., offs[i]:offs[i + 1]]
    (a_qkv, a_z, a_b, a_a, b_q, b_k, b_v, b_r, b_lr, c_q, c_k, c_v, gate) = (seg(i) for i in range(len(sizes)))
    small = jnp.concatenate([a_b, a_a, b_lr], -1)
    fill = jnp.zeros(w.shape[:-1] + (P_PAD - C_SM - small.shape[-1],), w.dtype)
    return jnp.concatenate([gate, a_qkv, a_z, b_v, b_r, c_q, b_q, b_k, c_k, c_v, small, fill], -1)


def _lane_row(v, off):
    depth, n = v.shape
    return jnp.zeros((depth, 1, LANES), F32).at[:, 0, off:off + n].set(v.astype(F32))


def kernel(x_prompt, x_sample, c_prompt, c_sample, state_dn, state_dn_conv, state_gla, cache_swa_k, cache_swa_v, ln1_g, ln2_g, ada_w, ada_b, w_in, dn_conv_w, dn_a_log, dn_dt_bias, dn_onorm_g, gla_w2, gla_b, gla_onorm_g, swa_qnorm_g, swa_knorm_g, swa_sinks, w_branch, w_out, rc_w, rc_b, re_w, re_b, w1, w3, w2):
    batch, seq, d = x_prompt.shape
    nb = x_sample.shape[0]
    depth = w_in.shape[0]
    assert x_sample.shape[1] == 1 and d == D_MODEL and seq % CHUNK == 0 and nb % DEC_TILE == 0
    kvw = SWA_KV * SWA_HD

    w_in_p = _permute_w_in(w_in).astype(BF16)
    wb_b, wo_b = w_branch.astype(BF16), w_out.astype(BF16)
    wr = jnp.concatenate([rc_w, re_w, jnp.zeros((depth, d, LANES - N_GROUPS - N_EXPERTS), F32)], -1)
    rb = jnp.concatenate([rc_b, re_b, jnp.zeros((depth, LANES - N_GROUPS - N_EXPERTS), F32)], -1)[:, None, :]
    wr_hi = wr.astype(BF16)
    wr2 = jnp.concatenate([wr_hi, (wr - wr_hi.astype(F32)).astype(BF16)], -1)
    ln2_all = ln2_g[:, None, :]
    alog_row = _lane_row(dn_a_log, SM_G)
    dtb_row = _lane_row(dn_dt_bias, SM_G)
    w2pad = jnp.zeros((depth, LANES, GLA_HEADS * GLA_DK), F32).at[:, SM_LR:SM_LR + GLA_RANK].set(gla_w2).astype(BF16)
    kn2 = jnp.concatenate([swa_knorm_g] * SWA_KV, -1)[:, None, :]
    slopes = jnp.exp2(-8.0 * jnp.arange(1, SWA_HEADS + 1, dtype=F32) / SWA_HEADS)
    slp_col = jnp.tile(slopes, DEC_TILE)[:, None]

    pad_rows = (-(batch + nb)) % 8
    c_all = jnp.concatenate([c_prompt, c_sample, jnp.zeros((pad_rows, d), F32)], 0)
    mod = _ada(c_all, ada_w, ada_b)

    xp = x_prompt.reshape(batch * seq, d)
    xs = x_sample.reshape(nb, d)
    tm_p = 1024 if seq % 1024 == 0 else CHUNK
    tm_m = 512 if seq % 512 == 0 else CHUNK

    dec_states = (state_dn,
                  state_dn_conv.reshape(depth, nb, (CONV_W - 1) * DN_QKV),
                  state_gla.reshape(depth, nb, GLA_HEADS // 2, 2 * GLA_DK, GLA_DV),
                  cache_swa_k.reshape(depth, nb, WINDOW, kvw),
                  cache_swa_v.reshape(depth, nb, WINDOW, kvw))
    dec_out = None
    st_p = []
    for l in range(depth):
        mp = [m[:, None, :] for m in jnp.split(mod[l, :batch], 6, -1)]
        ms = [m[None] for m in jnp.split(mod[l, batch:batch + nb], 6, -1)]
        ln1, ln2 = ln1_g[l][None], ln2_g[l][None]
        conv_w = dn_conv_w[l]
        dn_on, gla_on = dn_onorm_g[l][None], gla_onorm_g[l][None]
        glb = gla_b[l][None]
        qn = swa_qnorm_g[l][None]
        snk = swa_sinks[l][None]

        pp = _in_proj(l, xp, mp[1], mp[0], ln1, w_in_p, tm_p, seq)
        pp3 = pp.reshape(batch, seq, P_PAD)
        ba, dn_s = _dn_prompt_mb(pp3, conv_w, alog_row[l], dtb_row[l], dn_on)
        bb, gla_st = _gla_prompt_mb(pp3, w2pad[l], glb, gla_on)
        bc, k_new = _swa_prompt_mb(pp3, jnp.concatenate([qn, qn], -1), kn2[l], snk)
        ba, bb, bc = (z.reshape(batch * seq, BRANCH_W) for z in (ba, bb, bc))
        x1, h2, comb = _merge(l, ba, bb, bc, pp, xp, mp[2], mp[4], mp[3], ln2_all, wb_b, wo_b, wr2, rb,
                              tm_m, seq)
        xp = _moe_sparse(l, h2, comb, x1, mp[5], w1, w3, w2, seq)
        pp3 = pp.reshape(batch, seq, P_PAD)
        st_p.append((dn_s,
                     pp3[:, seq - (CONV_W - 1):, C_AQKV:C_AQKV + DN_QKV],
                     jnp.swapaxes(gla_st, -1, -2),
                     k_new.reshape(batch, WINDOW, SWA_KV, SWA_HD),
                     pp3[:, seq - WINDOW:, C_CV:C_CV + kvw].reshape(batch, WINDOW, SWA_KV, SWA_HD)))

        ps = _in_proj(l, xs, ms[1], ms[0], ln1, w_in_p, nb, nb)
        cq_r = ps[:, C_CQ:C_CQ + SWA_HEADS * SWA_HD].reshape(nb * SWA_HEADS, SWA_HD)
        oa, ob, oc_r, *dec_out = _dec_mixers(
            l, ps, cq_r, dec_states, dec_out, conv_w, alog_row[l], dtb_row[l], dn_on, w2pad[l], glb, gla_on,
            qn, kn2[l], jnp.tile(swa_sinks[l], DEC_TILE)[:, None], slp_col)
        oc = oc_r.reshape(nb, SWA_HEADS * SWA_HD)
        x1, h2, comb = _merge(l, oa, ob, oc, ps, xs, ms[2], ms[4], ms[3], ln2_all, wb_b, wo_b, wr2, rb,
                              nb, nb)
        xs = _moe(l, h2, comb, x1, ms[5], w1, w3, w2, nb, nb)

    dn_p, conv_p, gla_p, k_p, v_p = [jnp.stack(z) for z in zip(*st_p)]
    dn_s = dec_out[0]
    conv_s = dec_out[1].reshape(depth, nb, CONV_W - 1, DN_QKV)
    gla_s = dec_out[2].reshape(depth, nb, GLA_HEADS, GLA_DK, GLA_DV)
    k_s = dec_out[3].reshape(depth, nb, WINDOW, SWA_KV, SWA_HD)
    v_s = dec_out[4].reshape(depth, nb, WINDOW, SWA_KV, SWA_HD)
    return (xp.reshape(batch, seq, d), xs.reshape(nb, 1, d), dn_p, dn_s, conv_p, conv_s, gla_p, gla_s,
            k_p, k_s, v_p, v_s)
```

```python
import functools

import jax
import jax.numpy as jnp
from jax import lax
from jax.experimental import pallas as pl
from jax.experimental.pallas import tpu as pltpu

F32 = jnp.float32
BF16 = jnp.bfloat16
HIGHEST = lax.Precision.HIGHEST

D_MODEL = 1024
DN_HEADS, DN_DK, DN_DV, CONV_W = 4, 128, 128, 4
DN_QKV = 2 * DN_HEADS * DN_DK + DN_HEADS * DN_DV
GLA_HEADS, GLA_DK, GLA_DV, GLA_RANK, GLA_TAU = 4, 64, 128, 16, 16.0
SWA_HEADS, SWA_KV, SWA_HD, WINDOW = 8, 2, 64, 128
N_BRANCH, BRANCH_W = 3, 512
N_GROUPS, EXP_PER_GROUP, TOP_K, D_EXPERT = 4, 8, 2, 256
N_EXPERTS = N_GROUPS * EXP_PER_GROUP
EPS = 1e-6

LANES = 128
CHUNK = 128
SUB = 16
VMEM_LIMIT = 56 * 1024 * 1024

C_GATE, C_AQKV, C_AZ, C_BV, C_BR, C_CQ = 0, 3072, 4608, 5120, 5632, 6144
C_BQ, C_BK, C_CK, C_CV, C_SM, P_PAD = 6656, 6912, 7168, 7296, 7424, 7680
SM_BETA, SM_G, SM_LR = 0, DN_HEADS, 2 * DN_HEADS
R_EXP = N_GROUPS


def _cparams(*sem):
    return pltpu.CompilerParams(dimension_semantics=sem, vmem_limit_bytes=VMEM_LIMIT)


def _dot(a, b, precision=None):
    return jnp.dot(a, b, preferred_element_type=F32, precision=precision)


def _dot_nt(a, b, precision=None):
    return lax.dot_general(a, b, (((1,), (1,)), ((), ())), preferred_element_type=F32, precision=precision)


def _dot_tn(a, b, precision=None):
    return lax.dot_general(a, b, (((0,), (0,)), ((), ())), preferred_element_type=F32, precision=precision)


def _silu(x):
    return x * jax.nn.sigmoid(x)


def _iota(shape, dim):
    return lax.broadcasted_iota(jnp.int32, shape, dim)


def _ada_kernel(c_ref, w_ref, b_ref, o_ref):
    c = _silu(c_ref[...]).astype(BF16)
    o_ref[0] = _dot(c, w_ref[0].astype(BF16)) + b_ref[0]


def _ada(c_all, ada_w, ada_b):
    depth, d, n = ada_w.shape
    rows = c_all.shape[0]
    tn = 1536
    return pl.pallas_call(
        _ada_kernel,
        grid=(depth, n // tn),
        in_specs=[pl.BlockSpec((rows, d), lambda l, j: (0, 0)),
                  pl.BlockSpec((1, d, tn), lambda l, j: (l, 0, j)),
                  pl.BlockSpec((1, 1, tn), lambda l, j: (l, 0, j))],
        out_specs=pl.BlockSpec((1, rows, tn), lambda l, j: (l, 0, j)),
        out_shape=jax.ShapeDtypeStruct((depth, rows, n), F32),
        compiler_params=_cparams("parallel", "parallel"),
    )(c_all, ada_w, ada_b.reshape(depth, 1, n))


def _mod_spec(mod, tm, tokens_per_row):
    _, r, d = mod.shape
    assert tokens_per_row % tm == 0
    per = tokens_per_row // tm
    return pl.BlockSpec((1, r, d), lambda i, *_: (i // per, 0, 0))


def _in_proj_kernel(x_ref, sc_ref, sh_ref, g_ref, w_ref, o_ref, h_ref):
    @pl.when(pl.program_id(1) == 0)
    def _():
        x = x_ref[...]
        y = x * lax.rsqrt(jnp.mean(x * x, -1, keepdims=True) + EPS) * g_ref[...]
        h_ref[...] = (y * (1.0 + sc_ref[0]) + sh_ref[0]).astype(BF16)

    o_ref[...] = _dot(h_ref[...], w_ref[...])


def _in_proj(layer, x, sc, sh, g, w, tm, tokens_per_row):
    t, d = x.shape
    n = w.shape[2]
    tn = 768 if tm >= 2048 else 1536
    return pl.pallas_call(
        _in_proj_kernel,
        grid=(t // tm, n // tn),
        in_specs=[pl.BlockSpec((tm, d), lambda i, j: (i, 0)),
                  _mod_spec(sc, tm, tokens_per_row), _mod_spec(sh, tm, tokens_per_row),
                  pl.BlockSpec((1, d), lambda i, j: (0, 0)),
                  pl.BlockSpec((None, d, tn), lambda i, j: (layer, 0, j))],
        out_specs=pl.BlockSpec((tm, tn), lambda i, j: (i, j)),
        out_shape=jax.ShapeDtypeStruct((t, n), F32),
        scratch_shapes=[pltpu.VMEM((tm, d), BF16)],
        compiler_params=_cparams("parallel", "arbitrary"),
    )(x, sc, sh, g, w)


def _strict_lower_inverse_minus_eye(a):
    c = a.shape[0]
    row, col = _iota((c, c), 0), _iota((c, c), 1)
    diag_blk = (row // SUB) == (col // SUB)
    ad = jnp.where(diag_blk, a, 0.0)
    ao = a - ad
    n = -ad
    p = n
    steps = SUB.bit_length() - 2
    for _ in range(steps):
        p = _dot(p, p)
        n = n + p + _dot(n, p)
    bm = -(ao + _dot(n, ao))
    m = bm
    q = bm
    steps = (c // SUB).bit_length() - 2
    for _ in range(steps):
        q = _dot(q, q)
        m = m + q + _dot(m, q)
    return m + n + _dot(m, n)


def _strict_lower_inverse_minus_eye_multi(a_list):
    c = a_list[0].shape[0]
    diag_blk = (_iota((c, c), 0) // SUB) == (_iota((c, c), 1) // SUB)
    idx = range(len(a_list))
    ad = [jnp.where(diag_blk, a, 0.0) for a in a_list]
    ao = [a_list[i] - ad[i] for i in idx]
    n = [-x for x in ad]
    p = n
    for _ in range(SUB.bit_length() - 2):
        p = [_dot(x, x) for x in p]
        np_ = [_dot(n[i], p[i]) for i in idx]
        n = [n[i] + p[i] + np_[i] for i in idx]
    nao = [_dot(n[i], ao[i]) for i in idx]
    m = [-(ao[i] + nao[i]) for i in idx]
    q = m
    for _ in range((c // SUB).bit_length() - 2):
        q = [_dot(x, x) for x in q]
        mq = [_dot(m[i], q[i]) for i in idx]
        m = [m[i] + q[i] + mq[i] for i in idx]
    mn = [_dot(m[i], n[i]) for i in idx]
    return [m[i] + n[i] + mn[i] for i in idx]


def _dn_kernel(qkv_ref, z_ref, sm_ref, cw_ref, alog_ref, dtb_ref, on_ref,
               o_ref, s_out_ref, s_ref, xp_ref):
    c_idx = pl.program_id(1)
    c = CHUNK
    pad = 8

    @pl.when(c_idx == 0)
    def _():
        s_ref[...] = jnp.zeros_like(s_ref)
        xp_ref[0:pad, :] = jnp.zeros((pad, DN_QKV), F32)

    xp_ref[pad:pad + c, :] = qkv_ref[...]
    acc = xp_ref[pad - 3:pad - 3 + c, :] * cw_ref[0:1, :]
    for j in range(1, CONV_W):
        acc = acc + xp_ref[pad - 3 + j:pad - 3 + j + c, :] * cw_ref[j:j + 1, :]
    y = _silu(acc)
    xp_ref[pad - 3:pad, :] = xp_ref[pad + c - 3:pad + c, :]

    sm = sm_ref[...]
    beta_all = jax.nn.sigmoid(sm)
    g_all = -jnp.exp(alog_ref[...]) * jax.nn.softplus(sm + dtb_ref[...])
    row, col = _iota((c, c), 0), _iota((c, c), 1)
    incl = row >= col
    strict = row > col
    gam_all = _dot(incl.astype(F32), g_all, HIGHEST)
    gam_t = gam_all.T
    hk = DN_HEADS * DN_DK
    for h in range(DN_HEADS):
        q = y[:, h * DN_DK:(h + 1) * DN_DK]
        k = y[:, hk + h * DN_DK:hk + (h + 1) * DN_DK]
        v = y[:, 2 * hk + h * DN_DV:2 * hk + (h + 1) * DN_DV]
        q = q * lax.rsqrt(jnp.sum(q * q, -1, keepdims=True) + EPS) * DN_DK ** -0.5
        k = k * lax.rsqrt(jnp.sum(k * k, -1, keepdims=True) + EPS)
        beta = beta_all[:, SM_BETA + h:SM_BETA + h + 1]
        gam = gam_all[:, SM_G + h:SM_G + h + 1]
        gam_row = gam_t[SM_G + h:SM_G + h + 1, :]
        dec = jnp.where(incl, jnp.exp(jnp.where(incl, gam - gam_row, 0.0)), 0.0)
        eg = jnp.exp(gam)
        gl = gam[c - 1:c, :]
        kb = k * beta
        kbf = k.astype(BF16)
        a = jnp.where(strict, _dot_nt(kb.astype(BF16), kbf) * dec, 0.0)
        w = _strict_lower_inverse_minus_eye(a)
        rhs = jnp.concatenate([v * beta, kb * eg], -1)
        sol = rhs + _dot(w, rhs)
        s = s_ref[h]
        sbf = s.astype(BF16)
        u = sol[:, :DN_DV] - _dot(sol[:, DN_DV:].astype(BF16), sbf)
        ubf = u.astype(BF16)
        qk = _dot_nt(q.astype(BF16), kbf) * dec
        o = _dot((q * eg).astype(BF16), sbf) + _dot(qk.astype(BF16), ubf)
        kd = k * jnp.exp(gl - gam)
        s_ref[h] = s * jnp.exp(gl) + _dot_tn(kd.astype(BF16), ubf)
        o = o * lax.rsqrt(jnp.mean(o * o, -1, keepdims=True) + EPS) * on_ref[...]
        o_ref[:, h * DN_DV:(h + 1) * DN_DV] = (o * _silu(z_ref[:, h * DN_DV:(h + 1) * DN_DV])).astype(BF16)

    @pl.when(c_idx == pl.num_programs(1) - 1)
    def _():
        s_out_ref[0] = s_ref[...]


def _dn_prompt(p, batch, seq, conv_w, alog_row, dtb_row, onorm):
    c = CHUNK
    nc = seq // c
    return pl.pallas_call(
        _dn_kernel,
        grid=(batch, nc),
        in_specs=[pl.BlockSpec((c, DN_QKV), lambda b, i: (b * nc + i, C_AQKV // DN_QKV)),
                  pl.BlockSpec((c, BRANCH_W), lambda b, i: (b * nc + i, C_AZ // BRANCH_W)),
                  pl.BlockSpec((c, LANES), lambda b, i: (b * nc + i, C_SM // LANES)),
                  pl.BlockSpec((CONV_W, DN_QKV), lambda b, i: (0, 0)),
                  pl.BlockSpec((1, LANES), lambda b, i: (0, 0)),
                  pl.BlockSpec((1, LANES), lambda b, i: (0, 0)),
                  pl.BlockSpec((1, DN_DV), lambda b, i: (0, 0))],
        out_specs=[pl.BlockSpec((c, BRANCH_W), lambda b, i: (b * nc + i, 0)),
                   pl.BlockSpec((1, DN_HEADS, DN_DK, DN_DV), lambda b, i: (b, 0, 0, 0))],
        out_shape=[jax.ShapeDtypeStruct((batch * seq, BRANCH_W), BF16),
                   jax.ShapeDtypeStruct((batch, DN_HEADS, DN_DK, DN_DV), F32)],
        scratch_shapes=[pltpu.VMEM((DN_HEADS, DN_DK, DN_DV), F32),
                        pltpu.VMEM((c + 8, DN_QKV), F32)],
        compiler_params=_cparams("parallel", "arbitrary"),
    )(p, p, p, conv_w, alog_row, dtb_row, onorm)


EXP_CAP = 80.0


def _gla_kernel(q_ref, k_ref, v_ref, r_ref, sm_ref, w2_ref, b_ref, on_ref,
                o_ref, s_out_ref, st_ref):
    c_idx = pl.program_id(1)
    c = CHUNK
    hw = GLA_HEADS * GLA_DK

    @pl.when(c_idx == 0)
    def _():
        st_ref[...] = jnp.zeros_like(st_ref)

    lg = jax.nn.log_sigmoid(_dot(sm_ref[...].astype(BF16), w2_ref[...]) + b_ref[...]) / GLA_TAU
    row, col = _iota((c, c), 0), _iota((c, c), 1)
    gam = _dot((row >= col).astype(F32), lg, HIGHEST)
    gl = gam[c - 1:c, :]
    qs = q_ref[...] * GLA_DK ** -0.5
    k = k_ref[...]
    qg = (qs * jnp.exp(gam)).astype(BF16)
    kd = (k * jnp.exp(gl - gam)).astype(BF16)
    egl = jnp.exp(gl)

    att_rows = [[] for _ in range(GLA_HEADS)]
    for i in range(c // SUB):
        lo, hi = i * SUB, (i + 1) * SUB
        ref_pt = gam[lo - 1:lo, :] if i > 0 else jnp.zeros((1, hw), F32)
        qi = (qs[lo:hi] * jnp.exp(gam[lo:hi] - ref_pt)).astype(BF16)
        ki = (k[:hi] * jnp.exp(jnp.minimum(ref_pt - gam[:hi], EXP_CAP))).astype(BF16)
        keep = (_iota((SUB, hi), 0) + lo) >= _iota((SUB, hi), 1)
        for h in range(GLA_HEADS):
            sl = slice(h * GLA_DK, (h + 1) * GLA_DK)
            att = jnp.where(keep, _dot_nt(qi[:, sl], ki[:, sl]), 0.0)
            att_rows[h].append(_dot(att.astype(BF16), v_ref[:hi, h * GLA_DV:(h + 1) * GLA_DV].astype(BF16)))

    for h in range(GLA_HEADS):
        sl = slice(h * GLA_DK, (h + 1) * GLA_DK)
        vh = v_ref[:, h * GLA_DV:(h + 1) * GLA_DV].astype(BF16)
        st = st_ref[h]
        o = _dot_nt(qg[:, sl], st.astype(BF16)) + jnp.concatenate(att_rows[h], 0)
        st_ref[h] = st * egl[:, sl] + _dot_tn(vh, kd[:, sl])
        o = o * lax.rsqrt(jnp.mean(o * o, -1, keepdims=True) + EPS) * on_ref[...]
        o_ref[:, h * GLA_DV:(h + 1) * GLA_DV] = (o * _silu(r_ref[:, h * GLA_DV:(h + 1) * GLA_DV])).astype(BF16)

    @pl.when(c_idx == pl.num_programs(1) - 1)
    def _():
        s_out_ref[0] = st_ref[...]


def _gla_prompt(p, batch, seq, w2pad, gla_b, onorm):
    c = CHUNK
    nc = seq // c
    hw = GLA_HEADS * GLA_DK
    return pl.pallas_call(
        _gla_kernel,
        grid=(batch, nc),
        in_specs=[pl.BlockSpec((c, hw), lambda b, i: (b * nc + i, C_BQ // hw)),
                  pl.BlockSpec((c, hw), lambda b, i: (b * nc + i, C_BK // hw)),
                  pl.BlockSpec((c, BRANCH_W), lambda b, i: (b * nc + i, C_BV // BRANCH_W)),
                  pl.BlockSpec((c, BRANCH_W), lambda b, i: (b * nc + i, C_BR // BRANCH_W)),
                  pl.BlockSpec((c, LANES), lambda b, i: (b * nc + i, C_SM // LANES)),
                  pl.BlockSpec((LANES, hw), lambda b, i: (0, 0)),
                  pl.BlockSpec((1, hw), lambda b, i: (0, 0)),
                  pl.BlockSpec((1, GLA_DV), lambda b, i: (0, 0))],
        out_specs=[pl.BlockSpec((c, BRANCH_W), lambda b, i: (b * nc + i, 0)),
                   pl.BlockSpec((1, GLA_HEADS, GLA_DV, GLA_DK), lambda b, i: (b, 0, 0, 0))],
        out_shape=[jax.ShapeDtypeStruct((batch * seq, BRANCH_W), BF16),
                   jax.ShapeDtypeStruct((batch, GLA_HEADS, GLA_DV, GLA_DK), F32)],
        scratch_shapes=[pltpu.VMEM((GLA_HEADS, GLA_DV, GLA_DK), F32)],
        compiler_params=_cparams("parallel", "arbitrary"),
    )(p, p, p, p, p, w2pad, gla_b, onorm)


def _half_rms(x, g2):
    lane = _iota(x.shape, 1)
    first = lane < SWA_HD
    sq = x * x
    s0 = jnp.sum(jnp.where(first, sq, 0.0), -1, keepdims=True)
    s1 = jnp.sum(jnp.where(first, 0.0, sq), -1, keepdims=True)
    ms = jnp.where(first, s0, s1) * (1.0 / SWA_HD)
    return x * lax.rsqrt(ms + EPS) * g2


def _swa_kernel(q_ref, kp_ref, kc_ref, vp_ref, vc_ref, qn_ref, kn_ref, snk_ref,
                o_ref, knew_ref):
    i = pl.program_id(1)
    w = WINDOW
    kc = _half_rms(kc_ref[...], kn_ref[...])
    kk = jnp.concatenate([_half_rms(kp_ref[...], kn_ref[...]), kc], 0).astype(BF16)
    vv = jnp.concatenate([vp_ref[...], vc_ref[...]], 0).astype(BF16)
    t = _iota((w, 2 * w), 0)
    j = _iota((w, 2 * w), 1)
    dist = w + t - j
    valid = (dist >= 0) & (dist <= w) & ((j >= w) | (i > 0))
    distf = dist.astype(F32)
    g = SWA_HEADS // SWA_KV
    for h in range(SWA_HEADS):
        kv = h // g
        q = q_ref[:, h * SWA_HD:(h + 1) * SWA_HD]
        q = q * lax.rsqrt(jnp.mean(q * q, -1, keepdims=True) + EPS) * qn_ref[...] * SWA_HD ** -0.5
        s = _dot_nt(q.astype(BF16), kk[:, kv * SWA_HD:(kv + 1) * SWA_HD])
        s = s - (2.0 ** (-8.0 * (h + 1) / SWA_HEADS)) * distf
        s = jnp.where(valid, s, -jnp.inf)
        snk = snk_ref[:, h:h + 1]
        m = jnp.maximum(jnp.max(s, -1, keepdims=True), snk)
        pr = jnp.exp(s - m)
        pr = pr / (jnp.sum(pr, -1, keepdims=True) + jnp.exp(snk - m))
        o = _dot(pr.astype(BF16), vv[:, kv * SWA_HD:(kv + 1) * SWA_HD])
        o_ref[:, h * SWA_HD:(h + 1) * SWA_HD] = o.astype(BF16)

    @pl.when(i == pl.num_programs(1) - 1)
    def _():
        knew_ref[0] = kc


def _swa_prompt(p, batch, seq, qn, kn2, sinks):
    w = WINDOW
    nw = seq // w
    kvw = SWA_KV * SWA_HD
    return pl.pallas_call(
        _swa_kernel,
        grid=(batch, nw),
        in_specs=[pl.BlockSpec((w, BRANCH_W), lambda b, i: (b * nw + i, C_CQ // BRANCH_W)),
                  pl.BlockSpec((w, kvw), lambda b, i: (b * nw + jnp.maximum(i - 1, 0), C_CK // kvw)),
                  pl.BlockSpec((w, kvw), lambda b, i: (b * nw + i, C_CK // kvw)),
                  pl.BlockSpec((w, kvw), lambda b, i: (b * nw + jnp.maximum(i - 1, 0), C_CV // kvw)),
                  pl.BlockSpec((w, kvw), lambda b, i: (b * nw + i, C_CV // kvw)),
                  pl.BlockSpec((1, SWA_HD), lambda b, i: (0, 0)),
                  pl.BlockSpec((1, kvw), lambda b, i: (0, 0)),
                  pl.BlockSpec((1, SWA_HEADS), lambda b, i: (0, 0))],
        out_specs=[pl.BlockSpec((w, BRANCH_W), lambda b, i: (b * nw + i, 0)),
                   pl.BlockSpec((1, w, kvw), lambda b, i: (b, 0, 0))],
        out_shape=[jax.ShapeDtypeStruct((batch * seq, BRANCH_W), BF16),
                   jax.ShapeDtypeStruct((batch, w, kvw), F32)],
        compiler_params=_cparams("parallel", "arbitrary"),
    )(p, p, p, p, p, qn, kn2, sinks)


def _dn_mb_kernel(qkv_ref, z_ref, sm_ref, cw_ref, alog_ref, dtb_ref, on_ref,
                  o_ref, s_out_ref, s_ref, xp_ref, y_ref):
    c_idx = pl.program_id(0)
    nb = qkv_ref.shape[0]
    c = CHUNK
    pad = 8

    @pl.when(c_idx == 0)
    def _():
        s_ref[...] = jnp.zeros_like(s_ref)
        xp_ref[:, 0:pad, :] = jnp.zeros((nb, pad, DN_QKV), F32)

    row, col = _iota((c, c), 0), _iota((c, c), 1)
    incl = row >= col
    strict = row > col
    incl_f = incl.astype(F32)
    hk = DN_HEADS * DN_DK
    gam_all, gam_t, beta_all = [], [], []
    for b in range(nb):
        xp_ref[b, pad:pad + c, :] = qkv_ref[b]
        acc = xp_ref[b, pad - 3:pad - 3 + c, :] * cw_ref[0:1, :]
        for j in range(1, CONV_W):
            acc = acc + xp_ref[b, pad - 3 + j:pad - 3 + j + c, :] * cw_ref[j:j + 1, :]
        y_ref[b] = _silu(acc)
        xp_ref[b, pad - 3:pad, :] = xp_ref[b, pad + c - 3:pad + c, :]
        sm = sm_ref[b]
        beta_all.append(jax.nn.sigmoid(sm))
        g_all = -jnp.exp(alog_ref[...]) * jax.nn.softplus(sm + dtb_ref[...])
        gam_all.append(_dot(incl_f, g_all, HIGHEST))
        gam_t.append(gam_all[b].T)

    chains = [(b, h) for b in range(nb) for h in range(DN_HEADS)]
    n = len(chains)
    q, k, dec, eg, gl, gam, kb, rhs, a = ([None] * n for _ in range(9))
    for i, (b, h) in enumerate(chains):
        qi = y_ref[b, :, h * DN_DK:(h + 1) * DN_DK]
        ki = y_ref[b, :, hk + h * DN_DK:hk + (h + 1) * DN_DK]
        vi = y_ref[b, :, 2 * hk + h * DN_DV:2 * hk + (h + 1) * DN_DV]
        q[i] = qi * lax.rsqrt(jnp.sum(qi * qi, -1, keepdims=True) + EPS) * DN_DK ** -0.5
        k[i] = ki * lax.rsqrt(jnp.sum(ki * ki, -1, keepdims=True) + EPS)
        beta = beta_all[b][:, SM_BETA + h:SM_BETA + h + 1]
        gam[i] = gam_all[b][:, SM_G + h:SM_G + h + 1]
        gam_row = gam_t[b][SM_G + h:SM_G + h + 1, :]
        dec[i] = jnp.where(incl, jnp.exp(jnp.minimum(gam[i] - gam_row, 0.0)), 0.0)
        eg[i] = jnp.exp(gam[i])
        gl[i] = gam[i][c - 1:c, :]
        kb[i] = k[i] * beta
        rhs[i] = jnp.concatenate([vi * beta, kb[i] * eg[i]], -1)
    kbf = [x.astype(BF16) for x in k]
    kk = [_dot_nt(kb[i].astype(BF16), kbf[i]) for i in range(n)]
    qk = [_dot_nt(q[i].astype(BF16), kbf[i]) for i in range(n)]
    a = [jnp.where(strict, kk[i] * dec[i], 0.0) for i in range(n)]
    w = _strict_lower_inverse_minus_eye_multi(a)
    sol = [rhs[i] + _dot(w[i], rhs[i]) for i in range(n)]
    s = [s_ref[b, h] for (b, h) in chains]
    sbf = [x.astype(BF16) for x in s]
    u = [sol[i][:, :DN_DV] - _dot(sol[i][:, DN_DV:].astype(BF16), sbf[i]) for i in range(n)]
    ubf = [x.astype(BF16) for x in u]
    o_s = [_dot((q[i] * eg[i]).astype(BF16), sbf[i]) for i in range(n)]
    o_u = [_dot((qk[i] * dec[i]).astype(BF16), ubf[i]) for i in range(n)]
    ds = [_dot_tn((k[i] * jnp.exp(gl[i] - gam[i])).astype(BF16), ubf[i]) for i in range(n)]
    for i, (b, h) in enumerate(chains):
        s_ref[b, h] = s[i] * jnp.exp(gl[i]) + ds[i]
        o = o_s[i] + o_u[i]
        o = o * lax.rsqrt(jnp.mean(o * o, -1, keepdims=True) + EPS) * on_ref[...]
        o_ref[b, :, h * DN_DV:(h + 1) * DN_DV] = (
            o * _silu(z_ref[b, :, h * DN_DV:(h + 1) * DN_DV])).astype(BF16)

    @pl.when(c_idx == pl.num_programs(0) - 1)
    def _():
        s_out_ref[...] = s_ref[...]


def _dn_prompt_mb(p3, conv_w, alog_row, dtb_row, onorm):
    batch, seq, _ = p3.shape
    c = CHUNK
    full = lambda shape: pl.BlockSpec(shape, lambda i: tuple(0 for _ in shape))
    return pl.pallas_call(
        _dn_mb_kernel,
        grid=(seq // c,),
        in_specs=[pl.BlockSpec((batch, c, DN_QKV), lambda i: (0, i, C_AQKV // DN_QKV)),
                  pl.BlockSpec((batch, c, BRANCH_W), lambda i: (0, i, C_AZ // BRANCH_W)),
                  pl.BlockSpec((batch, c, LANES), lambda i: (0, i, C_SM // LANES)),
                  full((CONV_W, DN_QKV)), full((1, LANES)), full((1, LANES)), full((1, DN_DV))],
        out_specs=[pl.BlockSpec((batch, c, BRANCH_W), lambda i: (0, i, 0)),
                   full((batch, DN_HEADS, DN_DK, DN_DV))],
        out_shape=[jax.ShapeDtypeStruct((batch, seq, BRANCH_W), BF16),
                   jax.ShapeDtypeStruct((batch, DN_HEADS, DN_DK, DN_DV), F32)],
        scratch_shapes=[pltpu.VMEM((batch, DN_HEADS, DN_DK, DN_DV), F32),
                        pltpu.VMEM((batch, c + 8, DN_QKV), F32),
                        pltpu.VMEM((batch, c, DN_QKV), F32)],
        compiler_params=_cparams("arbitrary"),
    )(p3, p3, p3, conv_w, alog_row, dtb_row, onorm)


def _gla_mb_kernel(q_ref, k_ref, v_ref, r_ref, sm_ref, w2_ref, b_ref, on_ref,
                   o_ref, s_out_ref, st_ref):
    c_idx = pl.program_id(0)
    nb = q_ref.shape[0]
    c = CHUNK
    hw = GLA_HEADS * GLA_DK
    hv = GLA_HEADS * GLA_DV

    @pl.when(c_idx == 0)
    def _():
        st_ref[...] = jnp.zeros_like(st_ref)

    incl_f = (_iota((c, c), 0) >= _iota((c, c), 1)).astype(F32)
    blk = (_iota((hv, hw), 0) // GLA_DV) == (_iota((hv, hw), 1) // GLA_DK)
    qsel = (_iota((GLA_HEADS * SUB, hw), 0) // SUB) == (_iota((GLA_HEADS * SUB, hw), 1) // GLA_DK)
    nbr = range(nb)
    lg = [jax.nn.log_sigmoid(_dot(sm_ref[b].astype(BF16), w2_ref[...]) + b_ref[...]) / GLA_TAU for b in nbr]
    gam = [_dot(incl_f, lg[b], HIGHEST) for b in nbr]
    gl = [gam[b][c - 1:c, :] for b in nbr]
    qs = [q_ref[b] * GLA_DK ** -0.5 for b in nbr]
    k = [k_ref[b] for b in nbr]
    qg = [(qs[b] * jnp.exp(gam[b])).astype(BF16) for b in nbr]
    kd = [(k[b] * jnp.exp(gl[b] - gam[b])).astype(BF16) for b in nbr]
    vbf = [v_ref[b].astype(BF16) for b in nbr]
    st = [st_ref[b] for b in nbr]
    o_inter = [_dot_nt(qg[b], st[b].astype(BF16)) for b in nbr]
    dst = [_dot_tn(vbf[b], kd[b]) for b in nbr]
    for b in nbr:
        st_ref[b] = st[b] * jnp.exp(gl[b]) + jnp.where(blk, dst[b], 0.0)

    res = [[] for _ in nbr]
    for i in range(c // SUB):
        lo, hi = i * SUB, (i + 1) * SUB
        keep = (_iota((GLA_HEADS * SUB, hi), 0) % SUB + lo) >= _iota((GLA_HEADS * SUB, hi), 1)
        qm, ki = [], []
        for b in nbr:
            ref_pt = gam[b][lo - 1:lo, :] if i > 0 else jnp.zeros((1, hw), F32)
            qi = qs[b][lo:hi] * jnp.exp(gam[b][lo:hi] - ref_pt)
            qm.append(jnp.where(qsel, jnp.concatenate([qi] * GLA_HEADS, 0), 0.0).astype(BF16))
            ki.append((k[b][:hi] * jnp.exp(jnp.minimum(ref_pt - gam[b][:hi], EXP_CAP))).astype(BF16))
        att = [_dot_nt(qm[b], ki[b]) for b in nbr]
        att = [jnp.where(keep, att[b], 0.0).astype(BF16) for b in nbr]
        for b in nbr:
            res[b].append(_dot(att[b], vbf[b][:hi]))
    for b in nbr:
        for h in range(GLA_HEADS):
            vs = slice(h * GLA_DV, (h + 1) * GLA_DV)
            o = o_inter[b][:, vs] + jnp.concatenate([r[h * SUB:(h + 1) * SUB, vs] for r in res[b]], 0)
            o = o * lax.rsqrt(jnp.mean(o * o, -1, keepdims=True) + EPS) * on_ref[...]
            o_ref[b, :, vs] = (o * _silu(r_ref[b, :, vs])).astype(BF16)

    @pl.when(c_idx == pl.num_programs(0) - 1)
    def _():
        for b in range(nb):
            for h in range(GLA_HEADS):
                s_out_ref[b, h] = st_ref[b, h * GLA_DV:(h + 1) * GLA_DV, h * GLA_DK:(h + 1) * GLA_DK]


def _gla_prompt_mb(p3, w2pad, gla_b, onorm):
    batch, seq, _ = p3.shape
    c = CHUNK
    hw = GLA_HEADS * GLA_DK
    hv = GLA_HEADS * GLA_DV
    full = lambda shape: pl.BlockSpec(shape, lambda i: tuple(0 for _ in shape))
    return pl.pallas_call(
        _gla_mb_kernel,
        grid=(seq // c,),
        in_specs=[pl.BlockSpec((batch, c, hw), lambda i: (0, i, C_BQ // hw)),
                  pl.BlockSpec((batch, c, hw), lambda i: (0, i, C_BK // hw)),
                  pl.BlockSpec((batch, c, hv), lambda i: (0, i, C_BV // hv)),
                  pl.BlockSpec((batch, c, hv), lambda i: (0, i, C_BR // hv)),
                  pl.BlockSpec((batch, c, LANES), lambda i: (0, i, C_SM // LANES)),
                  full((LANES, hw)), full((1, hw)), full((1, GLA_DV))],
        out_specs=[pl.BlockSpec((batch, c, hv), lambda i: (0, i, 0)),
                   full((batch, GLA_HEADS, GLA_DV, GLA_DK))],
        out_shape=[jax.ShapeDtypeStruct((batch, seq, hv), BF16),
                   jax.ShapeDtypeStruct((batch, GLA_HEADS, GLA_DV, GLA_DK), F32)],
        scratch_shapes=[pltpu.VMEM((batch, hv, hw), F32)],
        compiler_params=_cparams("arbitrary"),
    )(p3, p3, p3, p3, p3, w2pad, gla_b, onorm)


def _swa_mb_kernel(q_ref, kp_ref, kc_ref, vp_ref, vc_ref, qn_ref, kn_ref, snk_ref,
                   o_ref, knew_ref):
    i = pl.program_id(0)
    nb = q_ref.shape[0]
    w = WINDOW
    g = SWA_HEADS // SWA_KV
    nr = g * w
    first = _iota((w, 2 * SWA_HD), 1) < SWA_HD
    first2 = _iota((2 * w, 2 * SWA_HD), 1) < SWA_HD
    t = _iota((nr, 2 * w), 0) % w
    j = _iota((nr, 2 * w), 1)
    dist = w + t - j
    valid = (dist >= 0) & (dist <= w) & ((j >= w) | (i > 0))
    distf = dist.astype(F32)
    hrow = _iota((nr, 1), 0) // w
    slope, snk = [], []
    for kv in range(SWA_KV):
        sl = jnp.zeros((nr, 1), F32)
        sk = jnp.zeros((nr, 1), F32)
        for hh in range(g):
            h = kv * g + hh
            sl = jnp.where(hrow == hh, 2.0 ** (-8.0 * (h + 1) / SWA_HEADS), sl)
            sk = jnp.where(hrow == hh, snk_ref[:, h:h + 1], sk)
        slope.append(sl * distf)
        snk.append(sk)

    units = [(b, kv) for b in range(nb) for kv in range(SWA_KV)]
    kcs, k2, v2, qx = [], [], [], []
    for b in range(nb):
        kc = _half_rms(kc_ref[b], kn_ref[...])
        kcs.append(kc)
        kk = jnp.concatenate([_half_rms(kp_ref[b], kn_ref[...]), kc], 0)
        vv = jnp.concatenate([vp_ref[b], vc_ref[b]], 0)
        kk_sw = pltpu.roll(kk, SWA_HD, axis=1)
        vv_sw = pltpu.roll(vv, SWA_HD, axis=1)
        k2 += [jnp.where(first2, kk, kk_sw).astype(BF16), jnp.where(first2, kk_sw, kk).astype(BF16)]
        v2 += [jnp.where(first2, vv, vv_sw).astype(BF16), jnp.where(first2, vv_sw, vv).astype(BF16)]
        for kv in range(SWA_KV):
            rows = []
            for jj in range(g // 2):
                grp = kv * (g // 2) + jj
                qg = _half_rms(q_ref[b, :, grp * 2 * SWA_HD:(grp + 1) * 2 * SWA_HD], qn_ref[...]) * SWA_HD ** -0.5
                rows += [jnp.where(first, qg, 0.0), jnp.where(first, 0.0, qg)]
            qx.append(jnp.concatenate(rows, 0).astype(BF16))
    nu = range(len(units))
    s = [_dot_nt(qx[u], k2[u]) for u in nu]
    s = [jnp.where(valid, s[u] - slope[units[u][1]], -jnp.inf) for u in nu]
    m = [jnp.maximum(jnp.max(s[u], -1, keepdims=True), snk[units[u][1]]) for u in nu]
    pr = [jnp.exp(s[u] - m[u]) for u in nu]
    den = [jnp.sum(pr[u], -1, keepdims=True) + jnp.exp(snk[units[u][1]] - m[u]) for u in nu]
    pr = [(pr[u] * (1.0 / den[u])).astype(BF16) for u in nu]
    o = [_dot(pr[u], v2[u]) for u in nu]
    for u, (b, kv) in enumerate(units):
        for jj in range(g // 2):
            grp = kv * (g // 2) + jj
            o_ref[b, :, grp * 2 * SWA_HD:(grp + 1) * 2 * SWA_HD] = jnp.where(
                first, o[u][(2 * jj) * w:(2 * jj + 1) * w], o[u][(2 * jj + 1) * w:(2 * jj + 2) * w]).astype(BF16)

    @pl.when(i == pl.num_programs(0) - 1)
    def _():
        for b in range(nb):
            knew_ref[b] = kcs[b]


def _swa_prompt_mb(p3, qn2, kn2, sinks):
    batch, seq, _ = p3.shape
    w = WINDOW
    kvw = SWA_KV * SWA_HD
    full = lambda shape: pl.BlockSpec(shape, lambda i: tuple(0 for _ in shape))
    prev = lambda i: jnp.maximum(i - 1, 0)
    return pl.pallas_call(
        _swa_mb_kernel,
        grid=(seq // w,),
        in_specs=[pl.BlockSpec((batch, w, BRANCH_W), lambda i: (0, i, C_CQ // BRANCH_W)),
                  pl.BlockSpec((batch, w, kvw), lambda i: (0, prev(i), C_CK // kvw)),
                  pl.BlockSpec((batch, w, kvw), lambda i: (0, i, C_CK // kvw)),
                  pl.BlockSpec((batch, w, kvw), lambda i: (0, prev(i), C_CV // kvw)),
                  pl.BlockSpec((batch, w, kvw), lambda i: (0, i, C_CV // kvw)),
                  full((1, kvw)), full((1, kvw)), full((1, SWA_HEADS))],
        out_specs=[pl.BlockSpec((batch, w, BRANCH_W), lambda i: (0, i, 0)),
                   full((batch, w, kvw))],
        out_shape=[jax.ShapeDtypeStruct((batch, seq, BRANCH_W), BF16),
                   jax.ShapeDtypeStruct((batch, w, kvw), F32)],
        compiler_params=_cparams("arbitrary"),
    )(p3, p3, p3, p3, p3, qn2, kn2, sinks)


DEC_TILE = 8


def _dec_kernel(qkv_ref, z_ref, sm_ref, bq_ref, bk_ref, bv_ref, br_ref, cq_ref, ck_ref, cv_ref,
                sdn_ref, buf_ref, sgl_ref, kc_ref, vc_ref,
                cw_ref, alog_ref, dtb_ref, dnon_ref, w2_ref, glb_ref, glon_ref,
                qn_ref, kn_ref, snk_ref, slp_ref,
                oa_ref, ob_ref, oc_ref, sdn_out, buf_out, sgl_out, kc_out, vc_out):
    bt = DEC_TILE
    x = qkv_ref[...]
    buf = buf_ref[...]
    acc = x * cw_ref[CONV_W - 1:CONV_W, :]
    for j in range(CONV_W - 1):
        acc = acc + buf[:, j * DN_QKV:(j + 1) * DN_QKV] * cw_ref[j:j + 1, :]
    y = _silu(acc)
    buf_out[:, 0:(CONV_W - 2) * DN_QKV] = buf[:, DN_QKV:]
    buf_out[:, (CONV_W - 2) * DN_QKV:] = x

    sm = sm_ref[...]
    beta_all = jax.nn.sigmoid(sm)
    g_all = -jnp.exp(alog_ref[...]) * jax.nn.softplus(sm + dtb_ref[...])
    eye = (_iota((LANES, LANES), 0) == _iota((LANES, LANES), 1)).astype(F32)
    hk = DN_HEADS * DN_DK
    for h in range(DN_HEADS):
        q = y[:, h * DN_DK:(h + 1) * DN_DK]
        k = y[:, hk + h * DN_DK:hk + (h + 1) * DN_DK]
        v = y[:, 2 * hk + h * DN_DV:2 * hk + (h + 1) * DN_DV]
        q = q * lax.rsqrt(jnp.sum(q * q, -1, keepdims=True) + EPS) * DN_DK ** -0.5
        k = k * lax.rsqrt(jnp.sum(k * k, -1, keepdims=True) + EPS)
        beta = beta_all[:, SM_BETA + h:SM_BETA + h + 1]
        eg = jnp.exp(g_all[:, SM_G + h:SM_G + h + 1])
        kb = k * beta
        lhs = jnp.concatenate([kb * eg, q * eg], 0).astype(BF16)
        k_t = _dot_nt(eye, k)
        qk = jnp.sum(q * k, -1, keepdims=True)
        vb = v * beta
        o_rows = []
        for b in range(bt):
            s = sdn_ref[b, h]
            r = _dot(lhs, s.astype(BF16))
            u = vb[b:b + 1] - r[b:b + 1]
            o_rows.append(r[bt + b:bt + b + 1] + qk[b:b + 1] * u)
            sdn_out[b, h] = s * eg[b:b + 1] + k_t[:, b:b + 1] * u
        o = jnp.concatenate(o_rows, 0)
        o = o * lax.rsqrt(jnp.mean(o * o, -1, keepdims=True) + EPS) * dnon_ref[...]
        oa_ref[:, h * DN_DV:(h + 1) * DN_DV] = (o * _silu(z_ref[:, h * DN_DV:(h + 1) * DN_DV])).astype(BF16)

    lg = jax.nn.log_sigmoid(_dot(sm.astype(BF16), w2_ref[...]) + glb_ref[...]) / GLA_TAU
    elg = jnp.exp(lg)
    bq = bq_ref[...] * GLA_DK ** -0.5
    bk = bk_ref[...]
    qg = bq * elg
    lane = _iota((bt, LANES), 1)
    first = lane < GLA_DK
    rows_first = _iota((LANES, GLA_DV), 0) < GLA_DK
    for j in range(GLA_HEADS // 2):
        sl = slice(j * LANES, (j + 1) * LANES)
        qgj = qg[:, sl]
        lhs = jnp.concatenate([jnp.where(first, qgj, 0.0), jnp.where(first, 0.0, qgj)], 0).astype(BF16)
        cols = _dot_nt(eye, jnp.concatenate([elg[:, sl], bk[:, sl]], 0), HIGHEST)
        prod = bq[:, sl] * bk[:, sl]
        qk0 = jnp.sum(jnp.where(first, prod, 0.0), -1, keepdims=True)
        qk1 = jnp.sum(jnp.where(first, 0.0, prod), -1, keepdims=True)
        v0 = bv_ref[:, (2 * j) * GLA_DV:(2 * j + 1) * GLA_DV]
        v1 = bv_ref[:, (2 * j + 1) * GLA_DV:(2 * j + 2) * GLA_DV]
        o0, o1 = [], []
        for b in range(bt):
            s = sgl_ref[b, j]
            r = _dot(lhs, s.astype(BF16))
            o0.append(r[b:b + 1] + qk0[b:b + 1] * v0[b:b + 1])
            o1.append(r[bt + b:bt + b + 1] + qk1[b:b + 1] * v1[b:b + 1])
            vsel = jnp.where(rows_first, v0[b:b + 1], v1[b:b + 1])
            sgl_out[b, j] = s * cols[:, b:b + 1] + cols[:, bt + b:bt + b + 1] * vsel
        for hh, rows in ((2 * j, o0), (2 * j + 1, o1)):
            o = jnp.concatenate(rows, 0)
            o = o * lax.rsqrt(jnp.mean(o * o, -1, keepdims=True) + EPS) * glon_ref[...]
            ob_ref[:, hh * GLA_DV:(hh + 1) * GLA_DV] = (o * _silu(br_ref[:, hh * GLA_DV:(hh + 1) * GLA_DV])).astype(BF16)

    g = SWA_HEADS // SWA_KV
    nr = bt * SWA_HEADS
    cq = cq_ref[...]
    cq = cq * lax.rsqrt(jnp.mean(cq * cq, -1, keepdims=True) + EPS) * qn_ref[...] * SWA_HD ** -0.5
    head = _iota((nr, 2 * SWA_HD), 0) % SWA_HEADS
    in_half = (head // g) == (_iota((nr, 2 * SWA_HD), 1) // SWA_HD)
    qx = jnp.where(in_half, jnp.concatenate([cq, cq], -1), 0.0)
    knew = _half_rms(ck_ref[...], kn_ref[...])
    vnew = cv_ref[...]
    s_c, kn_rows, vn_rows = [], [], []
    for b in range(bt):
        s_c.append(_dot_nt(qx[b * SWA_HEADS:(b + 1) * SWA_HEADS].astype(BF16), kc_ref[b].astype(BF16)))
        kn_rows.append(jnp.broadcast_to(knew[b:b + 1], (SWA_HEADS, 2 * SWA_HD)))
        vn_rows.append(jnp.broadcast_to(vnew[b:b + 1], (SWA_HEADS, 2 * SWA_HD)))
    s_c = jnp.concatenate(s_c, 0)
    kn_x = jnp.concatenate(kn_rows, 0)
    vn_x = jnp.concatenate(vn_rows, 0)
    slopes = slp_ref[...]
    snk = snk_ref[...]
    dist = (WINDOW - _iota((nr, WINDOW), 1)).astype(F32)
    s_c = s_c - slopes * dist
    s_n = jnp.sum(qx * kn_x, -1, keepdims=True)
    m = jnp.maximum(jnp.maximum(jnp.max(s_c, -1, keepdims=True), s_n), snk)
    p_c = jnp.exp(s_c - m)
    p_n = jnp.exp(s_n - m)
    den = jnp.sum(p_c, -1, keepdims=True) + p_n + jnp.exp(snk - m)
    p_c = p_c / den
    p_n = p_n / den
    half_sel = (_iota((nr, SWA_HD), 0) % SWA_HEADS) < g
    for b in range(bt):
        rs = slice(b * SWA_HEADS, (b + 1) * SWA_HEADS)
        r = _dot(p_c[rs].astype(BF16), vc_ref[b].astype(BF16)) + p_n[rs] * vn_x[rs]
        oc_ref[rs, :] = jnp.where(half_sel[rs], r[:, :SWA_HD], r[:, SWA_HD:]).astype(BF16)
        kc_out[b, 0:WINDOW - 1, :] = kc_ref[b, 1:WINDOW, :]
        kc_out[b, WINDOW - 1:WINDOW, :] = knew[b:b + 1]
        vc_out[b, 0:WINDOW - 1, :] = vc_ref[b, 1:WINDOW, :]
        vc_out[b, WINDOW - 1:WINDOW, :] = vnew[b:b + 1]


def _dec_kernel_aliased(*refs):
    n_in, n_alias = 26, 5
    _dec_kernel(*refs[:n_in], *refs[n_in + n_alias:])


def _dec_mixers(layer, p, cq_r, states, prev_out, conv_w, alog_row, dtb_row, dn_on, w2pad, gla_b, gla_on,
                qn, kn2, snk_col, slp_col):
    n = p.shape[0]
    bt = DEC_TILE
    hw = GLA_HEADS * GLA_DK
    kvw = SWA_KV * SWA_HD
    nr = bt * SWA_HEADS

    def col(width, off):
        return pl.BlockSpec((bt, width), lambda i: (i, off // width))

    def full(shape):
        return pl.BlockSpec(shape, lambda i: tuple(0 for _ in shape))

    def slot(a):
        rest = a.shape[2:]
        return pl.BlockSpec((None, bt) + rest, lambda i: (layer, i) + tuple(0 for _ in rest))

    in_specs = [col(DN_QKV, C_AQKV), col(BRANCH_W, C_AZ), col(LANES, C_SM), col(hw, C_BQ), col(hw, C_BK),
                col(BRANCH_W, C_BV), col(BRANCH_W, C_BR),
                pl.BlockSpec((nr, SWA_HD), lambda i: (i, 0)),
                col(kvw, C_CK), col(kvw, C_CV)]
    in_specs += [slot(a) for a in states]
    in_specs += [full((CONV_W, DN_QKV)), full((1, LANES)), full((1, LANES)), full((1, DN_DV)),
                 full((LANES, hw)), full((1, hw)), full((1, GLA_DV)),
                 full((1, SWA_HD)), full((1, kvw)), full((nr, 1)), full((nr, 1))]
    args = [p, p, p, p, p, p, p, cq_r, p, p, *states,
            conv_w, alog_row, dtb_row, dn_on, w2pad, gla_b, gla_on, qn, kn2, snk_col, slp_col]
    body, aliases = _dec_kernel, {}
    if prev_out is not None:
        body = _dec_kernel_aliased
        aliases = {len(args) + k: 3 + k for k in range(len(prev_out))}
        in_specs += [pl.BlockSpec(memory_space=pl.ANY)] * len(prev_out)
        args += list(prev_out)
    return pl.pallas_call(
        body,
        grid=(n // bt,),
        in_specs=in_specs,
        out_specs=[pl.BlockSpec((bt, BRANCH_W), lambda i: (i, 0)),
                   pl.BlockSpec((bt, BRANCH_W), lambda i: (i, 0)),
                   pl.BlockSpec((nr, SWA_HD), lambda i: (i, 0))] + [slot(a) for a in states],
        out_shape=[jax.ShapeDtypeStruct((n, BRANCH_W), BF16),
                   jax.ShapeDtypeStruct((n, BRANCH_W), BF16),
                   jax.ShapeDtypeStruct((n * SWA_HEADS, SWA_HD), BF16)]
        + [jax.ShapeDtypeStruct(a.shape, F32) for a in states],
        input_output_aliases=aliases,
        compiler_params=_cparams("parallel"),
    )(*args)


def _merge_kernel(ba_ref, bb_ref, bc_ref, gate_ref, x_ref, gt_ref, sc_ref, sh_ref, ln_ref,
                  wb_ref, wo_ref, wr_ref, rb_ref, x1_ref, h2_ref, comb_ref):
    mix = None
    for n, br in enumerate((ba_ref, bb_ref, bc_ref)):
        up = _dot(br[...], wb_ref[n])
        term = jax.nn.sigmoid(gate_ref[:, n * D_MODEL:(n + 1) * D_MODEL]) * up
        mix = term if mix is None else mix + term
    x1 = x_ref[...] + gt_ref[0] * _dot(mix.astype(BF16), wo_ref[...])
    x1_ref[...] = x1
    h2 = x1 * lax.rsqrt(jnp.mean(x1 * x1, -1, keepdims=True) + EPS) * ln_ref[...]
    h2 = h2 * (1.0 + sc_ref[0]) + sh_ref[0]
    h2b = h2.astype(BF16)
    h2_ref[...] = h2b

    h2l = (h2 - h2b.astype(F32)).astype(BF16)
    hi_lo = _dot(h2b, wr_ref[...])
    logits = hi_lo[:, :LANES] + hi_lo[:, LANES:] + _dot(h2l, wr_ref[:, :LANES]) + rb_ref[...]
    lane = _iota(logits.shape, 1).astype(F32)
    big = float(LANES)
    lc = jnp.where(lane < N_GROUPS, logits, -jnp.inf)
    mc = jnp.max(lc, -1, keepdims=True)
    pg = 1.0 / jnp.sum(jnp.exp(lc - mc), -1, keepdims=True)
    grp = jnp.min(jnp.where(lc == mc, lane, big), -1, keepdims=True)
    lo = R_EXP + grp * EXP_PER_GROUP
    emask = (lane >= lo) & (lane < lo + EXP_PER_GROUP)
    le = jnp.where(emask, logits, -jnp.inf)
    pe = jnp.exp(le - jnp.max(le, -1, keepdims=True))
    pe = pe / jnp.sum(pe, -1, keepdims=True)
    v1 = jnp.max(pe, -1, keepdims=True)
    i1 = jnp.min(jnp.where(emask & (pe == v1), lane, big), -1, keepdims=True)
    pe2 = jnp.where(emask & (lane != i1), pe, -1.0)
    v2 = jnp.max(pe2, -1, keepdims=True)
    i2 = jnp.min(jnp.where(pe2 == v2, lane, big), -1, keepdims=True)
    tot = v1 + v2
    comb_ref[...] = jnp.where(lane == 0.0, i1 - R_EXP, jnp.where(lane == 1.0, i2 - R_EXP, jnp.where(
        lane == 2.0, pg * v1 / tot, jnp.where(lane == 3.0, pg * v2 / tot, 0.0))))


def _merge(layer, ba, bb, bc, p, x, gt, sc, sh, ln, wb, wo, wr, rb, tm, tokens_per_row):
    t, d = x.shape
    tok = lambda width: pl.BlockSpec((tm, width), lambda i: (i, 0))
    full = lambda shape: pl.BlockSpec((None,) + shape, lambda i: (layer,) + tuple(0 for _ in shape))
    return pl.pallas_call(
        _merge_kernel,
        grid=(t // tm,),
        in_specs=[tok(BRANCH_W), tok(BRANCH_W), tok(BRANCH_W),
                  pl.BlockSpec((tm, N_BRANCH * d), lambda i: (i, C_GATE)),
                  tok(d),
                  _mod_spec(gt, tm, tokens_per_row), _mod_spec(sc, tm, tokens_per_row),
                  _mod_spec(sh, tm, tokens_per_row),
                  full((1, d)), full((N_BRANCH, BRANCH_W, d)), full((d, d)), full((d, 2 * LANES)),
                  full((1, LANES))],
        out_specs=[tok(d), tok(d), tok(LANES)],
        out_shape=[jax.ShapeDtypeStruct((t, d), F32),
                   jax.ShapeDtypeStruct((t, d), BF16),
                   jax.ShapeDtypeStruct((t, LANES), F32)],
        compiler_params=_cparams("parallel"),
    )(ba, bb, bc, p, x, gt, sc, sh, ln, wb, wo, wr, rb)


def _moe_kernel(h_ref, comb_ref, x1_ref, gt_ref, w1_ref, w3_ref, w2_ref, o_ref, acc_ref):
    e = pl.program_id(1)

    @pl.when(e == 0)
    def _():
        acc_ref[...] = jnp.zeros_like(acc_ref)

    h = h_ref[...]
    he = _silu(_dot(h, w1_ref[0].astype(BF16))) * _dot(h, w3_ref[0].astype(BF16))
    ye = _dot(he.astype(BF16), w2_ref[0].astype(BF16))
    comb = comb_ref[...]
    ef = e.astype(F32)
    ce = (jnp.where(comb[:, 0:1] == ef, comb[:, 2:3], 0.0) + jnp.where(comb[:, 1:2] == ef, comb[:, 3:4], 0.0))
    acc_ref[...] += ce * ye

    @pl.when(e == pl.num_programs(1) - 1)
    def _():
        o_ref[...] = x1_ref[...] + gt_ref[0] * acc_ref[...]


def _moe(layer, h2, comb, x1, gt, w1, w3, w2, tm, tokens_per_row):
    t, d = x1.shape
    _, ne, _, de = w1.shape
    return pl.pallas_call(
        _moe_kernel,
        grid=(t // tm, ne),
        in_specs=[pl.BlockSpec((tm, d), lambda i, e: (i, 0)),
                  pl.BlockSpec((tm, LANES), lambda i, e: (i, 0)),
                  pl.BlockSpec((tm, d), lambda i, e: (i, 0)),
                  _mod_spec(gt, tm, tokens_per_row),
                  pl.BlockSpec((None, 1, d, de), lambda i, e: (layer, e, 0, 0)),
                  pl.BlockSpec((None, 1, d, de), lambda i, e: (layer, e, 0, 0)),
                  pl.BlockSpec((None, 1, de, d), lambda i, e: (layer, e, 0, 0))],
        out_specs=pl.BlockSpec((tm, d), lambda i, e: (i, 0)),
        out_shape=jax.ShapeDtypeStruct((t, d), F32),
        scratch_shapes=[pltpu.VMEM((tm, d), F32)],
        compiler_params=_cparams("parallel", "arbitrary"),
    )(h2, comb, x1, gt, w1, w3, w2)


MOE_TK = 512
MOE_TM = 512
ROW_CHUNK = 16
MOE_LB = 2 * MOE_TK + 512
MOE_CH = MOE_LB // ROW_CHUNK
MOE_XW = D_MODEL + LANES
assert 2 * MOE_TK + N_EXPERTS * (ROW_CHUNK - 1) <= MOE_LB - ROW_CHUNK


def _moe_sorted_tiles(t):
    rows = 2 * t + (t // MOE_TK) * N_EXPERTS * (ROW_CHUNK - 1) + N_EXPERTS * (MOE_TM - 1)
    return -(-rows // MOE_TM)


def _plan_kernel(r_ref, lp_ref, dc_ref, te_ref, tot_ref, base_ref):
    ph, t = pl.program_id(0), pl.program_id(1)
    tk = r_ref.shape[0]
    ntp = te_ref.shape[0]
    r = r_ref[...]
    lane = _iota((tk, LANES), 1).astype(F32)
    sel1, sel2 = lane == r[:, 0:1], lane == r[:, 1:2]
    oh = jnp.where(sel1 | sel2, 1.0, 0.0)
    cnt = jnp.sum(oh, 0, keepdims=True)
    c8 = jnp.floor((cnt + (ROW_CHUNK - 1)) * (1.0 / ROW_CHUNK)) * ROW_CHUNK
    upper = (_iota((LANES, LANES), 0) < _iota((LANES, LANES), 1)).astype(F32)

    def excl_cumsum(v):
        return _dot(jnp.broadcast_to(v, (8, LANES)), upper, HIGHEST)[0:1]

    @pl.when(ph == 0)
    def _():
        @pl.when(t == 0)
        def _():
            tot_ref[...] = jnp.zeros_like(tot_ref)

        tot_ref[...] += c8

    @pl.when(ph == 1)
    def _():
        @pl.when(t == 0)
        def _():
            tot = tot_ref[...]
            gp = jnp.floor((tot + (MOE_TM - 1)) * (1.0 / MOE_TM)) * MOE_TM
            off = excl_cumsum(gp)
            base_ref[...] = off
            end = off + gp
            lane_t = _iota((ntp, LANES), 1).astype(F32)
            start = _iota((ntp, 1), 0).astype(F32) * MOE_TM
            te = jnp.sum(jnp.where((lane_t < N_EXPERTS) & (end <= start), 1.0, 0.0), -1, keepdims=True)
            mine = lane_t == te
            filled = jnp.sum(jnp.where(mine, tot + off, 0.0), -1, keepdims=True)
            tv = jnp.clip(filled - start, 0.0, float(MOE_TM))
            n_used = jnp.sum(jnp.where(lane_t == N_EXPERTS - 1, end, 0.0), -1, keepdims=True) * (1.0 / MOE_TM)
            te_ref[...] = jnp.where(lane_t == 0.0, jnp.minimum(te, N_EXPERTS - 1.0),
                                    jnp.where(lane_t == 1.0, tv, jnp.where(lane_t == 2.0, n_used, 0.0))
                                    ).astype(jnp.int32)

        base = base_ref[...]
        lo = excl_cumsum(c8)
        below = (_iota((tk, tk), 0) > _iota((tk, tk), 1)).astype(BF16)
        p = _dot(below, oh.astype(BF16)) + lo
        lp1 = jnp.sum(jnp.where(sel1, p, 0.0), -1, keepdims=True)
        lp2 = jnp.sum(jnp.where(sel2, p, 0.0), -1, keepdims=True)
        lp_ref[...] = jnp.where(lane == 0.0, lp1, jnp.where(lane == 1.0, lp2, jnp.where(lane < 4.0, r, 0.0)))
        lane_c = _iota((MOE_CH, LANES), 1).astype(F32)
        cstart = _iota((MOE_CH, 1), 0).astype(F32) * ROW_CHUNK
        ej = jnp.sum(jnp.where((lane_c < N_EXPERTS) & (lo + c8 <= cstart), 1.0, 0.0), -1, keepdims=True)
        dj = jnp.sum(jnp.where(lane_c == ej, base - lo, 0.0), -1, keepdims=True) + cstart
        nrows = jnp.sum(c8, -1, keepdims=True)
        last = _iota((MOE_CH, 1), 0) == MOE_CH - 1
        dcv = jnp.where(last, nrows * (1.0 / ROW_CHUNK), jnp.where(cstart < nrows, dj, 0.0))
        dc_ref[...] = jnp.broadcast_to(dcv, (MOE_CH, LANES)).astype(jnp.int32)
        base_ref[...] = base + c8


def _moe_plan(route):
    t = route.shape[0]
    nt = t // MOE_TK
    ntp = -(-_moe_sorted_tiles(t) // 8) * 8
    return pl.pallas_call(
        _plan_kernel,
        grid=(2, nt),
        in_specs=[pl.BlockSpec((MOE_TK, LANES), lambda ph, i: (i, 0))],
        out_specs=[pl.BlockSpec((MOE_TK, LANES), lambda ph, i: (i * ph, 0)),
                   pl.BlockSpec((MOE_CH, LANES), lambda ph, i: (i * ph, 0)),
                   pl.BlockSpec((ntp, LANES), lambda ph, i: (0, 0))],
        out_shape=[jax.ShapeDtypeStruct((t, LANES), F32),
                   jax.ShapeDtypeStruct((nt * MOE_CH, LANES), jnp.int32),
                   jax.ShapeDtypeStruct((ntp, LANES), jnp.int32)],
        scratch_shapes=[pltpu.VMEM((1, LANES), F32), pltpu.VMEM((1, LANES), F32)],
        compiler_params=_cparams("arbitrary", "arbitrary"),
    )(route)


def _pick_onehot(lp):
    pos = _iota((lp.shape[0], MOE_LB), 1).astype(F32)
    return pos == lp[:, 0:1], pos == lp[:, 1:2]


def _chunk_loop(n, fn):
    def body(j, carry):
        fn(j)
        return carry

    lax.fori_loop(0, n, body, 0)


def _dispatch_kernel(dc_ref, h_ref, lp_ref, xs_ref, buf_ref, sem, nprev_ref):
    t, nt = pl.program_id(0), pl.num_programs(0)
    slot = t % 2
    tk = h_ref.shape[0]
    lp = lp_ref[...]
    oh1, oh2 = _pick_onehot(lp)
    lane = _iota((tk, LANES), 1)

    def split3(w):
        hi = w.astype(BF16).astype(F32)
        mid = (w - hi).astype(BF16).astype(F32)
        low = w - hi - mid
        return jnp.where(lane == 0, hi, jnp.where(lane == 1, mid, jnp.where(lane == 2, low, 0.0))).astype(BF16)

    b1, b2 = jnp.where(oh1, 1.0, 0.0).astype(BF16), jnp.where(oh2, 1.0, 0.0).astype(BF16)
    buf_ref[slot, :, 0:D_MODEL] = _dot_tn(b1 + b2, h_ref[...]).astype(BF16)
    buf_ref[slot, :, D_MODEL:] = (_dot_tn(b1, split3(lp[:, 2:3])) + _dot_tn(b2, split3(lp[:, 3:4]))).astype(BF16)

    def chunk_copy(j, s):
        src = buf_ref.at[s, pl.ds(pl.multiple_of(j * ROW_CHUNK, ROW_CHUNK), ROW_CHUNK), :]
        dst = xs_ref.at[pl.ds(pl.multiple_of(dc_ref[0, 0, j], ROW_CHUNK), ROW_CHUNK), :]
        return pltpu.make_async_copy(src, dst, sem.at[s])

    nch = dc_ref[0, 0, MOE_CH - 1]
    _chunk_loop(nch, lambda j: chunk_copy(j, slot).start())

    @pl.when(t > 0)
    def _():
        _chunk_loop(nprev_ref[0], lambda j: chunk_copy(0, 1 - slot).wait())

    nprev_ref[0] = nch

    @pl.when(t == nt - 1)
    def _():
        _chunk_loop(nch, lambda j: chunk_copy(0, slot).wait())


def _moe_dispatch(dc3, h2, lp, n_rows):
    t, d = h2.shape
    nt = t // MOE_TK
    return pl.pallas_call(
        _dispatch_kernel,
        grid=(nt,),
        in_specs=[pl.BlockSpec((1, 1, MOE_CH), lambda i: (i, 0, 0), memory_space=pltpu.SMEM),
                  pl.BlockSpec((MOE_TK, d), lambda i: (i, 0)),
                  pl.BlockSpec((MOE_TK, LANES), lambda i: (i, 0))],
        out_specs=pl.BlockSpec(memory_space=pl.ANY),
        out_shape=jax.ShapeDtypeStruct((n_rows, MOE_XW), BF16),
        scratch_shapes=[pltpu.VMEM((2, MOE_LB, MOE_XW), BF16), pltpu.SemaphoreType.DMA((2,)),
                        pltpu.SMEM((1,), jnp.int32)],
        compiler_params=_cparams("arbitrary"),
    )(dc3, h2, lp)


def _experts_kernel(te_ref, tv_ref, nu_ref, x_ref, w1_ref, w3_ref, w2_ref, o_ref, w1b, w3b, w2b):
    i = pl.program_id(0)

    @pl.when(i < nu_ref[0])
    def _():
        @pl.when((i == 0) | (te_ref[i] != te_ref[jnp.maximum(i - 1, 0)]))
        def _():
            w1b[...] = w1_ref[0].astype(BF16)
            w3b[...] = w3_ref[0].astype(BF16)
            w2b[...] = w2_ref[0].astype(BF16)

        tm = x_ref.shape[0]
        valid = _iota((tm, 1), 0) < tv_ref[i]
        h = jnp.where(valid, x_ref[:, :D_MODEL], jnp.zeros((), BF16))
        wx = x_ref[:, D_MODEL:].astype(F32)
        wv = jnp.where(valid, wx[:, 0:1] + wx[:, 1:2] + wx[:, 2:3], 0.0)
        he = _silu(_dot(h, w1b[...])) * _dot(h, w3b[...])
        o_ref[...] = (wv * _dot(he.astype(BF16), w2b[...])).astype(BF16)


def _moe_experts(layer, te, tv, nu, xs, w1, w3, w2):
    n_rows = xs.shape[0]
    _, _, d, de = w1.shape
    cur = lambda i, te, tv, nu: jnp.minimum(i, nu[0] - 1)
    wspec = lambda a, b: pl.BlockSpec((None, 1, a, b), lambda i, te, tv, nu: (layer, te[cur(i, te, tv, nu)], 0, 0))
    return pl.pallas_call(
        _experts_kernel,
        grid_spec=pltpu.PrefetchScalarGridSpec(
            num_scalar_prefetch=3,
            grid=(n_rows // MOE_TM,),
            in_specs=[pl.BlockSpec((MOE_TM, MOE_XW), lambda i, te, tv, nu: (cur(i, te, tv, nu), 0)),
                      wspec(d, de), wspec(d, de), wspec(de, d)],
            out_specs=pl.BlockSpec((MOE_TM, d), lambda i, te, tv, nu: (cur(i, te, tv, nu), 0)),
            scratch_shapes=[pltpu.VMEM((d, de), BF16), pltpu.VMEM((d, de), BF16), pltpu.VMEM((de, d), BF16)]),
        out_shape=jax.ShapeDtypeStruct((n_rows, d), BF16),
        compiler_params=_cparams("arbitrary"),
    )(te, tv, nu, xs, w1, w3, w2)


def _combine_kernel(dc_ref, dcn_ref, lp_ref, x1_ref, gt_ref, ys_ref, o_ref, buf_ref, sem):
    t, nt = pl.program_id(0), pl.num_programs(0)
    slot = t % 2

    def chunk_copy(tab, j, s):
        src = ys_ref.at[pl.ds(pl.multiple_of(tab[0, 0, j], ROW_CHUNK), ROW_CHUNK), :]
        dst = buf_ref.at[s, pl.ds(pl.multiple_of(j * ROW_CHUNK, ROW_CHUNK), ROW_CHUNK), :]
        return pltpu.make_async_copy(src, dst, sem.at[s])

    nch = dc_ref[0, 0, MOE_CH - 1]

    @pl.when(t == 0)
    def _():
        buf_ref[...] = jnp.zeros_like(buf_ref)
        _chunk_loop(nch, lambda j: chunk_copy(dc_ref, j, slot).start())

    @pl.when(t + 1 < nt)
    def _():
        _chunk_loop(dcn_ref[0, 0, MOE_CH - 1], lambda j: chunk_copy(dcn_ref, j, 1 - slot).start())

    _chunk_loop(nch, lambda j: chunk_copy(dc_ref, 0, slot).wait())
    live = _iota((MOE_LB, 1), 0) < nch * ROW_CHUNK
    local = jnp.where(live, buf_ref[slot], jnp.zeros((), BF16))
    oh1, oh2 = _pick_onehot(lp_ref[...])
    perm = jnp.where(oh1 | oh2, 1.0, 0.0).astype(BF16)
    o_ref[...] = x1_ref[...] + gt_ref[0] * _dot(perm, local)


def _moe_combine(dc3, lp, x1, gt, ys, tokens_per_row):
    t, d = x1.shape
    nt = t // MOE_TK
    smem = lambda f: pl.BlockSpec((1, 1, MOE_CH), f, memory_space=pltpu.SMEM)
    return pl.pallas_call(
        _combine_kernel,
        grid=(nt,),
        in_specs=[smem(lambda i: (i, 0, 0)), smem(lambda i: (jnp.minimum(i + 1, nt - 1), 0, 0)),
                  pl.BlockSpec((MOE_TK, LANES), lambda i: (i, 0)),
                  pl.BlockSpec((MOE_TK, d), lambda i: (i, 0)),
                  _mod_spec(gt, MOE_TK, tokens_per_row),
                  pl.BlockSpec(memory_space=pl.ANY)],
        out_specs=pl.BlockSpec((MOE_TK, d), lambda i: (i, 0)),
        out_shape=jax.ShapeDtypeStruct((t, d), F32),
        scratch_shapes=[pltpu.VMEM((2, MOE_LB, d), BF16), pltpu.SemaphoreType.DMA((2,))],
        compiler_params=_cparams("arbitrary"),
    )(dc3, dc3, lp, x1, gt, ys)


def _moe_sparse(layer, h2, route, x1, gt, w1, w3, w2, tokens_per_row):
    t = h2.shape[0]
    nt = t // MOE_TK
    lp, dc, tmeta = _moe_plan(route)
    dc3 = dc[:, 0].reshape(nt, 1, MOE_CH)
    n_tiles = _moe_sorted_tiles(t)
    xs = _moe_dispatch(dc3, h2, lp, n_tiles * MOE_TM)
    ys = _moe_experts(layer, tmeta[:n_tiles, 0], tmeta[:n_tiles, 1], tmeta[0:1, 2], xs, w1, w3, w2)
    return _moe_combine(dc3, lp, x1, gt, ys, tokens_per_row)


def _permute_w_in(w):
    sizes = (DN_QKV, DN_HEADS * DN_DV, DN_HEADS, DN_HEADS,
             GLA_HEADS * GLA_DK, GLA_HEADS * GLA_DK, GLA_HEADS * GLA_DV, GLA_HEADS * GLA_DV, GLA_RANK,
             SWA_HEADS * SWA_HD, SWA_KV * SWA_HD, SWA_KV * SWA_HD, N_BRANCH * D_MODEL)
    offs = [0]
    for s in sizes:
        offs.append(offs[-1] + s)
    seg = lambda i: w[..., offs[i]:offs[i + 1]]
    (a_qkv, a_z, a_b, a_a, b_q, b_k, b_v, b_r, b_lr, c_q, c_k, c_v, gate) = (seg(i) for i in range(len(sizes)))
    small = jnp.concatenate([a_b, a_a, b_lr], -1)
    fill = jnp.zeros(w.shape[:-1] + (P_PAD - C_SM - small.shape[-1],), w.dtype)
    return jnp.concatenate([gate, a_qkv, a_z, b_v, b_r, c_q, b_q, b_k, c_k, c_v, small, fill], -1)


def _lane_row(v, off):
    depth, n = v.shape
    return jnp.zeros((depth, 1, LANES), F32).at[:, 0, off:off + n].set(v.astype(F32))


def kernel(x_prompt, x_sample, c_prompt, c_sample, state_dn, state_dn_conv, state_gla, cache_swa_k, cache_swa_v, ln1_g, ln2_g, ada_w, ada_b, w_in, dn_conv_w, dn_a_log, dn_dt_bias, dn_onorm_g, gla_w2, gla_b, gla_onorm_g, swa_qnorm_g, swa_knorm_g, swa_sinks, w_branch, w_out, rc_w, rc_b, re_w, re_b, w1, w3, w2):
    batch, seq, d = x_prompt.shape
    nb = x_sample.shape[0]
    depth = w_in.shape[0]
    assert x_sample.shape[1] == 1 and d == D_MODEL and seq % CHUNK == 0 and nb % DEC_TILE == 0
    kvw = SWA_KV * SWA_HD

    w_in_p = _permute_w_in(w_in.astype(BF16))
    wb_b, wo_b = w_branch.astype(BF16), w_out.astype(BF16)
    wr = jnp.concatenate([rc_w, re_w, jnp.zeros((depth, d, LANES - N_GROUPS - N_EXPERTS), F32)], -1)
    rb = jnp.concatenate([rc_b, re_b, jnp.zeros((depth, LANES - N_GROUPS - N_EXPERTS), F32)], -1)[:, None, :]
    wr_hi = wr.astype(BF16)
    wr2 = jnp.concatenate([wr_hi, (wr - wr_hi.astype(F32)).astype(BF16)], -1)
    ln2_all = ln2_g[:, None, :]
    alog_row = _lane_row(dn_a_log, SM_G)
    dtb_row = _lane_row(dn_dt_bias, SM_G)
    w2pad = jnp.zeros((depth, LANES, GLA_HEADS * GLA_DK), F32).at[:, SM_LR:SM_LR + GLA_RANK].set(gla_w2).astype(BF16)
    kn2 = jnp.concatenate([swa_knorm_g] * SWA_KV, -1)[:, None, :]
    slopes = jnp.exp2(-8.0 * jnp.arange(1, SWA_HEADS + 1, dtype=F32) / SWA_HEADS)
    slp_col = jnp.tile(slopes, DEC_TILE)[:, None]

    pad_rows = (-(batch + nb)) % 8
    c_all = jnp.concatenate([c_prompt, c_sample, jnp.zeros((pad_rows, d), F32)], 0)
    mod = _ada(c_all, ada_w, ada_b)

    xp = x_prompt.reshape(batch * seq, d)
    xs = x_sample.reshape(nb, d)
    tm_p = 2048 if seq % 2048 == 0 else CHUNK
    tm_m = 512 if seq % 512 == 0 else CHUNK

    dec_states = (state_dn,
                  state_dn_conv.reshape(depth, nb, (CONV_W - 1) * DN_QKV),
                  state_gla.reshape(depth, nb, GLA_HEADS // 2, 2 * GLA_DK, GLA_DV),
                  cache_swa_k.reshape(depth, nb, WINDOW, kvw),
                  cache_swa_v.reshape(depth, nb, WINDOW, kvw))
    dec_out = None
    st_p = []
    for l in range(depth):
        mp = [m[:, None, :] for m in jnp.split(mod[l, :batch], 6, -1)]
        ms = [m[None] for m in jnp.split(mod[l, batch:batch + nb], 6, -1)]
        ln1, ln2 = ln1_g[l][None], ln2_g[l][None]
        conv_w = dn_conv_w[l]
        dn_on, gla_on = dn_onorm_g[l][None], gla_onorm_g[l][None]
        glb = gla_b[l][None]
        qn = swa_qnorm_g[l][None]
        snk = swa_sinks[l][None]

        pp = _in_proj(l, xp, mp[1], mp[0], ln1, w_in_p, tm_p, seq)
        pp3 = pp.reshape(batch, seq, P_PAD)
        ba, dn_s = _dn_prompt_mb(pp3, conv_w, alog_row[l], dtb_row[l], dn_on)
        bb, gla_st = _gla_prompt_mb(pp3, w2pad[l], glb, gla_on)
        bc, k_new = _swa_prompt_mb(pp3, jnp.concatenate([qn, qn], -1), kn2[l], snk)
        ba, bb, bc = (z.reshape(batch * seq, BRANCH_W) for z in (ba, bb, bc))
        x1, h2, comb = _merge(l, ba, bb, bc, pp, xp, mp[2], mp[4], mp[3], ln2_all, wb_b, wo_b, wr2, rb,
                              tm_m, seq)
        xp = _moe_sparse(l, h2, comb, x1, mp[5], w1, w3, w2, seq)
        pp3 = pp.reshape(batch, seq, P_PAD)
        st_p.append((dn_s,
                     pp3[:, seq - (CONV_W - 1):, C_AQKV:C_AQKV + DN_QKV],
                     jnp.swapaxes(gla_st, -1, -2),
                     k_new.reshape(batch, WINDOW, SWA_KV, SWA_HD),
                     pp3[:, seq - WINDOW:, C_CV:C_CV + kvw].reshape(batch, WINDOW, SWA_KV, SWA_HD)))

        ps = _in_proj(l, xs, ms[1], ms[0], ln1, w_in_p, nb, nb)
        cq_r = ps[:, C_CQ:C_CQ + SWA_HEADS * SWA_HD].reshape(nb * SWA_HEADS, SWA_HD)
        oa, ob, oc_r, *dec_out = _dec_mixers(
            l, ps, cq_r, dec_states, dec_out, conv_w, alog_row[l], dtb_row[l], dn_on, w2pad[l], glb, gla_on,
            qn, kn2[l], jnp.tile(swa_sinks[l], DEC_TILE)[:, None], slp_col)
        oc = oc_r.reshape(nb, SWA_HEADS * SWA_HD)
        x1, h2, comb = _merge(l, oa, ob, oc, ps, xs, ms[2], ms[4], ms[3], ln2_all, wb_b, wo_b, wr2, rb,
                              nb, nb)
        xs = _moe(l, h2, comb, x1, ms[5], w1, w3, w2, nb, nb)

    dn_p, conv_p, gla_p, k_p, v_p = [jnp.stack(z) for z in zip(*st_p)]
    dn_s = dec_out[0]
    conv_s = dec_out[1].reshape(depth, nb, CONV_W - 1, DN_QKV)
    gla_s = dec_out[2].reshape(depth, nb, GLA_HEADS, GLA_DK, GLA_DV)
    k_s = dec_out[3].reshape(depth, nb, WINDOW, SWA_KV, SWA_HD)
    v_s = dec_out[4].reshape(depth, nb, WINDOW, SWA_KV, SWA_HD)
    return (xp.reshape(batch, seq, d), xs.reshape(nb, 1, d), dn_p, dn_s, conv_p, conv_s, gla_p, gla_s,
            k_p, k_s, v_p, v_s)
```

```python
import functools

import jax
import jax.numpy as jnp
from jax import lax
from jax.experimental import pallas as pl
from jax.experimental.pallas import tpu as pltpu

F32 = jnp.float32
BF16 = jnp.bfloat16
HIGHEST = lax.Precision.HIGHEST

D_MODEL = 1024
DN_HEADS, DN_DK, DN_DV, CONV_W = 4, 128, 128, 4
DN_QKV = 2 * DN_HEADS * DN_DK + DN_HEADS * DN_DV
GLA_HEADS, GLA_DK, GLA_DV, GLA_RANK, GLA_TAU = 4, 64, 128, 16, 16.0
SWA_HEADS, SWA_KV, SWA_HD, WINDOW = 8, 2, 64, 128
N_BRANCH, BRANCH_W = 3, 512
N_GROUPS, EXP_PER_GROUP, TOP_K, D_EXPERT = 4, 8, 2, 256
N_EXPERTS = N_GROUPS * EXP_PER_GROUP
EPS = 1e-6

LANES = 128
CHUNK = 128
SUB = 16
VMEM_LIMIT = 56 * 1024 * 1024

C_GATE, C_AQKV, C_AZ, C_BV, C_BR, C_CQ = 0, 3072, 4608, 5120, 5632, 6144
C_BQ, C_BK, C_CK, C_CV, C_SM, P_PAD = 6656, 6912, 7168, 7296, 7424, 7680
SM_BETA, SM_G, SM_LR = 0, DN_HEADS, 2 * DN_HEADS
R_EXP = N_GROUPS


def _cparams(*sem):
    return pltpu.CompilerParams(dimension_semantics=sem, vmem_limit_bytes=VMEM_LIMIT)


def _dot(a, b, precision=None):
    return jnp.dot(a, b, preferred_element_type=F32, precision=precision)


def _dot_nt(a, b, precision=None):
    return lax.dot_general(a, b, (((1,), (1,)), ((), ())), preferred_element_type=F32, precision=precision)


def _dot_tn(a, b, precision=None):
    return lax.dot_general(a, b, (((0,), (0,)), ((), ())), preferred_element_type=F32, precision=precision)


def _silu(x):
    return x * jax.nn.sigmoid(x)


def _iota(shape, dim):
    return lax.broadcasted_iota(jnp.int32, shape, dim)


def _ada_kernel(c_ref, w_ref, b_ref, o_ref):
    c = _silu(c_ref[...]).astype(BF16)
    o_ref[0] = _dot(c, w_ref[0].astype(BF16)) + b_ref[0]


def _ada(c_all, ada_w, ada_b):
    depth, d, n = ada_w.shape
    rows = c_all.shape[0]
    tn = 1536
    return pl.pallas_call(
        _ada_kernel,
        grid=(depth, n // tn),
        in_specs=[pl.BlockSpec((rows, d), lambda l, j: (0, 0)),
                  pl.BlockSpec((1, d, tn), lambda l, j: (l, 0, j)),
                  pl.BlockSpec((1, 1, tn), lambda l, j: (l, 0, j))],
        out_specs=pl.BlockSpec((1, rows, tn), lambda l, j: (l, 0, j)),
        out_shape=jax.ShapeDtypeStruct((depth, rows, n), F32),
        compiler_params=_cparams("parallel", "parallel"),
    )(c_all, ada_w, ada_b.reshape(depth, 1, n))


def _mod_spec(mod, tm, tokens_per_row):
    _, r, d = mod.shape
    assert tokens_per_row % tm == 0
    per = tokens_per_row // tm
    return pl.BlockSpec((1, r, d), lambda i, *_: (i // per, 0, 0))


def _in_proj_kernel(x_ref, sc_ref, sh_ref, g_ref, w_ref, o_ref, h_ref):
    @pl.when(pl.program_id(1) == 0)
    def _():
        x = x_ref[...]
        y = x * lax.rsqrt(jnp.mean(x * x, -1, keepdims=True) + EPS) * g_ref[...]
        h_ref[...] = (y * (1.0 + sc_ref[0]) + sh_ref[0]).astype(BF16)

    o_ref[...] = _dot(h_ref[...], w_ref[...])


def _in_proj(layer, x, sc, sh, g, w, tm, tokens_per_row):
    t, d = x.shape
    n = w.shape[2]
    tn = 768 if tm >= 2048 else 1536
    return pl.pallas_call(
        _in_proj_kernel,
        grid=(t // tm, n // tn),
        in_specs=[pl.BlockSpec((tm, d), lambda i, j: (i, 0)),
                  _mod_spec(sc, tm, tokens_per_row), _mod_spec(sh, tm, tokens_per_row),
                  pl.BlockSpec((1, d), lambda i, j: (0, 0)),
                  pl.BlockSpec((None, d, tn), lambda i, j: (layer, 0, j))],
        out_specs=pl.BlockSpec((tm, tn), lambda i, j: (i, j)),
        out_shape=jax.ShapeDtypeStruct((t, n), F32),
        scratch_shapes=[pltpu.VMEM((tm, d), BF16)],
        compiler_params=_cparams("parallel", "arbitrary"),
    )(x, sc, sh, g, w)


def _strict_lower_inverse_minus_eye(a):
    c = a.shape[0]
    row, col = _iota((c, c), 0), _iota((c, c), 1)
    diag_blk = (row // SUB) == (col // SUB)
    ad = jnp.where(diag_blk, a, 0.0)
    ao = a - ad
    n = -ad
    p = n
    steps = SUB.bit_length() - 2
    for _ in range(steps):
        p = _dot(p, p)
        n = n + p + _dot(n, p)
    bm = -(ao + _dot(n, ao))
    m = bm
    q = bm
    steps = (c // SUB).bit_length() - 2
    for _ in range(steps):
        q = _dot(q, q)
        m = m + q + _dot(m, q)
    return m + n + _dot(m, n)


def _strict_lower_inverse_minus_eye_multi(a_list):
    c = a_list[0].shape[0]
    diag_blk = (_iota((c, c), 0) // SUB) == (_iota((c, c), 1) // SUB)
    idx = range(len(a_list))
    ad = [jnp.where(diag_blk, a, 0.0) for a in a_list]
    ao = [a_list[i] - ad[i] for i in idx]
    n = [-x for x in ad]
    p = n
    for _ in range(SUB.bit_length() - 2):
        p = [_dot(x, x) for x in p]
        np_ = [_dot(n[i], p[i]) for i in idx]
        n = [n[i] + p[i] + np_[i] for i in idx]
    nao = [_dot(n[i], ao[i]) for i in idx]
    m = [-(ao[i] + nao[i]) for i in idx]
    q = m
    for _ in range((c // SUB).bit_length() - 2):
        q = [_dot(x, x) for x in q]
        mq = [_dot(m[i], q[i]) for i in idx]
        m = [m[i] + q[i] + mq[i] for i in idx]
    mn = [_dot(m[i], n[i]) for i in idx]
    return [m[i] + n[i] + mn[i] for i in idx]


def _dn_kernel(qkv_ref, z_ref, sm_ref, cw_ref, alog_ref, dtb_ref, on_ref,
               o_ref, s_out_ref, s_ref, xp_ref):
    c_idx = pl.program_id(1)
    c = CHUNK
    pad = 8

    @pl.when(c_idx == 0)
    def _():
        s_ref[...] = jnp.zeros_like(s_ref)
        xp_ref[0:pad, :] = jnp.zeros((pad, DN_QKV), F32)

    xp_ref[pad:pad + c, :] = qkv_ref[...]
    acc = xp_ref[pad - 3:pad - 3 + c, :] * cw_ref[0:1, :]
    for j in range(1, CONV_W):
        acc = acc + xp_ref[pad - 3 + j:pad - 3 + j + c, :] * cw_ref[j:j + 1, :]
    y = _silu(acc)
    xp_ref[pad - 3:pad, :] = xp_ref[pad + c - 3:pad + c, :]

    sm = sm_ref[...]
    beta_all = jax.nn.sigmoid(sm)
    g_all = -jnp.exp(alog_ref[...]) * jax.nn.softplus(sm + dtb_ref[...])
    row, col = _iota((c, c), 0), _iota((c, c), 1)
    incl = row >= col
    strict = row > col
    gam_all = _dot(incl.astype(F32), g_all, HIGHEST)
    gam_t = gam_all.T
    hk = DN_HEADS * DN_DK
    for h in range(DN_HEADS):
        q = y[:, h * DN_DK:(h + 1) * DN_DK]
        k = y[:, hk + h * DN_DK:hk + (h + 1) * DN_DK]
        v = y[:, 2 * hk + h * DN_DV:2 * hk + (h + 1) * DN_DV]
        q = q * lax.rsqrt(jnp.sum(q * q, -1, keepdims=True) + EPS) * DN_DK ** -0.5
        k = k * lax.rsqrt(jnp.sum(k * k, -1, keepdims=True) + EPS)
        beta = beta_all[:, SM_BETA + h:SM_BETA + h + 1]
        gam = gam_all[:, SM_G + h:SM_G + h + 1]
        gam_row = gam_t[SM_G + h:SM_G + h + 1, :]
        dec = jnp.where(incl, jnp.exp(jnp.where(incl, gam - gam_row, 0.0)), 0.0)
        eg = jnp.exp(gam)
        gl = gam[c - 1:c, :]
        kb = k * beta
        kbf = k.astype(BF16)
        a = jnp.where(strict, _dot_nt(kb.astype(BF16), kbf) * dec, 0.0)
        w = _strict_lower_inverse_minus_eye(a)
        rhs = jnp.concatenate([v * beta, kb * eg], -1)
        sol = rhs + _dot(w, rhs)
        s = s_ref[h]
        sbf = s.astype(BF16)
        u = sol[:, :DN_DV] - _dot(sol[:, DN_DV:].astype(BF16), sbf)
        ubf = u.astype(BF16)
        qk = _dot_nt(q.astype(BF16), kbf) * dec
        o = _dot((q * eg).astype(BF16), sbf) + _dot(qk.astype(BF16), ubf)
        kd = k * jnp.exp(gl - gam)
        s_ref[h] = s * jnp.exp(gl) + _dot_tn(kd.astype(BF16), ubf)
        o = o * lax.rsqrt(jnp.mean(o * o, -1, keepdims=True) + EPS) * on_ref[...]
        o_ref[:, h * DN_DV:(h + 1) * DN_DV] = (o * _silu(z_ref[:, h * DN_DV:(h + 1) * DN_DV])).astype(BF16)

    @pl.when(c_idx == pl.num_programs(1) - 1)
    def _():
        s_out_ref[0] = s_ref[...]


def _dn_prompt(p, batch, seq, conv_w, alog_row, dtb_row, onorm):
    c = CHUNK
    nc = seq // c
    return pl.pallas_call(
        _dn_kernel,
        grid=(batch, nc),
        in_specs=[pl.BlockSpec((c, DN_QKV), lambda b, i: (b * nc + i, C_AQKV // DN_QKV)),
                  pl.BlockSpec((c, BRANCH_W), lambda b, i: (b * nc + i, C_AZ // BRANCH_W)),
                  pl.BlockSpec((c, LANES), lambda b, i: (b * nc + i, C_SM // LANES)),
                  pl.BlockSpec((CONV_W, DN_QKV), lambda b, i: (0, 0)),
                  pl.BlockSpec((1, LANES), lambda b, i: (0, 0)),
                  pl.BlockSpec((1, LANES), lambda b, i: (0, 0)),
                  pl.BlockSpec((1, DN_DV), lambda b, i: (0, 0))],
        out_specs=[pl.BlockSpec((c, BRANCH_W), lambda b, i: (b * nc + i, 0)),
                   pl.BlockSpec((1, DN_HEADS, DN_DK, DN_DV), lambda b, i: (b, 0, 0, 0))],
        out_shape=[jax.ShapeDtypeStruct((batch * seq, BRANCH_W), BF16),
                   jax.ShapeDtypeStruct((batch, DN_HEADS, DN_DK, DN_DV), F32)],
        scratch_shapes=[pltpu.VMEM((DN_HEADS, DN_DK, DN_DV), F32),
                        pltpu.VMEM((c + 8, DN_QKV), F32)],
        compiler_params=_cparams("parallel", "arbitrary"),
    )(p, p, p, conv_w, alog_row, dtb_row, onorm)


EXP_CAP = 80.0


def _gla_kernel(q_ref, k_ref, v_ref, r_ref, sm_ref, w2_ref, b_ref, on_ref,
                o_ref, s_out_ref, st_ref):
    c_idx = pl.program_id(1)
    c = CHUNK
    hw = GLA_HEADS * GLA_DK

    @pl.when(c_idx == 0)
    def _():
        st_ref[...] = jnp.zeros_like(st_ref)

    lg = jax.nn.log_sigmoid(_dot(sm_ref[...].astype(BF16), w2_ref[...]) + b_ref[...]) / GLA_TAU
    row, col = _iota((c, c), 0), _iota((c, c), 1)
    gam = _dot((row >= col).astype(F32), lg, HIGHEST)
    gl = gam[c - 1:c, :]
    qs = q_ref[...] * GLA_DK ** -0.5
    k = k_ref[...]
    qg = (qs * jnp.exp(gam)).astype(BF16)
    kd = (k * jnp.exp(gl - gam)).astype(BF16)
    egl = jnp.exp(gl)

    att_rows = [[] for _ in range(GLA_HEADS)]
    for i in range(c // SUB):
        lo, hi = i * SUB, (i + 1) * SUB
        ref_pt = gam[lo - 1:lo, :] if i > 0 else jnp.zeros((1, hw), F32)
        qi = (qs[lo:hi] * jnp.exp(gam[lo:hi] - ref_pt)).astype(BF16)
        ki = (k[:hi] * jnp.exp(jnp.minimum(ref_pt - gam[:hi], EXP_CAP))).astype(BF16)
        keep = (_iota((SUB, hi), 0) + lo) >= _iota((SUB, hi), 1)
        for h in range(GLA_HEADS):
            sl = slice(h * GLA_DK, (h + 1) * GLA_DK)
            att = jnp.where(keep, _dot_nt(qi[:, sl], ki[:, sl]), 0.0)
            att_rows[h].append(_dot(att.astype(BF16), v_ref[:hi, h * GLA_DV:(h + 1) * GLA_DV].astype(BF16)))

    for h in range(GLA_HEADS):
        sl = slice(h * GLA_DK, (h + 1) * GLA_DK)
        vh = v_ref[:, h * GLA_DV:(h + 1) * GLA_DV].astype(BF16)
        st = st_ref[h]
        o = _dot_nt(qg[:, sl], st.astype(BF16)) + jnp.concatenate(att_rows[h], 0)
        st_ref[h] = st * egl[:, sl] + _dot_tn(vh, kd[:, sl])
        o = o * lax.rsqrt(jnp.mean(o * o, -1, keepdims=True) + EPS) * on_ref[...]
        o_ref[:, h * GLA_DV:(h + 1) * GLA_DV] = (o * _silu(r_ref[:, h * GLA_DV:(h + 1) * GLA_DV])).astype(BF16)

    @pl.when(c_idx == pl.num_programs(1) - 1)
    def _():
        s_out_ref[0] = st_ref[...]


def _gla_prompt(p, batch, seq, w2pad, gla_b, onorm):
    c = CHUNK
    nc = seq // c
    hw = GLA_HEADS * GLA_DK
    return pl.pallas_call(
        _gla_kernel,
        grid=(batch, nc),
        in_specs=[pl.BlockSpec((c, hw), lambda b, i: (b * nc + i, C_BQ // hw)),
                  pl.BlockSpec((c, hw), lambda b, i: (b * nc + i, C_BK // hw)),
                  pl.BlockSpec((c, BRANCH_W), lambda b, i: (b * nc + i, C_BV // BRANCH_W)),
                  pl.BlockSpec((c, BRANCH_W), lambda b, i: (b * nc + i, C_BR // BRANCH_W)),
                  pl.BlockSpec((c, LANES), lambda b, i: (b * nc + i, C_SM // LANES)),
                  pl.BlockSpec((LANES, hw), lambda b, i: (0, 0)),
                  pl.BlockSpec((1, hw), lambda b, i: (0, 0)),
                  pl.BlockSpec((1, GLA_DV), lambda b, i: (0, 0))],
        out_specs=[pl.BlockSpec((c, BRANCH_W), lambda b, i: (b * nc + i, 0)),
                   pl.BlockSpec((1, GLA_HEADS, GLA_DV, GLA_DK), lambda b, i: (b, 0, 0, 0))],
        out_shape=[jax.ShapeDtypeStruct((batch * seq, BRANCH_W), BF16),
                   jax.ShapeDtypeStruct((batch, GLA_HEADS, GLA_DV, GLA_DK), F32)],
        scratch_shapes=[pltpu.VMEM((GLA_HEADS, GLA_DV, GLA_DK), F32)],
        compiler_params=_cparams("parallel", "arbitrary"),
    )(p, p, p, p, p, w2pad, gla_b, onorm)


def _half_rms(x, g2):
    lane = _iota(x.shape, 1)
    first = lane < SWA_HD
    sq = x * x
    s0 = jnp.sum(jnp.where(first, sq, 0.0), -1, keepdims=True)
    s1 = jnp.sum(jnp.where(first, 0.0, sq), -1, keepdims=True)
    ms = jnp.where(first, s0, s1) * (1.0 / SWA_HD)
    return x * lax.rsqrt(ms + EPS) * g2


def _swa_kernel(q_ref, kp_ref, kc_ref, vp_ref, vc_ref, qn_ref, kn_ref, snk_ref,
                o_ref, knew_ref):
    i = pl.program_id(1)
    w = WINDOW
    kc = _half_rms(kc_ref[...], kn_ref[...])
    kk = jnp.concatenate([_half_rms(kp_ref[...], kn_ref[...]), kc], 0).astype(BF16)
    vv = jnp.concatenate([vp_ref[...], vc_ref[...]], 0).astype(BF16)
    t = _iota((w, 2 * w), 0)
    j = _iota((w, 2 * w), 1)
    dist = w + t - j
    valid = (dist >= 0) & (dist <= w) & ((j >= w) | (i > 0))
    distf = dist.astype(F32)
    g = SWA_HEADS // SWA_KV
    for h in range(SWA_HEADS):
        kv = h // g
        q = q_ref[:, h * SWA_HD:(h + 1) * SWA_HD]
        q = q * lax.rsqrt(jnp.mean(q * q, -1, keepdims=True) + EPS) * qn_ref[...] * SWA_HD ** -0.5
        s = _dot_nt(q.astype(BF16), kk[:, kv * SWA_HD:(kv + 1) * SWA_HD])
        s = s - (2.0 ** (-8.0 * (h + 1) / SWA_HEADS)) * distf
        s = jnp.where(valid, s, -jnp.inf)
        snk = snk_ref[:, h:h + 1]
        m = jnp.maximum(jnp.max(s, -1, keepdims=True), snk)
        pr = jnp.exp(s - m)
        pr = pr / (jnp.sum(pr, -1, keepdims=True) + jnp.exp(snk - m))
        o = _dot(pr.astype(BF16), vv[:, kv * SWA_HD:(kv + 1) * SWA_HD])
        o_ref[:, h * SWA_HD:(h + 1) * SWA_HD] = o.astype(BF16)

    @pl.when(i == pl.num_programs(1) - 1)
    def _():
        knew_ref[0] = kc


def _swa_prompt(p, batch, seq, qn, kn2, sinks):
    w = WINDOW
    nw = seq // w
    kvw = SWA_KV * SWA_HD
    return pl.pallas_call(
        _swa_kernel,
        grid=(batch, nw),
        in_specs=[pl.BlockSpec((w, BRANCH_W), lambda b, i: (b * nw + i, C_CQ // BRANCH_W)),
                  pl.BlockSpec((w, kvw), lambda b, i: (b * nw + jnp.maximum(i - 1, 0), C_CK // kvw)),
                  pl.BlockSpec((w, kvw), lambda b, i: (b * nw + i, C_CK // kvw)),
                  pl.BlockSpec((w, kvw), lambda b, i: (b * nw + jnp.maximum(i - 1, 0), C_CV // kvw)),
                  pl.BlockSpec((w, kvw), lambda b, i: (b * nw + i, C_CV // kvw)),
                  pl.BlockSpec((1, SWA_HD), lambda b, i: (0, 0)),
                  pl.BlockSpec((1, kvw), lambda b, i: (0, 0)),
                  pl.BlockSpec((1, SWA_HEADS), lambda b, i: (0, 0))],
        out_specs=[pl.BlockSpec((w, BRANCH_W), lambda b, i: (b * nw + i, 0)),
                   pl.BlockSpec((1, w, kvw), lambda b, i: (b, 0, 0))],
        out_shape=[jax.ShapeDtypeStruct((batch * seq, BRANCH_W), BF16),
                   jax.ShapeDtypeStruct((batch, w, kvw), F32)],
        compiler_params=_cparams("parallel", "arbitrary"),
    )(p, p, p, p, p, qn, kn2, sinks)


def _dn_mb_kernel(qkv_ref, z_ref, sm_ref, cw_ref, alog_ref, dtb_ref, on_ref,
                  o_ref, s_out_ref, s_ref, xp_ref, y_ref):
    c_idx = pl.program_id(0)
    nb = qkv_ref.shape[0]
    c = CHUNK
    pad = 8

    @pl.when(c_idx == 0)
    def _():
        s_ref[...] = jnp.zeros_like(s_ref)
        xp_ref[:, 0:pad, :] = jnp.zeros((nb, pad, DN_QKV), F32)

    row, col = _iota((c, c), 0), _iota((c, c), 1)
    incl = row >= col
    strict = row > col
    incl_f = incl.astype(F32)
    hk = DN_HEADS * DN_DK
    gam_all, gam_t, beta_all = [], [], []
    for b in range(nb):
        xp_ref[b, pad:pad + c, :] = qkv_ref[b]
        acc = xp_ref[b, pad - 3:pad - 3 + c, :] * cw_ref[0:1, :]
        for j in range(1, CONV_W):
            acc = acc + xp_ref[b, pad - 3 + j:pad - 3 + j + c, :] * cw_ref[j:j + 1, :]
        y_ref[b] = _silu(acc)
        xp_ref[b, pad - 3:pad, :] = xp_ref[b, pad + c - 3:pad + c, :]
        sm = sm_ref[b]
        beta_all.append(jax.nn.sigmoid(sm))
        g_all = -jnp.exp(alog_ref[...]) * jax.nn.softplus(sm + dtb_ref[...])
        gam_all.append(_dot(incl_f, g_all, HIGHEST))
        gam_t.append(gam_all[b].T)

    chains = [(b, h) for b in range(nb) for h in range(DN_HEADS)]
    n = len(chains)
    q, k, dec, eg, gl, gam, kb, rhs, a = ([None] * n for _ in range(9))
    for i, (b, h) in enumerate(chains):
        qi = y_ref[b, :, h * DN_DK:(h + 1) * DN_DK]
        ki = y_ref[b, :, hk + h * DN_DK:hk + (h + 1) * DN_DK]
        vi = y_ref[b, :, 2 * hk + h * DN_DV:2 * hk + (h + 1) * DN_DV]
        q[i] = qi * lax.rsqrt(jnp.sum(qi * qi, -1, keepdims=True) + EPS) * DN_DK ** -0.5
        k[i] = ki * lax.rsqrt(jnp.sum(ki * ki, -1, keepdims=True) + EPS)
        beta = beta_all[b][:, SM_BETA + h:SM_BETA + h + 1]
        gam[i] = gam_all[b][:, SM_G + h:SM_G + h + 1]
        gam_row = gam_t[b][SM_G + h:SM_G + h + 1, :]
        dec[i] = jnp.where(incl, jnp.exp(jnp.minimum(gam[i] - gam_row, 0.0)), 0.0)
        eg[i] = jnp.exp(gam[i])
        gl[i] = gam[i][c - 1:c, :]
        kb[i] = k[i] * beta
        rhs[i] = jnp.concatenate([vi * beta, kb[i] * eg[i]], -1)
    kbf = [x.astype(BF16) for x in k]
    kk = [_dot_nt(kb[i].astype(BF16), kbf[i]) for i in range(n)]
    qk = [_dot_nt(q[i].astype(BF16), kbf[i]) for i in range(n)]
    a = [jnp.where(strict, kk[i] * dec[i], 0.0) for i in range(n)]
    w = _strict_lower_inverse_minus_eye_multi(a)
    sol = [rhs[i] + _dot(w[i], rhs[i]) for i in range(n)]
    s = [s_ref[b, h] for (b, h) in chains]
    sbf = [x.astype(BF16) for x in s]
    u = [sol[i][:, :DN_DV] - _dot(sol[i][:, DN_DV:].astype(BF16), sbf[i]) for i in range(n)]
    ubf = [x.astype(BF16) for x in u]
    o_s = [_dot((q[i] * eg[i]).astype(BF16), sbf[i]) for i in range(n)]
    o_u = [_dot((qk[i] * dec[i]).astype(BF16), ubf[i]) for i in range(n)]
    ds = [_dot_tn((k[i] * jnp.exp(gl[i] - gam[i])).astype(BF16), ubf[i]) for i in range(n)]
    for i, (b, h) in enumerate(chains):
        s_ref[b, h] = s[i] * jnp.exp(gl[i]) + ds[i]
        o = o_s[i] + o_u[i]
        o = o * lax.rsqrt(jnp.mean(o * o, -1, keepdims=True) + EPS) * on_ref[...]
        o_ref[b, :, h * DN_DV:(h + 1) * DN_DV] = (
            o * _silu(z_ref[b, :, h * DN_DV:(h + 1) * DN_DV])).astype(BF16)

    @pl.when(c_idx == pl.num_programs(0) - 1)
    def _():
        s_out_ref[...] = s_ref[...]


def _dn_prompt_mb(p3, conv_w, alog_row, dtb_row, onorm):
    batch, seq, _ = p3.shape
    c = CHUNK
    full = lambda shape: pl.BlockSpec(shape, lambda i: tuple(0 for _ in shape))
    return pl.pallas_call(
        _dn_mb_kernel,
        grid=(seq // c,),
        in_specs=[pl.BlockSpec((batch, c, DN_QKV), lambda i: (0, i, C_AQKV // DN_QKV)),
                  pl.BlockSpec((batch, c, BRANCH_W), lambda i: (0, i, C_AZ // BRANCH_W)),
                  pl.BlockSpec((batch, c, LANES), lambda i: (0, i, C_SM // LANES)),
                  full((CONV_W, DN_QKV)), full((1, LANES)), full((1, LANES)), full((1, DN_DV))],
        out_specs=[pl.BlockSpec((batch, c, BRANCH_W), lambda i: (0, i, 0)),
                   full((batch, DN_HEADS, DN_DK, DN_DV))],
        out_shape=[jax.ShapeDtypeStruct((batch, seq, BRANCH_W), BF16),
                   jax.ShapeDtypeStruct((batch, DN_HEADS, DN_DK, DN_DV), F32)],
        scratch_shapes=[pltpu.VMEM((batch, DN_HEADS, DN_DK, DN_DV), F32),
                        pltpu.VMEM((batch, c + 8, DN_QKV), F32),
                        pltpu.VMEM((batch, c, DN_QKV), F32)],
        compiler_params=_cparams("arbitrary"),
    )(p3, p3, p3, conv_w, alog_row, dtb_row, onorm)


def _gla_mb_kernel(q_ref, k_ref, v_ref, r_ref, sm_ref, w2_ref, b_ref, on_ref,
                   o_ref, s_out_ref, st_ref):
    c_idx = pl.program_id(0)
    nb = q_ref.shape[0]
    c = CHUNK
    hw = GLA_HEADS * GLA_DK
    hv = GLA_HEADS * GLA_DV

    @pl.when(c_idx == 0)
    def _():
        st_ref[...] = jnp.zeros_like(st_ref)

    incl_f = (_iota((c, c), 0) >= _iota((c, c), 1)).astype(F32)
    blk = (_iota((hv, hw), 0) // GLA_DV) == (_iota((hv, hw), 1) // GLA_DK)
    qsel = (_iota((GLA_HEADS * SUB, hw), 0) // SUB) == (_iota((GLA_HEADS * SUB, hw), 1) // GLA_DK)
    nbr = range(nb)
    lg = [jax.nn.log_sigmoid(_dot(sm_ref[b].astype(BF16), w2_ref[...]) + b_ref[...]) / GLA_TAU for b in nbr]
    gam = [_dot(incl_f, lg[b], HIGHEST) for b in nbr]
    gl = [gam[b][c - 1:c, :] for b in nbr]
    qs = [q_ref[b] * GLA_DK ** -0.5 for b in nbr]
    k = [k_ref[b] for b in nbr]
    qg = [(qs[b] * jnp.exp(gam[b])).astype(BF16) for b in nbr]
    kd = [(k[b] * jnp.exp(gl[b] - gam[b])).astype(BF16) for b in nbr]
    vbf = [v_ref[b].astype(BF16) for b in nbr]
    st = [st_ref[b] for b in nbr]
    o_inter = [_dot_nt(qg[b], st[b].astype(BF16)) for b in nbr]
    dst = [_dot_tn(vbf[b], kd[b]) for b in nbr]
    for b in nbr:
        st_ref[b] = st[b] * jnp.exp(gl[b]) + jnp.where(blk, dst[b], 0.0)

    res = [[] for _ in nbr]
    for i in range(c // SUB):
        lo, hi = i * SUB, (i + 1) * SUB
        keep = (_iota((GLA_HEADS * SUB, hi), 0) % SUB + lo) >= _iota((GLA_HEADS * SUB, hi), 1)
        qm, ki = [], []
        for b in nbr:
            ref_pt = gam[b][lo - 1:lo, :] if i > 0 else jnp.zeros((1, hw), F32)
            qi = qs[b][lo:hi] * jnp.exp(gam[b][lo:hi] - ref_pt)
            qm.append(jnp.where(qsel, jnp.concatenate([qi] * GLA_HEADS, 0), 0.0).astype(BF16))
            ki.append((k[b][:hi] * jnp.exp(jnp.minimum(ref_pt - gam[b][:hi], EXP_CAP))).astype(BF16))
        att = [_dot_nt(qm[b], ki[b]) for b in nbr]
        att = [jnp.where(keep, att[b], 0.0).astype(BF16) for b in nbr]
        for b in nbr:
            res[b].append(_dot(att[b], vbf[b][:hi]))
    for b in nbr:
        for h in range(GLA_HEADS):
            vs = slice(h * GLA_DV, (h + 1) * GLA_DV)
            o = o_inter[b][:, vs] + jnp.concatenate([r[h * SUB:(h + 1) * SUB, vs] for r in res[b]], 0)
            o = o * lax.rsqrt(jnp.mean(o * o, -1, keepdims=True) + EPS) * on_ref[...]
            o_ref[b, :, vs] = (o * _silu(r_ref[b, :, vs])).astype(BF16)

    @pl.when(c_idx == pl.num_programs(0) - 1)
    def _():
        for b in range(nb):
            for h in range(GLA_HEADS):
                s_out_ref[b, h] = st_ref[b, h * GLA_DV:(h + 1) * GLA_DV, h * GLA_DK:(h + 1) * GLA_DK]


def _gla_prompt_mb(p3, w2pad, gla_b, onorm):
    batch, seq, _ = p3.shape
    c = CHUNK
    hw = GLA_HEADS * GLA_DK
    hv = GLA_HEADS * GLA_DV
    full = lambda shape: pl.BlockSpec(shape, lambda i: tuple(0 for _ in shape))
    return pl.pallas_call(
        _gla_mb_kernel,
        grid=(seq // c,),
        in_specs=[pl.BlockSpec((batch, c, hw), lambda i: (0, i, C_BQ // hw)),
                  pl.BlockSpec((batch, c, hw), lambda i: (0, i, C_BK // hw)),
                  pl.BlockSpec((batch, c, hv), lambda i: (0, i, C_BV // hv)),
                  pl.BlockSpec((batch, c, hv), lambda i: (0, i, C_BR // hv)),
                  pl.BlockSpec((batch, c, LANES), lambda i: (0, i, C_SM // LANES)),
                  full((LANES, hw)), full((1, hw)), full((1, GLA_DV))],
        out_specs=[pl.BlockSpec((batch, c, hv), lambda i: (0, i, 0)),
                   full((batch, GLA_HEADS, GLA_DV, GLA_DK))],
        out_shape=[jax.ShapeDtypeStruct((batch, seq, hv), BF16),
                   jax.ShapeDtypeStruct((batch, GLA_HEADS, GLA_DV, GLA_DK), F32)],
        scratch_shapes=[pltpu.VMEM((batch, hv, hw), F32)],
        compiler_params=_cparams("arbitrary"),
    )(p3, p3, p3, p3, p3, w2pad, gla_b, onorm)


def _swa_mb_kernel(q_ref, kp_ref, kc_ref, vp_ref, vc_ref, qn_ref, kn_ref, snk_ref,
                   o_ref, knew_ref):
    i = pl.program_id(0)
    nb = q_ref.shape[0]
    w = WINDOW
    g = SWA_HEADS // SWA_KV
    nr = g * w
    first = _iota((w, 2 * SWA_HD), 1) < SWA_HD
    first2 = _iota((2 * w, 2 * SWA_HD), 1) < SWA_HD
    t = _iota((nr, 2 * w), 0) % w
    j = _iota((nr, 2 * w), 1)
    dist = w + t - j
    valid = (dist >= 0) & (dist <= w) & ((j >= w) | (i > 0))
    distf = dist.astype(F32)
    hrow = _iota((nr, 1), 0) // w
    slope, snk = [], []
    for kv in range(SWA_KV):
        sl = jnp.zeros((nr, 1), F32)
        sk = jnp.zeros((nr, 1), F32)
        for hh in range(g):
            h = kv * g + hh
            sl = jnp.where(hrow == hh, 2.0 ** (-8.0 * (h + 1) / SWA_HEADS), sl)
            sk = jnp.where(hrow == hh, snk_ref[:, h:h + 1], sk)
        slope.append(sl * distf)
        snk.append(sk)

    units = [(b, kv) for b in range(nb) for kv in range(SWA_KV)]
    kcs, k2, v2, qx = [], [], [], []
    for b in range(nb):
        kc = _half_rms(kc_ref[b], kn_ref[...])
        kcs.append(kc)
        kk = jnp.concatenate([_half_rms(kp_ref[b], kn_ref[...]), kc], 0)
        vv = jnp.concatenate([vp_ref[b], vc_ref[b]], 0)
        kk_sw = pltpu.roll(kk, SWA_HD, axis=1)
        vv_sw = pltpu.roll(vv, SWA_HD, axis=1)
        k2 += [jnp.where(first2, kk, kk_sw).astype(BF16), jnp.where(first2, kk_sw, kk).astype(BF16)]
        v2 += [jnp.where(first2, vv, vv_sw).astype(BF16), jnp.where(first2, vv_sw, vv).astype(BF16)]
        for kv in range(SWA_KV):
            rows = []
            for jj in range(g // 2):
                grp = kv * (g // 2) + jj
                qg = _half_rms(q_ref[b, :, grp * 2 * SWA_HD:(grp + 1) * 2 * SWA_HD], qn_ref[...]) * SWA_HD ** -0.5
                rows += [jnp.where(first, qg, 0.0), jnp.where(first, 0.0, qg)]
            qx.append(jnp.concatenate(rows, 0).astype(BF16))
    nu = range(len(units))
    s = [_dot_nt(qx[u], k2[u]) for u in nu]
    s = [jnp.where(valid, s[u] - slope[units[u][1]], -jnp.inf) for u in nu]
    m = [jnp.maximum(jnp.max(s[u], -1, keepdims=True), snk[units[u][1]]) for u in nu]
    pr = [jnp.exp(s[u] - m[u]) for u in nu]
    den = [jnp.sum(pr[u], -1, keepdims=True) + jnp.exp(snk[units[u][1]] - m[u]) for u in nu]
    pr = [(pr[u] * (1.0 / den[u])).astype(BF16) for u in nu]
    o = [_dot(pr[u], v2[u]) for u in nu]
    for u, (b, kv) in enumerate(units):
        for jj in range(g // 2):
            grp = kv * (g // 2) + jj
            o_ref[b, :, grp * 2 * SWA_HD:(grp + 1) * 2 * SWA_HD] = jnp.where(
                first, o[u][(2 * jj) * w:(2 * jj + 1) * w], o[u][(2 * jj + 1) * w:(2 * jj + 2) * w]).astype(BF16)

    @pl.when(i == pl.num_programs(0) - 1)
    def _():
        for b in range(nb):
            knew_ref[b] = kcs[b]


def _swa_prompt_mb(p3, qn2, kn2, sinks):
    batch, seq, _ = p3.shape
    w = WINDOW
    kvw = SWA_KV * SWA_HD
    full = lambda shape: pl.BlockSpec(shape, lambda i: tuple(0 for _ in shape))
    prev = lambda i: jnp.maximum(i - 1, 0)
    return pl.pallas_call(
        _swa_mb_kernel,
        grid=(seq // w,),
        in_specs=[pl.BlockSpec((batch, w, BRANCH_W), lambda i: (0, i, C_CQ // BRANCH_W)),
                  pl.BlockSpec((batch, w, kvw), lambda i: (0, prev(i), C_CK // kvw)),
                  pl.BlockSpec((batch, w, kvw), lambda i: (0, i, C_CK // kvw)),
                  pl.BlockSpec((batch, w, kvw), lambda i: (0, prev(i), C_CV // kvw)),
                  pl.BlockSpec((batch, w, kvw), lambda i: (0, i, C_CV // kvw)),
                  full((1, kvw)), full((1, kvw)), full((1, SWA_HEADS))],
        out_specs=[pl.BlockSpec((batch, w, BRANCH_W), lambda i: (0, i, 0)),
                   full((batch, w, kvw))],
        out_shape=[jax.ShapeDtypeStruct((batch, seq, BRANCH_W), BF16),
                   jax.ShapeDtypeStruct((batch, w, kvw), F32)],
        compiler_params=_cparams("arbitrary"),
    )(p3, p3, p3, p3, p3, qn2, kn2, sinks)


DEC_TILE = 8


def _dec_kernel(qkv_ref, z_ref, sm_ref, bq_ref, bk_ref, bv_ref, br_ref, cq_ref, ck_ref, cv_ref,
                sdn_ref, buf_ref, sgl_ref, kc_ref, vc_ref,
                cw_ref, alog_ref, dtb_ref, dnon_ref, w2_ref, glb_ref, glon_ref,
                qn_ref, kn_ref, snk_ref, slp_ref,
                oa_ref, ob_ref, oc_ref, sdn_out, buf_out, sgl_out, kc_out, vc_out):
    bt = DEC_TILE
    x = qkv_ref[...]
    buf = buf_ref[...]
    acc = x * cw_ref[CONV_W - 1:CONV_W, :]
    for j in range(CONV_W - 1):
        acc = acc + buf[:, j * DN_QKV:(j + 1) * DN_QKV] * cw_ref[j:j + 1, :]
    y = _silu(acc)
    buf_out[:, 0:(CONV_W - 2) * DN_QKV] = buf[:, DN_QKV:]
    buf_out[:, (CONV_W - 2) * DN_QKV:] = x

    sm = sm_ref[...]
    beta_all = jax.nn.sigmoid(sm)
    g_all = -jnp.exp(alog_ref[...]) * jax.nn.softplus(sm + dtb_ref[...])
    eye = (_iota((LANES, LANES), 0) == _iota((LANES, LANES), 1)).astype(F32)
    hk = DN_HEADS * DN_DK
    for h in range(DN_HEADS):
        q = y[:, h * DN_DK:(h + 1) * DN_DK]
        k = y[:, hk + h * DN_DK:hk + (h + 1) * DN_DK]
        v = y[:, 2 * hk + h * DN_DV:2 * hk + (h + 1) * DN_DV]
        q = q * lax.rsqrt(jnp.sum(q * q, -1, keepdims=True) + EPS) * DN_DK ** -0.5
        k = k * lax.rsqrt(jnp.sum(k * k, -1, keepdims=True) + EPS)
        beta = beta_all[:, SM_BETA + h:SM_BETA + h + 1]
        eg = jnp.exp(g_all[:, SM_G + h:SM_G + h + 1])
        kb = k * beta
        lhs = jnp.concatenate([kb * eg, q * eg], 0).astype(BF16)
        k_t = _dot_nt(eye, k)
        qk = jnp.sum(q * k, -1, keepdims=True)
        vb = v * beta
        o_rows = []
        for b in range(bt):
            s = sdn_ref[b, h]
            r = _dot(lhs, s.astype(BF16))
            u = vb[b:b + 1] - r[b:b + 1]
            o_rows.append(r[bt + b:bt + b + 1] + qk[b:b + 1] * u)
            sdn_out[b, h] = s * eg[b:b + 1] + k_t[:, b:b + 1] * u
        o = jnp.concatenate(o_rows, 0)
        o = o * lax.rsqrt(jnp.mean(o * o, -1, keepdims=True) + EPS) * dnon_ref[...]
        oa_ref[:, h * DN_DV:(h + 1) * DN_DV] = (o * _silu(z_ref[:, h * DN_DV:(h + 1) * DN_DV])).astype(BF16)

    lg = jax.nn.log_sigmoid(_dot(sm.astype(BF16), w2_ref[...]) + glb_ref[...]) / GLA_TAU
    elg = jnp.exp(lg)
    bq = bq_ref[...] * GLA_DK ** -0.5
    bk = bk_ref[...]
    qg = bq * elg
    lane = _iota((bt, LANES), 1)
    first = lane < GLA_DK
    rows_first = _iota((LANES, GLA_DV), 0) < GLA_DK
    for j in range(GLA_HEADS // 2):
        sl = slice(j * LANES, (j + 1) * LANES)
        qgj = qg[:, sl]
        lhs = jnp.concatenate([jnp.where(first, qgj, 0.0), jnp.where(first, 0.0, qgj)], 0).astype(BF16)
        cols = _dot_nt(eye, jnp.concatenate([elg[:, sl], bk[:, sl]], 0), HIGHEST)
        prod = bq[:, sl] * bk[:, sl]
        qk0 = jnp.sum(jnp.where(first, prod, 0.0), -1, keepdims=True)
        qk1 = jnp.sum(jnp.where(first, 0.0, prod), -1, keepdims=True)
        v0 = bv_ref[:, (2 * j) * GLA_DV:(2 * j + 1) * GLA_DV]
        v1 = bv_ref[:, (2 * j + 1) * GLA_DV:(2 * j + 2) * GLA_DV]
        o0, o1 = [], []
        for b in range(bt):
            s = sgl_ref[b, j]
            r = _dot(lhs, s.astype(BF16))
            o0.append(r[b:b + 1] + qk0[b:b + 1] * v0[b:b + 1])
            o1.append(r[bt + b:bt + b + 1] + qk1[b:b + 1] * v1[b:b + 1])
            vsel = jnp.where(rows_first, v0[b:b + 1], v1[b:b + 1])
            sgl_out[b, j] = s * cols[:, b:b + 1] + cols[:, bt + b:bt + b + 1] * vsel
        for hh, rows in ((2 * j, o0), (2 * j + 1, o1)):
            o = jnp.concatenate(rows, 0)
            o = o * lax.rsqrt(jnp.mean(o * o, -1, keepdims=True) + EPS) * glon_ref[...]
            ob_ref[:, hh * GLA_DV:(hh + 1) * GLA_DV] = (o * _silu(br_ref[:, hh * GLA_DV:(hh + 1) * GLA_DV])).astype(BF16)

    g = SWA_HEADS // SWA_KV
    nr = bt * SWA_HEADS
    cq = cq_ref[...]
    cq = cq * lax.rsqrt(jnp.mean(cq * cq, -1, keepdims=True) + EPS) * qn_ref[...] * SWA_HD ** -0.5
    head = _iota((nr, 2 * SWA_HD), 0) % SWA_HEADS
    in_half = (head // g) == (_iota((nr, 2 * SWA_HD), 1) // SWA_HD)
    qx = jnp.where(in_half, jnp.concatenate([cq, cq], -1), 0.0)
    knew = _half_rms(ck_ref[...], kn_ref[...])
    vnew = cv_ref[...]
    s_c, kn_rows, vn_rows = [], [], []
    for b in range(bt):
        s_c.append(_dot_nt(qx[b * SWA_HEADS:(b + 1) * SWA_HEADS].astype(BF16), kc_ref[b].astype(BF16)))
        kn_rows.append(jnp.broadcast_to(knew[b:b + 1], (SWA_HEADS, 2 * SWA_HD)))
        vn_rows.append(jnp.broadcast_to(vnew[b:b + 1], (SWA_HEADS, 2 * SWA_HD)))
    s_c = jnp.concatenate(s_c, 0)
    kn_x = jnp.concatenate(kn_rows, 0)
    vn_x = jnp.concatenate(vn_rows, 0)
    slopes = slp_ref[...]
    snk = snk_ref[...]
    dist = (WINDOW - _iota((nr, WINDOW), 1)).astype(F32)
    s_c = s_c - slopes * dist
    s_n = jnp.sum(qx * kn_x, -1, keepdims=True)
    m = jnp.maximum(jnp.maximum(jnp.max(s_c, -1, keepdims=True), s_n), snk)
    p_c = jnp.exp(s_c - m)
    p_n = jnp.exp(s_n - m)
    den = jnp.sum(p_c, -1, keepdims=True) + p_n + jnp.exp(snk - m)
    p_c = p_c / den
    p_n = p_n / den
    half_sel = (_iota((nr, SWA_HD), 0) % SWA_HEADS) < g
    for b in range(bt):
        rs = slice(b * SWA_HEADS, (b + 1) * SWA_HEADS)
        r = _dot(p_c[rs].astype(BF16), vc_ref[b].astype(BF16)) + p_n[rs] * vn_x[rs]
        oc_ref[rs, :] = jnp.where(half_sel[rs], r[:, :SWA_HD], r[:, SWA_HD:]).astype(BF16)
        kc_out[b, 0:WINDOW - 1, :] = kc_ref[b, 1:WINDOW, :]
        kc_out[b, WINDOW - 1:WINDOW, :] = knew[b:b + 1]
        vc_out[b, 0:WINDOW - 1, :] = vc_ref[b, 1:WINDOW, :]
        vc_out[b, WINDOW - 1:WINDOW, :] = vnew[b:b + 1]


def _dec_kernel_aliased(*refs):
    n_in, n_alias = 26, 5
    _dec_kernel(*refs[:n_in], *refs[n_in + n_alias:])


def _dec_mixers(layer, p, cq_r, states, prev_out, conv_w, alog_row, dtb_row, dn_on, w2pad, gla_b, gla_on,
                qn, kn2, snk_col, slp_col):
    n = p.shape[0]
    bt = DEC_TILE
    hw = GLA_HEADS * GLA_DK
    kvw = SWA_KV * SWA_HD
    nr = bt * SWA_HEADS

    def col(width, off):
        return pl.BlockSpec((bt, width), lambda i: (i, off // width))

    def full(shape):
        return pl.BlockSpec(shape, lambda i: tuple(0 for _ in shape))

    def slot(a):
        rest = a.shape[2:]
        return pl.BlockSpec((None, bt) + rest, lambda i: (layer, i) + tuple(0 for _ in rest))

    in_specs = [col(DN_QKV, C_AQKV), col(BRANCH_W, C_AZ), col(LANES, C_SM), col(hw, C_BQ), col(hw, C_BK),
                col(BRANCH_W, C_BV), col(BRANCH_W, C_BR),
                pl.BlockSpec((nr, SWA_HD), lambda i: (i, 0)),
                col(kvw, C_CK), col(kvw, C_CV)]
    in_specs += [slot(a) for a in states]
    in_specs += [full((CONV_W, DN_QKV)), full((1, LANES)), full((1, LANES)), full((1, DN_DV)),
                 full((LANES, hw)), full((1, hw)), full((1, GLA_DV)),
                 full((1, SWA_HD)), full((1, kvw)), full((nr, 1)), full((nr, 1))]
    args = [p, p, p, p, p, p, p, cq_r, p, p, *states,
            conv_w, alog_row, dtb_row, dn_on, w2pad, gla_b, gla_on, qn, kn2, snk_col, slp_col]
    body, aliases = _dec_kernel, {}
    if prev_out is not None:
        body = _dec_kernel_aliased
        aliases = {len(args) + k: 3 + k for k in range(len(prev_out))}
        in_specs += [pl.BlockSpec(memory_space=pl.ANY)] * len(prev_out)
        args += list(prev_out)
    return pl.pallas_call(
        body,
        grid=(n // bt,),
        in_specs=in_specs,
        out_specs=[pl.BlockSpec((bt, BRANCH_W), lambda i: (i, 0)),
                   pl.BlockSpec((bt, BRANCH_W), lambda i: (i, 0)),
                   pl.BlockSpec((nr, SWA_HD), lambda i: (i, 0))] + [slot(a) for a in states],
        out_shape=[jax.ShapeDtypeStruct((n, BRANCH_W), BF16),
                   jax.ShapeDtypeStruct((n, BRANCH_W), BF16),
                   jax.ShapeDtypeStruct((n * SWA_HEADS, SWA_HD), BF16)]
        + [jax.ShapeDtypeStruct(a.shape, F32) for a in states],
        input_output_aliases=aliases,
        compiler_params=_cparams("parallel"),
    )(*args)


def _merge_kernel(ba_ref, bb_ref, bc_ref, gate_ref, x_ref, gt_ref, sc_ref, sh_ref, ln_ref,
                  wb_ref, wo_ref, wr_ref, rb_ref, x1_ref, h2_ref, comb_ref, cnt_ref):
    mix = None
    for n, br in enumerate((ba_ref, bb_ref, bc_ref)):
        up = _dot(br[...], wb_ref[n])
        term = jax.nn.sigmoid(gate_ref[:, n * D_MODEL:(n + 1) * D_MODEL]) * up
        mix = term if mix is None else mix + term
    x1 = x_ref[...] + gt_ref[0] * _dot(mix.astype(BF16), wo_ref[...])
    x1_ref[...] = x1
    h2 = x1 * lax.rsqrt(jnp.mean(x1 * x1, -1, keepdims=True) + EPS) * ln_ref[...]
    h2 = h2 * (1.0 + sc_ref[0]) + sh_ref[0]
    h2b = h2.astype(BF16)
    h2_ref[...] = h2b

    h2l = (h2 - h2b.astype(F32)).astype(BF16)
    hi_lo = _dot(h2b, wr_ref[...])
    logits = hi_lo[:, :LANES] + hi_lo[:, LANES:] + _dot(h2l, wr_ref[:, :LANES]) + rb_ref[...]
    lane = _iota(logits.shape, 1).astype(F32)
    big = float(LANES)
    lc = jnp.where(lane < N_GROUPS, logits, -jnp.inf)
    mc = jnp.max(lc, -1, keepdims=True)
    pg = 1.0 / jnp.sum(jnp.exp(lc - mc), -1, keepdims=True)
    grp = jnp.min(jnp.where(lc == mc, lane, big), -1, keepdims=True)
    lo = R_EXP + grp * EXP_PER_GROUP
    emask = (lane >= lo) & (lane < lo + EXP_PER_GROUP)
    le = jnp.where(emask, logits, -jnp.inf)
    pe = jnp.exp(le - jnp.max(le, -1, keepdims=True))
    pe = pe / jnp.sum(pe, -1, keepdims=True)
    v1 = jnp.max(pe, -1, keepdims=True)
    i1 = jnp.min(jnp.where(emask & (pe == v1), lane, big), -1, keepdims=True)
    pe2 = jnp.where(emask & (lane != i1), pe, -1.0)
    v2 = jnp.max(pe2, -1, keepdims=True)
    i2 = jnp.min(jnp.where(pe2 == v2, lane, big), -1, keepdims=True)
    tot = v1 + v2
    comb_ref[...] = jnp.where(lane == 0.0, i1 - R_EXP, jnp.where(lane == 1.0, i2 - R_EXP, jnp.where(
        lane == 2.0, pg * v1 / tot, jnp.where(lane == 3.0, pg * v2 / tot, 0.0))))
    picks = jnp.where((lane == i1 - R_EXP) | (lane == i2 - R_EXP), 1.0, 0.0)
    cnt_ref[...] = jnp.broadcast_to(jnp.sum(picks, 0, keepdims=True), cnt_ref.shape)


def _merge(layer, ba, bb, bc, p, x, gt, sc, sh, ln, wb, wo, wr, rb, tm, tokens_per_row):
    t, d = x.shape
    tok = lambda width: pl.BlockSpec((tm, width), lambda i: (i, 0))
    full = lambda shape: pl.BlockSpec((None,) + shape, lambda i: (layer,) + tuple(0 for _ in shape))
    return pl.pallas_call(
        _merge_kernel,
        grid=(t // tm,),
        in_specs=[tok(BRANCH_W), tok(BRANCH_W), tok(BRANCH_W),
                  pl.BlockSpec((tm, N_BRANCH * d), lambda i: (i, C_GATE)),
                  tok(d),
                  _mod_spec(gt, tm, tokens_per_row), _mod_spec(sc, tm, tokens_per_row),
                  _mod_spec(sh, tm, tokens_per_row),
                  full((1, d)), full((N_BRANCH, BRANCH_W, d)), full((d, d)), full((d, 2 * LANES)),
                  full((1, LANES))],
        out_specs=[tok(d), tok(d), tok(LANES), pl.BlockSpec((8, LANES), lambda i: (i, 0))],
        out_shape=[jax.ShapeDtypeStruct((t, d), F32),
                   jax.ShapeDtypeStruct((t, d), BF16),
                   jax.ShapeDtypeStruct((t, LANES), F32),
                   jax.ShapeDtypeStruct((t // tm * 8, LANES), F32)],
        compiler_params=_cparams("parallel"),
    )(ba, bb, bc, p, x, gt, sc, sh, ln, wb, wo, wr, rb)


def _moe_kernel(h_ref, comb_ref, x1_ref, gt_ref, w1_ref, w3_ref, w2_ref, o_ref, acc_ref):
    e = pl.program_id(1)

    @pl.when(e == 0)
    def _():
        acc_ref[...] = jnp.zeros_like(acc_ref)

    h = h_ref[...]
    he = _silu(_dot(h, w1_ref[0].astype(BF16))) * _dot(h, w3_ref[0].astype(BF16))
    ye = _dot(he.astype(BF16), w2_ref[0].astype(BF16))
    comb = comb_ref[...]
    ef = e.astype(F32)
    ce = (jnp.where(comb[:, 0:1] == ef, comb[:, 2:3], 0.0) + jnp.where(comb[:, 1:2] == ef, comb[:, 3:4], 0.0))
    acc_ref[...] += ce * ye

    @pl.when(e == pl.num_programs(1) - 1)
    def _():
        o_ref[...] = x1_ref[...] + gt_ref[0] * acc_ref[...]


def _moe(layer, h2, comb, x1, gt, w1, w3, w2, tm, tokens_per_row):
    t, d = x1.shape
    _, ne, _, de = w1.shape
    return pl.pallas_call(
        _moe_kernel,
        grid=(t // tm, ne),
        in_specs=[pl.BlockSpec((tm, d), lambda i, e: (i, 0)),
                  pl.BlockSpec((tm, LANES), lambda i, e: (i, 0)),
                  pl.BlockSpec((tm, d), lambda i, e: (i, 0)),
                  _mod_spec(gt, tm, tokens_per_row),
                  pl.BlockSpec((None, 1, d, de), lambda i, e: (layer, e, 0, 0)),
                  pl.BlockSpec((None, 1, d, de), lambda i, e: (layer, e, 0, 0)),
                  pl.BlockSpec((None, 1, de, d), lambda i, e: (layer, e, 0, 0))],
        out_specs=pl.BlockSpec((tm, d), lambda i, e: (i, 0)),
        out_shape=jax.ShapeDtypeStruct((t, d), F32),
        scratch_shapes=[pltpu.VMEM((tm, d), F32)],
        compiler_params=_cparams("parallel", "arbitrary"),
    )(h2, comb, x1, gt, w1, w3, w2)


MOE_TK = 512
MOE_TM = 512
ROW_CHUNK = 16
MOE_LB = 2 * MOE_TK + 512
MOE_CH = MOE_LB // ROW_CHUNK
MOE_XW = D_MODEL + LANES
assert 2 * MOE_TK + N_EXPERTS * (ROW_CHUNK - 1) <= MOE_LB - ROW_CHUNK


def _moe_sorted_tiles(t):
    rows = 2 * t + (t // MOE_TK) * N_EXPERTS * (ROW_CHUNK - 1) + N_EXPERTS * (MOE_TM - 1)
    return -(-rows // MOE_TM)


def _plan_kernel(r_ref, cnt_ref, lp_ref, dc_ref, te_ref, base_ref):
    t = pl.program_id(0)
    tk = r_ref.shape[0]
    ntp = te_ref.shape[0]
    r = r_ref[...]
    lane = _iota((tk, LANES), 1).astype(F32)
    sel1, sel2 = lane == r[:, 0:1], lane == r[:, 1:2]
    oh = jnp.where(sel1 | sel2, 1.0, 0.0)
    upper = (_iota((LANES, LANES), 0) < _iota((LANES, LANES), 1)).astype(F32)

    def whole_chunks(cnt):
        return jnp.floor((cnt + (ROW_CHUNK - 1)) * (1.0 / ROW_CHUNK)) * ROW_CHUNK

    def excl_cumsum(v):
        return _dot(jnp.broadcast_to(v, (8, LANES)), upper, HIGHEST)[0:1]

    @pl.when(t == 0)
    def _():
        tot = jnp.sum(whole_chunks(cnt_ref[...]), 0, keepdims=True) * 0.125
        gp = jnp.floor((tot + (MOE_TM - 1)) * (1.0 / MOE_TM)) * MOE_TM
        off = excl_cumsum(gp)
        base_ref[...] = off
        end = off + gp
        lane_t = _iota((ntp, LANES), 1).astype(F32)
        start = _iota((ntp, 1), 0).astype(F32) * MOE_TM
        te = jnp.sum(jnp.where((lane_t < N_EXPERTS) & (end <= start), 1.0, 0.0), -1, keepdims=True)
        mine = lane_t == te
        filled = jnp.sum(jnp.where(mine, tot + off, 0.0), -1, keepdims=True)
        tv = jnp.clip(filled - start, 0.0, float(MOE_TM))
        n_used = jnp.sum(jnp.where(lane_t == N_EXPERTS - 1, end, 0.0), -1, keepdims=True) * (1.0 / MOE_TM)
        te_ref[...] = jnp.where(lane_t == 0.0, jnp.minimum(te, N_EXPERTS - 1.0),
                                jnp.where(lane_t == 1.0, tv, jnp.where(lane_t == 2.0, n_used, 0.0))
                                ).astype(jnp.int32)

    c8 = whole_chunks(jnp.sum(oh, 0, keepdims=True))
    base = base_ref[...]
    lo = excl_cumsum(c8)
    below = (_iota((tk, tk), 0) > _iota((tk, tk), 1)).astype(BF16)
    p = _dot(below, oh.astype(BF16)) + lo
    lp1 = jnp.sum(jnp.where(sel1, p, 0.0), -1, keepdims=True)
    lp2 = jnp.sum(jnp.where(sel2, p, 0.0), -1, keepdims=True)
    lp_ref[...] = jnp.where(lane == 0.0, lp1, jnp.where(lane == 1.0, lp2, jnp.where(lane < 4.0, r, 0.0)))
    lane_c = _iota((MOE_CH, LANES), 1).astype(F32)
    cstart = _iota((MOE_CH, 1), 0).astype(F32) * ROW_CHUNK
    ej = jnp.sum(jnp.where((lane_c < N_EXPERTS) & (lo + c8 <= cstart), 1.0, 0.0), -1, keepdims=True)
    dj = jnp.sum(jnp.where(lane_c == ej, base - lo, 0.0), -1, keepdims=True) + cstart
    nrows = jnp.sum(c8, -1, keepdims=True)
    last = _iota((MOE_CH, 1), 0) == MOE_CH - 1
    dcv = jnp.where(last, nrows * (1.0 / ROW_CHUNK), jnp.where(cstart < nrows, dj, 0.0))
    dc_ref[...] = jnp.broadcast_to(dcv, (MOE_CH, LANES)).astype(jnp.int32)
    base_ref[...] = base + c8


def _moe_plan(route, counts):
    t = route.shape[0]
    nt = t // MOE_TK
    ntp = -(-_moe_sorted_tiles(t) // 8) * 8
    assert counts.shape == (nt * 8, LANES)
    return pl.pallas_call(
        _plan_kernel,
        grid=(nt,),
        in_specs=[pl.BlockSpec((MOE_TK, LANES), lambda i: (i, 0)),
                  pl.BlockSpec((nt * 8, LANES), lambda i: (0, 0))],
        out_specs=[pl.BlockSpec((MOE_TK, LANES), lambda i: (i, 0)),
                   pl.BlockSpec((MOE_CH, LANES), lambda i: (i, 0)),
                   pl.BlockSpec((ntp, LANES), lambda i: (0, 0))],
        out_shape=[jax.ShapeDtypeStruct((t, LANES), F32),
                   jax.ShapeDtypeStruct((nt * MOE_CH, LANES), jnp.int32),
                   jax.ShapeDtypeStruct((ntp, LANES), jnp.int32)],
        scratch_shapes=[pltpu.VMEM((1, LANES), F32)],
        compiler_params=_cparams("arbitrary"),
    )(route, counts)


def _pick_onehot(lp):
    pos = _iota((lp.shape[0], MOE_LB), 1).astype(F32)
    return pos == lp[:, 0:1], pos == lp[:, 1:2]


def _chunk_loop(n, fn):
    def body(j, carry):
        fn(j)
        return carry

    lax.fori_loop(0, n, body, 0)


def _dispatch_kernel(dc_ref, h_ref, lp_ref, xs_ref, buf_ref, sem, nprev_ref):
    t, nt = pl.program_id(0), pl.num_programs(0)
    slot = t % 2
    tk = h_ref.shape[0]
    lp = lp_ref[...]
    oh1, oh2 = _pick_onehot(lp)
    lane = _iota((tk, LANES), 1)

    def split3(w):
        hi = w.astype(BF16).astype(F32)
        mid = (w - hi).astype(BF16).astype(F32)
        low = w - hi - mid
        return jnp.where(lane == 0, hi, jnp.where(lane == 1, mid, jnp.where(lane == 2, low, 0.0))).astype(BF16)

    b1, b2 = jnp.where(oh1, 1.0, 0.0).astype(BF16), jnp.where(oh2, 1.0, 0.0).astype(BF16)
    buf_ref[slot, :, 0:D_MODEL] = _dot_tn(b1 + b2, h_ref[...]).astype(BF16)
    buf_ref[slot, :, D_MODEL:] = (_dot_tn(b1, split3(lp[:, 2:3])) + _dot_tn(b2, split3(lp[:, 3:4]))).astype(BF16)

    def chunk_copy(j, s):
        src = buf_ref.at[s, pl.ds(pl.multiple_of(j * ROW_CHUNK, ROW_CHUNK), ROW_CHUNK), :]
        dst = xs_ref.at[pl.ds(pl.multiple_of(dc_ref[0, 0, j], ROW_CHUNK), ROW_CHUNK), :]
        return pltpu.make_async_copy(src, dst, sem.at[s])

    nch = dc_ref[0, 0, MOE_CH - 1]
    _chunk_loop(nch, lambda j: chunk_copy(j, slot).start())

    @pl.when(t > 0)
    def _():
        _chunk_loop(nprev_ref[0], lambda j: chunk_copy(0, 1 - slot).wait())

    nprev_ref[0] = nch

    @pl.when(t == nt - 1)
    def _():
        _chunk_loop(nch, lambda j: chunk_copy(0, slot).wait())


def _moe_dispatch(dc3, h2, lp, n_rows):
    t, d = h2.shape
    nt = t // MOE_TK
    return pl.pallas_call(
        _dispatch_kernel,
        grid=(nt,),
        in_specs=[pl.BlockSpec((1, 1, MOE_CH), lambda i: (i, 0, 0), memory_space=pltpu.SMEM),
                  pl.BlockSpec((MOE_TK, d), lambda i: (i, 0)),
                  pl.BlockSpec((MOE_TK, LANES), lambda i: (i, 0))],
        out_specs=pl.BlockSpec(memory_space=pl.ANY),
        out_shape=jax.ShapeDtypeStruct((n_rows, MOE_XW), BF16),
        scratch_shapes=[pltpu.VMEM((2, MOE_LB, MOE_XW), BF16), pltpu.SemaphoreType.DMA((2,)),
                        pltpu.SMEM((1,), jnp.int32)],
        compiler_params=_cparams("arbitrary"),
    )(dc3, h2, lp)


def _experts_kernel(te_ref, tv_ref, nu_ref, x_ref, w1_ref, w3_ref, w2_ref, o_ref, w1b, w3b, w2b):
    i = pl.program_id(0)

    @pl.when(i < nu_ref[0])
    def _():
        @pl.when((i == 0) | (te_ref[i] != te_ref[jnp.maximum(i - 1, 0)]))
        def _():
            w1b[...] = w1_ref[0].astype(BF16)
            w3b[...] = w3_ref[0].astype(BF16)
            w2b[...] = w2_ref[0].astype(BF16)

        tm = x_ref.shape[0]
        valid = _iota((tm, 1), 0) < tv_ref[i]
        h = jnp.where(valid, x_ref[:, :D_MODEL], jnp.zeros((), BF16))
        wx = x_ref[:, D_MODEL:].astype(F32)
        wv = jnp.where(valid, wx[:, 0:1] + wx[:, 1:2] + wx[:, 2:3], 0.0)
        he = _silu(_dot(h, w1b[...])) * _dot(h, w3b[...])
        o_ref[...] = (wv * _dot(he.astype(BF16), w2b[...])).astype(BF16)


def _moe_experts(layer, te, tv, nu, xs, w1, w3, w2):
    n_rows = xs.shape[0]
    _, _, d, de = w1.shape
    cur = lambda i, te, tv, nu: jnp.minimum(i, nu[0] - 1)
    wspec = lambda a, b: pl.BlockSpec((None, 1, a, b), lambda i, te, tv, nu: (layer, te[cur(i, te, tv, nu)], 0, 0))
    return pl.pallas_call(
        _experts_kernel,
        grid_spec=pltpu.PrefetchScalarGridSpec(
            num_scalar_prefetch=3,
            grid=(n_rows // MOE_TM,),
            in_specs=[pl.BlockSpec((MOE_TM, MOE_XW), lambda i, te, tv, nu: (cur(i, te, tv, nu), 0)),
                      wspec(d, de), wspec(d, de), wspec(de, d)],
            out_specs=pl.BlockSpec((MOE_TM, d), lambda i, te, tv, nu: (cur(i, te, tv, nu), 0)),
            scratch_shapes=[pltpu.VMEM((d, de), BF16), pltpu.VMEM((d, de), BF16), pltpu.VMEM((de, d), BF16)]),
        out_shape=jax.ShapeDtypeStruct((n_rows, d), BF16),
        compiler_params=_cparams("arbitrary"),
    )(te, tv, nu, xs, w1, w3, w2)


def _combine_kernel(dc_ref, dcn_ref, lp_ref, x1_ref, gt_ref, ys_ref, o_ref, buf_ref, sem):
    t, nt = pl.program_id(0), pl.num_programs(0)
    slot = t % 2

    def chunk_copy(tab, j, s):
        src = ys_ref.at[pl.ds(pl.multiple_of(tab[0, 0, j], ROW_CHUNK), ROW_CHUNK), :]
        dst = buf_ref.at[s, pl.ds(pl.multiple_of(j * ROW_CHUNK, ROW_CHUNK), ROW_CHUNK), :]
        return pltpu.make_async_copy(src, dst, sem.at[s])

    nch = dc_ref[0, 0, MOE_CH - 1]

    @pl.when(t == 0)
    def _():
        buf_ref[...] = jnp.zeros_like(buf_ref)
        _chunk_loop(nch, lambda j: chunk_copy(dc_ref, j, slot).start())

    @pl.when(t + 1 < nt)
    def _():
        _chunk_loop(dcn_ref[0, 0, MOE_CH - 1], lambda j: chunk_copy(dcn_ref, j, 1 - slot).start())

    _chunk_loop(nch, lambda j: chunk_copy(dc_ref, 0, slot).wait())
    live = _iota((MOE_LB, 1), 0) < nch * ROW_CHUNK
    local = jnp.where(live, buf_ref[slot], jnp.zeros((), BF16))
    oh1, oh2 = _pick_onehot(lp_ref[...])
    perm = jnp.where(oh1 | oh2, 1.0, 0.0).astype(BF16)
    o_ref[...] = x1_ref[...] + gt_ref[0] * _dot(perm, local)


def _moe_combine(dc3, lp, x1, gt, ys, tokens_per_row):
    t, d = x1.shape
    nt = t // MOE_TK
    smem = lambda f: pl.BlockSpec((1, 1, MOE_CH), f, memory_space=pltpu.SMEM)
    return pl.pallas_call(
        _combine_kernel,
        grid=(nt,),
        in_specs=[smem(lambda i: (i, 0, 0)), smem(lambda i: (jnp.minimum(i + 1, nt - 1), 0, 0)),
                  pl.BlockSpec((MOE_TK, LANES), lambda i: (i, 0)),
                  pl.BlockSpec((MOE_TK, d), lambda i: (i, 0)),
                  _mod_spec(gt, MOE_TK, tokens_per_row),
                  pl.BlockSpec(memory_space=pl.ANY)],
        out_specs=pl.BlockSpec((MOE_TK, d), lambda i: (i, 0)),
        out_shape=jax.ShapeDtypeStruct((t, d), F32),
        scratch_shapes=[pltpu.VMEM((2, MOE_LB, d), BF16), pltpu.SemaphoreType.DMA((2,))],
        compiler_params=_cparams("arbitrary"),
    )(dc3, dc3, lp, x1, gt, ys)


def _moe_sparse(layer, h2, route, counts, x1, gt, w1, w3, w2, tokens_per_row):
    t = h2.shape[0]
    nt = t // MOE_TK
    lp, dc, tmeta = _moe_plan(route, counts)
    dc3 = dc[:, 0].reshape(nt, 1, MOE_CH)
    n_tiles = _moe_sorted_tiles(t)
    xs = _moe_dispatch(dc3, h2, lp, n_tiles * MOE_TM)
    ys = _moe_experts(layer, tmeta[:n_tiles, 0], tmeta[:n_tiles, 1], tmeta[0:1, 2], xs, w1, w3, w2)
    return _moe_combine(dc3, lp, x1, gt, ys, tokens_per_row)


def _w_in_segments():
    sizes = (DN_QKV, DN_HEADS * DN_DV, DN_HEADS, DN_HEADS,
             GLA_HEADS * GLA_DK, GLA_HEADS * GLA_DK, GLA_HEADS * GLA_DV, GLA_HEADS * GLA_DV, GLA_RANK,
             SWA_HEADS * SWA_HD, SWA_KV * SWA_HD, SWA_KV * SWA_HD, N_BRANCH * D_MODEL)
    offs = [0]
    for s in sizes:
        offs.append(offs[-1] + s)
    (a_qkv, a_z, a_b, a_a, b_q, b_k, b_v, b_r, b_lr, c_q, c_k, c_v, gate) = range(len(sizes))
    dst = {gate: C_GATE, a_qkv: C_AQKV, a_z: C_AZ, b_v: C_BV, b_r: C_BR, c_q: C_CQ, b_q: C_BQ, b_k: C_BK,
           c_k: C_CK, c_v: C_CV, a_b: C_SM + SM_BETA, a_a: C_SM + SM_G, b_lr: C_SM + SM_LR}
    return [(offs[i], dst[i], sizes[i]) for i in range(len(sizes))], offs[-1]


def _w_in_prep_kernel(w_ref, o_ref):
    segments, _ = _w_in_segments()
    n_src = w_ref.shape[-1]
    o_ref[:, C_SM:] = jnp.zeros((o_ref.shape[0], P_PAD - C_SM), BF16)
    for src, dst, width in segments:
        lo = (src // LANES) * LANES
        hi = min(-(-(src + width) // LANES) * LANES, n_src)
        win = w_ref[:, lo:hi]
        o_ref[:, dst:dst + width] = win[:, src - lo:src - lo + width].astype(BF16)


def _permute_w_in(w):
    depth, d, n = w.shape
    assert n == _w_in_segments()[1]
    rows = 256
    return pl.pallas_call(
        _w_in_prep_kernel,
        grid=(depth, d // rows),
        in_specs=[pl.BlockSpec((None, rows, n), lambda l, i: (l, i, 0))],
        out_specs=pl.BlockSpec((None, rows, P_PAD), lambda l, i: (l, i, 0)),
        out_shape=jax.ShapeDtypeStruct((depth, d, P_PAD), BF16),
        compiler_params=_cparams("parallel", "parallel"),
    )(w)


def _lane_row(v, off):
    depth, n = v.shape
    return jnp.zeros((depth, 1, LANES), F32).at[:, 0, off:off + n].set(v.astype(F32))


def kernel(x_prompt, x_sample, c_prompt, c_sample, state_dn, state_dn_conv, state_gla, cache_swa_k, cache_swa_v, ln1_g, ln2_g, ada_w, ada_b, w_in, dn_conv_w, dn_a_log, dn_dt_bias, dn_onorm_g, gla_w2, gla_b, gla_onorm_g, swa_qnorm_g, swa_knorm_g, swa_sinks, w_branch, w_out, rc_w, rc_b, re_w, re_b, w1, w3, w2):
    batch, seq, d = x_prompt.shape
    nb = x_sample.shape[0]
    depth = w_in.shape[0]
    assert x_sample.shape[1] == 1 and d == D_MODEL and seq % CHUNK == 0 and nb % DEC_TILE == 0
    kvw = SWA_KV * SWA_HD

    w_in_p = _permute_w_in(w_in)
    wb_b, wo_b = w_branch.astype(BF16), w_out.astype(BF16)
    wr = jnp.concatenate([rc_w, re_w, jnp.zeros((depth, d, LANES - N_GROUPS - N_EXPERTS), F32)], -1)
    rb = jnp.concatenate([rc_b, re_b, jnp.zeros((depth, LANES - N_GROUPS - N_EXPERTS), F32)], -1)[:, None, :]
    wr_hi = wr.astype(BF16)
    wr2 = jnp.concatenate([wr_hi, (wr - wr_hi.astype(F32)).astype(BF16)], -1)
    ln2_all = ln2_g[:, None, :]
    alog_row = _lane_row(dn_a_log, SM_G)
    dtb_row = _lane_row(dn_dt_bias, SM_G)
    w2pad = jnp.zeros((depth, LANES, GLA_HEADS * GLA_DK), F32).at[:, SM_LR:SM_LR + GLA_RANK].set(gla_w2).astype(BF16)
    kn2 = jnp.concatenate([swa_knorm_g] * SWA_KV, -1)[:, None, :]
    slopes = jnp.exp2(-8.0 * jnp.arange(1, SWA_HEADS + 1, dtype=F32) / SWA_HEADS)
    slp_col = jnp.tile(slopes, DEC_TILE)[:, None]

    pad_rows = (-(batch + nb)) % 8
    c_all = jnp.concatenate([c_prompt, c_sample, jnp.zeros((pad_rows, d), F32)], 0)
    mod = _ada(c_all, ada_w, ada_b)

    xp = x_prompt.reshape(batch * seq, d)
    xs = x_sample.reshape(nb, d)
    tm_p = 2048 if seq % 2048 == 0 else CHUNK
    tm_m = 512 if seq % 512 == 0 else CHUNK

    dec_states = (state_dn,
                  state_dn_conv.reshape(depth, nb, (CONV_W - 1) * DN_QKV),
                  state_gla.reshape(depth, nb, GLA_HEADS // 2, 2 * GLA_DK, GLA_DV),
                  cache_swa_k.reshape(depth, nb, WINDOW, kvw),
                  cache_swa_v.reshape(depth, nb, WINDOW, kvw))
    dec_out = None
    st_p = []
    for l in range(depth):
        mp = [m[:, None, :] for m in jnp.split(mod[l, :batch], 6, -1)]
        ms = [m[None] for m in jnp.split(mod[l, batch:batch + nb], 6, -1)]
        ln1, ln2 = ln1_g[l][None], ln2_g[l][None]
        conv_w = dn_conv_w[l]
        dn_on, gla_on = dn_onorm_g[l][None], gla_onorm_g[l][None]
        glb = gla_b[l][None]
        qn = swa_qnorm_g[l][None]
        snk = swa_sinks[l][None]

        pp = _in_proj(l, xp, mp[1], mp[0], ln1, w_in_p, tm_p, seq)
        pp3 = pp.reshape(batch, seq, P_PAD)
        ba, dn_s = _dn_prompt_mb(pp3, conv_w, alog_row[l], dtb_row[l], dn_on)
        bb, gla_st = _gla_prompt_mb(pp3, w2pad[l], glb, gla_on)
        bc, k_new = _swa_prompt_mb(pp3, jnp.concatenate([qn, qn], -1), kn2[l], snk)
        ba, bb, bc = (z.reshape(batch * seq, BRANCH_W) for z in (ba, bb, bc))
        x1, h2, comb, counts = _merge(l, ba, bb, bc, pp, xp, mp[2], mp[4], mp[3], ln2_all, wb_b, wo_b, wr2, rb,
                              tm_m, seq)
        xp = _moe_sparse(l, h2, comb, counts, x1, mp[5], w1, w3, w2, seq)
        pp3 = pp.reshape(batch, seq, P_PAD)
        st_p.append((dn_s,
                     pp3[:, seq - (CONV_W - 1):, C_AQKV:C_AQKV + DN_QKV],
                     jnp.swapaxes(gla_st, -1, -2),
                     k_new.reshape(batch, WINDOW, SWA_KV, SWA_HD),
                     pp3[:, seq - WINDOW:, C_CV:C_CV + kvw].reshape(batch, WINDOW, SWA_KV, SWA_HD)))

        ps = _in_proj(l, xs, ms[1], ms[0], ln1, w_in_p, nb, nb)
        cq_r = ps[:, C_CQ:C_CQ + SWA_HEADS * SWA_HD].reshape(nb * SWA_HEADS, SWA_HD)
        oa, ob, oc_r, *dec_out = _dec_mixers(
            l, ps, cq_r, dec_states, dec_out, conv_w, alog_row[l], dtb_row[l], dn_on, w2pad[l], glb, gla_on,
            qn, kn2[l], jnp.tile(swa_sinks[l], DEC_TILE)[:, None], slp_col)
        oc = oc_r.reshape(nb, SWA_HEADS * SWA_HD)
        x1, h2, comb, _ = _merge(l, oa, ob, oc, ps, xs, ms[2], ms[4], ms[3], ln2_all, wb_b, wo_b, wr2, rb,
                              nb, nb)
        xs = _moe(l, h2, comb, x1, ms[5], w1, w3, w2, nb, nb)

    dn_p, conv_p, gla_p, k_p, v_p = [jnp.stack(z) for z in zip(*st_p)]
    dn_s = dec_out[0]
    conv_s = dec_out[1].reshape(depth, nb, CONV_W - 1, DN_QKV)
    gla_s = dec_out[2].reshape(depth, nb, GLA_HEADS, GLA_DK, GLA_DV)
    k_s = dec_out[3].reshape(depth, nb, WINDOW, SWA_KV, SWA_HD)
    v_s = dec_out[4].reshape(depth, nb, WINDOW, SWA_KV, SWA_HD)
    return (xp.reshape(batch, seq, d), xs.reshape(nb, 1, d), dn_p, dn_s, conv_p, conv_s, gla_p, gla_s,
            k_p, k_s, v_p, v_s)
```

```python
import jax
import jax.numpy as jnp
from jax import lax
from jax.experimental import pallas as pl
from jax.experimental.pallas import tpu as pltpu

F32 = jnp.float32
BF16 = jnp.bfloat16
HIGHEST = lax.Precision.HIGHEST

D_MODEL = 1024
DN_HEADS, DN_DK, DN_DV, CONV_W = 4, 128, 128, 4
DN_QKV = 2 * DN_HEADS * DN_DK + DN_HEADS * DN_DV
GLA_HEADS, GLA_DK, GLA_DV, GLA_RANK, GLA_TAU = 4, 64, 128, 16, 16.0
SWA_HEADS, SWA_KV, SWA_HD, WINDOW = 8, 2, 64, 128
N_BRANCH, BRANCH_W = 3, 512
N_GROUPS, EXP_PER_GROUP, TOP_K, D_EXPERT = 4, 8, 2, 256
N_EXPERTS = N_GROUPS * EXP_PER_GROUP
EPS = 1e-6

LANES = 128
CHUNK = 128
SUB = 16
VMEM_LIMIT = 56 * 1024 * 1024

C_GATE, C_AQKV, C_AZ, C_BV, C_BR, C_CQ = 0, 3072, 4608, 5120, 5632, 6144
C_BQ, C_BK, C_CK, C_CV, C_SM, P_PAD = 6656, 6912, 7168, 7296, 7424, 7680
SM_BETA, SM_G, SM_LR = 0, DN_HEADS, 2 * DN_HEADS
R_EXP = N_GROUPS


def _cparams(*sem):
    return pltpu.CompilerParams(dimension_semantics=sem, vmem_limit_bytes=VMEM_LIMIT)


def _dot(a, b, precision=None):
    return jnp.dot(a, b, preferred_element_type=F32, precision=precision)


def _dot_nt(a, b, precision=None):
    return lax.dot_general(a, b, (((1,), (1,)), ((), ())), preferred_element_type=F32, precision=precision)


def _dot_tn(a, b, precision=None):
    return lax.dot_general(a, b, (((0,), (0,)), ((), ())), preferred_element_type=F32, precision=precision)


def _sigmoid(x):
    return 0.5 * jnp.tanh(0.5 * x) + 0.5


def _silu(x):
    h = 0.5 * x
    return h * jnp.tanh(h) + h


def _iota(shape, dim):
    return lax.broadcasted_iota(jnp.int32, shape, dim)


def _ada_kernel(c_ref, w_ref, b_ref, o_ref):
    c = _silu(c_ref[...]).astype(BF16)
    o_ref[0] = _dot(c, w_ref[0].astype(BF16)) + b_ref[0]


def _ada(c_all, ada_w, ada_b):
    depth, d, n = ada_w.shape
    rows = c_all.shape[0]
    tn = 1536
    return pl.pallas_call(
        _ada_kernel,
        grid=(depth, n // tn),
        in_specs=[pl.BlockSpec((rows, d), lambda l, j: (0, 0)),
                  pl.BlockSpec((1, d, tn), lambda l, j: (l, 0, j)),
                  pl.BlockSpec((1, 1, tn), lambda l, j: (l, 0, j))],
        out_specs=pl.BlockSpec((1, rows, tn), lambda l, j: (l, 0, j)),
        out_shape=jax.ShapeDtypeStruct((depth, rows, n), F32),
        compiler_params=_cparams("parallel", "parallel"),
    )(c_all, ada_w, ada_b.reshape(depth, 1, n))


def _mod_spec(mod, tm, tokens_per_row):
    _, r, d = mod.shape
    assert tokens_per_row % tm == 0
    per = tokens_per_row // tm
    return pl.BlockSpec((1, r, d), lambda i, *_: (i // per, 0, 0))


def _in_proj_kernel(x_ref, sc_ref, sh_ref, g_ref, w_ref, o_ref, h_ref):
    @pl.when(pl.program_id(1) == 0)
    def _():
        x = x_ref[...]
        y = x * lax.rsqrt(jnp.mean(x * x, -1, keepdims=True) + EPS) * g_ref[...]
        h_ref[...] = (y * (1.0 + sc_ref[0]) + sh_ref[0]).astype(BF16)

    o_ref[...] = _dot(h_ref[...], w_ref[...])


def _in_proj(layer, x, sc, sh, g, w, tm, tokens_per_row):
    t, d = x.shape
    n = w.shape[2]
    tn = 768 if tm >= 2048 else 1536
    return pl.pallas_call(
        _in_proj_kernel,
        grid=(t // tm, n // tn),
        in_specs=[pl.BlockSpec((tm, d), lambda i, j: (i, 0)),
                  _mod_spec(sc, tm, tokens_per_row), _mod_spec(sh, tm, tokens_per_row),
                  pl.BlockSpec((1, d), lambda i, j: (0, 0)),
                  pl.BlockSpec((None, d, tn), lambda i, j: (layer, 0, j))],
        out_specs=pl.BlockSpec((tm, tn), lambda i, j: (i, j)),
        out_shape=jax.ShapeDtypeStruct((t, n), F32),
        scratch_shapes=[pltpu.VMEM((tm, d), BF16)],
        compiler_params=_cparams("parallel", "arbitrary"),
    )(x, sc, sh, g, w)


def _strict_lower_inverse_minus_eye_multi(a_list):
    c = a_list[0].shape[0]
    diag_blk = (_iota((c, c), 0) // SUB) == (_iota((c, c), 1) // SUB)
    idx = range(len(a_list))
    ad = [jnp.where(diag_blk, a, 0.0) for a in a_list]
    ao = [a_list[i] - ad[i] for i in idx]
    n = [-x for x in ad]
    p = n
    for _ in range(SUB.bit_length() - 2):
        p = [_dot(x, x) for x in p]
        np_ = [_dot(n[i], p[i]) for i in idx]
        n = [n[i] + p[i] + np_[i] for i in idx]
    nao = [_dot(n[i], ao[i]) for i in idx]
    m = [-(ao[i] + nao[i]) for i in idx]
    q = m
    for _ in range((c // SUB).bit_length() - 2):
        q = [_dot(x, x) for x in q]
        mq = [_dot(m[i], q[i]) for i in idx]
        m = [m[i] + q[i] + mq[i] for i in idx]
    mn = [_dot(m[i], n[i]) for i in idx]
    return [m[i] + n[i] + mn[i] for i in idx]


EXP_CAP = 80.0


def _half_rms(x, g2):
    lane = _iota(x.shape, 1)
    first = lane < SWA_HD
    sq = x * x
    s0 = jnp.sum(jnp.where(first, sq, 0.0), -1, keepdims=True)
    s1 = jnp.sum(jnp.where(first, 0.0, sq), -1, keepdims=True)
    ms = jnp.where(first, s0, s1) * (1.0 / SWA_HD)
    return x * lax.rsqrt(ms + EPS) * g2


def _dn_mb_kernel(qkv_ref, z_ref, sm_ref, cw_ref, alog_ref, dtb_ref, on_ref,
                  o_ref, s_out_ref, s_ref, xp_ref, y_ref):
    c_idx = pl.program_id(0)
    nb = qkv_ref.shape[0]
    c = CHUNK
    pad = 8

    @pl.when(c_idx == 0)
    def _():
        s_ref[...] = jnp.zeros_like(s_ref)
        xp_ref[:, 0:pad, :] = jnp.zeros((nb, pad, DN_QKV), F32)

    row, col = _iota((c, c), 0), _iota((c, c), 1)
    incl = row >= col
    strict = row > col
    incl_f = incl.astype(F32)
    hk = DN_HEADS * DN_DK
    gam_all, gam_t, beta_all = [], [], []
    for b in range(nb):
        xp_ref[b, pad:pad + c, :] = qkv_ref[b]
        acc = xp_ref[b, pad - 3:pad - 3 + c, :] * cw_ref[0:1, :]
        for j in range(1, CONV_W):
            acc = acc + xp_ref[b, pad - 3 + j:pad - 3 + j + c, :] * cw_ref[j:j + 1, :]
        y_ref[b] = _silu(acc)
        xp_ref[b, pad - 3:pad, :] = xp_ref[b, pad + c - 3:pad + c, :]
        sm = sm_ref[b]
        beta_all.append(_sigmoid(sm))
        g_all = -jnp.exp(alog_ref[...]) * jax.nn.softplus(sm + dtb_ref[...])
        gam_all.append(_dot(incl_f, g_all, HIGHEST))
        gam_t.append(gam_all[b].T)

    chains = [(b, h) for b in range(nb) for h in range(DN_HEADS)]
    n = len(chains)
    q, k, dec, eg, gl, gam, kb, rhs, a = ([None] * n for _ in range(9))
    for i, (b, h) in enumerate(chains):
        qi = y_ref[b, :, h * DN_DK:(h + 1) * DN_DK]
        ki = y_ref[b, :, hk + h * DN_DK:hk + (h + 1) * DN_DK]
        vi = y_ref[b, :, 2 * hk + h * DN_DV:2 * hk + (h + 1) * DN_DV]
        q[i] = qi * lax.rsqrt(jnp.sum(qi * qi, -1, keepdims=True) + EPS) * DN_DK ** -0.5
        k[i] = ki * lax.rsqrt(jnp.sum(ki * ki, -1, keepdims=True) + EPS)
        beta = beta_all[b][:, SM_BETA + h:SM_BETA + h + 1]
        gam[i] = gam_all[b][:, SM_G + h:SM_G + h + 1]
        gam_row = gam_t[b][SM_G + h:SM_G + h + 1, :]
        dec[i] = jnp.where(incl, jnp.exp(jnp.minimum(gam[i] - gam_row, 0.0)), 0.0)
        eg[i] = jnp.exp(gam[i])
        gl[i] = gam[i][c - 1:c, :]
        kb[i] = k[i] * beta
        rhs[i] = jnp.concatenate([vi * beta, kb[i] * eg[i]], -1)
    kbf = [x.astype(BF16) for x in k]
    kk = [_dot_nt(kb[i].astype(BF16), kbf[i]) for i in range(n)]
    qk = [_dot_nt(q[i].astype(BF16), kbf[i]) for i in range(n)]
    a = [jnp.where(strict, kk[i] * dec[i], 0.0) for i in range(n)]
    w = _strict_lower_inverse_minus_eye_multi(a)
    sol = [rhs[i] + _dot(w[i], rhs[i]) for i in range(n)]
    s = [s_ref[b, h] for (b, h) in chains]
    sbf = [x.astype(BF16) for x in s]
    u = [sol[i][:, :DN_DV] - _dot(sol[i][:, DN_DV:].astype(BF16), sbf[i]) for i in range(n)]
    ubf = [x.astype(BF16) for x in u]
    o_s = [_dot((q[i] * eg[i]).astype(BF16), sbf[i]) for i in range(n)]
    o_u = [_dot((qk[i] * dec[i]).astype(BF16), ubf[i]) for i in range(n)]
    ds = [_dot_tn((k[i] * jnp.exp(gl[i] - gam[i])).astype(BF16), ubf[i]) for i in range(n)]
    for i, (b, h) in enumerate(chains):
        s_ref[b, h] = s[i] * jnp.exp(gl[i]) + ds[i]
        o = o_s[i] + o_u[i]
        o = o * lax.rsqrt(jnp.mean(o * o, -1, keepdims=True) + EPS) * on_ref[...]
        o_ref[b, :, h * DN_DV:(h + 1) * DN_DV] = (
            o * _silu(z_ref[b, :, h * DN_DV:(h + 1) * DN_DV])).astype(BF16)

    @pl.when(c_idx == pl.num_programs(0) - 1)
    def _():
        s_out_ref[...] = s_ref[...]


def _dn_prompt_mb(p3, conv_w, alog_row, dtb_row, onorm):
    batch, seq, _ = p3.shape
    c = CHUNK
    full = lambda shape: pl.BlockSpec(shape, lambda i: tuple(0 for _ in shape))
    return pl.pallas_call(
        _dn_mb_kernel,
        grid=(seq // c,),
        in_specs=[pl.BlockSpec((batch, c, DN_QKV), lambda i: (0, i, C_AQKV // DN_QKV)),
                  pl.BlockSpec((batch, c, BRANCH_W), lambda i: (0, i, C_AZ // BRANCH_W)),
                  pl.BlockSpec((batch, c, LANES), lambda i: (0, i, C_SM // LANES)),
                  full((CONV_W, DN_QKV)), full((1, LANES)), full((1, LANES)), full((1, DN_DV))],
        out_specs=[pl.BlockSpec((batch, c, BRANCH_W), lambda i: (0, i, 0)),
                   full((batch, DN_HEADS, DN_DK, DN_DV))],
        out_shape=[jax.ShapeDtypeStruct((batch, seq, BRANCH_W), BF16),
                   jax.ShapeDtypeStruct((batch, DN_HEADS, DN_DK, DN_DV), F32)],
        scratch_shapes=[pltpu.VMEM((batch, DN_HEADS, DN_DK, DN_DV), F32),
                        pltpu.VMEM((batch, c + 8, DN_QKV), F32),
                        pltpu.VMEM((batch, c, DN_QKV), F32)],
        compiler_params=_cparams("arbitrary"),
    )(p3, p3, p3, conv_w, alog_row, dtb_row, onorm)


def _gla_mb_kernel(q_ref, k_ref, v_ref, r_ref, sm_ref, w2_ref, b_ref, on_ref,
                   o_ref, s_out_ref, st_ref):
    c_idx = pl.program_id(0)
    nb = q_ref.shape[0]
    c = CHUNK
    hw = GLA_HEADS * GLA_DK
    hv = GLA_HEADS * GLA_DV

    @pl.when(c_idx == 0)
    def _():
        st_ref[...] = jnp.zeros_like(st_ref)

    incl_f = (_iota((c, c), 0) >= _iota((c, c), 1)).astype(F32)
    blk = (_iota((hv, hw), 0) // GLA_DV) == (_iota((hv, hw), 1) // GLA_DK)
    qsel = (_iota((GLA_HEADS * SUB, hw), 0) // SUB) == (_iota((GLA_HEADS * SUB, hw), 1) // GLA_DK)
    nbr = range(nb)
    lg = [jax.nn.log_sigmoid(_dot(sm_ref[b].astype(BF16), w2_ref[...]) + b_ref[...]) / GLA_TAU for b in nbr]
    gam = [_dot(incl_f, lg[b], HIGHEST) for b in nbr]
    gl = [gam[b][c - 1:c, :] for b in nbr]
    qs = [q_ref[b] * GLA_DK ** -0.5 for b in nbr]
    k = [k_ref[b] for b in nbr]
    qg = [(qs[b] * jnp.exp(gam[b])).astype(BF16) for b in nbr]
    kd = [(k[b] * jnp.exp(gl[b] - gam[b])).astype(BF16) for b in nbr]
    vbf = [v_ref[b].astype(BF16) for b in nbr]
    st = [st_ref[b] for b in nbr]
    o_inter = [_dot_nt(qg[b], st[b].astype(BF16)) for b in nbr]
    dst = [_dot_tn(vbf[b], kd[b]) for b in nbr]
    for b in nbr:
        st_ref[b] = st[b] * jnp.exp(gl[b]) + jnp.where(blk, dst[b], 0.0)

    res = [[] for _ in nbr]
    for i in range(c // SUB):
        lo, hi = i * SUB, (i + 1) * SUB
        keep = (_iota((GLA_HEADS * SUB, hi), 0) % SUB + lo) >= _iota((GLA_HEADS * SUB, hi), 1)
        qm, ki = [], []
        for b in nbr:
            ref_pt = gam[b][lo - 1:lo, :] if i > 0 else jnp.zeros((1, hw), F32)
            qi = qs[b][lo:hi] * jnp.exp(gam[b][lo:hi] - ref_pt)
            qm.append(jnp.where(qsel, jnp.concatenate([qi] * GLA_HEADS, 0), 0.0).astype(BF16))
            ki.append((k[b][:hi] * jnp.exp(jnp.minimum(ref_pt - gam[b][:hi], EXP_CAP))).astype(BF16))
        att = [_dot_nt(qm[b], ki[b]) for b in nbr]
        att = [jnp.where(keep, att[b], 0.0).astype(BF16) for b in nbr]
        for b in nbr:
            res[b].append(_dot(att[b], vbf[b][:hi]))
    for b in nbr:
        for h in range(GLA_HEADS):
            vs = slice(h * GLA_DV, (h + 1) * GLA_DV)
            o = o_inter[b][:, vs] + jnp.concatenate([r[h * SUB:(h + 1) * SUB, vs] for r in res[b]], 0)
            o = o * lax.rsqrt(jnp.mean(o * o, -1, keepdims=True) + EPS) * on_ref[...]
            o_ref[b, :, vs] = (o * _silu(r_ref[b, :, vs])).astype(BF16)

    @pl.when(c_idx == pl.num_programs(0) - 1)
    def _():
        for b in range(nb):
            for h in range(GLA_HEADS):
                s_out_ref[b, h] = st_ref[b, h * GLA_DV:(h + 1) * GLA_DV, h * GLA_DK:(h + 1) * GLA_DK]


def _gla_prompt_mb(p3, w2pad, gla_b, onorm):
    batch, seq, _ = p3.shape
    c = CHUNK
    hw = GLA_HEADS * GLA_DK
    hv = GLA_HEADS * GLA_DV
    full = lambda shape: pl.BlockSpec(shape, lambda i: tuple(0 for _ in shape))
    return pl.pallas_call(
        _gla_mb_kernel,
        grid=(seq // c,),
        in_specs=[pl.BlockSpec((batch, c, hw), lambda i: (0, i, C_BQ // hw)),
                  pl.BlockSpec((batch, c, hw), lambda i: (0, i, C_BK // hw)),
                  pl.BlockSpec((batch, c, hv), lambda i: (0, i, C_BV // hv)),
                  pl.BlockSpec((batch, c, hv), lambda i: (0, i, C_BR // hv)),
                  pl.BlockSpec((batch, c, LANES), lambda i: (0, i, C_SM // LANES)),
                  full((LANES, hw)), full((1, hw)), full((1, GLA_DV))],
        out_specs=[pl.BlockSpec((batch, c, hv), lambda i: (0, i, 0)),
                   full((batch, GLA_HEADS, GLA_DV, GLA_DK))],
        out_shape=[jax.ShapeDtypeStruct((batch, seq, hv), BF16),
                   jax.ShapeDtypeStruct((batch, GLA_HEADS, GLA_DV, GLA_DK), F32)],
        scratch_shapes=[pltpu.VMEM((batch, hv, hw), F32)],
        compiler_params=_cparams("arbitrary"),
    )(p3, p3, p3, p3, p3, w2pad, gla_b, onorm)


def _swa_mb_kernel(q_ref, kp_ref, kc_ref, vp_ref, vc_ref, qn_ref, kn_ref, snk_ref,
                   o_ref, knew_ref):
    i = pl.program_id(0)
    nb = q_ref.shape[0]
    w = WINDOW
    g = SWA_HEADS // SWA_KV
    nr = g * w
    first = _iota((w, 2 * SWA_HD), 1) < SWA_HD
    first2 = _iota((2 * w, 2 * SWA_HD), 1) < SWA_HD
    t = _iota((nr, 2 * w), 0) % w
    j = _iota((nr, 2 * w), 1)
    dist = w + t - j
    valid = (dist >= 0) & (dist <= w) & ((j >= w) | (i > 0))
    distf = dist.astype(F32)
    hrow = _iota((nr, 1), 0) // w
    slope, snk = [], []
    for kv in range(SWA_KV):
        sl = jnp.zeros((nr, 1), F32)
        sk = jnp.zeros((nr, 1), F32)
        for hh in range(g):
            h = kv * g + hh
            sl = jnp.where(hrow == hh, 2.0 ** (-8.0 * (h + 1) / SWA_HEADS), sl)
            sk = jnp.where(hrow == hh, snk_ref[:, h:h + 1], sk)
        slope.append(sl * distf)
        snk.append(sk)

    units = [(b, kv) for b in range(nb) for kv in range(SWA_KV)]
    kcs, k2, v2, qx = [], [], [], []
    for b in range(nb):
        kc = _half_rms(kc_ref[b], kn_ref[...])
        kcs.append(kc)
        kk = jnp.concatenate([_half_rms(kp_ref[b], kn_ref[...]), kc], 0)
        vv = jnp.concatenate([vp_ref[b], vc_ref[b]], 0)
        kk_sw = pltpu.roll(kk, SWA_HD, axis=1)
        vv_sw = pltpu.roll(vv, SWA_HD, axis=1)
        k2 += [jnp.where(first2, kk, kk_sw).astype(BF16), jnp.where(first2, kk_sw, kk).astype(BF16)]
        v2 += [jnp.where(first2, vv, vv_sw).astype(BF16), jnp.where(first2, vv_sw, vv).astype(BF16)]
        for kv in range(SWA_KV):
            rows = []
            for jj in range(g // 2):
                grp = kv * (g // 2) + jj
                qg = _half_rms(q_ref[b, :, grp * 2 * SWA_HD:(grp + 1) * 2 * SWA_HD], qn_ref[...]) * SWA_HD ** -0.5
                rows += [jnp.where(first, qg, 0.0), jnp.where(first, 0.0, qg)]
            qx.append(jnp.concatenate(rows, 0).astype(BF16))
    nu = range(len(units))
    s = [_dot_nt(qx[u], k2[u]) for u in nu]
    s = [jnp.where(valid, s[u] - slope[units[u][1]], -jnp.inf) for u in nu]
    m = [jnp.maximum(jnp.max(s[u], -1, keepdims=True), snk[units[u][1]]) for u in nu]
    pr = [jnp.exp(s[u] - m[u]) for u in nu]
    den = [jnp.sum(pr[u], -1, keepdims=True) + jnp.exp(snk[units[u][1]] - m[u]) for u in nu]
    pr = [(pr[u] * (1.0 / den[u])).astype(BF16) for u in nu]
    o = [_dot(pr[u], v2[u]) for u in nu]
    for u, (b, kv) in enumerate(units):
        for jj in range(g // 2):
            grp = kv * (g // 2) + jj
            o_ref[b, :, grp * 2 * SWA_HD:(grp + 1) * 2 * SWA_HD] = jnp.where(
                first, o[u][(2 * jj) * w:(2 * jj + 1) * w], o[u][(2 * jj + 1) * w:(2 * jj + 2) * w]).astype(BF16)

    @pl.when(i == pl.num_programs(0) - 1)
    def _():
        for b in range(nb):
            knew_ref[b] = kcs[b]


def _swa_prompt_mb(p3, qn2, kn2, sinks):
    batch, seq, _ = p3.shape
    w = WINDOW
    kvw = SWA_KV * SWA_HD
    full = lambda shape: pl.BlockSpec(shape, lambda i: tuple(0 for _ in shape))
    prev = lambda i: jnp.maximum(i - 1, 0)
    return pl.pallas_call(
        _swa_mb_kernel,
        grid=(seq // w,),
        in_specs=[pl.BlockSpec((batch, w, BRANCH_W), lambda i: (0, i, C_CQ // BRANCH_W)),
                  pl.BlockSpec((batch, w, kvw), lambda i: (0, prev(i), C_CK // kvw)),
                  pl.BlockSpec((batch, w, kvw), lambda i: (0, i, C_CK // kvw)),
                  pl.BlockSpec((batch, w, kvw), lambda i: (0, prev(i), C_CV // kvw)),
                  pl.BlockSpec((batch, w, kvw), lambda i: (0, i, C_CV // kvw)),
                  full((1, kvw)), full((1, kvw)), full((1, SWA_HEADS))],
        out_specs=[pl.BlockSpec((batch, w, BRANCH_W), lambda i: (0, i, 0)),
                   full((batch, w, kvw))],
        out_shape=[jax.ShapeDtypeStruct((batch, seq, BRANCH_W), BF16),
                   jax.ShapeDtypeStruct((batch, w, kvw), F32)],
        compiler_params=_cparams("arbitrary"),
    )(p3, p3, p3, p3, p3, qn2, kn2, sinks)


DEC_TILE = 8


def _dec_kernel(qkv_ref, z_ref, sm_ref, bq_ref, bk_ref, bv_ref, br_ref, cq_ref, ck_ref, cv_ref,
                sdn_ref, buf_ref, sgl_ref, kc_ref, vc_ref,
                cw_ref, alog_ref, dtb_ref, dnon_ref, w2_ref, glb_ref, glon_ref,
                qn_ref, kn_ref, snk_ref, slp_ref,
                oa_ref, ob_ref, oc_ref, sdn_out, buf_out, sgl_out, kc_out, vc_out):
    bt = DEC_TILE
    x = qkv_ref[...]
    buf = buf_ref[...]
    acc = x * cw_ref[CONV_W - 1:CONV_W, :]
    for j in range(CONV_W - 1):
        acc = acc + buf[:, j * DN_QKV:(j + 1) * DN_QKV] * cw_ref[j:j + 1, :]
    y = _silu(acc)
    buf_out[:, 0:(CONV_W - 2) * DN_QKV] = buf[:, DN_QKV:]
    buf_out[:, (CONV_W - 2) * DN_QKV:] = x

    sm = sm_ref[...]
    beta_all = _sigmoid(sm)
    g_all = -jnp.exp(alog_ref[...]) * jax.nn.softplus(sm + dtb_ref[...])
    eye = (_iota((LANES, LANES), 0) == _iota((LANES, LANES), 1)).astype(F32)
    hk = DN_HEADS * DN_DK
    for h in range(DN_HEADS):
        q = y[:, h * DN_DK:(h + 1) * DN_DK]
        k = y[:, hk + h * DN_DK:hk + (h + 1) * DN_DK]
        v = y[:, 2 * hk + h * DN_DV:2 * hk + (h + 1) * DN_DV]
        q = q * lax.rsqrt(jnp.sum(q * q, -1, keepdims=True) + EPS) * DN_DK ** -0.5
        k = k * lax.rsqrt(jnp.sum(k * k, -1, keepdims=True) + EPS)
        beta = beta_all[:, SM_BETA + h:SM_BETA + h + 1]
        eg = jnp.exp(g_all[:, SM_G + h:SM_G + h + 1])
        kb = k * beta
        lhs = jnp.concatenate([kb * eg, q * eg], 0).astype(BF16)
        k_t = _dot_nt(eye, k)
        qk = jnp.sum(q * k, -1, keepdims=True)
        vb = v * beta
        o_rows = []
        for b in range(bt):
            s = sdn_ref[b, h]
            r = _dot(lhs, s.astype(BF16))
            u = vb[b:b + 1] - r[b:b + 1]
            o_rows.append(r[bt + b:bt + b + 1] + qk[b:b + 1] * u)
            sdn_out[b, h] = s * eg[b:b + 1] + k_t[:, b:b + 1] * u
        o = jnp.concatenate(o_rows, 0)
        o = o * lax.rsqrt(jnp.mean(o * o, -1, keepdims=True) + EPS) * dnon_ref[...]
        oa_ref[:, h * DN_DV:(h + 1) * DN_DV] = (o * _silu(z_ref[:, h * DN_DV:(h + 1) * DN_DV])).astype(BF16)

    lg = jax.nn.log_sigmoid(_dot(sm.astype(BF16), w2_ref[...]) + glb_ref[...]) / GLA_TAU
    elg = jnp.exp(lg)
    bq = bq_ref[...] * GLA_DK ** -0.5
    bk = bk_ref[...]
    qg = bq * elg
    lane = _iota((bt, LANES), 1)
    first = lane < GLA_DK
    rows_first = _iota((LANES, GLA_DV), 0) < GLA_DK
    for j in range(GLA_HEADS // 2):
        sl = slice(j * LANES, (j + 1) * LANES)
        qgj = qg[:, sl]
        lhs = jnp.concatenate([jnp.where(first, qgj, 0.0), jnp.where(first, 0.0, qgj)], 0).astype(BF16)
        cols = _dot_nt(eye, jnp.concatenate([elg[:, sl], bk[:, sl]], 0), HIGHEST)
        prod = bq[:, sl] * bk[:, sl]
        qk0 = jnp.sum(jnp.where(first, prod, 0.0), -1, keepdims=True)
        qk1 = jnp.sum(jnp.where(first, 0.0, prod), -1, keepdims=True)
        v0 = bv_ref[:, (2 * j) * GLA_DV:(2 * j + 1) * GLA_DV]
        v1 = bv_ref[:, (2 * j + 1) * GLA_DV:(2 * j + 2) * GLA_DV]
        o0, o1 = [], []
        for b in range(bt):
            s = sgl_ref[b, j]
            r = _dot(lhs, s.astype(BF16))
            o0.append(r[b:b + 1] + qk0[b:b + 1] * v0[b:b + 1])
            o1.append(r[bt + b:bt + b + 1] + qk1[b:b + 1] * v1[b:b + 1])
            vsel = jnp.where(rows_first, v0[b:b + 1], v1[b:b + 1])
            sgl_out[b, j] = s * cols[:, b:b + 1] + cols[:, bt + b:bt + b + 1] * vsel
        for hh, rows in ((2 * j, o0), (2 * j + 1, o1)):
            o = jnp.concatenate(rows, 0)
            o = o * lax.rsqrt(jnp.mean(o * o, -1, keepdims=True) + EPS) * glon_ref[...]
            ob_ref[:, hh * GLA_DV:(hh + 1) * GLA_DV] = (o * _silu(br_ref[:, hh * GLA_DV:(hh + 1) * GLA_DV])).astype(BF16)

    g = SWA_HEADS // SWA_KV
    nr = bt * SWA_HEADS
    cq = cq_ref[...]
    cq = cq * lax.rsqrt(jnp.mean(cq * cq, -1, keepdims=True) + EPS) * qn_ref[...] * SWA_HD ** -0.5
    head = _iota((nr, 2 * SWA_HD), 0) % SWA_HEADS
    in_half = (head // g) == (_iota((nr, 2 * SWA_HD), 1) // SWA_HD)
    qx = jnp.where(in_half, jnp.concatenate([cq, cq], -1), 0.0)
    knew = _half_rms(ck_ref[...], kn_ref[...])
    vnew = cv_ref[...]
    s_c, kn_rows, vn_rows = [], [], []
    for b in range(bt):
        s_c.append(_dot_nt(qx[b * SWA_HEADS:(b + 1) * SWA_HEADS].astype(BF16), kc_ref[b].astype(BF16)))
        kn_rows.append(jnp.broadcast_to(knew[b:b + 1], (SWA_HEADS, 2 * SWA_HD)))
        vn_rows.append(jnp.broadcast_to(vnew[b:b + 1], (SWA_HEADS, 2 * SWA_HD)))
    s_c = jnp.concatenate(s_c, 0)
    kn_x = jnp.concatenate(kn_rows, 0)
    vn_x = jnp.concatenate(vn_rows, 0)
    slopes = slp_ref[...]
    snk = snk_ref[...]
    dist = (WINDOW - _iota((nr, WINDOW), 1)).astype(F32)
    s_c = s_c - slopes * dist
    s_n = jnp.sum(qx * kn_x, -1, keepdims=True)
    m = jnp.maximum(jnp.maximum(jnp.max(s_c, -1, keepdims=True), s_n), snk)
    p_c = jnp.exp(s_c - m)
    p_n = jnp.exp(s_n - m)
    den = jnp.sum(p_c, -1, keepdims=True) + p_n + jnp.exp(snk - m)
    p_c = p_c / den
    p_n = p_n / den
    half_sel = (_iota((nr, SWA_HD), 0) % SWA_HEADS) < g
    for b in range(bt):
        rs = slice(b * SWA_HEADS, (b + 1) * SWA_HEADS)
        r = _dot(p_c[rs].astype(BF16), vc_ref[b].astype(BF16)) + p_n[rs] * vn_x[rs]
        oc_ref[rs, :] = jnp.where(half_sel[rs], r[:, :SWA_HD], r[:, SWA_HD:]).astype(BF16)
        kc_out[b, 0:WINDOW - 1, :] = kc_ref[b, 1:WINDOW, :]
        kc_out[b, WINDOW - 1:WINDOW, :] = knew[b:b + 1]
        vc_out[b, 0:WINDOW - 1, :] = vc_ref[b, 1:WINDOW, :]
        vc_out[b, WINDOW - 1:WINDOW, :] = vnew[b:b + 1]


def _dec_kernel_aliased(*refs):
    n_in, n_alias = 26, 5
    _dec_kernel(*refs[:n_in], *refs[n_in + n_alias:])


def _dec_mixers(layer, p, cq_r, states, prev_out, conv_w, alog_row, dtb_row, dn_on, w2pad, gla_b, gla_on,
                qn, kn2, snk_col, slp_col):
    n = p.shape[0]
    bt = DEC_TILE
    hw = GLA_HEADS * GLA_DK
    kvw = SWA_KV * SWA_HD
    nr = bt * SWA_HEADS

    def col(width, off):
        return pl.BlockSpec((bt, width), lambda i: (i, off // width))

    def full(shape):
        return pl.BlockSpec(shape, lambda i: tuple(0 for _ in shape))

    def slot(a):
        rest = a.shape[2:]
        return pl.BlockSpec((None, bt) + rest, lambda i: (layer, i) + tuple(0 for _ in rest))

    in_specs = [col(DN_QKV, C_AQKV), col(BRANCH_W, C_AZ), col(LANES, C_SM), col(hw, C_BQ), col(hw, C_BK),
                col(BRANCH_W, C_BV), col(BRANCH_W, C_BR),
                pl.BlockSpec((nr, SWA_HD), lambda i: (i, 0)),
                col(kvw, C_CK), col(kvw, C_CV)]
    in_specs += [slot(a) for a in states]
    in_specs += [full((CONV_W, DN_QKV)), full((1, LANES)), full((1, LANES)), full((1, DN_DV)),
                 full((LANES, hw)), full((1, hw)), full((1, GLA_DV)),
                 full((1, SWA_HD)), full((1, kvw)), full((nr, 1)), full((nr, 1))]
    args = [p, p, p, p, p, p, p, cq_r, p, p, *states,
            conv_w, alog_row, dtb_row, dn_on, w2pad, gla_b, gla_on, qn, kn2, snk_col, slp_col]
    body, aliases = _dec_kernel, {}
    if prev_out is not None:
        body = _dec_kernel_aliased
        aliases = {len(args) + k: 3 + k for k in range(len(prev_out))}
        in_specs += [pl.BlockSpec(memory_space=pl.ANY)] * len(prev_out)
        args += list(prev_out)
    return pl.pallas_call(
        body,
        grid=(n // bt,),
        in_specs=in_specs,
        out_specs=[pl.BlockSpec((bt, BRANCH_W), lambda i: (i, 0)),
                   pl.BlockSpec((bt, BRANCH_W), lambda i: (i, 0)),
                   pl.BlockSpec((nr, SWA_HD), lambda i: (i, 0))] + [slot(a) for a in states],
        out_shape=[jax.ShapeDtypeStruct((n, BRANCH_W), BF16),
                   jax.ShapeDtypeStruct((n, BRANCH_W), BF16),
                   jax.ShapeDtypeStruct((n * SWA_HEADS, SWA_HD), BF16)]
        + [jax.ShapeDtypeStruct(a.shape, F32) for a in states],
        input_output_aliases=aliases,
        compiler_params=_cparams("parallel"),
    )(*args)


def _merge_kernel(ba_ref, bb_ref, bc_ref, gate_ref, x_ref, gt_ref, sc_ref, sh_ref, ln_ref,
                  wb_ref, wo_ref, wr_ref, rb_ref, x1_ref, h2_ref, comb_ref, cnt_ref):
    mix = None
    for n, br in enumerate((ba_ref, bb_ref, bc_ref)):
        up = _dot(br[...], wb_ref[n])
        term = _sigmoid(gate_ref[:, n * D_MODEL:(n + 1) * D_MODEL]) * up
        mix = term if mix is None else mix + term
    x1 = x_ref[...] + gt_ref[0] * _dot(mix.astype(BF16), wo_ref[...])
    x1_ref[...] = x1
    h2 = x1 * lax.rsqrt(jnp.mean(x1 * x1, -1, keepdims=True) + EPS) * ln_ref[...]
    h2 = h2 * (1.0 + sc_ref[0]) + sh_ref[0]
    h2b = h2.astype(BF16)
    h2_ref[...] = h2b

    h2l = (h2 - h2b.astype(F32)).astype(BF16)
    hi_lo = _dot(h2b, wr_ref[...])
    logits = hi_lo[:, :LANES] + hi_lo[:, LANES:] + _dot(h2l, wr_ref[:, :LANES]) + rb_ref[...]
    lane = _iota(logits.shape, 1).astype(F32)
    big = float(LANES)
    lc = jnp.where(lane < N_GROUPS, logits, -jnp.inf)
    mc = jnp.max(lc, -1, keepdims=True)
    pg = 1.0 / jnp.sum(jnp.exp(lc - mc), -1, keepdims=True)
    grp = jnp.min(jnp.where(lc == mc, lane, big), -1, keepdims=True)
    lo = R_EXP + grp * EXP_PER_GROUP
    emask = (lane >= lo) & (lane < lo + EXP_PER_GROUP)
    le = jnp.where(emask, logits, -jnp.inf)
    pe = jnp.exp(le - jnp.max(le, -1, keepdims=True))
    pe = pe / jnp.sum(pe, -1, keepdims=True)
    v1 = jnp.max(pe, -1, keepdims=True)
    i1 = jnp.min(jnp.where(emask & (pe == v1), lane, big), -1, keepdims=True)
    pe2 = jnp.where(emask & (lane != i1), pe, -1.0)
    v2 = jnp.max(pe2, -1, keepdims=True)
    i2 = jnp.min(jnp.where(pe2 == v2, lane, big), -1, keepdims=True)
    tot = v1 + v2
    comb_ref[...] = jnp.where(lane == 0.0, i1 - R_EXP, jnp.where(lane == 1.0, i2 - R_EXP, jnp.where(
        lane == 2.0, pg * v1 / tot, jnp.where(lane == 3.0, pg * v2 / tot, 0.0))))
    picks = jnp.where((lane == i1 - R_EXP) | (lane == i2 - R_EXP), 1.0, 0.0)
    cnt_ref[...] = jnp.broadcast_to(jnp.sum(picks, 0, keepdims=True), cnt_ref.shape)


def _merge(layer, ba, bb, bc, p, x, gt, sc, sh, ln, wb, wo, wr, rb, tm, tokens_per_row):
    t, d = x.shape
    tok = lambda width: pl.BlockSpec((tm, width), lambda i: (i, 0))
    full = lambda shape: pl.BlockSpec((None,) + shape, lambda i: (layer,) + tuple(0 for _ in shape))
    return pl.pallas_call(
        _merge_kernel,
        grid=(t // tm,),
        in_specs=[tok(BRANCH_W), tok(BRANCH_W), tok(BRANCH_W),
                  pl.BlockSpec((tm, N_BRANCH * d), lambda i: (i, C_GATE)),
                  tok(d),
                  _mod_spec(gt, tm, tokens_per_row), _mod_spec(sc, tm, tokens_per_row),
                  _mod_spec(sh, tm, tokens_per_row),
                  full((1, d)), full((N_BRANCH, BRANCH_W, d)), full((d, d)), full((d, 2 * LANES)),
                  full((1, LANES))],
        out_specs=[tok(d), tok(d), tok(LANES), pl.BlockSpec((8, LANES), lambda i: (i, 0))],
        out_shape=[jax.ShapeDtypeStruct((t, d), F32),
                   jax.ShapeDtypeStruct((t, d), BF16),
                   jax.ShapeDtypeStruct((t, LANES), F32),
                   jax.ShapeDtypeStruct((t // tm * 8, LANES), F32)],
        compiler_params=_cparams("parallel"),
    )(ba, bb, bc, p, x, gt, sc, sh, ln, wb, wo, wr, rb)


def _moe_kernel(h_ref, comb_ref, x1_ref, gt_ref, w1_ref, w3_ref, w2_ref, o_ref, acc_ref):
    e = pl.program_id(1)

    @pl.when(e == 0)
    def _():
        acc_ref[...] = jnp.zeros_like(acc_ref)

    h = h_ref[...]
    he = _silu(_dot(h, w1_ref[0].astype(BF16))) * _dot(h, w3_ref[0].astype(BF16))
    ye = _dot(he.astype(BF16), w2_ref[0].astype(BF16))
    comb = comb_ref[...]
    ef = e.astype(F32)
    ce = (jnp.where(comb[:, 0:1] == ef, comb[:, 2:3], 0.0) + jnp.where(comb[:, 1:2] == ef, comb[:, 3:4], 0.0))
    acc_ref[...] += ce * ye

    @pl.when(e == pl.num_programs(1) - 1)
    def _():
        o_ref[...] = x1_ref[...] + gt_ref[0] * acc_ref[...]


def _moe(layer, h2, comb, x1, gt, w1, w3, w2, tm, tokens_per_row):
    t, d = x1.shape
    _, ne, _, de = w1.shape
    return pl.pallas_call(
        _moe_kernel,
        grid=(t // tm, ne),
        in_specs=[pl.BlockSpec((tm, d), lambda i, e: (i, 0)),
                  pl.BlockSpec((tm, LANES), lambda i, e: (i, 0)),
                  pl.BlockSpec((tm, d), lambda i, e: (i, 0)),
                  _mod_spec(gt, tm, tokens_per_row),
                  pl.BlockSpec((None, 1, d, de), lambda i, e: (layer, e, 0, 0)),
                  pl.BlockSpec((None, 1, d, de), lambda i, e: (layer, e, 0, 0)),
                  pl.BlockSpec((None, 1, de, d), lambda i, e: (layer, e, 0, 0))],
        out_specs=pl.BlockSpec((tm, d), lambda i, e: (i, 0)),
        out_shape=jax.ShapeDtypeStruct((t, d), F32),
        scratch_shapes=[pltpu.VMEM((tm, d), F32)],
        compiler_params=_cparams("parallel", "arbitrary"),
    )(h2, comb, x1, gt, w1, w3, w2)


MOE_TK = 512
MOE_TM = 512
ROW_CHUNK = 16
MOE_LB = 2 * MOE_TK + 512
MOE_CH = MOE_LB // ROW_CHUNK
MOE_XW = D_MODEL + LANES
assert 2 * MOE_TK + N_EXPERTS * (ROW_CHUNK - 1) <= MOE_LB - ROW_CHUNK


def _moe_sorted_tiles(t):
    rows = 2 * t + (t // MOE_TK) * N_EXPERTS * (ROW_CHUNK - 1) + N_EXPERTS * (MOE_TM - 1)
    return -(-rows // MOE_TM)


def _plan_kernel(r_ref, cnt_ref, lp_ref, dc_ref, te_ref, base_ref):
    t = pl.program_id(0)
    tk = r_ref.shape[0]
    ntp = te_ref.shape[0]
    r = r_ref[...]
    lane = _iota((tk, LANES), 1).astype(F32)
    sel1, sel2 = lane == r[:, 0:1], lane == r[:, 1:2]
    oh = jnp.where(sel1 | sel2, 1.0, 0.0)
    upper = (_iota((LANES, LANES), 0) < _iota((LANES, LANES), 1)).astype(F32)

    def whole_chunks(cnt):
        return jnp.floor((cnt + (ROW_CHUNK - 1)) * (1.0 / ROW_CHUNK)) * ROW_CHUNK

    def excl_cumsum(v):
        return _dot(jnp.broadcast_to(v, (8, LANES)), upper, HIGHEST)[0:1]

    @pl.when(t == 0)
    def _():
        tot = jnp.sum(whole_chunks(cnt_ref[...]), 0, keepdims=True) * 0.125
        gp = jnp.floor((tot + (MOE_TM - 1)) * (1.0 / MOE_TM)) * MOE_TM
        off = excl_cumsum(gp)
        base_ref[...] = off
        end = off + gp
        lane_t = _iota((ntp, LANES), 1).astype(F32)
        start = _iota((ntp, 1), 0).astype(F32) * MOE_TM
        te = jnp.sum(jnp.where((lane_t < N_EXPERTS) & (end <= start), 1.0, 0.0), -1, keepdims=True)
        mine = lane_t == te
        filled = jnp.sum(jnp.where(mine, tot + off, 0.0), -1, keepdims=True)
        tv = jnp.clip(filled - start, 0.0, float(MOE_TM))
        n_used = jnp.sum(jnp.where(lane_t == N_EXPERTS - 1, end, 0.0), -1, keepdims=True) * (1.0 / MOE_TM)
        te_ref[...] = jnp.where(lane_t == 0.0, jnp.minimum(te, N_EXPERTS - 1.0),
                                jnp.where(lane_t == 1.0, tv, jnp.where(lane_t == 2.0, n_used, 0.0))
                                ).astype(jnp.int32)

    c8 = whole_chunks(jnp.sum(oh, 0, keepdims=True))
    base = base_ref[...]
    lo = excl_cumsum(c8)
    below = (_iota((tk, tk), 0) > _iota((tk, tk), 1)).astype(BF16)
    p = _dot(below, oh.astype(BF16)) + lo
    lp1 = jnp.sum(jnp.where(sel1, p, 0.0), -1, keepdims=True)
    lp2 = jnp.sum(jnp.where(sel2, p, 0.0), -1, keepdims=True)
    lp_ref[...] = jnp.where(lane == 0.0, lp1, jnp.where(lane == 1.0, lp2, jnp.where(lane < 4.0, r, 0.0)))
    lane_c = _iota((MOE_CH, LANES), 1).astype(F32)
    cstart = _iota((MOE_CH, 1), 0).astype(F32) * ROW_CHUNK
    ej = jnp.sum(jnp.where((lane_c < N_EXPERTS) & (lo + c8 <= cstart), 1.0, 0.0), -1, keepdims=True)
    dj = jnp.sum(jnp.where(lane_c == ej, base - lo, 0.0), -1, keepdims=True) + cstart
    nrows = jnp.sum(c8, -1, keepdims=True)
    last = _iota((MOE_CH, 1), 0) == MOE_CH - 1
    dcv = jnp.where(last, nrows * (1.0 / ROW_CHUNK), jnp.where(cstart < nrows, dj, 0.0))
    dc_ref[...] = jnp.broadcast_to(dcv, (MOE_CH, LANES)).astype(jnp.int32)
    base_ref[...] = base + c8


def _moe_plan(route, counts):
    t = route.shape[0]
    nt = t // MOE_TK
    ntp = -(-_moe_sorted_tiles(t) // 8) * 8
    assert counts.shape == (nt * 8, LANES)
    return pl.pallas_call(
        _plan_kernel,
        grid=(nt,),
        in_specs=[pl.BlockSpec((MOE_TK, LANES), lambda i: (i, 0)),
                  pl.BlockSpec((nt * 8, LANES), lambda i: (0, 0))],
        out_specs=[pl.BlockSpec((MOE_TK, LANES), lambda i: (i, 0)),
                   pl.BlockSpec((MOE_CH, LANES), lambda i: (i, 0)),
                   pl.BlockSpec((ntp, LANES), lambda i: (0, 0))],
        out_shape=[jax.ShapeDtypeStruct((t, LANES), F32),
                   jax.ShapeDtypeStruct((nt * MOE_CH, LANES), jnp.int32),
                   jax.ShapeDtypeStruct((ntp, LANES), jnp.int32)],
        scratch_shapes=[pltpu.VMEM((1, LANES), F32)],
        compiler_params=_cparams("arbitrary"),
    )(route, counts)


def _pick_onehot(lp):
    pos = _iota((lp.shape[0], MOE_LB), 1).astype(F32)
    return pos == lp[:, 0:1], pos == lp[:, 1:2]


def _chunk_loop(n, fn):
    def body(j, carry):
        fn(j)
        return carry

    lax.fori_loop(0, n, body, 0)


def _dispatch_kernel(dc_ref, h_ref, lp_ref, xs_ref, buf_ref, sem, nprev_ref):
    t, nt = pl.program_id(0), pl.num_programs(0)
    slot = t % 2
    tk = h_ref.shape[0]
    lp = lp_ref[...]
    oh1, oh2 = _pick_onehot(lp)
    lane = _iota((tk, LANES), 1)

    def split3(w):
        hi = w.astype(BF16).astype(F32)
        mid = (w - hi).astype(BF16).astype(F32)
        low = w - hi - mid
        return jnp.where(lane == 0, hi, jnp.where(lane == 1, mid, jnp.where(lane == 2, low, 0.0))).astype(BF16)

    b1, b2 = jnp.where(oh1, 1.0, 0.0).astype(BF16), jnp.where(oh2, 1.0, 0.0).astype(BF16)
    buf_ref[slot, :, 0:D_MODEL] = _dot_tn(b1 + b2, h_ref[...]).astype(BF16)
    buf_ref[slot, :, D_MODEL:] = (_dot_tn(b1, split3(lp[:, 2:3])) + _dot_tn(b2, split3(lp[:, 3:4]))).astype(BF16)

    def chunk_copy(j, s):
        src = buf_ref.at[s, pl.ds(pl.multiple_of(j * ROW_CHUNK, ROW_CHUNK), ROW_CHUNK), :]
        dst = xs_ref.at[pl.ds(pl.multiple_of(dc_ref[0, 0, j], ROW_CHUNK), ROW_CHUNK), :]
        return pltpu.make_async_copy(src, dst, sem.at[s])

    nch = dc_ref[0, 0, MOE_CH - 1]
    _chunk_loop(nch, lambda j: chunk_copy(j, slot).start())

    @pl.when(t > 0)
    def _():
        _chunk_loop(nprev_ref[0], lambda j: chunk_copy(0, 1 - slot).wait())

    nprev_ref[0] = nch

    @pl.when(t == nt - 1)
    def _():
        _chunk_loop(nch, lambda j: chunk_copy(0, slot).wait())


def _moe_dispatch(dc3, h2, lp, n_rows):
    t, d = h2.shape
    nt = t // MOE_TK
    return pl.pallas_call(
        _dispatch_kernel,
        grid=(nt,),
        in_specs=[pl.BlockSpec((1, 1, MOE_CH), lambda i: (i, 0, 0), memory_space=pltpu.SMEM),
                  pl.BlockSpec((MOE_TK, d), lambda i: (i, 0)),
                  pl.BlockSpec((MOE_TK, LANES), lambda i: (i, 0))],
        out_specs=pl.BlockSpec(memory_space=pl.ANY),
        out_shape=jax.ShapeDtypeStruct((n_rows, MOE_XW), BF16),
        scratch_shapes=[pltpu.VMEM((2, MOE_LB, MOE_XW), BF16), pltpu.SemaphoreType.DMA((2,)),
                        pltpu.SMEM((1,), jnp.int32)],
        compiler_params=_cparams("arbitrary"),
    )(dc3, h2, lp)


def _experts_kernel(te_ref, tv_ref, nu_ref, x_ref, w1_ref, w3_ref, w2_ref, o_ref, w1b, w3b, w2b):
    i = pl.program_id(0)

    @pl.when(i < nu_ref[0])
    def _():
        @pl.when((i == 0) | (te_ref[i] != te_ref[jnp.maximum(i - 1, 0)]))
        def _():
            w1b[...] = w1_ref[0].astype(BF16)
            w3b[...] = w3_ref[0].astype(BF16)
            w2b[...] = w2_ref[0].astype(BF16)

        tm = x_ref.shape[0]
        valid = _iota((tm, 1), 0) < tv_ref[i]
        h = jnp.where(valid, x_ref[:, :D_MODEL], jnp.zeros((), BF16))
        wx = x_ref[:, D_MODEL:].astype(F32)
        wv = jnp.where(valid, wx[:, 0:1] + wx[:, 1:2] + wx[:, 2:3], 0.0)
        he = _silu(_dot(h, w1b[...])) * _dot(h, w3b[...])
        o_ref[...] = (wv * _dot(he.astype(BF16), w2b[...])).astype(BF16)


def _moe_experts(layer, te, tv, nu, xs, w1, w3, w2):
    n_rows = xs.shape[0]
    _, _, d, de = w1.shape
    cur = lambda i, te, tv, nu: jnp.minimum(i, nu[0] - 1)
    wspec = lambda a, b: pl.BlockSpec((None, 1, a, b), lambda i, te, tv, nu: (layer, te[cur(i, te, tv, nu)], 0, 0))
    return pl.pallas_call(
        _experts_kernel,
        grid_spec=pltpu.PrefetchScalarGridSpec(
            num_scalar_prefetch=3,
            grid=(n_rows // MOE_TM,),
            in_specs=[pl.BlockSpec((MOE_TM, MOE_XW), lambda i, te, tv, nu: (cur(i, te, tv, nu), 0)),
                      wspec(d, de), wspec(d, de), wspec(de, d)],
            out_specs=pl.BlockSpec((MOE_TM, d), lambda i, te, tv, nu: (cur(i, te, tv, nu), 0)),
            scratch_shapes=[pltpu.VMEM((d, de), BF16), pltpu.VMEM((d, de), BF16), pltpu.VMEM((de, d), BF16)]),
        out_shape=jax.ShapeDtypeStruct((n_rows, d), BF16),
        compiler_params=_cparams("arbitrary"),
    )(te, tv, nu, xs, w1, w3, w2)


def _combine_kernel(dc_ref, dcn_ref, lp_ref, x1_ref, gt_ref, ys_ref, o_ref, buf_ref, sem):
    t, nt = pl.program_id(0), pl.num_programs(0)
    slot = t % 2

    def chunk_copy(tab, j, s):
        src = ys_ref.at[pl.ds(pl.multiple_of(tab[0, 0, j], ROW_CHUNK), ROW_CHUNK), :]
        dst = buf_ref.at[s, pl.ds(pl.multiple_of(j * ROW_CHUNK, ROW_CHUNK), ROW_CHUNK), :]
        return pltpu.make_async_copy(src, dst, sem.at[s])

    nch = dc_ref[0, 0, MOE_CH - 1]

    @pl.when(t == 0)
    def _():
        buf_ref[...] = jnp.zeros_like(buf_ref)
        _chunk_loop(nch, lambda j: chunk_copy(dc_ref, j, slot).start())

    @pl.when(t + 1 < nt)
    def _():
        _chunk_loop(dcn_ref[0, 0, MOE_CH - 1], lambda j: chunk_copy(dcn_ref, j, 1 - slot).start())

    _chunk_loop(nch, lambda j: chunk_copy(dc_ref, 0, slot).wait())
    live = _iota((MOE_LB, 1), 0) < nch * ROW_CHUNK
    local = jnp.where(live, buf_ref[slot], jnp.zeros((), BF16))
    oh1, oh2 = _pick_onehot(lp_ref[...])
    perm = jnp.where(oh1 | oh2, 1.0, 0.0).astype(BF16)
    o_ref[...] = x1_ref[...] + gt_ref[0] * _dot(perm, local)


def _moe_combine(dc3, lp, x1, gt, ys, tokens_per_row):
    t, d = x1.shape
    nt = t // MOE_TK
    smem = lambda f: pl.BlockSpec((1, 1, MOE_CH), f, memory_space=pltpu.SMEM)
    return pl.pallas_call(
        _combine_kernel,
        grid=(nt,),
        in_specs=[smem(lambda i: (i, 0, 0)), smem(lambda i: (jnp.minimum(i + 1, nt - 1), 0, 0)),
                  pl.BlockSpec((MOE_TK, LANES), lambda i: (i, 0)),
                  pl.BlockSpec((MOE_TK, d), lambda i: (i, 0)),
                  _mod_spec(gt, MOE_TK, tokens_per_row),
                  pl.BlockSpec(memory_space=pl.ANY)],
        out_specs=pl.BlockSpec((MOE_TK, d), lambda i: (i, 0)),
        out_shape=jax.ShapeDtypeStruct((t, d), F32),
        scratch_shapes=[pltpu.VMEM((2, MOE_LB, d), BF16), pltpu.SemaphoreType.DMA((2,))],
        compiler_params=_cparams("arbitrary"),
    )(dc3, dc3, lp, x1, gt, ys)


def _moe_sparse(layer, h2, route, counts, x1, gt, w1, w3, w2, tokens_per_row):
    t = h2.shape[0]
    nt = t // MOE_TK
    lp, dc, tmeta = _moe_plan(route, counts)
    dc3 = dc[:, 0].reshape(nt, 1, MOE_CH)
    n_tiles = _moe_sorted_tiles(t)
    xs = _moe_dispatch(dc3, h2, lp, n_tiles * MOE_TM)
    ys = _moe_experts(layer, tmeta[:n_tiles, 0], tmeta[:n_tiles, 1], tmeta[0:1, 2], xs, w1, w3, w2)
    return _moe_combine(dc3, lp, x1, gt, ys, tokens_per_row)


def _w_in_segments():
    sizes = (DN_QKV, DN_HEADS * DN_DV, DN_HEADS, DN_HEADS,
             GLA_HEADS * GLA_DK, GLA_HEADS * GLA_DK, GLA_HEADS * GLA_DV, GLA_HEADS * GLA_DV, GLA_RANK,
             SWA_HEADS * SWA_HD, SWA_KV * SWA_HD, SWA_KV * SWA_HD, N_BRANCH * D_MODEL)
    offs = [0]
    for s in sizes:
        offs.append(offs[-1] + s)
    (a_qkv, a_z, a_b, a_a, b_q, b_k, b_v, b_r, b_lr, c_q, c_k, c_v, gate) = range(len(sizes))
    dst = {gate: C_GATE, a_qkv: C_AQKV, a_z: C_AZ, b_v: C_BV, b_r: C_BR, c_q: C_CQ, b_q: C_BQ, b_k: C_BK,
           c_k: C_CK, c_v: C_CV, a_b: C_SM + SM_BETA, a_a: C_SM + SM_G, b_lr: C_SM + SM_LR}
    return [(offs[i], dst[i], sizes[i]) for i in range(len(sizes))], offs[-1]


def _w_in_prep_kernel(w_ref, o_ref):
    segments, _ = _w_in_segments()
    n_src = w_ref.shape[-1]
    o_ref[:, C_SM:] = jnp.zeros((o_ref.shape[0], P_PAD - C_SM), BF16)
    for src, dst, width in segments:
        lo = (src // LANES) * LANES
        hi = min(-(-(src + width) // LANES) * LANES, n_src)
        win = w_ref[:, lo:hi]
        o_ref[:, dst:dst + width] = win[:, src - lo:src - lo + width].astype(BF16)


def _permute_w_in(w):
    depth, d, n = w.shape
    assert n == _w_in_segments()[1]
    rows = 256
    return pl.pallas_call(
        _w_in_prep_kernel,
        grid=(depth, d // rows),
        in_specs=[pl.BlockSpec((None, rows, n), lambda l, i: (l, i, 0))],
        out_specs=pl.BlockSpec((None, rows, P_PAD), lambda l, i: (l, i, 0)),
        out_shape=jax.ShapeDtypeStruct((depth, d, P_PAD), BF16),
        compiler_params=_cparams("parallel", "parallel"),
    )(w)


def _lane_row(v, off):
    depth, n = v.shape
    return jnp.zeros((depth, 1, LANES), F32).at[:, 0, off:off + n].set(v.astype(F32))


def kernel(x_prompt, x_sample, c_prompt, c_sample, state_dn, state_dn_conv, state_gla, cache_swa_k, cache_swa_v, ln1_g, ln2_g, ada_w, ada_b, w_in, dn_conv_w, dn_a_log, dn_dt_bias, dn_onorm_g, gla_w2, gla_b, gla_onorm_g, swa_qnorm_g, swa_knorm_g, swa_sinks, w_branch, w_out, rc_w, rc_b, re_w, re_b, w1, w3, w2):
    batch, seq, d = x_prompt.shape
    nb = x_sample.shape[0]
    depth = w_in.shape[0]
    assert x_sample.shape[1] == 1 and d == D_MODEL and seq % CHUNK == 0 and nb % DEC_TILE == 0
    kvw = SWA_KV * SWA_HD

    w_in_p = _permute_w_in(w_in)
    wb_b, wo_b = w_branch.astype(BF16), w_out.astype(BF16)
    wr = jnp.concatenate([rc_w, re_w, jnp.zeros((depth, d, LANES - N_GROUPS - N_EXPERTS), F32)], -1)
    rb = jnp.concatenate([rc_b, re_b, jnp.zeros((depth, LANES - N_GROUPS - N_EXPERTS), F32)], -1)[:, None, :]
    wr_hi = wr.astype(BF16)
    wr2 = jnp.concatenate([wr_hi, (wr - wr_hi.astype(F32)).astype(BF16)], -1)
    ln2_all = ln2_g[:, None, :]
    alog_row = _lane_row(dn_a_log, SM_G)
    dtb_row = _lane_row(dn_dt_bias, SM_G)
    w2pad = jnp.zeros((depth, LANES, GLA_HEADS * GLA_DK), F32).at[:, SM_LR:SM_LR + GLA_RANK].set(gla_w2).astype(BF16)
    kn2 = jnp.concatenate([swa_knorm_g] * SWA_KV, -1)[:, None, :]
    slopes = jnp.exp2(-8.0 * jnp.arange(1, SWA_HEADS + 1, dtype=F32) / SWA_HEADS)
    slp_col = jnp.tile(slopes, DEC_TILE)[:, None]

    pad_rows = (-(batch + nb)) % 8
    c_all = jnp.concatenate([c_prompt, c_sample, jnp.zeros((pad_rows, d), F32)], 0)
    mod = _ada(c_all, ada_w, ada_b)

    xp = x_prompt.reshape(batch * seq, d)
    xs = x_sample.reshape(nb, d)
    tm_p = 2048 if seq % 2048 == 0 else CHUNK
    tm_m = 512 if seq % 512 == 0 else CHUNK

    dec_states = (state_dn,
                  state_dn_conv.reshape(depth, nb, (CONV_W - 1) * DN_QKV),
                  state_gla.reshape(depth, nb, GLA_HEADS // 2, 2 * GLA_DK, GLA_DV),
                  cache_swa_k.reshape(depth, nb, WINDOW, kvw),
                  cache_swa_v.reshape(depth, nb, WINDOW, kvw))
    dec_out = None
    st_p = []
    for l in range(depth):
        mp = [m[:, None, :] for m in jnp.split(mod[l, :batch], 6, -1)]
        ms = [m[None] for m in jnp.split(mod[l, batch:batch + nb], 6, -1)]
        ln1, ln2 = ln1_g[l][None], ln2_g[l][None]
        conv_w = dn_conv_w[l]
        dn_on, gla_on = dn_onorm_g[l][None], gla_onorm_g[l][None]
        glb = gla_b[l][None]
        qn = swa_qnorm_g[l][None]
        snk = swa_sinks[l][None]

        pp = _in_proj(l, xp, mp[1], mp[0], ln1, w_in_p, tm_p, seq)
        pp3 = pp.reshape(batch, seq, P_PAD)
        ba, dn_s = _dn_prompt_mb(pp3, conv_w, alog_row[l], dtb_row[l], dn_on)
        bb, gla_st = _gla_prompt_mb(pp3, w2pad[l], glb, gla_on)
        bc, k_new = _swa_prompt_mb(pp3, jnp.concatenate([qn, qn], -1), kn2[l], snk)
        ba, bb, bc = (z.reshape(batch * seq, BRANCH_W) for z in (ba, bb, bc))
        x1, h2, comb, counts = _merge(l, ba, bb, bc, pp, xp, mp[2], mp[4], mp[3], ln2_all, wb_b, wo_b, wr2, rb,
                              tm_m, seq)
        xp = _moe_sparse(l, h2, comb, counts, x1, mp[5], w1, w3, w2, seq)
        pp3 = pp.reshape(batch, seq, P_PAD)
        st_p.append((dn_s,
                     pp3[:, seq - (CONV_W - 1):, C_AQKV:C_AQKV + DN_QKV],
                     jnp.swapaxes(gla_st, -1, -2),
                     k_new.reshape(batch, WINDOW, SWA_KV, SWA_HD),
                     pp3[:, seq - WINDOW:, C_CV:C_CV + kvw].reshape(batch, WINDOW, SWA_KV, SWA_HD)))

        ps = _in_proj(l, xs, ms[1], ms[0], ln1, w_in_p, nb, nb)
        cq_r = ps[:, C_CQ:C_CQ + SWA_HEADS * SWA_HD].reshape(nb * SWA_HEADS, SWA_HD)
        oa, ob, oc_r, *dec_out = _dec_mixers(
            l, ps, cq_r, dec_states, dec_out, conv_w, alog_row[l], dtb_row[l], dn_on, w2pad[l], glb, gla_on,
            qn, kn2[l], jnp.tile(swa_sinks[l], DEC_TILE)[:, None], slp_col)
        oc = oc_r.reshape(nb, SWA_HEADS * SWA_HD)
        x1, h2, comb, _ = _merge(l, oa, ob, oc, ps, xs, ms[2], ms[4], ms[3], ln2_all, wb_b, wo_b, wr2, rb,
                              nb, nb)
        xs = _moe(l, h2, comb, x1, ms[5], w1, w3, w2, nb, nb)

    dn_p, conv_p, gla_p, k_p, v_p = [jnp.stack(z) for z in zip(*st_p)]
    dn_s = dec_out[0]
    conv_s = dec_out[1].reshape(depth, nb, CONV_W - 1, DN_QKV)
    gla_s = dec_out[2].reshape(depth, nb, GLA_HEADS, GLA_DK, GLA_DV)
    k_s = dec_out[3].reshape(depth, nb, WINDOW, SWA_KV, SWA_HD)
    v_s = dec_out[4].reshape(depth, nb, WINDOW, SWA_KV, SWA_HD)
    return (xp.reshape(batch, seq, d), xs.reshape(nb, 1, d), dn_p, dn_s, conv_p, conv_s, gla_p, gla_s,
            k_p, k_s, v_p, v_s)
```

```python
import jax
import jax.numpy as jnp
from jax import lax
from jax.experimental import pallas as pl
from jax.experimental.pallas import tpu as pltpu

F32 = jnp.float32
BF16 = jnp.bfloat16
HIGHEST = lax.Precision.HIGHEST

D_MODEL = 1024
DN_HEADS, DN_DK, DN_DV, CONV_W = 4, 128, 128, 4
DN_QKV = 2 * DN_HEADS * DN_DK + DN_HEADS * DN_DV
GLA_HEADS, GLA_DK, GLA_DV, GLA_RANK, GLA_TAU = 4, 64, 128, 16, 16.0
SWA_HEADS, SWA_KV, SWA_HD, WINDOW = 8, 2, 64, 128
N_BRANCH, BRANCH_W = 3, 512
N_GROUPS, EXP_PER_GROUP, TOP_K, D_EXPERT = 4, 8, 2, 256
N_EXPERTS = N_GROUPS * EXP_PER_GROUP
EPS = 1e-6

LANES = 128
CHUNK = 128
SUB = 16
VMEM_LIMIT = 56 * 1024 * 1024

C_GATE, C_AQKV, C_AZ, C_BV, C_BR, C_CQ = 0, 3072, 4608, 5120, 5632, 6144
C_BQ, C_BK, C_CK, C_CV, C_SM, P_PAD = 6656, 6912, 7168, 7296, 7424, 7680
SM_BETA, SM_G, SM_LR = 0, DN_HEADS, 2 * DN_HEADS
R_EXP = N_GROUPS


def _cparams(*sem):
    return pltpu.CompilerParams(dimension_semantics=sem, vmem_limit_bytes=VMEM_LIMIT)


def _dot(a, b, precision=None):
    return jnp.dot(a, b, preferred_element_type=F32, precision=precision)


def _dot_nt(a, b, precision=None):
    return lax.dot_general(a, b, (((1,), (1,)), ((), ())), preferred_element_type=F32, precision=precision)


def _dot_tn(a, b, precision=None):
    return lax.dot_general(a, b, (((0,), (0,)), ((), ())), preferred_element_type=F32, precision=precision)


def _sigmoid(x):
    return 0.5 * jnp.tanh(0.5 * x) + 0.5


def _silu(x):
    h = 0.5 * x
    return h * jnp.tanh(h) + h


def _iota(shape, dim):
    return lax.broadcasted_iota(jnp.int32, shape, dim)


def _dot_01(tri, x):
    hi = x.astype(BF16)
    r = x - hi.astype(F32)
    mid = r.astype(BF16)
    low = (r - mid.astype(F32)).astype(BF16)
    n = x.shape[-1]
    y = _dot(tri, jnp.concatenate([hi, mid, low], -1))
    return y[:, :n] + y[:, n:2 * n] + y[:, 2 * n:]


def _ada_kernel(c_ref, w_ref, b_ref, o_ref):
    c = _silu(c_ref[...]).astype(BF16)
    o_ref[0] = _dot(c, w_ref[0].astype(BF16)) + b_ref[0]


def _ada(c_all, ada_w, ada_b):
    depth, d, n = ada_w.shape
    rows = c_all.shape[0]
    tn = 1536
    return pl.pallas_call(
        _ada_kernel,
        grid=(depth, n // tn),
        in_specs=[pl.BlockSpec((rows, d), lambda l, j: (0, 0)),
                  pl.BlockSpec((1, d, tn), lambda l, j: (l, 0, j)),
                  pl.BlockSpec((1, 1, tn), lambda l, j: (l, 0, j))],
        out_specs=pl.BlockSpec((1, rows, tn), lambda l, j: (l, 0, j)),
        out_shape=jax.ShapeDtypeStruct((depth, rows, n), F32),
        compiler_params=_cparams("parallel", "parallel"),
    )(c_all, ada_w, ada_b.reshape(depth, 1, n))


def _mod_spec(mod, tm, tokens_per_row):
    _, r, d = mod.shape
    assert tokens_per_row % tm == 0
    per = tokens_per_row // tm
    return pl.BlockSpec((1, r, d), lambda i, *_: (i // per, 0, 0))


def _in_proj_kernel(x_ref, sc_ref, sh_ref, g_ref, w_ref, o_ref, h_ref):
    @pl.when(pl.program_id(1) == 0)
    def _():
        x = x_ref[...]
        y = x * lax.rsqrt(jnp.mean(x * x, -1, keepdims=True) + EPS) * g_ref[...]
        h_ref[...] = (y * (1.0 + sc_ref[0]) + sh_ref[0]).astype(BF16)

    o_ref[...] = _dot(h_ref[...], w_ref[...])


def _in_proj(layer, x, sc, sh, g, w, tm, tokens_per_row):
    t, d = x.shape
    n = w.shape[2]
    tn = 768 if tm >= 2048 else 1536
    return pl.pallas_call(
        _in_proj_kernel,
        grid=(t // tm, n // tn),
        in_specs=[pl.BlockSpec((tm, d), lambda i, j: (i, 0)),
                  _mod_spec(sc, tm, tokens_per_row), _mod_spec(sh, tm, tokens_per_row),
                  pl.BlockSpec((1, d), lambda i, j: (0, 0)),
                  pl.BlockSpec((None, d, tn), lambda i, j: (layer, 0, j))],
        out_specs=pl.BlockSpec((tm, tn), lambda i, j: (i, j)),
        out_shape=jax.ShapeDtypeStruct((t, n), F32),
        scratch_shapes=[pltpu.VMEM((tm, d), BF16)],
        compiler_params=_cparams("parallel", "arbitrary"),
    )(x, sc, sh, g, w)


def _strict_lower_inverse_minus_eye_multi(a_list):
    c = a_list[0].shape[0]
    diag_blk = (_iota((c, c), 0) // SUB) == (_iota((c, c), 1) // SUB)
    idx = range(len(a_list))
    ad = [jnp.where(diag_blk, a, 0.0) for a in a_list]
    ao = [a_list[i] - ad[i] for i in idx]
    n = [-x for x in ad]
    p = n
    for _ in range(SUB.bit_length() - 2):
        p = [_dot(x, x) for x in p]
        np_ = [_dot(n[i], p[i]) for i in idx]
        n = [n[i] + p[i] + np_[i] for i in idx]
    nao = [_dot(n[i], ao[i]) for i in idx]
    m = [-(ao[i] + nao[i]) for i in idx]
    q = m
    for _ in range((c // SUB).bit_length() - 2):
        q = [_dot(x, x) for x in q]
        mq = [_dot(m[i], q[i]) for i in idx]
        m = [m[i] + q[i] + mq[i] for i in idx]
    mn = [_dot(m[i], n[i]) for i in idx]
    return [m[i] + n[i] + mn[i] for i in idx]


EXP_CAP = 80.0


def _half_rms(x, g2):
    lane = _iota(x.shape, 1)
    first = lane < SWA_HD
    sq = x * x
    s0 = jnp.sum(jnp.where(first, sq, 0.0), -1, keepdims=True)
    s1 = jnp.sum(jnp.where(first, 0.0, sq), -1, keepdims=True)
    ms = jnp.where(first, s0, s1) * (1.0 / SWA_HD)
    return x * lax.rsqrt(ms + EPS) * g2


def _dn_mb_kernel(qkv_ref, z_ref, sm_ref, cw_ref, alog_ref, dtb_ref, on_ref,
                  o_ref, s_out_ref, s_ref, xp_ref, y_ref):
    c_idx = pl.program_id(0)
    nb = qkv_ref.shape[0]
    c = CHUNK
    pad = 8

    @pl.when(c_idx == 0)
    def _():
        s_ref[...] = jnp.zeros_like(s_ref)
        xp_ref[:, 0:pad, :] = jnp.zeros((nb, pad, DN_QKV), F32)

    row, col = _iota((c, c), 0), _iota((c, c), 1)
    incl = row >= col
    strict = row > col
    incl_b = jnp.where(incl, 1.0, 0.0).astype(BF16)
    hk = DN_HEADS * DN_DK
    gam_all, gam_t, beta_all = [], [], []
    for b in range(nb):
        xp_ref[b, pad:pad + c, :] = qkv_ref[b]
        acc = xp_ref[b, pad - 3:pad - 3 + c, :] * cw_ref[0:1, :]
        for j in range(1, CONV_W):
            acc = acc + xp_ref[b, pad - 3 + j:pad - 3 + j + c, :] * cw_ref[j:j + 1, :]
        y_ref[b] = _silu(acc)
        xp_ref[b, pad - 3:pad, :] = xp_ref[b, pad + c - 3:pad + c, :]
        sm = sm_ref[b]
        beta_all.append(_sigmoid(sm))
        g_all = -jnp.exp(alog_ref[...]) * jax.nn.softplus(sm + dtb_ref[...])
        gam_all.append(_dot_01(incl_b, g_all))
        gam_t.append(gam_all[b].T)

    chains = [(b, h) for b in range(nb) for h in range(DN_HEADS)]
    n = len(chains)
    q, k, dec, eg, gl, gam, kb, rhs, a = ([None] * n for _ in range(9))
    for i, (b, h) in enumerate(chains):
        qi = y_ref[b, :, h * DN_DK:(h + 1) * DN_DK]
        ki = y_ref[b, :, hk + h * DN_DK:hk + (h + 1) * DN_DK]
        vi = y_ref[b, :, 2 * hk + h * DN_DV:2 * hk + (h + 1) * DN_DV]
        q[i] = qi * lax.rsqrt(jnp.sum(qi * qi, -1, keepdims=True) + EPS) * DN_DK ** -0.5
        k[i] = ki * lax.rsqrt(jnp.sum(ki * ki, -1, keepdims=True) + EPS)
        beta = beta_all[b][:, SM_BETA + h:SM_BETA + h + 1]
        gam[i] = gam_all[b][:, SM_G + h:SM_G + h + 1]
        gam_row = gam_t[b][SM_G + h:SM_G + h + 1, :]
        dec[i] = jnp.where(incl, jnp.exp(jnp.minimum(gam[i] - gam_row, 0.0)), 0.0)
        eg[i] = jnp.exp(gam[i])
        gl[i] = gam[i][c - 1:c, :]
        kb[i] = k[i] * beta
        rhs[i] = jnp.concatenate([vi * beta, kb[i] * eg[i]], -1)
    kbf = [x.astype(BF16) for x in k]
    kk = [_dot_nt(kb[i].astype(BF16), kbf[i]) for i in range(n)]
    qk = [_dot_nt(q[i].astype(BF16), kbf[i]) for i in range(n)]
    a = [jnp.where(strict, kk[i] * dec[i], 0.0) for i in range(n)]
    w = _strict_lower_inverse_minus_eye_multi(a)
    sol = [rhs[i] + _dot(w[i], rhs[i]) for i in range(n)]
    s = [s_ref[b, h] for (b, h) in chains]
    sbf = [x.astype(BF16) for x in s]
    u = [sol[i][:, :DN_DV] - _dot(sol[i][:, DN_DV:].astype(BF16), sbf[i]) for i in range(n)]
    ubf = [x.astype(BF16) for x in u]
    o_s = [_dot((q[i] * eg[i]).astype(BF16), sbf[i]) for i in range(n)]
    o_u = [_dot((qk[i] * dec[i]).astype(BF16), ubf[i]) for i in range(n)]
    ds = [_dot_tn((k[i] * jnp.exp(gl[i] - gam[i])).astype(BF16), ubf[i]) for i in range(n)]
    for i, (b, h) in enumerate(chains):
        s_ref[b, h] = s[i] * jnp.exp(gl[i]) + ds[i]
        o = o_s[i] + o_u[i]
        o = o * lax.rsqrt(jnp.mean(o * o, -1, keepdims=True) + EPS) * on_ref[...]
        o_ref[b, :, h * DN_DV:(h + 1) * DN_DV] = (
            o * _silu(z_ref[b, :, h * DN_DV:(h + 1) * DN_DV])).astype(BF16)

    @pl.when(c_idx == pl.num_programs(0) - 1)
    def _():
        s_out_ref[...] = s_ref[...]


def _dn_prompt_mb(p3, conv_w, alog_row, dtb_row, onorm):
    batch, seq, _ = p3.shape
    c = CHUNK
    full = lambda shape: pl.BlockSpec(shape, lambda i: tuple(0 for _ in shape))
    return pl.pallas_call(
        _dn_mb_kernel,
        grid=(seq // c,),
        in_specs=[pl.BlockSpec((batch, c, DN_QKV), lambda i: (0, i, C_AQKV // DN_QKV)),
                  pl.BlockSpec((batch, c, BRANCH_W), lambda i: (0, i, C_AZ // BRANCH_W)),
                  pl.BlockSpec((batch, c, LANES), lambda i: (0, i, C_SM // LANES)),
                  full((CONV_W, DN_QKV)), full((1, LANES)), full((1, LANES)), full((1, DN_DV))],
        out_specs=[pl.BlockSpec((batch, c, BRANCH_W), lambda i: (0, i, 0)),
                   full((batch, DN_HEADS, DN_DK, DN_DV))],
        out_shape=[jax.ShapeDtypeStruct((batch, seq, BRANCH_W), BF16),
                   jax.ShapeDtypeStruct((batch, DN_HEADS, DN_DK, DN_DV), F32)],
        scratch_shapes=[pltpu.VMEM((batch, DN_HEADS, DN_DK, DN_DV), F32),
                        pltpu.VMEM((batch, c + 8, DN_QKV), F32),
                        pltpu.VMEM((batch, c, DN_QKV), F32)],
        compiler_params=_cparams("arbitrary"),
    )(p3, p3, p3, conv_w, alog_row, dtb_row, onorm)


def _gla_mb_kernel(q_ref, k_ref, v_ref, r_ref, sm_ref, w2_ref, b_ref, on_ref,
                   o_ref, s_out_ref, st_ref):
    c_idx = pl.program_id(0)
    nb = q_ref.shape[0]
    c = CHUNK
    hw = GLA_HEADS * GLA_DK
    hv = GLA_HEADS * GLA_DV

    @pl.when(c_idx == 0)
    def _():
        st_ref[...] = jnp.zeros_like(st_ref)

    incl_b = jnp.where(_iota((c, c), 0) >= _iota((c, c), 1), 1.0, 0.0).astype(BF16)
    blk = (_iota((hv, hw), 0) // GLA_DV) == (_iota((hv, hw), 1) // GLA_DK)
    qsel = (_iota((GLA_HEADS * SUB, hw), 0) // SUB) == (_iota((GLA_HEADS * SUB, hw), 1) // GLA_DK)
    nbr = range(nb)
    lg = [jax.nn.log_sigmoid(_dot(sm_ref[b].astype(BF16), w2_ref[...]) + b_ref[...]) / GLA_TAU for b in nbr]
    gam = [_dot_01(incl_b, lg[b]) for b in nbr]
    gl = [gam[b][c - 1:c, :] for b in nbr]
    qs = [q_ref[b] * GLA_DK ** -0.5 for b in nbr]
    k = [k_ref[b] for b in nbr]
    qg = [(qs[b] * jnp.exp(gam[b])).astype(BF16) for b in nbr]
    kd = [(k[b] * jnp.exp(gl[b] - gam[b])).astype(BF16) for b in nbr]
    vbf = [v_ref[b].astype(BF16) for b in nbr]
    st = [st_ref[b] for b in nbr]
    o_inter = [_dot_nt(qg[b], st[b].astype(BF16)) for b in nbr]
    dst = [_dot_tn(vbf[b], kd[b]) for b in nbr]
    for b in nbr:
        st_ref[b] = st[b] * jnp.exp(gl[b]) + jnp.where(blk, dst[b], 0.0)

    res = [[] for _ in nbr]
    for i in range(c // SUB):
        lo, hi = i * SUB, (i + 1) * SUB
        keep = (_iota((GLA_HEADS * SUB, hi), 0) % SUB + lo) >= _iota((GLA_HEADS * SUB, hi), 1)
        qm, ki = [], []
        for b in nbr:
            ref_pt = gam[b][lo - 1:lo, :] if i > 0 else jnp.zeros((1, hw), F32)
            qi = qs[b][lo:hi] * jnp.exp(gam[b][lo:hi] - ref_pt)
            qm.append(jnp.where(qsel, jnp.concatenate([qi] * GLA_HEADS, 0), 0.0).astype(BF16))
            ki.append((k[b][:hi] * jnp.exp(jnp.minimum(ref_pt - gam[b][:hi], EXP_CAP))).astype(BF16))
        att = [_dot_nt(qm[b], ki[b]) for b in nbr]
        att = [jnp.where(keep, att[b], 0.0).astype(BF16) for b in nbr]
        for b in nbr:
            res[b].append(_dot(att[b], vbf[b][:hi]))
    for b in nbr:
        for h in range(GLA_HEADS):
            vs = slice(h * GLA_DV, (h + 1) * GLA_DV)
            o = o_inter[b][:, vs] + jnp.concatenate([r[h * SUB:(h + 1) * SUB, vs] for r in res[b]], 0)
            o = o * lax.rsqrt(jnp.mean(o * o, -1, keepdims=True) + EPS) * on_ref[...]
            o_ref[b, :, vs] = (o * _silu(r_ref[b, :, vs])).astype(BF16)

    @pl.when(c_idx == pl.num_programs(0) - 1)
    def _():
        for b in range(nb):
            for h in range(GLA_HEADS):
                s_out_ref[b, h] = st_ref[b, h * GLA_DV:(h + 1) * GLA_DV, h * GLA_DK:(h + 1) * GLA_DK]


def _gla_prompt_mb(p3, w2pad, gla_b, onorm):
    batch, seq, _ = p3.shape
    c = CHUNK
    hw = GLA_HEADS * GLA_DK
    hv = GLA_HEADS * GLA_DV
    full = lambda shape: pl.BlockSpec(shape, lambda i: tuple(0 for _ in shape))
    return pl.pallas_call(
        _gla_mb_kernel,
        grid=(seq // c,),
        in_specs=[pl.BlockSpec((batch, c, hw), lambda i: (0, i, C_BQ // hw)),
                  pl.BlockSpec((batch, c, hw), lambda i: (0, i, C_BK // hw)),
                  pl.BlockSpec((batch, c, hv), lambda i: (0, i, C_BV // hv)),
                  pl.BlockSpec((batch, c, hv), lambda i: (0, i, C_BR // hv)),
                  pl.BlockSpec((batch, c, LANES), lambda i: (0, i, C_SM // LANES)),
                  full((LANES, hw)), full((1, hw)), full((1, GLA_DV))],
        out_specs=[pl.BlockSpec((batch, c, hv), lambda i: (0, i, 0)),
                   full((batch, GLA_HEADS, GLA_DV, GLA_DK))],
        out_shape=[jax.ShapeDtypeStruct((batch, seq, hv), BF16),
                   jax.ShapeDtypeStruct((batch, GLA_HEADS, GLA_DV, GLA_DK), F32)],
        scratch_shapes=[pltpu.VMEM((batch, hv, hw), F32)],
        compiler_params=_cparams("arbitrary"),
    )(p3, p3, p3, p3, p3, w2pad, gla_b, onorm)


def _swa_mb_kernel(q_ref, kp_ref, kc_ref, vp_ref, vc_ref, qn_ref, kn_ref, snk_ref,
                   o_ref, knew_ref):
    i = pl.program_id(0)
    nb = q_ref.shape[0]
    w = WINDOW
    g = SWA_HEADS // SWA_KV
    nr = g * w
    first = _iota((w, 2 * SWA_HD), 1) < SWA_HD
    first2 = _iota((2 * w, 2 * SWA_HD), 1) < SWA_HD
    t = _iota((nr, 2 * w), 0) % w
    j = _iota((nr, 2 * w), 1)
    dist = w + t - j
    valid = (dist >= 0) & (dist <= w) & ((j >= w) | (i > 0))
    distf = dist.astype(F32)
    hrow = _iota((nr, 1), 0) // w
    slope, snk = [], []
    for kv in range(SWA_KV):
        sl = jnp.zeros((nr, 1), F32)
        sk = jnp.zeros((nr, 1), F32)
        for hh in range(g):
            h = kv * g + hh
            sl = jnp.where(hrow == hh, 2.0 ** (-8.0 * (h + 1) / SWA_HEADS), sl)
            sk = jnp.where(hrow == hh, snk_ref[:, h:h + 1], sk)
        slope.append(sl * distf)
        snk.append(sk)

    units = [(b, kv) for b in range(nb) for kv in range(SWA_KV)]
    kcs, k2, v2, qx = [], [], [], []
    for b in range(nb):
        kc = _half_rms(kc_ref[b], kn_ref[...])
        kcs.append(kc)
        kk = jnp.concatenate([_half_rms(kp_ref[b], kn_ref[...]), kc], 0)
        vv = jnp.concatenate([vp_ref[b], vc_ref[b]], 0)
        kk_sw = pltpu.roll(kk, SWA_HD, axis=1)
        vv_sw = pltpu.roll(vv, SWA_HD, axis=1)
        k2 += [jnp.where(first2, kk, kk_sw).astype(BF16), jnp.where(first2, kk_sw, kk).astype(BF16)]
        v2 += [jnp.where(first2, vv, vv_sw).astype(BF16), jnp.where(first2, vv_sw, vv).astype(BF16)]
        for kv in range(SWA_KV):
            rows = []
            for jj in range(g // 2):
                grp = kv * (g // 2) + jj
                qg = _half_rms(q_ref[b, :, grp * 2 * SWA_HD:(grp + 1) * 2 * SWA_HD], qn_ref[...]) * SWA_HD ** -0.5
                rows += [jnp.where(first, qg, 0.0), jnp.where(first, 0.0, qg)]
            qx.append(jnp.concatenate(rows, 0).astype(BF16))
    nu = range(len(units))
    s = [_dot_nt(qx[u], k2[u]) for u in nu]
    s = [jnp.where(valid, s[u] - slope[units[u][1]], -jnp.inf) for u in nu]
    m = [jnp.maximum(jnp.max(s[u], -1, keepdims=True), snk[units[u][1]]) for u in nu]
    pr = [jnp.exp(s[u] - m[u]) for u in nu]
    den = [jnp.sum(pr[u], -1, keepdims=True) + jnp.exp(snk[units[u][1]] - m[u]) for u in nu]
    pr = [(pr[u] * (1.0 / den[u])).astype(BF16) for u in nu]
    o = [_dot(pr[u], v2[u]) for u in nu]
    for u, (b, kv) in enumerate(units):
        for jj in range(g // 2):
            grp = kv * (g // 2) + jj
            o_ref[b, :, grp * 2 * SWA_HD:(grp + 1) * 2 * SWA_HD] = jnp.where(
                first, o[u][(2 * jj) * w:(2 * jj + 1) * w], o[u][(2 * jj + 1) * w:(2 * jj + 2) * w]).astype(BF16)

    @pl.when(i == pl.num_programs(0) - 1)
    def _():
        for b in range(nb):
            knew_ref[b] = kcs[b]


def _swa_prompt_mb(p3, qn2, kn2, sinks):
    batch, seq, _ = p3.shape
    w = WINDOW
    kvw = SWA_KV * SWA_HD
    full = lambda shape: pl.BlockSpec(shape, lambda i: tuple(0 for _ in shape))
    prev = lambda i: jnp.maximum(i - 1, 0)
    return pl.pallas_call(
        _swa_mb_kernel,
        grid=(seq // w,),
        in_specs=[pl.BlockSpec((batch, w, BRANCH_W), lambda i: (0, i, C_CQ // BRANCH_W)),
                  pl.BlockSpec((batch, w, kvw), lambda i: (0, prev(i), C_CK // kvw)),
                  pl.BlockSpec((batch, w, kvw), lambda i: (0, i, C_CK // kvw)),
                  pl.BlockSpec((batch, w, kvw), lambda i: (0, prev(i), C_CV // kvw)),
                  pl.BlockSpec((batch, w, kvw), lambda i: (0, i, C_CV // kvw)),
                  full((1, kvw)), full((1, kvw)), full((1, SWA_HEADS))],
        out_specs=[pl.BlockSpec((batch, w, BRANCH_W), lambda i: (0, i, 0)),
                   full((batch, w, kvw))],
        out_shape=[jax.ShapeDtypeStruct((batch, seq, BRANCH_W), BF16),
                   jax.ShapeDtypeStruct((batch, w, kvw), F32)],
        compiler_params=_cparams("arbitrary"),
    )(p3, p3, p3, p3, p3, qn2, kn2, sinks)


DEC_TILE = 8


def _dec_kernel(qkv_ref, z_ref, sm_ref, bq_ref, bk_ref, bv_ref, br_ref, cq_ref, ck_ref, cv_ref,
                sdn_ref, buf_ref, sgl_ref, kc_ref, vc_ref,
                cw_ref, alog_ref, dtb_ref, dnon_ref, w2_ref, glb_ref, glon_ref,
                qn_ref, kn_ref, snk_ref, slp_ref,
                oa_ref, ob_ref, oc_ref, sdn_out, buf_out, sgl_out, kc_out, vc_out):
    bt = DEC_TILE
    x = qkv_ref[...]
    buf = buf_ref[...]
    acc = x * cw_ref[CONV_W - 1:CONV_W, :]
    for j in range(CONV_W - 1):
        acc = acc + buf[:, j * DN_QKV:(j + 1) * DN_QKV] * cw_ref[j:j + 1, :]
    y = _silu(acc)
    buf_out[:, 0:(CONV_W - 2) * DN_QKV] = buf[:, DN_QKV:]
    buf_out[:, (CONV_W - 2) * DN_QKV:] = x

    sm = sm_ref[...]
    beta_all = _sigmoid(sm)
    g_all = -jnp.exp(alog_ref[...]) * jax.nn.softplus(sm + dtb_ref[...])
    eye = (_iota((LANES, LANES), 0) == _iota((LANES, LANES), 1)).astype(F32)
    hk = DN_HEADS * DN_DK
    for h in range(DN_HEADS):
        q = y[:, h * DN_DK:(h + 1) * DN_DK]
        k = y[:, hk + h * DN_DK:hk + (h + 1) * DN_DK]
        v = y[:, 2 * hk + h * DN_DV:2 * hk + (h + 1) * DN_DV]
        q = q * lax.rsqrt(jnp.sum(q * q, -1, keepdims=True) + EPS) * DN_DK ** -0.5
        k = k * lax.rsqrt(jnp.sum(k * k, -1, keepdims=True) + EPS)
        beta = beta_all[:, SM_BETA + h:SM_BETA + h + 1]
        eg = jnp.exp(g_all[:, SM_G + h:SM_G + h + 1])
        kb = k * beta
        lhs = jnp.concatenate([kb * eg, q * eg], 0).astype(BF16)
        k_t = _dot_nt(eye, k)
        qk = jnp.sum(q * k, -1, keepdims=True)
        vb = v * beta
        o_rows = []
        for b in range(bt):
            s = sdn_ref[b, h]
            r = _dot(lhs, s.astype(BF16))
            u = vb[b:b + 1] - r[b:b + 1]
            o_rows.append(r[bt + b:bt + b + 1] + qk[b:b + 1] * u)
            sdn_out[b, h] = s * eg[b:b + 1] + k_t[:, b:b + 1] * u
        o = jnp.concatenate(o_rows, 0)
        o = o * lax.rsqrt(jnp.mean(o * o, -1, keepdims=True) + EPS) * dnon_ref[...]
        oa_ref[:, h * DN_DV:(h + 1) * DN_DV] = (o * _silu(z_ref[:, h * DN_DV:(h + 1) * DN_DV])).astype(BF16)

    lg = jax.nn.log_sigmoid(_dot(sm.astype(BF16), w2_ref[...]) + glb_ref[...]) / GLA_TAU
    elg = jnp.exp(lg)
    bq = bq_ref[...] * GLA_DK ** -0.5
    bk = bk_ref[...]
    qg = bq * elg
    lane = _iota((bt, LANES), 1)
    first = lane < GLA_DK
    rows_first = _iota((LANES, GLA_DV), 0) < GLA_DK
    for j in range(GLA_HEADS // 2):
        sl = slice(j * LANES, (j + 1) * LANES)
        qgj = qg[:, sl]
        lhs = jnp.concatenate([jnp.where(first, qgj, 0.0), jnp.where(first, 0.0, qgj)], 0).astype(BF16)
        cols = _dot_nt(eye, jnp.concatenate([elg[:, sl], bk[:, sl]], 0), HIGHEST)
        prod = bq[:, sl] * bk[:, sl]
        qk0 = jnp.sum(jnp.where(first, prod, 0.0), -1, keepdims=True)
        qk1 = jnp.sum(jnp.where(first, 0.0, prod), -1, keepdims=True)
        v0 = bv_ref[:, (2 * j) * GLA_DV:(2 * j + 1) * GLA_DV]
        v1 = bv_ref[:, (2 * j + 1) * GLA_DV:(2 * j + 2) * GLA_DV]
        o0, o1 = [], []
        for b in range(bt):
            s = sgl_ref[b, j]
            r = _dot(lhs, s.astype(BF16))
            o0.append(r[b:b + 1] + qk0[b:b + 1] * v0[b:b + 1])
            o1.append(r[bt + b:bt + b + 1] + qk1[b:b + 1] * v1[b:b + 1])
            vsel = jnp.where(rows_first, v0[b:b + 1], v1[b:b + 1])
            sgl_out[b, j] = s * cols[:, b:b + 1] + cols[:, bt + b:bt + b + 1] * vsel
        for hh, rows in ((2 * j, o0), (2 * j + 1, o1)):
            o = jnp.concatenate(rows, 0)
            o = o * lax.rsqrt(jnp.mean(o * o, -1, keepdims=True) + EPS) * glon_ref[...]
            ob_ref[:, hh * GLA_DV:(hh + 1) * GLA_DV] = (o * _silu(br_ref[:, hh * GLA_DV:(hh + 1) * GLA_DV])).astype(BF16)

    g = SWA_HEADS // SWA_KV
    nr = bt * SWA_HEADS
    cq = cq_ref[...]
    cq = cq * lax.rsqrt(jnp.mean(cq * cq, -1, keepdims=True) + EPS) * qn_ref[...] * SWA_HD ** -0.5
    head = _iota((nr, 2 * SWA_HD), 0) % SWA_HEADS
    in_half = (head // g) == (_iota((nr, 2 * SWA_HD), 1) // SWA_HD)
    qx = jnp.where(in_half, jnp.concatenate([cq, cq], -1), 0.0)
    knew = _half_rms(ck_ref[...], kn_ref[...])
    vnew = cv_ref[...]
    s_c, kn_rows, vn_rows = [], [], []
    for b in range(bt):
        s_c.append(_dot_nt(qx[b * SWA_HEADS:(b + 1) * SWA_HEADS].astype(BF16), kc_ref[b].astype(BF16)))
        kn_rows.append(jnp.broadcast_to(knew[b:b + 1], (SWA_HEADS, 2 * SWA_HD)))
        vn_rows.append(jnp.broadcast_to(vnew[b:b + 1], (SWA_HEADS, 2 * SWA_HD)))
    s_c = jnp.concatenate(s_c, 0)
    kn_x = jnp.concatenate(kn_rows, 0)
    vn_x = jnp.concatenate(vn_rows, 0)
    slopes = slp_ref[...]
    snk = snk_ref[...]
    dist = (WINDOW - _iota((nr, WINDOW), 1)).astype(F32)
    s_c = s_c - slopes * dist
    s_n = jnp.sum(qx * kn_x, -1, keepdims=True)
    m = jnp.maximum(jnp.maximum(jnp.max(s_c, -1, keepdims=True), s_n), snk)
    p_c = jnp.exp(s_c - m)
    p_n = jnp.exp(s_n - m)
    den = jnp.sum(p_c, -1, keepdims=True) + p_n + jnp.exp(snk - m)
    p_c = p_c / den
    p_n = p_n / den
    half_sel = (_iota((nr, SWA_HD), 0) % SWA_HEADS) < g
    for b in range(bt):
        rs = slice(b * SWA_HEADS, (b + 1) * SWA_HEADS)
        r = _dot(p_c[rs].astype(BF16), vc_ref[b].astype(BF16)) + p_n[rs] * vn_x[rs]
        oc_ref[rs, :] = jnp.where(half_sel[rs], r[:, :SWA_HD], r[:, SWA_HD:]).astype(BF16)
        kc_out[b, 0:WINDOW - 1, :] = kc_ref[b, 1:WINDOW, :]
        kc_out[b, WINDOW - 1:WINDOW, :] = knew[b:b + 1]
        vc_out[b, 0:WINDOW - 1, :] = vc_ref[b, 1:WINDOW, :]
        vc_out[b, WINDOW - 1:WINDOW, :] = vnew[b:b + 1]


def _dec_kernel_aliased(*refs):
    n_in, n_alias = 26, 5
    _dec_kernel(*refs[:n_in], *refs[n_in + n_alias:])


def _dec_mixers(layer, p, cq_r, states, prev_out, conv_w, alog_row, dtb_row, dn_on, w2pad, gla_b, gla_on,
                qn, kn2, snk_col, slp_col):
    n = p.shape[0]
    bt = DEC_TILE
    hw = GLA_HEADS * GLA_DK
    kvw = SWA_KV * SWA_HD
    nr = bt * SWA_HEADS

    def col(width, off):
        return pl.BlockSpec((bt, width), lambda i: (i, off // width))

    def full(shape):
        return pl.BlockSpec(shape, lambda i: tuple(0 for _ in shape))

    def slot(a):
        rest = a.shape[2:]
        return pl.BlockSpec((None, bt) + rest, lambda i: (layer, i) + tuple(0 for _ in rest))

    in_specs = [col(DN_QKV, C_AQKV), col(BRANCH_W, C_AZ), col(LANES, C_SM), col(hw, C_BQ), col(hw, C_BK),
                col(BRANCH_W, C_BV), col(BRANCH_W, C_BR),
                pl.BlockSpec((nr, SWA_HD), lambda i: (i, 0)),
                col(kvw, C_CK), col(kvw, C_CV)]
    in_specs += [slot(a) for a in states]
    in_specs += [full((CONV_W, DN_QKV)), full((1, LANES)), full((1, LANES)), full((1, DN_DV)),
                 full((LANES, hw)), full((1, hw)), full((1, GLA_DV)),
                 full((1, SWA_HD)), full((1, kvw)), full((nr, 1)), full((nr, 1))]
    args = [p, p, p, p, p, p, p, cq_r, p, p, *states,
            conv_w, alog_row, dtb_row, dn_on, w2pad, gla_b, gla_on, qn, kn2, snk_col, slp_col]
    body, aliases = _dec_kernel, {}
    if prev_out is not None:
        body = _dec_kernel_aliased
        aliases = {len(args) + k: 3 + k for k in range(len(prev_out))}
        in_specs += [pl.BlockSpec(memory_space=pl.ANY)] * len(prev_out)
        args += list(prev_out)
    return pl.pallas_call(
        body,
        grid=(n // bt,),
        in_specs=in_specs,
        out_specs=[pl.BlockSpec((bt, BRANCH_W), lambda i: (i, 0)),
                   pl.BlockSpec((bt, BRANCH_W), lambda i: (i, 0)),
                   pl.BlockSpec((nr, SWA_HD), lambda i: (i, 0))] + [slot(a) for a in states],
        out_shape=[jax.ShapeDtypeStruct((n, BRANCH_W), BF16),
                   jax.ShapeDtypeStruct((n, BRANCH_W), BF16),
                   jax.ShapeDtypeStruct((n * SWA_HEADS, SWA_HD), BF16)]
        + [jax.ShapeDtypeStruct(a.shape, F32) for a in states],
        input_output_aliases=aliases,
        compiler_params=_cparams("parallel"),
    )(*args)


def _merge_kernel(ba_ref, bb_ref, bc_ref, gate_ref, x_ref, gt_ref, sc_ref, sh_ref, ln_ref,
                  wb_ref, wo_ref, wr_ref, rb_ref, x1_ref, h2_ref, comb_ref, cnt_ref):
    mix = None
    for n, br in enumerate((ba_ref, bb_ref, bc_ref)):
        up = _dot(br[...], wb_ref[n])
        term = _sigmoid(gate_ref[:, n * D_MODEL:(n + 1) * D_MODEL]) * up
        mix = term if mix is None else mix + term
    x1 = x_ref[...] + gt_ref[0] * _dot(mix.astype(BF16), wo_ref[...])
    x1_ref[...] = x1
    h2 = x1 * lax.rsqrt(jnp.mean(x1 * x1, -1, keepdims=True) + EPS) * ln_ref[...]
    h2 = h2 * (1.0 + sc_ref[0]) + sh_ref[0]
    h2b = h2.astype(BF16)
    h2_ref[...] = h2b

    h2l = (h2 - h2b.astype(F32)).astype(BF16)
    hi_lo = _dot(h2b, wr_ref[...])
    logits = hi_lo[:, :LANES] + hi_lo[:, LANES:] + _dot(h2l, wr_ref[:, :LANES]) + rb_ref[...]
    lane = _iota(logits.shape, 1).astype(F32)
    big = float(LANES)
    lc = jnp.where(lane < N_GROUPS, logits, -jnp.inf)
    mc = jnp.max(lc, -1, keepdims=True)
    pg = 1.0 / jnp.sum(jnp.exp(lc - mc), -1, keepdims=True)
    grp = jnp.min(jnp.where(lc == mc, lane, big), -1, keepdims=True)
    lo = R_EXP + grp * EXP_PER_GROUP
    emask = (lane >= lo) & (lane < lo + EXP_PER_GROUP)
    le = jnp.where(emask, logits, -jnp.inf)
    pe = jnp.exp(le - jnp.max(le, -1, keepdims=True))
    pe = pe / jnp.sum(pe, -1, keepdims=True)
    v1 = jnp.max(pe, -1, keepdims=True)
    i1 = jnp.min(jnp.where(emask & (pe == v1), lane, big), -1, keepdims=True)
    pe2 = jnp.where(emask & (lane != i1), pe, -1.0)
    v2 = jnp.max(pe2, -1, keepdims=True)
    i2 = jnp.min(jnp.where(pe2 == v2, lane, big), -1, keepdims=True)
    tot = v1 + v2
    comb_ref[...] = jnp.where(lane == 0.0, i1 - R_EXP, jnp.where(lane == 1.0, i2 - R_EXP, jnp.where(
        lane == 2.0, pg * v1 / tot, jnp.where(lane == 3.0, pg * v2 / tot, 0.0))))
    picks = jnp.where((lane == i1 - R_EXP) | (lane == i2 - R_EXP), 1.0, 0.0)
    cnt_ref[...] = jnp.broadcast_to(jnp.sum(picks, 0, keepdims=True), cnt_ref.shape)


def _merge(layer, ba, bb, bc, p, x, gt, sc, sh, ln, wb, wo, wr, rb, tm, tokens_per_row):
    t, d = x.shape
    tok = lambda width: pl.BlockSpec((tm, width), lambda i: (i, 0))
    full = lambda shape: pl.BlockSpec((None,) + shape, lambda i: (layer,) + tuple(0 for _ in shape))
    return pl.pallas_call(
        _merge_kernel,
        grid=(t // tm,),
        in_specs=[tok(BRANCH_W), tok(BRANCH_W), tok(BRANCH_W),
                  pl.BlockSpec((tm, N_BRANCH * d), lambda i: (i, C_GATE)),
                  tok(d),
                  _mod_spec(gt, tm, tokens_per_row), _mod_spec(sc, tm, tokens_per_row),
                  _mod_spec(sh, tm, tokens_per_row),
                  full((1, d)), full((N_BRANCH, BRANCH_W, d)), full((d, d)), full((d, 2 * LANES)),
                  full((1, LANES))],
        out_specs=[tok(d), tok(d), tok(LANES), pl.BlockSpec((8, LANES), lambda i: (i, 0))],
        out_shape=[jax.ShapeDtypeStruct((t, d), F32),
                   jax.ShapeDtypeStruct((t, d), BF16),
                   jax.ShapeDtypeStruct((t, LANES), F32),
                   jax.ShapeDtypeStruct((t // tm * 8, LANES), F32)],
        compiler_params=_cparams("parallel"),
    )(ba, bb, bc, p, x, gt, sc, sh, ln, wb, wo, wr, rb)


def _moe_kernel(h_ref, comb_ref, x1_ref, gt_ref, w1_ref, w3_ref, w2_ref, o_ref, acc_ref):
    e = pl.program_id(1)

    @pl.when(e == 0)
    def _():
        acc_ref[...] = jnp.zeros_like(acc_ref)

    h = h_ref[...]
    he = _silu(_dot(h, w1_ref[0].astype(BF16))) * _dot(h, w3_ref[0].astype(BF16))
    ye = _dot(he.astype(BF16), w2_ref[0].astype(BF16))
    comb = comb_ref[...]
    ef = e.astype(F32)
    ce = (jnp.where(comb[:, 0:1] == ef, comb[:, 2:3], 0.0) + jnp.where(comb[:, 1:2] == ef, comb[:, 3:4], 0.0))
    acc_ref[...] += ce * ye

    @pl.when(e == pl.num_programs(1) - 1)
    def _():
        o_ref[...] = x1_ref[...] + gt_ref[0] * acc_ref[...]


def _moe(layer, h2, comb, x1, gt, w1, w3, w2, tm, tokens_per_row):
    t, d = x1.shape
    _, ne, _, de = w1.shape
    return pl.pallas_call(
        _moe_kernel,
        grid=(t // tm, ne),
        in_specs=[pl.BlockSpec((tm, d), lambda i, e: (i, 0)),
                  pl.BlockSpec((tm, LANES), lambda i, e: (i, 0)),
                  pl.BlockSpec((tm, d), lambda i, e: (i, 0)),
                  _mod_spec(gt, tm, tokens_per_row),
                  pl.BlockSpec((None, 1, d, de), lambda i, e: (layer, e, 0, 0)),
                  pl.BlockSpec((None, 1, d, de), lambda i, e: (layer, e, 0, 0)),
                  pl.BlockSpec((None, 1, de, d), lambda i, e: (layer, e, 0, 0))],
        out_specs=pl.BlockSpec((tm, d), lambda i, e: (i, 0)),
        out_shape=jax.ShapeDtypeStruct((t, d), F32),
        scratch_shapes=[pltpu.VMEM((tm, d), F32)],
        compiler_params=_cparams("parallel", "arbitrary"),
    )(h2, comb, x1, gt, w1, w3, w2)


MOE_TK = 512
MOE_TM = 512
ROW_CHUNK = 16
MOE_LB = 2 * MOE_TK + 512
MOE_CH = MOE_LB // ROW_CHUNK
MOE_XW = D_MODEL + LANES
MOE_LB_SHORT = 2 * MOE_TK + 256
assert 2 * MOE_TK + N_EXPERTS * (ROW_CHUNK - 1) <= MOE_LB - ROW_CHUNK


def _moe_sorted_tiles(t):
    rows = 2 * t + (t // MOE_TK) * N_EXPERTS * (ROW_CHUNK - 1) + N_EXPERTS * (MOE_TM - 1)
    return -(-rows // MOE_TM)


def _plan_kernel(r_ref, cnt_ref, lp_ref, dc_ref, te_ref, base_ref):
    t = pl.program_id(0)
    tk = r_ref.shape[0]
    ntp = te_ref.shape[0]
    r = r_ref[...]
    lane = _iota((tk, LANES), 1).astype(F32)
    sel1, sel2 = lane == r[:, 0:1], lane == r[:, 1:2]
    oh = jnp.where(sel1 | sel2, 1.0, 0.0)
    upper = (_iota((LANES, LANES), 0) < _iota((LANES, LANES), 1)).astype(F32)

    def whole_chunks(cnt):
        return jnp.floor((cnt + (ROW_CHUNK - 1)) * (1.0 / ROW_CHUNK)) * ROW_CHUNK

    def excl_cumsum(v):
        return _dot(jnp.broadcast_to(v, (8, LANES)), upper, HIGHEST)[0:1]

    @pl.when(t == 0)
    def _():
        tot = jnp.sum(whole_chunks(cnt_ref[...]), 0, keepdims=True) * 0.125
        gp = jnp.floor((tot + (MOE_TM - 1)) * (1.0 / MOE_TM)) * MOE_TM
        off = excl_cumsum(gp)
        base_ref[...] = off
        end = off + gp
        lane_t = _iota((ntp, LANES), 1).astype(F32)
        start = _iota((ntp, 1), 0).astype(F32) * MOE_TM
        te = jnp.sum(jnp.where((lane_t < N_EXPERTS) & (end <= start), 1.0, 0.0), -1, keepdims=True)
        mine = lane_t == te
        filled = jnp.sum(jnp.where(mine, tot + off, 0.0), -1, keepdims=True)
        tv = jnp.clip(filled - start, 0.0, float(MOE_TM))
        n_used = jnp.sum(jnp.where(lane_t == N_EXPERTS - 1, end, 0.0), -1, keepdims=True) * (1.0 / MOE_TM)
        te_ref[...] = jnp.where(lane_t == 0.0, jnp.minimum(te, N_EXPERTS - 1.0),
                                jnp.where(lane_t == 1.0, tv, jnp.where(lane_t == 2.0, n_used, 0.0))
                                ).astype(jnp.int32)

    c8 = whole_chunks(jnp.sum(oh, 0, keepdims=True))
    base = base_ref[...]
    lo = excl_cumsum(c8)
    below = (_iota((tk, tk), 0) > _iota((tk, tk), 1)).astype(BF16)
    p = _dot(below, oh.astype(BF16)) + lo
    lp1 = jnp.sum(jnp.where(sel1, p, 0.0), -1, keepdims=True)
    lp2 = jnp.sum(jnp.where(sel2, p, 0.0), -1, keepdims=True)
    lp_ref[...] = jnp.where(lane == 0.0, lp1, jnp.where(lane == 1.0, lp2, jnp.where(lane < 4.0, r, 0.0)))
    lane_c = _iota((MOE_CH, LANES), 1).astype(F32)
    cstart = _iota((MOE_CH, 1), 0).astype(F32) * ROW_CHUNK
    ej = jnp.sum(jnp.where((lane_c < N_EXPERTS) & (lo + c8 <= cstart), 1.0, 0.0), -1, keepdims=True)
    dj = jnp.sum(jnp.where(lane_c == ej, base - lo, 0.0), -1, keepdims=True) + cstart
    nrows = jnp.sum(c8, -1, keepdims=True)
    last = _iota((MOE_CH, 1), 0) == MOE_CH - 1
    dcv = jnp.where(last, nrows * (1.0 / ROW_CHUNK), jnp.where(cstart < nrows, dj, 0.0))
    dc_ref[...] = jnp.broadcast_to(dcv, (MOE_CH, LANES)).astype(jnp.int32)
    base_ref[...] = base + c8


def _moe_plan(route, counts):
    t = route.shape[0]
    nt = t // MOE_TK
    ntp = -(-_moe_sorted_tiles(t) // 8) * 8
    assert counts.shape == (nt * 8, LANES)
    return pl.pallas_call(
        _plan_kernel,
        grid=(nt,),
        in_specs=[pl.BlockSpec((MOE_TK, LANES), lambda i: (i, 0)),
                  pl.BlockSpec((nt * 8, LANES), lambda i: (0, 0))],
        out_specs=[pl.BlockSpec((MOE_TK, LANES), lambda i: (i, 0)),
                   pl.BlockSpec((MOE_CH, LANES), lambda i: (i, 0)),
                   pl.BlockSpec((ntp, LANES), lambda i: (0, 0))],
        out_shape=[jax.ShapeDtypeStruct((t, LANES), F32),
                   jax.ShapeDtypeStruct((nt * MOE_CH, LANES), jnp.int32),
                   jax.ShapeDtypeStruct((ntp, LANES), jnp.int32)],
        scratch_shapes=[pltpu.VMEM((1, LANES), F32)],
        compiler_params=_cparams("arbitrary"),
    )(route, counts)


def _pick_onehot(lp, rows):
    pos = _iota((lp.shape[0], rows), 1).astype(F32)
    return pos == lp[:, 0:1], pos == lp[:, 1:2]


def _for_staged_rows(nch, fn):
    few = nch * ROW_CHUNK <= MOE_LB_SHORT

    @pl.when(few)
    def _():
        fn(MOE_LB_SHORT)

    @pl.when(jnp.logical_not(few))
    def _():
        fn(MOE_LB)


def _chunk_loop(n, fn):
    def body(j, carry):
        fn(j)
        return carry

    lax.fori_loop(0, n, body, 0)


def _dispatch_kernel(dc_ref, h_ref, lp_ref, xs_ref, buf_ref, sem, nprev_ref):
    t, nt = pl.program_id(0), pl.num_programs(0)
    slot = t % 2
    tk = h_ref.shape[0]
    lp = lp_ref[...]
    lane = _iota((tk, LANES), 1)

    def split3(w):
        hi = w.astype(BF16).astype(F32)
        mid = (w - hi).astype(BF16).astype(F32)
        low = w - hi - mid
        return jnp.where(lane == 0, hi, jnp.where(lane == 1, mid, jnp.where(lane == 2, low, 0.0))).astype(BF16)

    w1, w2 = split3(lp[:, 2:3]), split3(lp[:, 3:4])
    nch = dc_ref[0, 0, MOE_CH - 1]

    def stage(rows):
        oh1, oh2 = _pick_onehot(lp, rows)
        b1, b2 = jnp.where(oh1, 1.0, 0.0).astype(BF16), jnp.where(oh2, 1.0, 0.0).astype(BF16)
        buf_ref[slot, 0:rows, 0:D_MODEL] = _dot_tn(b1 + b2, h_ref[...]).astype(BF16)
        buf_ref[slot, 0:rows, D_MODEL:] = (_dot_tn(b1, w1) + _dot_tn(b2, w2)).astype(BF16)

    _for_staged_rows(nch, stage)

    def chunk_copy(j, s):
        src = buf_ref.at[s, pl.ds(pl.multiple_of(j * ROW_CHUNK, ROW_CHUNK), ROW_CHUNK), :]
        dst = xs_ref.at[pl.ds(pl.multiple_of(dc_ref[0, 0, j], ROW_CHUNK), ROW_CHUNK), :]
        return pltpu.make_async_copy(src, dst, sem.at[s])

    _chunk_loop(nch, lambda j: chunk_copy(j, slot).start())

    @pl.when(t > 0)
    def _():
        _chunk_loop(nprev_ref[0], lambda j: chunk_copy(0, 1 - slot).wait())

    nprev_ref[0] = nch

    @pl.when(t == nt - 1)
    def _():
        _chunk_loop(nch, lambda j: chunk_copy(0, slot).wait())


def _moe_dispatch(dc3, h2, lp, n_rows):
    t, d = h2.shape
    nt = t // MOE_TK
    return pl.pallas_call(
        _dispatch_kernel,
        grid=(nt,),
        in_specs=[pl.BlockSpec((1, 1, MOE_CH), lambda i: (i, 0, 0), memory_space=pltpu.SMEM),
                  pl.BlockSpec((MOE_TK, d), lambda i: (i, 0)),
                  pl.BlockSpec((MOE_TK, LANES), lambda i: (i, 0))],
        out_specs=pl.BlockSpec(memory_space=pl.ANY),
        out_shape=jax.ShapeDtypeStruct((n_rows, MOE_XW), BF16),
        scratch_shapes=[pltpu.VMEM((2, MOE_LB, MOE_XW), BF16), pltpu.SemaphoreType.DMA((2,)),
                        pltpu.SMEM((1,), jnp.int32)],
        compiler_params=_cparams("arbitrary"),
    )(dc3, h2, lp)


def _experts_kernel(te_ref, tv_ref, nu_ref, x_ref, w1_ref, w3_ref, w2_ref, o_ref, w1b, w3b, w2b):
    i = pl.program_id(0)

    @pl.when(i < nu_ref[0])
    def _():
        @pl.when((i == 0) | (te_ref[i] != te_ref[jnp.maximum(i - 1, 0)]))
        def _():
            w1b[...] = w1_ref[0].astype(BF16)
            w3b[...] = w3_ref[0].astype(BF16)
            w2b[...] = w2_ref[0].astype(BF16)

        tm = x_ref.shape[0]
        valid = _iota((tm, 1), 0) < tv_ref[i]
        h = jnp.where(valid, x_ref[:, :D_MODEL], jnp.zeros((), BF16))
        wx = x_ref[:, D_MODEL:].astype(F32)
        wv = jnp.where(valid, wx[:, 0:1] + wx[:, 1:2] + wx[:, 2:3], 0.0)
        he = _silu(_dot(h, w1b[...])) * _dot(h, w3b[...])
        o_ref[...] = (wv * _dot(he.astype(BF16), w2b[...])).astype(BF16)


def _moe_experts(layer, te, tv, nu, xs, w1, w3, w2):
    n_rows = xs.shape[0]
    _, _, d, de = w1.shape
    cur = lambda i, te, tv, nu: jnp.minimum(i, nu[0] - 1)
    wspec = lambda a, b: pl.BlockSpec((None, 1, a, b), lambda i, te, tv, nu: (layer, te[cur(i, te, tv, nu)], 0, 0))
    return pl.pallas_call(
        _experts_kernel,
        grid_spec=pltpu.PrefetchScalarGridSpec(
            num_scalar_prefetch=3,
            grid=(n_rows // MOE_TM,),
            in_specs=[pl.BlockSpec((MOE_TM, MOE_XW), lambda i, te, tv, nu: (cur(i, te, tv, nu), 0)),
                      wspec(d, de), wspec(d, de), wspec(de, d)],
            out_specs=pl.BlockSpec((MOE_TM, d), lambda i, te, tv, nu: (cur(i, te, tv, nu), 0)),
            scratch_shapes=[pltpu.VMEM((d, de), BF16), pltpu.VMEM((d, de), BF16), pltpu.VMEM((de, d), BF16)]),
        out_shape=jax.ShapeDtypeStruct((n_rows, d), BF16),
        compiler_params=_cparams("arbitrary"),
    )(te, tv, nu, xs, w1, w3, w2)


def _combine_kernel(dc_ref, dcn_ref, lp_ref, x1_ref, gt_ref, ys_ref, o_ref, buf_ref, sem):
    t, nt = pl.program_id(0), pl.num_programs(0)
    slot = t % 2

    def chunk_copy(tab, j, s):
        src = ys_ref.at[pl.ds(pl.multiple_of(tab[0, 0, j], ROW_CHUNK), ROW_CHUNK), :]
        dst = buf_ref.at[s, pl.ds(pl.multiple_of(j * ROW_CHUNK, ROW_CHUNK), ROW_CHUNK), :]
        return pltpu.make_async_copy(src, dst, sem.at[s])

    nch = dc_ref[0, 0, MOE_CH - 1]

    @pl.when(t == 0)
    def _():
        buf_ref[...] = jnp.zeros_like(buf_ref)
        _chunk_loop(nch, lambda j: chunk_copy(dc_ref, j, slot).start())

    @pl.when(t + 1 < nt)
    def _():
        _chunk_loop(dcn_ref[0, 0, MOE_CH - 1], lambda j: chunk_copy(dcn_ref, j, 1 - slot).start())

    _chunk_loop(nch, lambda j: chunk_copy(dc_ref, 0, slot).wait())
    def unpermute(rows):
        live = _iota((rows, 1), 0) < nch * ROW_CHUNK
        local = jnp.where(live, buf_ref[slot, 0:rows, :], jnp.zeros((), BF16))
        oh1, oh2 = _pick_onehot(lp_ref[...], rows)
        perm = jnp.where(oh1 | oh2, 1.0, 0.0).astype(BF16)
        o_ref[...] = x1_ref[...] + gt_ref[0] * _dot(perm, local)

    _for_staged_rows(nch, unpermute)


def _moe_combine(dc3, lp, x1, gt, ys, tokens_per_row):
    t, d = x1.shape
    nt = t // MOE_TK
    smem = lambda f: pl.BlockSpec((1, 1, MOE_CH), f, memory_space=pltpu.SMEM)
    return pl.pallas_call(
        _combine_kernel,
        grid=(nt,),
        in_specs=[smem(lambda i: (i, 0, 0)), smem(lambda i: (jnp.minimum(i + 1, nt - 1), 0, 0)),
                  pl.BlockSpec((MOE_TK, LANES), lambda i: (i, 0)),
                  pl.BlockSpec((MOE_TK, d), lambda i: (i, 0)),
                  _mod_spec(gt, MOE_TK, tokens_per_row),
                  pl.BlockSpec(memory_space=pl.ANY)],
        out_specs=pl.BlockSpec((MOE_TK, d), lambda i: (i, 0)),
        out_shape=jax.ShapeDtypeStruct((t, d), F32),
        scratch_shapes=[pltpu.VMEM((2, MOE_LB, d), BF16), pltpu.SemaphoreType.DMA((2,))],
        compiler_params=_cparams("arbitrary"),
    )(dc3, dc3, lp, x1, gt, ys)


def _moe_sparse(layer, h2, route, counts, x1, gt, w1, w3, w2, tokens_per_row):
    t = h2.shape[0]
    nt = t // MOE_TK
    lp, dc, tmeta = _moe_plan(route, counts)
    dc3 = dc[:, 0].reshape(nt, 1, MOE_CH)
    n_tiles = _moe_sorted_tiles(t)
    xs = _moe_dispatch(dc3, h2, lp, n_tiles * MOE_TM)
    ys = _moe_experts(layer, tmeta[:n_tiles, 0], tmeta[:n_tiles, 1], tmeta[0:1, 2], xs, w1, w3, w2)
    return _moe_combine(dc3, lp, x1, gt, ys, tokens_per_row)


def _w_in_segments():
    sizes = (DN_QKV, DN_HEADS * DN_DV, DN_HEADS, DN_HEADS,
             GLA_HEADS * GLA_DK, GLA_HEADS * GLA_DK, GLA_HEADS * GLA_DV, GLA_HEADS * GLA_DV, GLA_RANK,
             SWA_HEADS * SWA_HD, SWA_KV * SWA_HD, SWA_KV * SWA_HD, N_BRANCH * D_MODEL)
    offs = [0]
    for s in sizes:
        offs.append(offs[-1] + s)
    (a_qkv, a_z, a_b, a_a, b_q, b_k, b_v, b_r, b_lr, c_q, c_k, c_v, gate) = range(len(sizes))
    dst = {gate: C_GATE, a_qkv: C_AQKV, a_z: C_AZ, b_v: C_BV, b_r: C_BR, c_q: C_CQ, b_q: C_BQ, b_k: C_BK,
           c_k: C_CK, c_v: C_CV, a_b: C_SM + SM_BETA, a_a: C_SM + SM_G, b_lr: C_SM + SM_LR}
    return [(offs[i], dst[i], sizes[i]) for i in range(len(sizes))], offs[-1]


def _w_in_prep_kernel(w_ref, o_ref):
    segments, _ = _w_in_segments()
    n_src = w_ref.shape[-1]
    o_ref[:, C_SM:] = jnp.zeros((o_ref.shape[0], P_PAD - C_SM), BF16)
    for src, dst, width in segments:
        lo = (src // LANES) * LANES
        hi = min(-(-(src + width) // LANES) * LANES, n_src)
        win = w_ref[:, lo:hi]
        o_ref[:, dst:dst + width] = win[:, src - lo:src - lo + width].astype(BF16)


def _permute_w_in(w):
    depth, d, n = w.shape
    assert n == _w_in_segments()[1]
    rows = 256
    return pl.pallas_call(
        _w_in_prep_kernel,
        grid=(depth, d // rows),
        in_specs=[pl.BlockSpec((None, rows, n), lambda l, i: (l, i, 0))],
        out_specs=pl.BlockSpec((None, rows, P_PAD), lambda l, i: (l, i, 0)),
        out_shape=jax.ShapeDtypeStruct((depth, d, P_PAD), BF16),
        compiler_params=_cparams("parallel", "parallel"),
    )(w)


def _lane_row(v, off):
    depth, n = v.shape
    return jnp.zeros((depth, 1, LANES), F32).at[:, 0, off:off + n].set(v.astype(F32))


def kernel(x_prompt, x_sample, c_prompt, c_sample, state_dn, state_dn_conv, state_gla, cache_swa_k, cache_swa_v, ln1_g, ln2_g, ada_w, ada_b, w_in, dn_conv_w, dn_a_log, dn_dt_bias, dn_onorm_g, gla_w2, gla_b, gla_onorm_g, swa_qnorm_g, swa_knorm_g, swa_sinks, w_branch, w_out, rc_w, rc_b, re_w, re_b, w1, w3, w2):
    batch, seq, d = x_prompt.shape
    nb = x_sample.shape[0]
    depth = w_in.shape[0]
    assert x_sample.shape[1] == 1 and d == D_MODEL and seq % CHUNK == 0 and nb % DEC_TILE == 0
    kvw = SWA_KV * SWA_HD

    w_in_p = _permute_w_in(w_in)
    wb_b, wo_b = w_branch.astype(BF16), w_out.astype(BF16)
    wr = jnp.concatenate([rc_w, re_w, jnp.zeros((depth, d, LANES - N_GROUPS - N_EXPERTS), F32)], -1)
    rb = jnp.concatenate([rc_b, re_b, jnp.zeros((depth, LANES - N_GROUPS - N_EXPERTS), F32)], -1)[:, None, :]
    wr_hi = wr.astype(BF16)
    wr2 = jnp.concatenate([wr_hi, (wr - wr_hi.astype(F32)).astype(BF16)], -1)
    ln2_all = ln2_g[:, None, :]
    alog_row = _lane_row(dn_a_log, SM_G)
    dtb_row = _lane_row(dn_dt_bias, SM_G)
    w2pad = jnp.zeros((depth, LANES, GLA_HEADS * GLA_DK), F32).at[:, SM_LR:SM_LR + GLA_RANK].set(gla_w2).astype(BF16)
    kn2 = jnp.concatenate([swa_knorm_g] * SWA_KV, -1)[:, None, :]
    slopes = jnp.exp2(-8.0 * jnp.arange(1, SWA_HEADS + 1, dtype=F32) / SWA_HEADS)
    slp_col = jnp.tile(slopes, DEC_TILE)[:, None]

    pad_rows = (-(batch + nb)) % 8
    c_all = jnp.concatenate([c_prompt, c_sample, jnp.zeros((pad_rows, d), F32)], 0)
    mod = _ada(c_all, ada_w, ada_b)

    xp = x_prompt.reshape(batch * seq, d)
    xs = x_sample.reshape(nb, d)
    tm_p = 2048 if seq % 2048 == 0 else CHUNK
    tm_m = 512 if seq % 512 == 0 else CHUNK

    dec_states = (state_dn,
                  state_dn_conv.reshape(depth, nb, (CONV_W - 1) * DN_QKV),
                  state_gla.reshape(depth, nb, GLA_HEADS // 2, 2 * GLA_DK, GLA_DV),
                  cache_swa_k.reshape(depth, nb, WINDOW, kvw),
                  cache_swa_v.reshape(depth, nb, WINDOW, kvw))
    dec_out = None
    st_p = []
    for l in range(depth):
        mp = [m[:, None, :] for m in jnp.split(mod[l, :batch], 6, -1)]
        ms = [m[None] for m in jnp.split(mod[l, batch:batch + nb], 6, -1)]
        ln1, ln2 = ln1_g[l][None], ln2_g[l][None]
        conv_w = dn_conv_w[l]
        dn_on, gla_on = dn_onorm_g[l][None], gla_onorm_g[l][None]
        glb = gla_b[l][None]
        qn = swa_qnorm_g[l][None]
        snk = swa_sinks[l][None]

        pp = _in_proj(l, xp, mp[1], mp[0], ln1, w_in_p, tm_p, seq)
        pp3 = pp.reshape(batch, seq, P_PAD)
        ba, dn_s = _dn_prompt_mb(pp3, conv_w, alog_row[l], dtb_row[l], dn_on)
        bb, gla_st = _gla_prompt_mb(pp3, w2pad[l], glb, gla_on)
        bc, k_new = _swa_prompt_mb(pp3, jnp.concatenate([qn, qn], -1), kn2[l], snk)
        ba, bb, bc = (z.reshape(batch * seq, BRANCH_W) for z in (ba, bb, bc))
        x1, h2, comb, counts = _merge(l, ba, bb, bc, pp, xp, mp[2], mp[4], mp[3], ln2_all, wb_b, wo_b, wr2, rb,
                              tm_m, seq)
        xp = _moe_sparse(l, h2, comb, counts, x1, mp[5], w1, w3, w2, seq)
        pp3 = pp.reshape(batch, seq, P_PAD)
        st_p.append((dn_s,
                     pp3[:, seq - (CONV_W - 1):, C_AQKV:C_AQKV + DN_QKV],
                     jnp.swapaxes(gla_st, -1, -2),
                     k_new.reshape(batch, WINDOW, SWA_KV, SWA_HD),
                     pp3[:, seq - WINDOW:, C_CV:C_CV + kvw].reshape(batch, WINDOW, SWA_KV, SWA_HD)))

        ps = _in_proj(l, xs, ms[1], ms[0], ln1, w_in_p, nb, nb)
        cq_r = ps[:, C_CQ:C_CQ + SWA_HEADS * SWA_HD].reshape(nb * SWA_HEADS, SWA_HD)
        oa, ob, oc_r, *dec_out = _dec_mixers(
            l, ps, cq_r, dec_states, dec_out, conv_w, alog_row[l], dtb_row[l], dn_on, w2pad[l], glb, gla_on,
            qn, kn2[l], jnp.tile(swa_sinks[l], DEC_TILE)[:, None], slp_col)
        oc = oc_r.reshape(nb, SWA_HEADS * SWA_HD)
        x1, h2, comb, _ = _merge(l, oa, ob, oc, ps, xs, ms[2], ms[4], ms[3], ln2_all, wb_b, wo_b, wr2, rb,
                              nb, nb)
        xs = _moe(l, h2, comb, x1, ms[5], w1, w3, w2, nb, nb)

    dn_p, conv_p, gla_p, k_p, v_p = [jnp.stack(z) for z in zip(*st_p)]
    dn_s = dec_out[0]
    conv_s = dec_out[1].reshape(depth, nb, CONV_W - 1, DN_QKV)
    gla_s = dec_out[2].reshape(depth, nb, GLA_HEADS, GLA_DK, GLA_DV)
    k_s = dec_out[3].reshape(depth, nb, WINDOW, SWA_KV, SWA_HD)
    v_s = dec_out[4].reshape(depth, nb, WINDOW, SWA_KV, SWA_HD)
    return (xp.reshape(batch, seq, d), xs.reshape(nb, 1, d), dn_p, dn_s, conv_p, conv_s, gla_p, gla_s,
            k_p, k_s, v_p, v_s)
```

```python
import jax
import jax.numpy as jnp
from jax import lax
from jax.experimental import pallas as pl
from jax.experimental.pallas import tpu as pltpu

F32 = jnp.float32
BF16 = jnp.bfloat16
HIGHEST = lax.Precision.HIGHEST

D_MODEL = 1024
DN_HEADS, DN_DK, DN_DV, CONV_W = 4, 128, 128, 4
DN_QKV = 2 * DN_HEADS * DN_DK + DN_HEADS * DN_DV
GLA_HEADS, GLA_DK, GLA_DV, GLA_RANK, GLA_TAU = 4, 64, 128, 16, 16.0
SWA_HEADS, SWA_KV, SWA_HD, WINDOW = 8, 2, 64, 128
N_BRANCH, BRANCH_W = 3, 512
N_GROUPS, EXP_PER_GROUP, TOP_K, D_EXPERT = 4, 8, 2, 256
N_EXPERTS = N_GROUPS * EXP_PER_GROUP
EPS = 1e-6

LANES = 128
CHUNK = 128
SUB = 16
VMEM_LIMIT = 56 * 1024 * 1024

C_GATE, C_AQKV, C_AZ, C_BV, C_BR, C_CQ = 0, 3072, 4608, 5120, 5632, 6144
C_BQ, C_BK, C_CK, C_CV, C_SM, P_PAD = 6656, 6912, 7168, 7296, 7424, 7680
SM_BETA, SM_G, SM_LR = 0, DN_HEADS, 2 * DN_HEADS
R_EXP = N_GROUPS


def _cparams(*sem):
    return pltpu.CompilerParams(dimension_semantics=sem, vmem_limit_bytes=VMEM_LIMIT)


def _dot(a, b, precision=None):
    return jnp.dot(a, b, preferred_element_type=F32, precision=precision)


def _dot_nt(a, b, precision=None):
    return lax.dot_general(a, b, (((1,), (1,)), ((), ())), preferred_element_type=F32, precision=precision)


def _dot_tn(a, b, precision=None):
    return lax.dot_general(a, b, (((0,), (0,)), ((), ())), preferred_element_type=F32, precision=precision)


def _sigmoid(x):
    return 0.5 * jnp.tanh(0.5 * x) + 0.5


def _silu(x):
    h = 0.5 * x
    return h * jnp.tanh(h) + h


def _iota(shape, dim):
    return lax.broadcasted_iota(jnp.int32, shape, dim)


def _dot_01(tri, x):
    hi = x.astype(BF16)
    r = x - hi.astype(F32)
    mid = r.astype(BF16)
    low = (r - mid.astype(F32)).astype(BF16)
    n = x.shape[-1]
    y = _dot(tri, jnp.concatenate([hi, mid, low], -1))
    return y[:, :n] + y[:, n:2 * n] + y[:, 2 * n:]


def _ada_kernel(c_ref, w_ref, b_ref, o_ref):
    c = _silu(c_ref[...]).astype(BF16)
    o_ref[0] = _dot(c, w_ref[0].astype(BF16)) + b_ref[0]


def _ada(c_all, ada_w, ada_b):
    depth, d, n = ada_w.shape
    rows = c_all.shape[0]
    tn = 1536
    return pl.pallas_call(
        _ada_kernel,
        grid=(depth, n // tn),
        in_specs=[pl.BlockSpec((rows, d), lambda l, j: (0, 0)),
                  pl.BlockSpec((1, d, tn), lambda l, j: (l, 0, j)),
                  pl.BlockSpec((1, 1, tn), lambda l, j: (l, 0, j))],
        out_specs=pl.BlockSpec((1, rows, tn), lambda l, j: (l, 0, j)),
        out_shape=jax.ShapeDtypeStruct((depth, rows, n), F32),
        compiler_params=_cparams("parallel", "parallel"),
    )(c_all, ada_w, ada_b.reshape(depth, 1, n))


def _mod_spec(mod, tm, tokens_per_row):
    _, r, d = mod.shape
    assert tokens_per_row % tm == 0
    per = tokens_per_row // tm
    return pl.BlockSpec((1, r, d), lambda i, *_: (i // per, 0, 0))


def _in_proj_kernel(x_ref, sc_ref, sh_ref, g_ref, w_ref, o_ref, h_ref):
    @pl.when(pl.program_id(1) == 0)
    def _():
        x = x_ref[...]
        y = x * lax.rsqrt(jnp.mean(x * x, -1, keepdims=True) + EPS) * g_ref[...]
        h_ref[...] = (y * (1.0 + sc_ref[0]) + sh_ref[0]).astype(BF16)

    o_ref[...] = _dot(h_ref[...], w_ref[...])


def _in_proj(layer, x, sc, sh, g, w, tm, tokens_per_row):
    t, d = x.shape
    n = w.shape[2]
    tn = 768 if tm >= 2048 else 1536
    return pl.pallas_call(
        _in_proj_kernel,
        grid=(t // tm, n // tn),
        in_specs=[pl.BlockSpec((tm, d), lambda i, j: (i, 0)),
                  _mod_spec(sc, tm, tokens_per_row), _mod_spec(sh, tm, tokens_per_row),
                  pl.BlockSpec((1, d), lambda i, j: (0, 0)),
                  pl.BlockSpec((None, d, tn), lambda i, j: (layer, 0, j))],
        out_specs=pl.BlockSpec((tm, tn), lambda i, j: (i, j)),
        out_shape=jax.ShapeDtypeStruct((t, n), F32),
        scratch_shapes=[pltpu.VMEM((tm, d), BF16)],
        compiler_params=_cparams("parallel", "arbitrary"),
    )(x, sc, sh, g, w)


def _strict_lower_inverse_minus_eye_multi(a_list):
    c = a_list[0].shape[0]
    diag_blk = (_iota((c, c), 0) // SUB) == (_iota((c, c), 1) // SUB)
    idx = range(len(a_list))
    ad = [jnp.where(diag_blk, a, 0.0) for a in a_list]
    ao = [a_list[i] - ad[i] for i in idx]
    n = [-x for x in ad]
    p = n
    for _ in range(SUB.bit_length() - 2):
        p = [_dot(x, x) for x in p]
        np_ = [_dot(n[i], p[i]) for i in idx]
        n = [n[i] + p[i] + np_[i] for i in idx]
    nao = [_dot(n[i], ao[i]) for i in idx]
    m = [-(ao[i] + nao[i]) for i in idx]
    q = m
    for _ in range((c // SUB).bit_length() - 2):
        q = [_dot(x, x) for x in q]
        mq = [_dot(m[i], q[i]) for i in idx]
        m = [m[i] + q[i] + mq[i] for i in idx]
    mn = [_dot(m[i], n[i]) for i in idx]
    return [m[i] + n[i] + mn[i] for i in idx]


EXP_CAP = 80.0


def _half_rms(x, g2):
    lane = _iota(x.shape, 1)
    first = lane < SWA_HD
    sq = x * x
    s0 = jnp.sum(jnp.where(first, sq, 0.0), -1, keepdims=True)
    s1 = jnp.sum(jnp.where(first, 0.0, sq), -1, keepdims=True)
    ms = jnp.where(first, s0, s1) * (1.0 / SWA_HD)
    return x * lax.rsqrt(ms + EPS) * g2


def _dn_mb_kernel(qkv_ref, z_ref, sm_ref, cw_ref, alog_ref, dtb_ref, on_ref,
                  o_ref, s_out_ref, s_ref, xp_ref, y_ref):
    c_idx = pl.program_id(0)
    nb = qkv_ref.shape[0]
    c = CHUNK
    pad = 8

    @pl.when(c_idx == 0)
    def _():
        s_ref[...] = jnp.zeros_like(s_ref)
        xp_ref[:, 0:pad, :] = jnp.zeros((nb, pad, DN_QKV), F32)

    row, col = _iota((c, c), 0), _iota((c, c), 1)
    incl = row >= col
    strict = row > col
    incl_b = jnp.where(incl, 1.0, 0.0).astype(BF16)
    hk = DN_HEADS * DN_DK
    gam_all, gam_t, beta_all = [], [], []
    for b in range(nb):
        xp_ref[b, pad:pad + c, :] = qkv_ref[b]
        acc = xp_ref[b, pad - 3:pad - 3 + c, :] * cw_ref[0:1, :]
        for j in range(1, CONV_W):
            acc = acc + xp_ref[b, pad - 3 + j:pad - 3 + j + c, :] * cw_ref[j:j + 1, :]
        y_ref[b] = _silu(acc)
        xp_ref[b, pad - 3:pad, :] = xp_ref[b, pad + c - 3:pad + c, :]
        sm = sm_ref[b]
        beta_all.append(_sigmoid(sm))
        g_all = -jnp.exp(alog_ref[...]) * jax.nn.softplus(sm + dtb_ref[...])
        gam_all.append(_dot_01(incl_b, g_all))
        gam_t.append(gam_all[b].T)

    chains = [(b, h) for b in range(nb) for h in range(DN_HEADS)]
    n = len(chains)
    q, k, dec, eg, gl, gam, kb, rhs, a = ([None] * n for _ in range(9))
    for i, (b, h) in enumerate(chains):
        qi = y_ref[b, :, h * DN_DK:(h + 1) * DN_DK]
        ki = y_ref[b, :, hk + h * DN_DK:hk + (h + 1) * DN_DK]
        vi = y_ref[b, :, 2 * hk + h * DN_DV:2 * hk + (h + 1) * DN_DV]
        q[i] = qi * lax.rsqrt(jnp.sum(qi * qi, -1, keepdims=True) + EPS) * DN_DK ** -0.5
        k[i] = ki * lax.rsqrt(jnp.sum(ki * ki, -1, keepdims=True) + EPS)
        beta = beta_all[b][:, SM_BETA + h:SM_BETA + h + 1]
        gam[i] = gam_all[b][:, SM_G + h:SM_G + h + 1]
        gam_row = gam_t[b][SM_G + h:SM_G + h + 1, :]
        dec[i] = jnp.where(incl, jnp.exp(jnp.minimum(gam[i] - gam_row, 0.0)), 0.0)
        eg[i] = jnp.exp(gam[i])
        gl[i] = gam[i][c - 1:c, :]
        kb[i] = k[i] * beta
        rhs[i] = jnp.concatenate([vi * beta, kb[i] * eg[i]], -1)
    kbf = [x.astype(BF16) for x in k]
    kk = [_dot_nt(kb[i].astype(BF16), kbf[i]) for i in range(n)]
    qk = [_dot_nt(q[i].astype(BF16), kbf[i]) for i in range(n)]
    a = [jnp.where(strict, kk[i] * dec[i], 0.0) for i in range(n)]
    w = _strict_lower_inverse_minus_eye_multi(a)
    sol = [rhs[i] + _dot(w[i], rhs[i]) for i in range(n)]
    s = [s_ref[b, h] for (b, h) in chains]
    sbf = [x.astype(BF16) for x in s]
    u = [sol[i][:, :DN_DV] - _dot(sol[i][:, DN_DV:].astype(BF16), sbf[i]) for i in range(n)]
    ubf = [x.astype(BF16) for x in u]
    o_s = [_dot((q[i] * eg[i]).astype(BF16), sbf[i]) for i in range(n)]
    o_u = [_dot((qk[i] * dec[i]).astype(BF16), ubf[i]) for i in range(n)]
    ds = [_dot_tn((k[i] * jnp.exp(gl[i] - gam[i])).astype(BF16), ubf[i]) for i in range(n)]
    for i, (b, h) in enumerate(chains):
        s_ref[b, h] = s[i] * jnp.exp(gl[i]) + ds[i]
        o = o_s[i] + o_u[i]
        o = o * lax.rsqrt(jnp.mean(o * o, -1, keepdims=True) + EPS) * on_ref[...]
        o_ref[b, :, h * DN_DV:(h + 1) * DN_DV] = (
            o * _silu(z_ref[b, :, h * DN_DV:(h + 1) * DN_DV])).astype(BF16)

    @pl.when(c_idx == pl.num_programs(0) - 1)
    def _():
        s_out_ref[...] = s_ref[...]


def _dn_prompt_mb(p3, conv_w, alog_row, dtb_row, onorm):
    batch, seq, _ = p3.shape
    c = CHUNK
    full = lambda shape: pl.BlockSpec(shape, lambda i: tuple(0 for _ in shape))
    return pl.pallas_call(
        _dn_mb_kernel,
        grid=(seq // c,),
        in_specs=[pl.BlockSpec((batch, c, DN_QKV), lambda i: (0, i, C_AQKV // DN_QKV)),
                  pl.BlockSpec((batch, c, BRANCH_W), lambda i: (0, i, C_AZ // BRANCH_W)),
                  pl.BlockSpec((batch, c, LANES), lambda i: (0, i, C_SM // LANES)),
                  full((CONV_W, DN_QKV)), full((1, LANES)), full((1, LANES)), full((1, DN_DV))],
        out_specs=[pl.BlockSpec((batch, c, BRANCH_W), lambda i: (0, i, 0)),
                   full((batch, DN_HEADS, DN_DK, DN_DV))],
        out_shape=[jax.ShapeDtypeStruct((batch, seq, BRANCH_W), BF16),
                   jax.ShapeDtypeStruct((batch, DN_HEADS, DN_DK, DN_DV), F32)],
        scratch_shapes=[pltpu.VMEM((batch, DN_HEADS, DN_DK, DN_DV), F32),
                        pltpu.VMEM((batch, c + 8, DN_QKV), F32),
                        pltpu.VMEM((batch, c, DN_QKV), F32)],
        compiler_params=_cparams("arbitrary"),
    )(p3, p3, p3, conv_w, alog_row, dtb_row, onorm)


def _gla_mb_kernel(q_ref, k_ref, v_ref, r_ref, sm_ref, w2_ref, b_ref, on_ref,
                   o_ref, s_out_ref, st_ref):
    c_idx = pl.program_id(0)
    nb = q_ref.shape[0]
    c = CHUNK
    hw = GLA_HEADS * GLA_DK
    hv = GLA_HEADS * GLA_DV

    @pl.when(c_idx == 0)
    def _():
        st_ref[...] = jnp.zeros_like(st_ref)

    incl_b = jnp.where(_iota((c, c), 0) >= _iota((c, c), 1), 1.0, 0.0).astype(BF16)
    blk = (_iota((hv, hw), 0) // GLA_DV) == (_iota((hv, hw), 1) // GLA_DK)
    qsel = (_iota((GLA_HEADS * SUB, hw), 0) // SUB) == (_iota((GLA_HEADS * SUB, hw), 1) // GLA_DK)
    nbr = range(nb)
    lg = [jax.nn.log_sigmoid(_dot(sm_ref[b].astype(BF16), w2_ref[...]) + b_ref[...]) / GLA_TAU for b in nbr]
    gam = [_dot_01(incl_b, lg[b]) for b in nbr]
    gl = [gam[b][c - 1:c, :] for b in nbr]
    qs = [q_ref[b] * GLA_DK ** -0.5 for b in nbr]
    k = [k_ref[b] for b in nbr]
    qg = [(qs[b] * jnp.exp(gam[b])).astype(BF16) for b in nbr]
    kd = [(k[b] * jnp.exp(gl[b] - gam[b])).astype(BF16) for b in nbr]
    vbf = [v_ref[b].astype(BF16) for b in nbr]
    st = [st_ref[b] for b in nbr]
    o_inter = [_dot_nt(qg[b], st[b].astype(BF16)) for b in nbr]
    dst = [_dot_tn(vbf[b], kd[b]) for b in nbr]
    for b in nbr:
        st_ref[b] = st[b] * jnp.exp(gl[b]) + jnp.where(blk, dst[b], 0.0)

    res = [[] for _ in nbr]
    for i in range(c // SUB):
        lo, hi = i * SUB, (i + 1) * SUB
        keep = (_iota((GLA_HEADS * SUB, hi), 0) % SUB + lo) >= _iota((GLA_HEADS * SUB, hi), 1)
        qm, ki = [], []
        for b in nbr:
            ref_pt = gam[b][lo - 1:lo, :] if i > 0 else jnp.zeros((1, hw), F32)
            qi = qs[b][lo:hi] * jnp.exp(gam[b][lo:hi] - ref_pt)
            qm.append(jnp.where(qsel, jnp.concatenate([qi] * GLA_HEADS, 0), 0.0).astype(BF16))
            ki.append((k[b][:hi] * jnp.exp(jnp.minimum(ref_pt - gam[b][:hi], EXP_CAP))).astype(BF16))
        att = [_dot_nt(qm[b], ki[b]) for b in nbr]
        att = [jnp.where(keep, att[b], 0.0).astype(BF16) for b in nbr]
        for b in nbr:
            res[b].append(_dot(att[b], vbf[b][:hi]))
    for b in nbr:
        for h in range(GLA_HEADS):
            vs = slice(h * GLA_DV, (h + 1) * GLA_DV)
            o = o_inter[b][:, vs] + jnp.concatenate([r[h * SUB:(h + 1) * SUB, vs] for r in res[b]], 0)
            o = o * lax.rsqrt(jnp.mean(o * o, -1, keepdims=True) + EPS) * on_ref[...]
            o_ref[b, :, vs] = (o * _silu(r_ref[b, :, vs])).astype(BF16)

    @pl.when(c_idx == pl.num_programs(0) - 1)
    def _():
        for b in range(nb):
            for h in range(GLA_HEADS):
                s_out_ref[b, h] = st_ref[b, h * GLA_DV:(h + 1) * GLA_DV, h * GLA_DK:(h + 1) * GLA_DK]


def _gla_prompt_mb(p3, w2pad, gla_b, onorm):
    batch, seq, _ = p3.shape
    c = CHUNK
    hw = GLA_HEADS * GLA_DK
    hv = GLA_HEADS * GLA_DV
    full = lambda shape: pl.BlockSpec(shape, lambda i: tuple(0 for _ in shape))
    return pl.pallas_call(
        _gla_mb_kernel,
        grid=(seq // c,),
        in_specs=[pl.BlockSpec((batch, c, hw), lambda i: (0, i, C_BQ // hw)),
                  pl.BlockSpec((batch, c, hw), lambda i: (0, i, C_BK // hw)),
                  pl.BlockSpec((batch, c, hv), lambda i: (0, i, C_BV // hv)),
                  pl.BlockSpec((batch, c, hv), lambda i: (0, i, C_BR // hv)),
                  pl.BlockSpec((batch, c, LANES), lambda i: (0, i, C_SM // LANES)),
                  full((LANES, hw)), full((1, hw)), full((1, GLA_DV))],
        out_specs=[pl.BlockSpec((batch, c, hv), lambda i: (0, i, 0)),
                   full((batch, GLA_HEADS, GLA_DV, GLA_DK))],
        out_shape=[jax.ShapeDtypeStruct((batch, seq, hv), BF16),
                   jax.ShapeDtypeStruct((batch, GLA_HEADS, GLA_DV, GLA_DK), F32)],
        scratch_shapes=[pltpu.VMEM((batch, hv, hw), F32)],
        compiler_params=_cparams("arbitrary"),
    )(p3, p3, p3, p3, p3, w2pad, gla_b, onorm)


def _swa_mb_kernel(q_ref, kp_ref, kc_ref, vp_ref, vc_ref, qn_ref, kn_ref, snk_ref,
                   o_ref, knew_ref):
    i = pl.program_id(0)
    nb = q_ref.shape[0]
    w = WINDOW
    g = SWA_HEADS // SWA_KV
    nr = g * w
    first = _iota((w, 2 * SWA_HD), 1) < SWA_HD
    first2 = _iota((2 * w, 2 * SWA_HD), 1) < SWA_HD
    t = _iota((nr, 2 * w), 0) % w
    j = _iota((nr, 2 * w), 1)
    dist = w + t - j
    valid = (dist >= 0) & (dist <= w) & ((j >= w) | (i > 0))
    distf = dist.astype(F32)
    hrow = _iota((nr, 1), 0) // w
    slope, snk = [], []
    for kv in range(SWA_KV):
        sl = jnp.zeros((nr, 1), F32)
        sk = jnp.zeros((nr, 1), F32)
        for hh in range(g):
            h = kv * g + hh
            sl = jnp.where(hrow == hh, 2.0 ** (-8.0 * (h + 1) / SWA_HEADS), sl)
            sk = jnp.where(hrow == hh, snk_ref[:, h:h + 1], sk)
        slope.append(sl * distf)
        snk.append(sk)

    units = [(b, kv) for b in range(nb) for kv in range(SWA_KV)]
    kcs, k2, v2, qx = [], [], [], []
    for b in range(nb):
        kc = _half_rms(kc_ref[b], kn_ref[...])
        kcs.append(kc)
        kk = jnp.concatenate([_half_rms(kp_ref[b], kn_ref[...]), kc], 0)
        vv = jnp.concatenate([vp_ref[b], vc_ref[b]], 0)
        kk_sw = pltpu.roll(kk, SWA_HD, axis=1)
        vv_sw = pltpu.roll(vv, SWA_HD, axis=1)
        k2 += [jnp.where(first2, kk, kk_sw).astype(BF16), jnp.where(first2, kk_sw, kk).astype(BF16)]
        v2 += [jnp.where(first2, vv, vv_sw).astype(BF16), jnp.where(first2, vv_sw, vv).astype(BF16)]
        for kv in range(SWA_KV):
            rows = []
            for jj in range(g // 2):
                grp = kv * (g // 2) + jj
                qg = _half_rms(q_ref[b, :, grp * 2 * SWA_HD:(grp + 1) * 2 * SWA_HD], qn_ref[...]) * SWA_HD ** -0.5
                rows += [jnp.where(first, qg, 0.0), jnp.where(first, 0.0, qg)]
            qx.append(jnp.concatenate(rows, 0).astype(BF16))
    nu = range(len(units))
    s = [_dot_nt(qx[u], k2[u]) for u in nu]
    s = [jnp.where(valid, s[u] - slope[units[u][1]], -jnp.inf) for u in nu]
    m = [jnp.maximum(jnp.max(s[u], -1, keepdims=True), snk[units[u][1]]) for u in nu]
    pr = [jnp.exp(s[u] - m[u]) for u in nu]
    den = [jnp.sum(pr[u], -1, keepdims=True) + jnp.exp(snk[units[u][1]] - m[u]) for u in nu]
    pr = [(pr[u] * (1.0 / den[u])).astype(BF16) for u in nu]
    o = [_dot(pr[u], v2[u]) for u in nu]
    for u, (b, kv) in enumerate(units):
        for jj in range(g // 2):
            grp = kv * (g // 2) + jj
            o_ref[b, :, grp * 2 * SWA_HD:(grp + 1) * 2 * SWA_HD] = jnp.where(
                first, o[u][(2 * jj) * w:(2 * jj + 1) * w], o[u][(2 * jj + 1) * w:(2 * jj + 2) * w]).astype(BF16)

    @pl.when(i == pl.num_programs(0) - 1)
    def _():
        for b in range(nb):
            knew_ref[b] = kcs[b]


def _swa_prompt_mb(p3, qn2, kn2, sinks):
    batch, seq, _ = p3.shape
    w = WINDOW
    kvw = SWA_KV * SWA_HD
    full = lambda shape: pl.BlockSpec(shape, lambda i: tuple(0 for _ in shape))
    prev = lambda i: jnp.maximum(i - 1, 0)
    return pl.pallas_call(
        _swa_mb_kernel,
        grid=(seq // w,),
        in_specs=[pl.BlockSpec((batch, w, BRANCH_W), lambda i: (0, i, C_CQ // BRANCH_W)),
                  pl.BlockSpec((batch, w, kvw), lambda i: (0, prev(i), C_CK // kvw)),
                  pl.BlockSpec((batch, w, kvw), lambda i: (0, i, C_CK // kvw)),
                  pl.BlockSpec((batch, w, kvw), lambda i: (0, prev(i), C_CV // kvw)),
                  pl.BlockSpec((batch, w, kvw), lambda i: (0, i, C_CV // kvw)),
                  full((1, kvw)), full((1, kvw)), full((1, SWA_HEADS))],
        out_specs=[pl.BlockSpec((batch, w, BRANCH_W), lambda i: (0, i, 0)),
                   full((batch, w, kvw))],
        out_shape=[jax.ShapeDtypeStruct((batch, seq, BRANCH_W), BF16),
                   jax.ShapeDtypeStruct((batch, w, kvw), F32)],
        compiler_params=_cparams("arbitrary"),
    )(p3, p3, p3, p3, p3, qn2, kn2, sinks)


DEC_TILE = 8


def _dec_kernel(qkv_ref, z_ref, sm_ref, bq_ref, bk_ref, bv_ref, br_ref, cq_ref, ck_ref, cv_ref,
                sdn_ref, buf_ref, sgl_ref, kc_ref, vc_ref,
                cw_ref, alog_ref, dtb_ref, dnon_ref, w2_ref, glb_ref, glon_ref,
                qn_ref, kn_ref, snk_ref, slp_ref,
                oa_ref, ob_ref, oc_ref, sdn_out, buf_out, sgl_out, kc_out, vc_out):
    bt = DEC_TILE
    x = qkv_ref[...]
    buf = buf_ref[...]
    acc = x * cw_ref[CONV_W - 1:CONV_W, :]
    for j in range(CONV_W - 1):
        acc = acc + buf[:, j * DN_QKV:(j + 1) * DN_QKV] * cw_ref[j:j + 1, :]
    y = _silu(acc)
    buf_out[:, 0:(CONV_W - 2) * DN_QKV] = buf[:, DN_QKV:]
    buf_out[:, (CONV_W - 2) * DN_QKV:] = x

    sm = sm_ref[...]
    beta_all = _sigmoid(sm)
    g_all = -jnp.exp(alog_ref[...]) * jax.nn.softplus(sm + dtb_ref[...])
    eye = (_iota((LANES, LANES), 0) == _iota((LANES, LANES), 1)).astype(F32)
    hk = DN_HEADS * DN_DK
    for h in range(DN_HEADS):
        q = y[:, h * DN_DK:(h + 1) * DN_DK]
        k = y[:, hk + h * DN_DK:hk + (h + 1) * DN_DK]
        v = y[:, 2 * hk + h * DN_DV:2 * hk + (h + 1) * DN_DV]
        q = q * lax.rsqrt(jnp.sum(q * q, -1, keepdims=True) + EPS) * DN_DK ** -0.5
        k = k * lax.rsqrt(jnp.sum(k * k, -1, keepdims=True) + EPS)
        beta = beta_all[:, SM_BETA + h:SM_BETA + h + 1]
        eg = jnp.exp(g_all[:, SM_G + h:SM_G + h + 1])
        kb = k * beta
        lhs = jnp.concatenate([kb * eg, q * eg], 0).astype(BF16)
        k_t = _dot_nt(eye, k)
        qk = jnp.sum(q * k, -1, keepdims=True)
        vb = v * beta
        o_rows = []
        for b in range(bt):
            s = sdn_ref[b, h]
            r = _dot(lhs, s.astype(BF16))
            u = vb[b:b + 1] - r[b:b + 1]
            o_rows.append(r[bt + b:bt + b + 1] + qk[b:b + 1] * u)
            sdn_out[b, h] = s * eg[b:b + 1] + k_t[:, b:b + 1] * u
        o = jnp.concatenate(o_rows, 0)
        o = o * lax.rsqrt(jnp.mean(o * o, -1, keepdims=True) + EPS) * dnon_ref[...]
        oa_ref[:, h * DN_DV:(h + 1) * DN_DV] = (o * _silu(z_ref[:, h * DN_DV:(h + 1) * DN_DV])).astype(BF16)

    lg = jax.nn.log_sigmoid(_dot(sm.astype(BF16), w2_ref[...]) + glb_ref[...]) / GLA_TAU
    elg = jnp.exp(lg)
    bq = bq_ref[...] * GLA_DK ** -0.5
    bk = bk_ref[...]
    qg = bq * elg
    lane = _iota((bt, LANES), 1)
    first = lane < GLA_DK
    rows_first = _iota((LANES, GLA_DV), 0) < GLA_DK
    for j in range(GLA_HEADS // 2):
        sl = slice(j * LANES, (j + 1) * LANES)
        qgj = qg[:, sl]
        lhs = jnp.concatenate([jnp.where(first, qgj, 0.0), jnp.where(first, 0.0, qgj)], 0).astype(BF16)
        cols = _dot_nt(eye, jnp.concatenate([elg[:, sl], bk[:, sl]], 0), HIGHEST)
        prod = bq[:, sl] * bk[:, sl]
        qk0 = jnp.sum(jnp.where(first, prod, 0.0), -1, keepdims=True)
        qk1 = jnp.sum(jnp.where(first, 0.0, prod), -1, keepdims=True)
        v0 = bv_ref[:, (2 * j) * GLA_DV:(2 * j + 1) * GLA_DV]
        v1 = bv_ref[:, (2 * j + 1) * GLA_DV:(2 * j + 2) * GLA_DV]
        o0, o1 = [], []
        for b in range(bt):
            s = sgl_ref[b, j]
            r = _dot(lhs, s.astype(BF16))
            o0.append(r[b:b + 1] + qk0[b:b + 1] * v0[b:b + 1])
            o1.append(r[bt + b:bt + b + 1] + qk1[b:b + 1] * v1[b:b + 1])
            vsel = jnp.where(rows_first, v0[b:b + 1], v1[b:b + 1])
            sgl_out[b, j] = s * cols[:, b:b + 1] + cols[:, bt + b:bt + b + 1] * vsel
        for hh, rows in ((2 * j, o0), (2 * j + 1, o1)):
            o = jnp.concatenate(rows, 0)
            o = o * lax.rsqrt(jnp.mean(o * o, -1, keepdims=True) + EPS) * glon_ref[...]
            ob_ref[:, hh * GLA_DV:(hh + 1) * GLA_DV] = (o * _silu(br_ref[:, hh * GLA_DV:(hh + 1) * GLA_DV])).astype(BF16)

    g = SWA_HEADS // SWA_KV
    nr = bt * SWA_HEADS
    cq = cq_ref[...]
    cq = cq * lax.rsqrt(jnp.mean(cq * cq, -1, keepdims=True) + EPS) * qn_ref[...] * SWA_HD ** -0.5
    head = _iota((nr, 2 * SWA_HD), 0) % SWA_HEADS
    in_half = (head // g) == (_iota((nr, 2 * SWA_HD), 1) // SWA_HD)
    qx = jnp.where(in_half, jnp.concatenate([cq, cq], -1), 0.0)
    knew = _half_rms(ck_ref[...], kn_ref[...])
    vnew = cv_ref[...]
    s_c, kn_rows, vn_rows = [], [], []
    for b in range(bt):
        s_c.append(_dot_nt(qx[b * SWA_HEADS:(b + 1) * SWA_HEADS].astype(BF16), kc_ref[b].astype(BF16)))
        kn_rows.append(jnp.broadcast_to(knew[b:b + 1], (SWA_HEADS, 2 * SWA_HD)))
        vn_rows.append(jnp.broadcast_to(vnew[b:b + 1], (SWA_HEADS, 2 * SWA_HD)))
    s_c = jnp.concatenate(s_c, 0)
    kn_x = jnp.concatenate(kn_rows, 0)
    vn_x = jnp.concatenate(vn_rows, 0)
    slopes = slp_ref[...]
    snk = snk_ref[...]
    dist = (WINDOW - _iota((nr, WINDOW), 1)).astype(F32)
    s_c = s_c - slopes * dist
    s_n = jnp.sum(qx * kn_x, -1, keepdims=True)
    m = jnp.maximum(jnp.maximum(jnp.max(s_c, -1, keepdims=True), s_n), snk)
    p_c = jnp.exp(s_c - m)
    p_n = jnp.exp(s_n - m)
    den = jnp.sum(p_c, -1, keepdims=True) + p_n + jnp.exp(snk - m)
    p_c = p_c / den
    p_n = p_n / den
    half_sel = (_iota((nr, SWA_HD), 0) % SWA_HEADS) < g
    for b in range(bt):
        rs = slice(b * SWA_HEADS, (b + 1) * SWA_HEADS)
        r = _dot(p_c[rs].astype(BF16), vc_ref[b].astype(BF16)) + p_n[rs] * vn_x[rs]
        oc_ref[rs, :] = jnp.where(half_sel[rs], r[:, :SWA_HD], r[:, SWA_HD:]).astype(BF16)
        kc_out[b, 0:WINDOW - 1, :] = kc_ref[b, 1:WINDOW, :]
        kc_out[b, WINDOW - 1:WINDOW, :] = knew[b:b + 1]
        vc_out[b, 0:WINDOW - 1, :] = vc_ref[b, 1:WINDOW, :]
        vc_out[b, WINDOW - 1:WINDOW, :] = vnew[b:b + 1]


def _dec_kernel_aliased(*refs):
    n_in, n_alias = 26, 5
    _dec_kernel(*refs[:n_in], *refs[n_in + n_alias:])


def _dec_mixers(layer, p, cq_r, states, prev_out, conv_w, alog_row, dtb_row, dn_on, w2pad, gla_b, gla_on,
                qn, kn2, snk_col, slp_col):
    n = p.shape[0]
    bt = DEC_TILE
    hw = GLA_HEADS * GLA_DK
    kvw = SWA_KV * SWA_HD
    nr = bt * SWA_HEADS

    def col(width, off):
        return pl.BlockSpec((bt, width), lambda i: (i, off // width))

    def full(shape):
        return pl.BlockSpec(shape, lambda i: tuple(0 for _ in shape))

    def slot(a):
        rest = a.shape[2:]
        return pl.BlockSpec((None, bt) + rest, lambda i: (layer, i) + tuple(0 for _ in rest))

    in_specs = [col(DN_QKV, C_AQKV), col(BRANCH_W, C_AZ), col(LANES, C_SM), col(hw, C_BQ), col(hw, C_BK),
                col(BRANCH_W, C_BV), col(BRANCH_W, C_BR),
                pl.BlockSpec((nr, SWA_HD), lambda i: (i, 0)),
                col(kvw, C_CK), col(kvw, C_CV)]
    in_specs += [slot(a) for a in states]
    in_specs += [full((CONV_W, DN_QKV)), full((1, LANES)), full((1, LANES)), full((1, DN_DV)),
                 full((LANES, hw)), full((1, hw)), full((1, GLA_DV)),
                 full((1, SWA_HD)), full((1, kvw)), full((nr, 1)), full((nr, 1))]
    args = [p, p, p, p, p, p, p, cq_r, p, p, *states,
            conv_w, alog_row, dtb_row, dn_on, w2pad, gla_b, gla_on, qn, kn2, snk_col, slp_col]
    body, aliases = _dec_kernel, {}
    if prev_out is not None:
        body = _dec_kernel_aliased
        aliases = {len(args) + k: 3 + k for k in range(len(prev_out))}
        in_specs += [pl.BlockSpec(memory_space=pl.ANY)] * len(prev_out)
        args += list(prev_out)
    return pl.pallas_call(
        body,
        grid=(n // bt,),
        in_specs=in_specs,
        out_specs=[pl.BlockSpec((bt, BRANCH_W), lambda i: (i, 0)),
                   pl.BlockSpec((bt, BRANCH_W), lambda i: (i, 0)),
                   pl.BlockSpec((nr, SWA_HD), lambda i: (i, 0))] + [slot(a) for a in states],
        out_shape=[jax.ShapeDtypeStruct((n, BRANCH_W), BF16),
                   jax.ShapeDtypeStruct((n, BRANCH_W), BF16),
                   jax.ShapeDtypeStruct((n * SWA_HEADS, SWA_HD), BF16)]
        + [jax.ShapeDtypeStruct(a.shape, F32) for a in states],
        input_output_aliases=aliases,
        compiler_params=_cparams("parallel"),
    )(*args)


def _merge_kernel(ba_ref, bb_ref, bc_ref, gate_ref, x_ref, gt_ref, sc_ref, sh_ref, ln_ref,
                  wb_ref, wo_ref, wr_ref, rb_ref, x1_ref, h2_ref, comb_ref, cnt_ref):
    mix = None
    for n, br in enumerate((ba_ref, bb_ref, bc_ref)):
        up = _dot(br[...], wb_ref[n])
        term = _sigmoid(gate_ref[:, n * D_MODEL:(n + 1) * D_MODEL]) * up
        mix = term if mix is None else mix + term
    x1 = x_ref[...] + gt_ref[0] * _dot(mix.astype(BF16), wo_ref[...])
    x1_ref[...] = x1
    h2 = x1 * lax.rsqrt(jnp.mean(x1 * x1, -1, keepdims=True) + EPS) * ln_ref[...]
    h2 = h2 * (1.0 + sc_ref[0]) + sh_ref[0]
    h2b = h2.astype(BF16)
    h2_ref[...] = h2b

    h2l = (h2 - h2b.astype(F32)).astype(BF16)
    hi_lo = _dot(h2b, wr_ref[...])
    logits = hi_lo[:, :LANES] + hi_lo[:, LANES:] + _dot(h2l, wr_ref[:, :LANES]) + rb_ref[...]
    lane = _iota(logits.shape, 1).astype(F32)
    big = float(LANES)
    lc = jnp.where(lane < N_GROUPS, logits, -jnp.inf)
    mc = jnp.max(lc, -1, keepdims=True)
    pg = 1.0 / jnp.sum(jnp.exp(lc - mc), -1, keepdims=True)
    grp = jnp.min(jnp.where(lc == mc, lane, big), -1, keepdims=True)
    lo = R_EXP + grp * EXP_PER_GROUP
    emask = (lane >= lo) & (lane < lo + EXP_PER_GROUP)
    le = jnp.where(emask, logits, -jnp.inf)
    pe = jnp.exp(le - jnp.max(le, -1, keepdims=True))
    pe = pe / jnp.sum(pe, -1, keepdims=True)
    v1 = jnp.max(pe, -1, keepdims=True)
    i1 = jnp.min(jnp.where(emask & (pe == v1), lane, big), -1, keepdims=True)
    pe2 = jnp.where(emask & (lane != i1), pe, -1.0)
    v2 = jnp.max(pe2, -1, keepdims=True)
    i2 = jnp.min(jnp.where(pe2 == v2, lane, big), -1, keepdims=True)
    tot = v1 + v2
    comb_ref[...] = jnp.where(lane == 0.0, i1 - R_EXP, jnp.where(lane == 1.0, i2 - R_EXP, jnp.where(
        lane == 2.0, pg * v1 / tot, jnp.where(lane == 3.0, pg * v2 / tot, 0.0))))
    picks = jnp.where((lane == i1 - R_EXP) | (lane == i2 - R_EXP), 1.0, 0.0)
    cnt_ref[...] = jnp.broadcast_to(jnp.sum(picks, 0, keepdims=True), cnt_ref.shape)


def _merge(layer, ba, bb, bc, p, x, gt, sc, sh, ln, wb, wo, wr, rb, tm, tokens_per_row):
    t, d = x.shape
    tok = lambda width: pl.BlockSpec((tm, width), lambda i: (i, 0))
    full = lambda shape: pl.BlockSpec((None,) + shape, lambda i: (layer,) + tuple(0 for _ in shape))
    return pl.pallas_call(
        _merge_kernel,
        grid=(t // tm,),
        in_specs=[tok(BRANCH_W), tok(BRANCH_W), tok(BRANCH_W),
                  pl.BlockSpec((tm, N_BRANCH * d), lambda i: (i, C_GATE)),
                  tok(d),
                  _mod_spec(gt, tm, tokens_per_row), _mod_spec(sc, tm, tokens_per_row),
                  _mod_spec(sh, tm, tokens_per_row),
                  full((1, d)), full((N_BRANCH, BRANCH_W, d)), full((d, d)), full((d, 2 * LANES)),
                  full((1, LANES))],
        out_specs=[tok(d), tok(d), tok(LANES), pl.BlockSpec((8, LANES), lambda i: (i, 0))],
        out_shape=[jax.ShapeDtypeStruct((t, d), F32),
                   jax.ShapeDtypeStruct((t, d), BF16),
                   jax.ShapeDtypeStruct((t, LANES), F32),
                   jax.ShapeDtypeStruct((t // tm * 8, LANES), F32)],
        compiler_params=_cparams("parallel"),
    )(ba, bb, bc, p, x, gt, sc, sh, ln, wb, wo, wr, rb)


def _moe_kernel(h_ref, comb_ref, x1_ref, gt_ref, w1_ref, w3_ref, w2_ref, o_ref, acc_ref):
    e = pl.program_id(1)

    @pl.when(e == 0)
    def _():
        acc_ref[...] = jnp.zeros_like(acc_ref)

    h = h_ref[...]
    he = _silu(_dot(h, w1_ref[0].astype(BF16))) * _dot(h, w3_ref[0].astype(BF16))
    ye = _dot(he.astype(BF16), w2_ref[0].astype(BF16))
    comb = comb_ref[...]
    ef = e.astype(F32)
    ce = (jnp.where(comb[:, 0:1] == ef, comb[:, 2:3], 0.0) + jnp.where(comb[:, 1:2] == ef, comb[:, 3:4], 0.0))
    acc_ref[...] += ce * ye

    @pl.when(e == pl.num_programs(1) - 1)
    def _():
        o_ref[...] = x1_ref[...] + gt_ref[0] * acc_ref[...]


def _moe(layer, h2, comb, x1, gt, w1, w3, w2, tm, tokens_per_row):
    t, d = x1.shape
    _, ne, _, de = w1.shape
    return pl.pallas_call(
        _moe_kernel,
        grid=(t // tm, ne),
        in_specs=[pl.BlockSpec((tm, d), lambda i, e: (i, 0)),
                  pl.BlockSpec((tm, LANES), lambda i, e: (i, 0)),
                  pl.BlockSpec((tm, d), lambda i, e: (i, 0)),
                  _mod_spec(gt, tm, tokens_per_row),
                  pl.BlockSpec((None, 1, d, de), lambda i, e: (layer, e, 0, 0)),
                  pl.BlockSpec((None, 1, d, de), lambda i, e: (layer, e, 0, 0)),
                  pl.BlockSpec((None, 1, de, d), lambda i, e: (layer, e, 0, 0))],
        out_specs=pl.BlockSpec((tm, d), lambda i, e: (i, 0)),
        out_shape=jax.ShapeDtypeStruct((t, d), F32),
        scratch_shapes=[pltpu.VMEM((tm, d), F32)],
        compiler_params=_cparams("parallel", "arbitrary"),
    )(h2, comb, x1, gt, w1, w3, w2)


MOE_TK = 512
MOE_TM = 512
ROW_CHUNK = 16
MOE_LB = 2 * MOE_TK + 512
MOE_CH = MOE_LB // ROW_CHUNK
MOE_XW = D_MODEL + LANES
MOE_LB_SHORT = 2 * MOE_TK + 256
assert 2 * MOE_TK + N_EXPERTS * (ROW_CHUNK - 1) <= MOE_LB - ROW_CHUNK


def _moe_sorted_tiles(t):
    rows = 2 * t + (t // MOE_TK) * N_EXPERTS * (ROW_CHUNK - 1) + N_EXPERTS * (MOE_TM - 1)
    return -(-rows // MOE_TM)


def _plan_kernel(r_ref, cnt_ref, lp_ref, dc_ref, te_ref, base_ref):
    t = pl.program_id(0)
    tk = r_ref.shape[0]
    ntp = te_ref.shape[0]
    r = r_ref[...]
    lane = _iota((tk, LANES), 1).astype(F32)
    sel1, sel2 = lane == r[:, 0:1], lane == r[:, 1:2]
    oh = jnp.where(sel1 | sel2, 1.0, 0.0)
    upper = (_iota((LANES, LANES), 0) < _iota((LANES, LANES), 1)).astype(F32)

    def whole_chunks(cnt):
        return jnp.floor((cnt + (ROW_CHUNK - 1)) * (1.0 / ROW_CHUNK)) * ROW_CHUNK

    def excl_cumsum(v):
        return _dot(jnp.broadcast_to(v, (8, LANES)), upper, HIGHEST)[0:1]

    @pl.when(t == 0)
    def _():
        tot = jnp.sum(whole_chunks(cnt_ref[...]), 0, keepdims=True) * 0.125
        gp = jnp.floor((tot + (MOE_TM - 1)) * (1.0 / MOE_TM)) * MOE_TM
        off = excl_cumsum(gp)
        base_ref[...] = off
        end = off + gp
        lane_t = _iota((ntp, LANES), 1).astype(F32)
        start = _iota((ntp, 1), 0).astype(F32) * MOE_TM
        te = jnp.sum(jnp.where((lane_t < N_EXPERTS) & (end <= start), 1.0, 0.0), -1, keepdims=True)
        mine = lane_t == te
        filled = jnp.sum(jnp.where(mine, tot + off, 0.0), -1, keepdims=True)
        tv = jnp.clip(filled - start, 0.0, float(MOE_TM))
        n_used = jnp.sum(jnp.where(lane_t == N_EXPERTS - 1, end, 0.0), -1, keepdims=True) * (1.0 / MOE_TM)
        te_ref[...] = jnp.where(lane_t == 0.0, jnp.minimum(te, N_EXPERTS - 1.0),
                                jnp.where(lane_t == 1.0, tv, jnp.where(lane_t == 2.0, n_used, 0.0))
                                ).astype(jnp.int32)

    c8 = whole_chunks(jnp.sum(oh, 0, keepdims=True))
    base = base_ref[...]
    lo = excl_cumsum(c8)
    below = (_iota((tk, tk), 0) > _iota((tk, tk), 1)).astype(BF16)
    p = _dot(below, oh.astype(BF16)) + lo
    lp1 = jnp.sum(jnp.where(sel1, p, 0.0), -1, keepdims=True)
    lp2 = jnp.sum(jnp.where(sel2, p, 0.0), -1, keepdims=True)
    lp_ref[...] = jnp.where(lane == 0.0, lp1, jnp.where(lane == 1.0, lp2, jnp.where(lane < 4.0, r, 0.0)))
    lane_c = _iota((MOE_CH, LANES), 1).astype(F32)
    cstart = _iota((MOE_CH, 1), 0).astype(F32) * ROW_CHUNK
    ej = jnp.sum(jnp.where((lane_c < N_EXPERTS) & (lo + c8 <= cstart), 1.0, 0.0), -1, keepdims=True)
    dj = jnp.sum(jnp.where(lane_c == ej, base - lo, 0.0), -1, keepdims=True) + cstart
    nrows = jnp.sum(c8, -1, keepdims=True)
    last = _iota((MOE_CH, 1), 0) == MOE_CH - 1
    dcv = jnp.where(last, nrows * (1.0 / ROW_CHUNK), jnp.where(cstart < nrows, dj, 0.0))
    dc_ref[...] = jnp.broadcast_to(dcv, (MOE_CH, LANES)).astype(jnp.int32)
    base_ref[...] = base + c8


def _moe_plan(route, counts):
    t = route.shape[0]
    nt = t // MOE_TK
    ntp = -(-_moe_sorted_tiles(t) // 8) * 8
    assert counts.shape == (nt * 8, LANES)
    return pl.pallas_call(
        _plan_kernel,
        grid=(nt,),
        in_specs=[pl.BlockSpec((MOE_TK, LANES), lambda i: (i, 0)),
                  pl.BlockSpec((nt * 8, LANES), lambda i: (0, 0))],
        out_specs=[pl.BlockSpec((MOE_TK, LANES), lambda i: (i, 0)),
                   pl.BlockSpec((MOE_CH, LANES), lambda i: (i, 0)),
                   pl.BlockSpec((ntp, LANES), lambda i: (0, 0))],
        out_shape=[jax.ShapeDtypeStruct((t, LANES), F32),
                   jax.ShapeDtypeStruct((nt * MOE_CH, LANES), jnp.int32),
                   jax.ShapeDtypeStruct((ntp, LANES), jnp.int32)],
        scratch_shapes=[pltpu.VMEM((1, LANES), F32)],
        compiler_params=_cparams("arbitrary"),
    )(route, counts)


def _pick_onehot(lp, rows):
    pos = _iota((lp.shape[0], rows), 1).astype(F32)
    return pos == lp[:, 0:1], pos == lp[:, 1:2]


def _for_staged_rows(nch, fn):
    few = nch * ROW_CHUNK <= MOE_LB_SHORT

    @pl.when(few)
    def _():
        fn(MOE_LB_SHORT)

    @pl.when(jnp.logical_not(few))
    def _():
        fn(MOE_LB)


def _chunk_loop(n, fn):
    def body(j, carry):
        fn(j)
        return carry

    lax.fori_loop(0, n, body, 0)


def _start_chunks(n, copy):
    def body(j, carry):
        copy(2 * j).start(priority=0)
        copy(2 * j + 1).start(priority=1)
        return carry

    lax.fori_loop(0, n >> 1, body, 0)

    @pl.when((n & 1) == 1)
    def _():
        copy(n - 1).start(priority=0)


def _dispatch_kernel(dc_ref, h_ref, lp_ref, xs_ref, buf_ref, sem, nprev_ref):
    t, nt = pl.program_id(0), pl.num_programs(0)
    slot = t % 2
    tk = h_ref.shape[0]
    lp = lp_ref[...]
    lane = _iota((tk, LANES), 1)

    def split3(w):
        hi = w.astype(BF16).astype(F32)
        mid = (w - hi).astype(BF16).astype(F32)
        low = w - hi - mid
        return jnp.where(lane == 0, hi, jnp.where(lane == 1, mid, jnp.where(lane == 2, low, 0.0))).astype(BF16)

    w1, w2 = split3(lp[:, 2:3]), split3(lp[:, 3:4])
    nch = dc_ref[0, 0, MOE_CH - 1]

    def stage(rows):
        oh1, oh2 = _pick_onehot(lp, rows)
        b1, b2 = jnp.where(oh1, 1.0, 0.0).astype(BF16), jnp.where(oh2, 1.0, 0.0).astype(BF16)
        buf_ref[slot, 0:rows, 0:D_MODEL] = _dot_tn(b1 + b2, h_ref[...]).astype(BF16)
        buf_ref[slot, 0:rows, D_MODEL:] = (_dot_tn(b1, w1) + _dot_tn(b2, w2)).astype(BF16)

    _for_staged_rows(nch, stage)

    def chunk_copy(j, s):
        src = buf_ref.at[s, pl.ds(pl.multiple_of(j * ROW_CHUNK, ROW_CHUNK), ROW_CHUNK), :]
        dst = xs_ref.at[pl.ds(pl.multiple_of(dc_ref[0, 0, j], ROW_CHUNK), ROW_CHUNK), :]
        return pltpu.make_async_copy(src, dst, sem.at[s])

    _start_chunks(nch, lambda j: chunk_copy(j, slot))

    @pl.when(t > 0)
    def _():
        _chunk_loop(nprev_ref[0], lambda j: chunk_copy(0, 1 - slot).wait())

    nprev_ref[0] = nch

    @pl.when(t == nt - 1)
    def _():
        _chunk_loop(nch, lambda j: chunk_copy(0, slot).wait())


def _moe_dispatch(dc3, h2, lp, n_rows):
    t, d = h2.shape
    nt = t // MOE_TK
    return pl.pallas_call(
        _dispatch_kernel,
        grid=(nt,),
        in_specs=[pl.BlockSpec((1, 1, MOE_CH), lambda i: (i, 0, 0), memory_space=pltpu.SMEM),
                  pl.BlockSpec((MOE_TK, d), lambda i: (i, 0)),
                  pl.BlockSpec((MOE_TK, LANES), lambda i: (i, 0))],
        out_specs=pl.BlockSpec(memory_space=pl.ANY),
        out_shape=jax.ShapeDtypeStruct((n_rows, MOE_XW), BF16),
        scratch_shapes=[pltpu.VMEM((2, MOE_LB, MOE_XW), BF16), pltpu.SemaphoreType.DMA((2,)),
                        pltpu.SMEM((1,), jnp.int32)],
        compiler_params=_cparams("arbitrary"),
    )(dc3, h2, lp)


def _experts_kernel(te_ref, tv_ref, nu_ref, x_ref, w1_ref, w3_ref, w2_ref, o_ref, w1b, w3b, w2b):
    i = pl.program_id(0)

    @pl.when(i < nu_ref[0])
    def _():
        @pl.when((i == 0) | (te_ref[i] != te_ref[jnp.maximum(i - 1, 0)]))
        def _():
            w1b[...] = w1_ref[0].astype(BF16)
            w3b[...] = w3_ref[0].astype(BF16)
            w2b[...] = w2_ref[0].astype(BF16)

        tm = x_ref.shape[0]
        valid = _iota((tm, 1), 0) < tv_ref[i]
        h = jnp.where(valid, x_ref[:, :D_MODEL], jnp.zeros((), BF16))
        wx = x_ref[:, D_MODEL:].astype(F32)
        wv = jnp.where(valid, wx[:, 0:1] + wx[:, 1:2] + wx[:, 2:3], 0.0)
        he = _silu(_dot(h, w1b[...])) * _dot(h, w3b[...])
        o_ref[...] = (wv * _dot(he.astype(BF16), w2b[...])).astype(BF16)


def _moe_experts(layer, te, tv, nu, xs, w1, w3, w2):
    n_rows = xs.shape[0]
    _, _, d, de = w1.shape
    cur = lambda i, te, tv, nu: jnp.minimum(i, nu[0] - 1)
    wspec = lambda a, b: pl.BlockSpec((None, 1, a, b), lambda i, te, tv, nu: (layer, te[cur(i, te, tv, nu)], 0, 0))
    return pl.pallas_call(
        _experts_kernel,
        grid_spec=pltpu.PrefetchScalarGridSpec(
            num_scalar_prefetch=3,
            grid=(n_rows // MOE_TM,),
            in_specs=[pl.BlockSpec((MOE_TM, MOE_XW), lambda i, te, tv, nu: (cur(i, te, tv, nu), 0)),
                      wspec(d, de), wspec(d, de), wspec(de, d)],
            out_specs=pl.BlockSpec((MOE_TM, d), lambda i, te, tv, nu: (cur(i, te, tv, nu), 0)),
            scratch_shapes=[pltpu.VMEM((d, de), BF16), pltpu.VMEM((d, de), BF16), pltpu.VMEM((de, d), BF16)]),
        out_shape=jax.ShapeDtypeStruct((n_rows, d), BF16),
        compiler_params=_cparams("arbitrary"),
    )(te, tv, nu, xs, w1, w3, w2)


def _combine_kernel(dc_ref, dcn_ref, lp_ref, x1_ref, gt_ref, ys_ref, o_ref, buf_ref, sem):
    t, nt = pl.program_id(0), pl.num_programs(0)
    slot = t % 2

    def chunk_copy(tab, j, s):
        src = ys_ref.at[pl.ds(pl.multiple_of(tab[0, 0, j], ROW_CHUNK), ROW_CHUNK), :]
        dst = buf_ref.at[s, pl.ds(pl.multiple_of(j * ROW_CHUNK, ROW_CHUNK), ROW_CHUNK), :]
        return pltpu.make_async_copy(src, dst, sem.at[s])

    nch = dc_ref[0, 0, MOE_CH - 1]

    @pl.when(t == 0)
    def _():
        buf_ref[...] = jnp.zeros_like(buf_ref)
        _start_chunks(nch, lambda j: chunk_copy(dc_ref, j, slot))

    @pl.when(t + 1 < nt)
    def _():
        _start_chunks(dcn_ref[0, 0, MOE_CH - 1], lambda j: chunk_copy(dcn_ref, j, 1 - slot))

    _chunk_loop(nch, lambda j: chunk_copy(dc_ref, 0, slot).wait())
    def unpermute(rows):
        live = _iota((rows, 1), 0) < nch * ROW_CHUNK
        local = jnp.where(live, buf_ref[slot, 0:rows, :], jnp.zeros((), BF16))
        oh1, oh2 = _pick_onehot(lp_ref[...], rows)
        perm = jnp.where(oh1 | oh2, 1.0, 0.0).astype(BF16)
        o_ref[...] = x1_ref[...] + gt_ref[0] * _dot(perm, local)

    _for_staged_rows(nch, unpermute)


def _moe_combine(dc3, lp, x1, gt, ys, tokens_per_row):
    t, d = x1.shape
    nt = t // MOE_TK
    smem = lambda f: pl.BlockSpec((1, 1, MOE_CH), f, memory_space=pltpu.SMEM)
    return pl.pallas_call(
        _combine_kernel,
        grid=(nt,),
        in_specs=[smem(lambda i: (i, 0, 0)), smem(lambda i: (jnp.minimum(i + 1, nt - 1), 0, 0)),
                  pl.BlockSpec((MOE_TK, LANES), lambda i: (i, 0)),
                  pl.BlockSpec((MOE_TK, d), lambda i: (i, 0)),
                  _mod_spec(gt, MOE_TK, tokens_per_row),
                  pl.BlockSpec(memory_space=pl.ANY)],
        out_specs=pl.BlockSpec((MOE_TK, d), lambda i: (i, 0)),
        out_shape=jax.ShapeDtypeStruct((t, d), F32),
        scratch_shapes=[pltpu.VMEM((2, MOE_LB, d), BF16), pltpu.SemaphoreType.DMA((2,))],
        compiler_params=_cparams("arbitrary"),
    )(dc3, dc3, lp, x1, gt, ys)


def _moe_sparse(layer, h2, route, counts, x1, gt, w1, w3, w2, tokens_per_row):
    t = h2.shape[0]
    nt = t // MOE_TK
    lp, dc, tmeta = _moe_plan(route, counts)
    dc3 = dc[:, 0].reshape(nt, 1, MOE_CH)
    n_tiles = _moe_sorted_tiles(t)
    xs = _moe_dispatch(dc3, h2, lp, n_tiles * MOE_TM)
    ys = _moe_experts(layer, tmeta[:n_tiles, 0], tmeta[:n_tiles, 1], tmeta[0:1, 2], xs, w1, w3, w2)
    return _moe_combine(dc3, lp, x1, gt, ys, tokens_per_row)


def _w_in_segments():
    sizes = (DN_QKV, DN_HEADS * DN_DV, DN_HEADS, DN_HEADS,
             GLA_HEADS * GLA_DK, GLA_HEADS * GLA_DK, GLA_HEADS * GLA_DV, GLA_HEADS * GLA_DV, GLA_RANK,
             SWA_HEADS * SWA_HD, SWA_KV * SWA_HD, SWA_KV * SWA_HD, N_BRANCH * D_MODEL)
    offs = [0]
    for s in sizes:
        offs.append(offs[-1] + s)
    (a_qkv, a_z, a_b, a_a, b_q, b_k, b_v, b_r, b_lr, c_q, c_k, c_v, gate) = range(len(sizes))
    dst = {gate: C_GATE, a_qkv: C_AQKV, a_z: C_AZ, b_v: C_BV, b_r: C_BR, c_q: C_CQ, b_q: C_BQ, b_k: C_BK,
           c_k: C_CK, c_v: C_CV, a_b: C_SM + SM_BETA, a_a: C_SM + SM_G, b_lr: C_SM + SM_LR}
    return [(offs[i], dst[i], sizes[i]) for i in range(len(sizes))], offs[-1]


def _w_in_prep_kernel(w_ref, o_ref):
    segments, _ = _w_in_segments()
    n_src = w_ref.shape[-1]
    o_ref[:, C_SM:] = jnp.zeros((o_ref.shape[0], P_PAD - C_SM), BF16)
    for src, dst, width in segments:
        lo = (src // LANES) * LANES
        hi = min(-(-(src + width) // LANES) * LANES, n_src)
        win = w_ref[:, lo:hi]
        o_ref[:, dst:dst + width] = win[:, src - lo:src - lo + width].astype(BF16)


def _permute_w_in(w):
    depth, d, n = w.shape
    assert n == _w_in_segments()[1]
    rows = 256
    return pl.pallas_call(
        _w_in_prep_kernel,
        grid=(depth, d // rows),
        in_specs=[pl.BlockSpec((None, rows, n), lambda l, i: (l, i, 0))],
        out_specs=pl.BlockSpec((None, rows, P_PAD), lambda l, i: (l, i, 0)),
        out_shape=jax.ShapeDtypeStruct((depth, d, P_PAD), BF16),
        compiler_params=_cparams("parallel", "parallel"),
    )(w)


def _lane_row(v, off):
    depth, n = v.shape
    return jnp.zeros((depth, 1, LANES), F32).at[:, 0, off:off + n].set(v.astype(F32))


def kernel(x_prompt, x_sample, c_prompt, c_sample, state_dn, state_dn_conv, state_gla, cache_swa_k, cache_swa_v, ln1_g, ln2_g, ada_w, ada_b, w_in, dn_conv_w, dn_a_log, dn_dt_bias, dn_onorm_g, gla_w2, gla_b, gla_onorm_g, swa_qnorm_g, swa_knorm_g, swa_sinks, w_branch, w_out, rc_w, rc_b, re_w, re_b, w1, w3, w2):
    batch, seq, d = x_prompt.shape
    nb = x_sample.shape[0]
    depth = w_in.shape[0]
    assert x_sample.shape[1] == 1 and d == D_MODEL and seq % CHUNK == 0 and nb % DEC_TILE == 0
    kvw = SWA_KV * SWA_HD

    w_in_p = _permute_w_in(w_in)
    wb_b, wo_b = w_branch.astype(BF16), w_out.astype(BF16)
    wr = jnp.concatenate([rc_w, re_w, jnp.zeros((depth, d, LANES - N_GROUPS - N_EXPERTS), F32)], -1)
    rb = jnp.concatenate([rc_b, re_b, jnp.zeros((depth, LANES - N_GROUPS - N_EXPERTS), F32)], -1)[:, None, :]
    wr_hi = wr.astype(BF16)
    wr2 = jnp.concatenate([wr_hi, (wr - wr_hi.astype(F32)).astype(BF16)], -1)
    ln2_all = ln2_g[:, None, :]
    alog_row = _lane_row(dn_a_log, SM_G)
    dtb_row = _lane_row(dn_dt_bias, SM_G)
    w2pad = jnp.zeros((depth, LANES, GLA_HEADS * GLA_DK), F32).at[:, SM_LR:SM_LR + GLA_RANK].set(gla_w2).astype(BF16)
    kn2 = jnp.concatenate([swa_knorm_g] * SWA_KV, -1)[:, None, :]
    slopes = jnp.exp2(-8.0 * jnp.arange(1, SWA_HEADS + 1, dtype=F32) / SWA_HEADS)
    slp_col = jnp.tile(slopes, DEC_TILE)[:, None]

    pad_rows = (-(batch + nb)) % 8
    c_all = jnp.concatenate([c_prompt, c_sample, jnp.zeros((pad_rows, d), F32)], 0)
    mod = _ada(c_all, ada_w, ada_b)

    xp = x_prompt.reshape(batch * seq, d)
    xs = x_sample.reshape(nb, d)
    tm_p = 2048 if seq % 2048 == 0 else CHUNK
    tm_m = 512 if seq % 512 == 0 else CHUNK

    dec_states = (state_dn,
                  state_dn_conv.reshape(depth, nb, (CONV_W - 1) * DN_QKV),
                  state_gla.reshape(depth, nb, GLA_HEADS // 2, 2 * GLA_DK, GLA_DV),
                  cache_swa_k.reshape(depth, nb, WINDOW, kvw),
                  cache_swa_v.reshape(depth, nb, WINDOW, kvw))
    dec_out = None
    st_p = []
    for l in range(depth):
        mp = [m[:, None, :] for m in jnp.split(mod[l, :batch], 6, -1)]
        ms = [m[None] for m in jnp.split(mod[l, batch:batch + nb], 6, -1)]
        ln1, ln2 = ln1_g[l][None], ln2_g[l][None]
        conv_w = dn_conv_w[l]
        dn_on, gla_on = dn_onorm_g[l][None], gla_onorm_g[l][None]
        glb = gla_b[l][None]
        qn = swa_qnorm_g[l][None]
        snk = swa_sinks[l][None]

        pp = _in_proj(l, xp, mp[1], mp[0], ln1, w_in_p, tm_p, seq)
        pp3 = pp.reshape(batch, seq, P_PAD)
        ba, dn_s = _dn_prompt_mb(pp3, conv_w, alog_row[l], dtb_row[l], dn_on)
        bb, gla_st = _gla_prompt_mb(pp3, w2pad[l], glb, gla_on)
        bc, k_new = _swa_prompt_mb(pp3, jnp.concatenate([qn, qn], -1), kn2[l], snk)
        ba, bb, bc = (z.reshape(batch * seq, BRANCH_W) for z in (ba, bb, bc))
        x1, h2, comb, counts = _merge(l, ba, bb, bc, pp, xp, mp[2], mp[4], mp[3], ln2_all, wb_b, wo_b, wr2, rb,
                              tm_m, seq)
        xp = _moe_sparse(l, h2, comb, counts, x1, mp[5], w1, w3, w2, seq)
        pp3 = pp.reshape(batch, seq, P_PAD)
        st_p.append((dn_s,
                     pp3[:, seq - (CONV_W - 1):, C_AQKV:C_AQKV + DN_QKV],
                     jnp.swapaxes(gla_st, -1, -2),
                     k_new.reshape(batch, WINDOW, SWA_KV, SWA_HD),
                     pp3[:, seq - WINDOW:, C_CV:C_CV + kvw].reshape(batch, WINDOW, SWA_KV, SWA_HD)))

        ps = _in_proj(l, xs, ms[1], ms[0], ln1, w_in_p, nb, nb)
        cq_r = ps[:, C_CQ:C_CQ + SWA_HEADS * SWA_HD].reshape(nb * SWA_HEADS, SWA_HD)
        oa, ob, oc_r, *dec_out = _dec_mixers(
            l, ps, cq_r, dec_states, dec_out, conv_w, alog_row[l], dtb_row[l], dn_on, w2pad[l], glb, gla_on,
            qn, kn2[l], jnp.tile(swa_sinks[l], DEC_TILE)[:, None], slp_col)
        oc = oc_r.reshape(nb, SWA_HEADS * SWA_HD)
        x1, h2, comb, _ = _merge(l, oa, ob, oc, ps, xs, ms[2], ms[4], ms[3], ln2_all, wb_b, wo_b, wr2, rb,
                              nb, nb)
        xs = _moe(l, h2, comb, x1, ms[5], w1, w3, w2, nb, nb)

    dn_p, conv_p, gla_p, k_p, v_p = [jnp.stack(z) for z in zip(*st_p)]
    dn_s = dec_out[0]
    conv_s = dec_out[1].reshape(depth, nb, CONV_W - 1, DN_QKV)
    gla_s = dec_out[2].reshape(depth, nb, GLA_HEADS, GLA_DK, GLA_DV)
    k_s = dec_out[3].reshape(depth, nb, WINDOW, SWA_KV, SWA_HD)
    v_s = dec_out[4].reshape(depth, nb, WINDOW, SWA_KV, SWA_HD)
    return (xp.reshape(batch, seq, d), xs.reshape(nb, 1, d), dn_p, dn_s, conv_p, conv_s, gla_p, gla_s,
            k_p, k_s, v_p, v_s)
```

```python
from functools import partial

import jax
import jax.numpy as jnp
from jax import lax
from jax.experimental import pallas as pl
from jax.experimental.pallas import tpu as pltpu

F32 = jnp.float32
BF16 = jnp.bfloat16
HIGHEST = lax.Precision.HIGHEST

D_MODEL = 1024
DN_HEADS, DN_DK, DN_DV, CONV_W = 4, 128, 128, 4
DN_QKV = 2 * DN_HEADS * DN_DK + DN_HEADS * DN_DV
GLA_HEADS, GLA_DK, GLA_DV, GLA_RANK, GLA_TAU = 4, 64, 128, 16, 16.0
SWA_HEADS, SWA_KV, SWA_HD, WINDOW = 8, 2, 64, 128
N_BRANCH, BRANCH_W = 3, 512
N_GROUPS, EXP_PER_GROUP, TOP_K, D_EXPERT = 4, 8, 2, 256
N_EXPERTS = N_GROUPS * EXP_PER_GROUP
EPS = 1e-6

LANES = 128
CHUNK = 128
SUB = 16
VMEM_LIMIT = 56 * 1024 * 1024

C_GATE, C_AQKV, C_AZ, C_BV, C_BR, C_CQ = 0, 3072, 4608, 5120, 5632, 6144
C_BQ, C_BK, C_CK, C_CV, C_SM, P_PAD = 6656, 6912, 7168, 7296, 7424, 7680
SM_BETA, SM_G, SM_LR = 0, DN_HEADS, 2 * DN_HEADS
R_EXP = N_GROUPS


def _cparams(*sem):
    return pltpu.CompilerParams(dimension_semantics=sem, vmem_limit_bytes=VMEM_LIMIT)


def _dot(a, b, precision=None):
    return jnp.dot(a, b, preferred_element_type=F32, precision=precision)


def _dot_nt(a, b, precision=None):
    return lax.dot_general(a, b, (((1,), (1,)), ((), ())), preferred_element_type=F32, precision=precision)


def _dot_tn(a, b, precision=None):
    return lax.dot_general(a, b, (((0,), (0,)), ((), ())), preferred_element_type=F32, precision=precision)


def _sigmoid(x):
    return 0.5 * jnp.tanh(0.5 * x) + 0.5


def _silu(x):
    h = 0.5 * x
    return h * jnp.tanh(h) + h


def _iota(shape, dim):
    return lax.broadcasted_iota(jnp.int32, shape, dim)


def _dot_01(tri, x):
    hi = x.astype(BF16)
    r = x - hi.astype(F32)
    mid = r.astype(BF16)
    low = (r - mid.astype(F32)).astype(BF16)
    n = x.shape[-1]
    y = _dot(tri, jnp.concatenate([hi, mid, low], -1))
    return y[:, :n] + y[:, n:2 * n] + y[:, 2 * n:]


def _ada_kernel(c_ref, w_ref, b_ref, o_ref):
    c = _silu(c_ref[...]).astype(BF16)
    o_ref[0] = _dot(c, w_ref[0].astype(BF16)) + b_ref[0]


def _ada(c_all, ada_w, ada_b):
    depth, d, n = ada_w.shape
    rows = c_all.shape[0]
    tn = 1536
    return pl.pallas_call(
        _ada_kernel,
        grid=(depth, n // tn),
        in_specs=[pl.BlockSpec((rows, d), lambda l, j: (0, 0)),
                  pl.BlockSpec((1, d, tn), lambda l, j: (l, 0, j)),
                  pl.BlockSpec((1, 1, tn), lambda l, j: (l, 0, j))],
        out_specs=pl.BlockSpec((1, rows, tn), lambda l, j: (l, 0, j)),
        out_shape=jax.ShapeDtypeStruct((depth, rows, n), F32),
        compiler_params=_cparams("parallel", "parallel"),
    )(c_all, ada_w, ada_b.reshape(depth, 1, n))


def _mod_spec(mod, tm, tokens_per_row):
    _, r, d = mod.shape
    assert tokens_per_row % tm == 0
    per = tokens_per_row // tm
    return pl.BlockSpec((1, r, d), lambda i, *_: (i // per, 0, 0))


def _in_proj_kernel(x_ref, sc_ref, sh_ref, g_ref, w_ref, o_ref, h_ref):
    @pl.when(pl.program_id(1) == 0)
    def _():
        x = x_ref[...]
        y = x * lax.rsqrt(jnp.mean(x * x, -1, keepdims=True) + EPS) * g_ref[...]
        h_ref[...] = (y * (1.0 + sc_ref[0]) + sh_ref[0]).astype(BF16)

    o_ref[...] = _dot(h_ref[...], w_ref[...])


def _in_proj(layer, x, sc, sh, g, w, tm, tokens_per_row):
    t, d = x.shape
    n = w.shape[2]
    tn = 768 if tm >= 2048 else 1536
    return pl.pallas_call(
        _in_proj_kernel,
        grid=(t // tm, n // tn),
        in_specs=[pl.BlockSpec((tm, d), lambda i, j: (i, 0)),
                  _mod_spec(sc, tm, tokens_per_row), _mod_spec(sh, tm, tokens_per_row),
                  pl.BlockSpec((1, d), lambda i, j: (0, 0)),
                  pl.BlockSpec((None, d, tn), lambda i, j: (layer, 0, j))],
        out_specs=pl.BlockSpec((tm, tn), lambda i, j: (i, j)),
        out_shape=jax.ShapeDtypeStruct((t, n), F32),
        scratch_shapes=[pltpu.VMEM((tm, d), BF16)],
        compiler_params=_cparams("parallel", "arbitrary"),
    )(x, sc, sh, g, w)


def _strict_lower_inverse_minus_eye_multi(a_list):
    c = a_list[0].shape[0]
    diag_blk = (_iota((c, c), 0) // SUB) == (_iota((c, c), 1) // SUB)
    idx = range(len(a_list))
    ad = [jnp.where(diag_blk, a, 0.0) for a in a_list]
    ao = [a_list[i] - ad[i] for i in idx]
    n = [-x for x in ad]
    p = n
    for _ in range(SUB.bit_length() - 2):
        p = [_dot(x, x) for x in p]
        np_ = [_dot(n[i], p[i]) for i in idx]
        n = [n[i] + p[i] + np_[i] for i in idx]
    nao = [_dot(n[i], ao[i]) for i in idx]
    m = [-(ao[i] + nao[i]) for i in idx]
    q = m
    for _ in range((c // SUB).bit_length() - 2):
        q = [_dot(x, x) for x in q]
        mq = [_dot(m[i], q[i]) for i in idx]
        m = [m[i] + q[i] + mq[i] for i in idx]
    mn = [_dot(m[i], n[i]) for i in idx]
    return [m[i] + n[i] + mn[i] for i in idx]


EXP_CAP = 80.0


def _half_rms(x, g2):
    lane = _iota(x.shape, 1)
    first = lane < SWA_HD
    sq = x * x
    s0 = jnp.sum(jnp.where(first, sq, 0.0), -1, keepdims=True)
    s1 = jnp.sum(jnp.where(first, 0.0, sq), -1, keepdims=True)
    ms = jnp.where(first, s0, s1) * (1.0 / SWA_HD)
    return x * lax.rsqrt(ms + EPS) * g2


def _dn_mb_kernel(qkv_ref, z_ref, sm_ref, cw_ref, alog_ref, dtb_ref, on_ref,
                  o_ref, s_out_ref, s_ref, xp_ref, y_ref):
    c_idx = pl.program_id(0)
    nb = qkv_ref.shape[0]
    c = CHUNK
    pad = 8

    @pl.when(c_idx == 0)
    def _():
        s_ref[...] = jnp.zeros_like(s_ref)
        xp_ref[:, 0:pad, :] = jnp.zeros((nb, pad, DN_QKV), F32)

    row, col = _iota((c, c), 0), _iota((c, c), 1)
    incl = row >= col
    strict = row > col
    incl_b = jnp.where(incl, 1.0, 0.0).astype(BF16)
    hk = DN_HEADS * DN_DK
    gam_all, gam_t, beta_all = [], [], []
    for b in range(nb):
        xp_ref[b, pad:pad + c, :] = qkv_ref[b]
        acc = xp_ref[b, pad - 3:pad - 3 + c, :] * cw_ref[0:1, :]
        for j in range(1, CONV_W):
            acc = acc + xp_ref[b, pad - 3 + j:pad - 3 + j + c, :] * cw_ref[j:j + 1, :]
        y_ref[b] = _silu(acc)
        xp_ref[b, pad - 3:pad, :] = xp_ref[b, pad + c - 3:pad + c, :]
        sm = sm_ref[b]
        beta_all.append(_sigmoid(sm))
        g_all = -jnp.exp(alog_ref[...]) * jax.nn.softplus(sm + dtb_ref[...])
        gam_all.append(_dot_01(incl_b, g_all))
        gam_t.append(gam_all[b].T)

    chains = [(b, h) for b in range(nb) for h in range(DN_HEADS)]
    n = len(chains)
    q, k, dec, eg, gl, gam, kb, rhs, a = ([None] * n for _ in range(9))
    for i, (b, h) in enumerate(chains):
        qi = y_ref[b, :, h * DN_DK:(h + 1) * DN_DK]
        ki = y_ref[b, :, hk + h * DN_DK:hk + (h + 1) * DN_DK]
        vi = y_ref[b, :, 2 * hk + h * DN_DV:2 * hk + (h + 1) * DN_DV]
        q[i] = qi * lax.rsqrt(jnp.sum(qi * qi, -1, keepdims=True) + EPS) * DN_DK ** -0.5
        k[i] = ki * lax.rsqrt(jnp.sum(ki * ki, -1, keepdims=True) + EPS)
        beta = beta_all[b][:, SM_BETA + h:SM_BETA + h + 1]
        gam[i] = gam_all[b][:, SM_G + h:SM_G + h + 1]
        gam_row = gam_t[b][SM_G + h:SM_G + h + 1, :]
        dec[i] = jnp.where(incl, jnp.exp(jnp.minimum(gam[i] - gam_row, 0.0)), 0.0)
        eg[i] = jnp.exp(gam[i])
        gl[i] = gam[i][c - 1:c, :]
        kb[i] = k[i] * beta
        rhs[i] = jnp.concatenate([vi * beta, kb[i] * eg[i]], -1)
    kbf = [x.astype(BF16) for x in k]
    kk = [_dot_nt(kb[i].astype(BF16), kbf[i]) for i in range(n)]
    qk = [_dot_nt(q[i].astype(BF16), kbf[i]) for i in range(n)]
    a = [jnp.where(strict, kk[i] * dec[i], 0.0) for i in range(n)]
    w = _strict_lower_inverse_minus_eye_multi(a)
    sol = [rhs[i] + _dot(w[i], rhs[i]) for i in range(n)]
    s = [s_ref[b, h] for (b, h) in chains]
    sbf = [x.astype(BF16) for x in s]
    u = [sol[i][:, :DN_DV] - _dot(sol[i][:, DN_DV:].astype(BF16), sbf[i]) for i in range(n)]
    ubf = [x.astype(BF16) for x in u]
    o_s = [_dot((q[i] * eg[i]).astype(BF16), sbf[i]) for i in range(n)]
    o_u = [_dot((qk[i] * dec[i]).astype(BF16), ubf[i]) for i in range(n)]
    ds = [_dot_tn((k[i] * jnp.exp(gl[i] - gam[i])).astype(BF16), ubf[i]) for i in range(n)]
    for i, (b, h) in enumerate(chains):
        s_ref[b, h] = s[i] * jnp.exp(gl[i]) + ds[i]
        o = o_s[i] + o_u[i]
        o = o * lax.rsqrt(jnp.mean(o * o, -1, keepdims=True) + EPS) * on_ref[...]
        o_ref[b, :, h * DN_DV:(h + 1) * DN_DV] = (
            o * _silu(z_ref[b, :, h * DN_DV:(h + 1) * DN_DV])).astype(BF16)

    @pl.when(c_idx == pl.num_programs(0) - 1)
    def _():
        s_out_ref[...] = s_ref[...]


def _dn_prompt_mb(p3, conv_w, alog_row, dtb_row, onorm):
    batch, seq, _ = p3.shape
    c = CHUNK
    full = lambda shape: pl.BlockSpec(shape, lambda i: tuple(0 for _ in shape))
    return pl.pallas_call(
        _dn_mb_kernel,
        grid=(seq // c,),
        in_specs=[pl.BlockSpec((batch, c, DN_QKV), lambda i: (0, i, C_AQKV // DN_QKV)),
                  pl.BlockSpec((batch, c, BRANCH_W), lambda i: (0, i, C_AZ // BRANCH_W)),
                  pl.BlockSpec((batch, c, LANES), lambda i: (0, i, C_SM // LANES)),
                  full((CONV_W, DN_QKV)), full((1, LANES)), full((1, LANES)), full((1, DN_DV))],
        out_specs=[pl.BlockSpec((batch, c, BRANCH_W), lambda i: (0, i, 0)),
                   full((batch, DN_HEADS, DN_DK, DN_DV))],
        out_shape=[jax.ShapeDtypeStruct((batch, seq, BRANCH_W), BF16),
                   jax.ShapeDtypeStruct((batch, DN_HEADS, DN_DK, DN_DV), F32)],
        scratch_shapes=[pltpu.VMEM((batch, DN_HEADS, DN_DK, DN_DV), F32),
                        pltpu.VMEM((batch, c + 8, DN_QKV), F32),
                        pltpu.VMEM((batch, c, DN_QKV), F32)],
        compiler_params=_cparams("arbitrary"),
    )(p3, p3, p3, conv_w, alog_row, dtb_row, onorm)


def _gla_mb_kernel(q_ref, k_ref, v_ref, r_ref, sm_ref, w2_ref, b_ref, on_ref,
                   o_ref, s_out_ref, st_ref):
    c_idx = pl.program_id(0)
    nb = q_ref.shape[0]
    c = CHUNK
    hw = GLA_HEADS * GLA_DK
    hv = GLA_HEADS * GLA_DV

    @pl.when(c_idx == 0)
    def _():
        st_ref[...] = jnp.zeros_like(st_ref)

    incl_b = jnp.where(_iota((c, c), 0) >= _iota((c, c), 1), 1.0, 0.0).astype(BF16)
    blk = (_iota((hv, hw), 0) // GLA_DV) == (_iota((hv, hw), 1) // GLA_DK)
    qsel = (_iota((GLA_HEADS * SUB, hw), 0) // SUB) == (_iota((GLA_HEADS * SUB, hw), 1) // GLA_DK)
    nbr = range(nb)
    lg = [jax.nn.log_sigmoid(_dot(sm_ref[b].astype(BF16), w2_ref[...]) + b_ref[...]) / GLA_TAU for b in nbr]
    gam = [_dot_01(incl_b, lg[b]) for b in nbr]
    gl = [gam[b][c - 1:c, :] for b in nbr]
    qs = [q_ref[b] * GLA_DK ** -0.5 for b in nbr]
    k = [k_ref[b] for b in nbr]
    qg = [(qs[b] * jnp.exp(gam[b])).astype(BF16) for b in nbr]
    kd = [(k[b] * jnp.exp(gl[b] - gam[b])).astype(BF16) for b in nbr]
    vbf = [v_ref[b].astype(BF16) for b in nbr]
    st = [st_ref[b] for b in nbr]
    o_inter = [_dot_nt(qg[b], st[b].astype(BF16)) for b in nbr]
    dst = [_dot_tn(vbf[b], kd[b]) for b in nbr]
    for b in nbr:
        st_ref[b] = st[b] * jnp.exp(gl[b]) + jnp.where(blk, dst[b], 0.0)

    res = [[] for _ in nbr]
    for i in range(c // SUB):
        lo, hi = i * SUB, (i + 1) * SUB
        keep = (_iota((GLA_HEADS * SUB, hi), 0) % SUB + lo) >= _iota((GLA_HEADS * SUB, hi), 1)
        qm, ki = [], []
        for b in nbr:
            ref_pt = gam[b][lo - 1:lo, :] if i > 0 else jnp.zeros((1, hw), F32)
            qi = qs[b][lo:hi] * jnp.exp(gam[b][lo:hi] - ref_pt)
            qm.append(jnp.where(qsel, jnp.concatenate([qi] * GLA_HEADS, 0), 0.0).astype(BF16))
            ki.append((k[b][:hi] * jnp.exp(jnp.minimum(ref_pt - gam[b][:hi], EXP_CAP))).astype(BF16))
        att = [_dot_nt(qm[b], ki[b]) for b in nbr]
        att = [jnp.where(keep, att[b], 0.0).astype(BF16) for b in nbr]
        for b in nbr:
            res[b].append(_dot(att[b], vbf[b][:hi]))
    for b in nbr:
        for h in range(GLA_HEADS):
            vs = slice(h * GLA_DV, (h + 1) * GLA_DV)
            o = o_inter[b][:, vs] + jnp.concatenate([r[h * SUB:(h + 1) * SUB, vs] for r in res[b]], 0)
            o = o * lax.rsqrt(jnp.mean(o * o, -1, keepdims=True) + EPS) * on_ref[...]
            o_ref[b, :, vs] = (o * _silu(r_ref[b, :, vs])).astype(BF16)

    @pl.when(c_idx == pl.num_programs(0) - 1)
    def _():
        for b in range(nb):
            for h in range(GLA_HEADS):
                s_out_ref[b, h] = st_ref[b, h * GLA_DV:(h + 1) * GLA_DV, h * GLA_DK:(h + 1) * GLA_DK]


def _gla_prompt_mb(p3, w2pad, gla_b, onorm):
    batch, seq, _ = p3.shape
    c = CHUNK
    hw = GLA_HEADS * GLA_DK
    hv = GLA_HEADS * GLA_DV
    full = lambda shape: pl.BlockSpec(shape, lambda i: tuple(0 for _ in shape))
    return pl.pallas_call(
        _gla_mb_kernel,
        grid=(seq // c,),
        in_specs=[pl.BlockSpec((batch, c, hw), lambda i: (0, i, C_BQ // hw)),
                  pl.BlockSpec((batch, c, hw), lambda i: (0, i, C_BK // hw)),
                  pl.BlockSpec((batch, c, hv), lambda i: (0, i, C_BV // hv)),
                  pl.BlockSpec((batch, c, hv), lambda i: (0, i, C_BR // hv)),
                  pl.BlockSpec((batch, c, LANES), lambda i: (0, i, C_SM // LANES)),
                  full((LANES, hw)), full((1, hw)), full((1, GLA_DV))],
        out_specs=[pl.BlockSpec((batch, c, hv), lambda i: (0, i, 0)),
                   full((batch, GLA_HEADS, GLA_DV, GLA_DK))],
        out_shape=[jax.ShapeDtypeStruct((batch, seq, hv), BF16),
                   jax.ShapeDtypeStruct((batch, GLA_HEADS, GLA_DV, GLA_DK), F32)],
        scratch_shapes=[pltpu.VMEM((batch, hv, hw), F32)],
        compiler_params=_cparams("arbitrary"),
    )(p3, p3, p3, p3, p3, w2pad, gla_b, onorm)


def _swa_mb_kernel(q_ref, kp_ref, kc_ref, vp_ref, vc_ref, qn_ref, kn_ref, snk_ref,
                   o_ref, knew_ref):
    i = pl.program_id(0)
    nb = q_ref.shape[0]
    w = WINDOW
    g = SWA_HEADS // SWA_KV
    nr = g * w
    first = _iota((w, 2 * SWA_HD), 1) < SWA_HD
    first2 = _iota((2 * w, 2 * SWA_HD), 1) < SWA_HD
    t = _iota((nr, 2 * w), 0) % w
    j = _iota((nr, 2 * w), 1)
    dist = w + t - j
    valid = (dist >= 0) & (dist <= w) & ((j >= w) | (i > 0))
    distf = dist.astype(F32)
    hrow = _iota((nr, 1), 0) // w
    slope, snk = [], []
    for kv in range(SWA_KV):
        sl = jnp.zeros((nr, 1), F32)
        sk = jnp.zeros((nr, 1), F32)
        for hh in range(g):
            h = kv * g + hh
            sl = jnp.where(hrow == hh, 2.0 ** (-8.0 * (h + 1) / SWA_HEADS), sl)
            sk = jnp.where(hrow == hh, snk_ref[:, h:h + 1], sk)
        slope.append(sl * distf)
        snk.append(sk)

    units = [(b, kv) for b in range(nb) for kv in range(SWA_KV)]
    kcs, k2, v2, qx = [], [], [], []
    for b in range(nb):
        kc = _half_rms(kc_ref[b], kn_ref[...])
        kcs.append(kc)
        kk = jnp.concatenate([_half_rms(kp_ref[b], kn_ref[...]), kc], 0)
        vv = jnp.concatenate([vp_ref[b], vc_ref[b]], 0)
        kk_sw = pltpu.roll(kk, SWA_HD, axis=1)
        vv_sw = pltpu.roll(vv, SWA_HD, axis=1)
        k2 += [jnp.where(first2, kk, kk_sw).astype(BF16), jnp.where(first2, kk_sw, kk).astype(BF16)]
        v2 += [jnp.where(first2, vv, vv_sw).astype(BF16), jnp.where(first2, vv_sw, vv).astype(BF16)]
        for kv in range(SWA_KV):
            rows = []
            for jj in range(g // 2):
                grp = kv * (g // 2) + jj
                qg = _half_rms(q_ref[b, :, grp * 2 * SWA_HD:(grp + 1) * 2 * SWA_HD], qn_ref[...]) * SWA_HD ** -0.5
                rows += [jnp.where(first, qg, 0.0), jnp.where(first, 0.0, qg)]
            qx.append(jnp.concatenate(rows, 0).astype(BF16))
    nu = range(len(units))
    s = [_dot_nt(qx[u], k2[u]) for u in nu]
    s = [jnp.where(valid, s[u] - slope[units[u][1]], -jnp.inf) for u in nu]
    m = [jnp.maximum(jnp.max(s[u], -1, keepdims=True), snk[units[u][1]]) for u in nu]
    pr = [jnp.exp(s[u] - m[u]) for u in nu]
    den = [jnp.sum(pr[u], -1, keepdims=True) + jnp.exp(snk[units[u][1]] - m[u]) for u in nu]
    pr = [(pr[u] * (1.0 / den[u])).astype(BF16) for u in nu]
    o = [_dot(pr[u], v2[u]) for u in nu]
    for u, (b, kv) in enumerate(units):
        for jj in range(g // 2):
            grp = kv * (g // 2) + jj
            o_ref[b, :, grp * 2 * SWA_HD:(grp + 1) * 2 * SWA_HD] = jnp.where(
                first, o[u][(2 * jj) * w:(2 * jj + 1) * w], o[u][(2 * jj + 1) * w:(2 * jj + 2) * w]).astype(BF16)

    @pl.when(i == pl.num_programs(0) - 1)
    def _():
        for b in range(nb):
            knew_ref[b] = kcs[b]


def _swa_prompt_mb(p3, qn2, kn2, sinks):
    batch, seq, _ = p3.shape
    w = WINDOW
    kvw = SWA_KV * SWA_HD
    full = lambda shape: pl.BlockSpec(shape, lambda i: tuple(0 for _ in shape))
    prev = lambda i: jnp.maximum(i - 1, 0)
    return pl.pallas_call(
        _swa_mb_kernel,
        grid=(seq // w,),
        in_specs=[pl.BlockSpec((batch, w, BRANCH_W), lambda i: (0, i, C_CQ // BRANCH_W)),
                  pl.BlockSpec((batch, w, kvw), lambda i: (0, prev(i), C_CK // kvw)),
                  pl.BlockSpec((batch, w, kvw), lambda i: (0, i, C_CK // kvw)),
                  pl.BlockSpec((batch, w, kvw), lambda i: (0, prev(i), C_CV // kvw)),
                  pl.BlockSpec((batch, w, kvw), lambda i: (0, i, C_CV // kvw)),
                  full((1, kvw)), full((1, kvw)), full((1, SWA_HEADS))],
        out_specs=[pl.BlockSpec((batch, w, BRANCH_W), lambda i: (0, i, 0)),
                   full((batch, w, kvw))],
        out_shape=[jax.ShapeDtypeStruct((batch, seq, BRANCH_W), BF16),
                   jax.ShapeDtypeStruct((batch, w, kvw), F32)],
        compiler_params=_cparams("arbitrary"),
    )(p3, p3, p3, p3, p3, qn2, kn2, sinks)


DEC_TILE = 8


def _dec_kernel(qkv_ref, z_ref, sm_ref, bq_ref, bk_ref, bv_ref, br_ref, cq_ref, ck_ref, cv_ref,
                sdn_ref, buf_ref, sgl_ref, kc_ref, vc_ref,
                cw_ref, alog_ref, dtb_ref, dnon_ref, w2_ref, glb_ref, glon_ref,
                qn_ref, kn_ref, snk_ref, slp_ref,
                oa_ref, ob_ref, oc_ref, sdn_out, buf_out, sgl_out, kc_out, vc_out):
    bt = DEC_TILE
    x = qkv_ref[...]
    buf = buf_ref[...]
    acc = x * cw_ref[CONV_W - 1:CONV_W, :]
    for j in range(CONV_W - 1):
        acc = acc + buf[:, j * DN_QKV:(j + 1) * DN_QKV] * cw_ref[j:j + 1, :]
    y = _silu(acc)
    buf_out[:, 0:(CONV_W - 2) * DN_QKV] = buf[:, DN_QKV:]
    buf_out[:, (CONV_W - 2) * DN_QKV:] = x

    sm = sm_ref[...]
    beta_all = _sigmoid(sm)
    g_all = -jnp.exp(alog_ref[...]) * jax.nn.softplus(sm + dtb_ref[...])
    eye = (_iota((LANES, LANES), 0) == _iota((LANES, LANES), 1)).astype(F32)
    hk = DN_HEADS * DN_DK
    for h in range(DN_HEADS):
        q = y[:, h * DN_DK:(h + 1) * DN_DK]
        k = y[:, hk + h * DN_DK:hk + (h + 1) * DN_DK]
        v = y[:, 2 * hk + h * DN_DV:2 * hk + (h + 1) * DN_DV]
        q = q * lax.rsqrt(jnp.sum(q * q, -1, keepdims=True) + EPS) * DN_DK ** -0.5
        k = k * lax.rsqrt(jnp.sum(k * k, -1, keepdims=True) + EPS)
        beta = beta_all[:, SM_BETA + h:SM_BETA + h + 1]
        eg = jnp.exp(g_all[:, SM_G + h:SM_G + h + 1])
        kb = k * beta
        lhs = jnp.concatenate([kb * eg, q * eg], 0).astype(BF16)
        k_t = _dot_nt(eye, k)
        qk = jnp.sum(q * k, -1, keepdims=True)
        vb = v * beta
        o_rows = []
        for b in range(bt):
            s = sdn_ref[b, h]
            r = _dot(lhs, s.astype(BF16))
            u = vb[b:b + 1] - r[b:b + 1]
            o_rows.append(r[bt + b:bt + b + 1] + qk[b:b + 1] * u)
            sdn_out[b, h] = s * eg[b:b + 1] + k_t[:, b:b + 1] * u
        o = jnp.concatenate(o_rows, 0)
        o = o * lax.rsqrt(jnp.mean(o * o, -1, keepdims=True) + EPS) * dnon_ref[...]
        oa_ref[:, h * DN_DV:(h + 1) * DN_DV] = (o * _silu(z_ref[:, h * DN_DV:(h + 1) * DN_DV])).astype(BF16)

    lg = jax.nn.log_sigmoid(_dot(sm.astype(BF16), w2_ref[...]) + glb_ref[...]) / GLA_TAU
    elg = jnp.exp(lg)
    bq = bq_ref[...] * GLA_DK ** -0.5
    bk = bk_ref[...]
    qg = bq * elg
    lane = _iota((bt, LANES), 1)
    first = lane < GLA_DK
    rows_first = _iota((LANES, GLA_DV), 0) < GLA_DK
    for j in range(GLA_HEADS // 2):
        sl = slice(j * LANES, (j + 1) * LANES)
        qgj = qg[:, sl]
        lhs = jnp.concatenate([jnp.where(first, qgj, 0.0), jnp.where(first, 0.0, qgj)], 0).astype(BF16)
        cols = _dot_nt(eye, jnp.concatenate([elg[:, sl], bk[:, sl]], 0), HIGHEST)
        prod = bq[:, sl] * bk[:, sl]
        qk0 = jnp.sum(jnp.where(first, prod, 0.0), -1, keepdims=True)
        qk1 = jnp.sum(jnp.where(first, 0.0, prod), -1, keepdims=True)
        v0 = bv_ref[:, (2 * j) * GLA_DV:(2 * j + 1) * GLA_DV]
        v1 = bv_ref[:, (2 * j + 1) * GLA_DV:(2 * j + 2) * GLA_DV]
        o0, o1 = [], []
        for b in range(bt):
            s = sgl_ref[b, j]
            r = _dot(lhs, s.astype(BF16))
            o0.append(r[b:b + 1] + qk0[b:b + 1] * v0[b:b + 1])
            o1.append(r[bt + b:bt + b + 1] + qk1[b:b + 1] * v1[b:b + 1])
            vsel = jnp.where(rows_first, v0[b:b + 1], v1[b:b + 1])
            sgl_out[b, j] = s * cols[:, b:b + 1] + cols[:, bt + b:bt + b + 1] * vsel
        for hh, rows in ((2 * j, o0), (2 * j + 1, o1)):
            o = jnp.concatenate(rows, 0)
            o = o * lax.rsqrt(jnp.mean(o * o, -1, keepdims=True) + EPS) * glon_ref[...]
            ob_ref[:, hh * GLA_DV:(hh + 1) * GLA_DV] = (o * _silu(br_ref[:, hh * GLA_DV:(hh + 1) * GLA_DV])).astype(BF16)

    g = SWA_HEADS // SWA_KV
    nr = bt * SWA_HEADS
    cq = cq_ref[...]
    cq = cq * lax.rsqrt(jnp.mean(cq * cq, -1, keepdims=True) + EPS) * qn_ref[...] * SWA_HD ** -0.5
    head = _iota((nr, 2 * SWA_HD), 0) % SWA_HEADS
    in_half = (head // g) == (_iota((nr, 2 * SWA_HD), 1) // SWA_HD)
    qx = jnp.where(in_half, jnp.concatenate([cq, cq], -1), 0.0)
    knew = _half_rms(ck_ref[...], kn_ref[...])
    vnew = cv_ref[...]
    s_c, kn_rows, vn_rows = [], [], []
    for b in range(bt):
        s_c.append(_dot_nt(qx[b * SWA_HEADS:(b + 1) * SWA_HEADS].astype(BF16), kc_ref[b].astype(BF16)))
        kn_rows.append(jnp.broadcast_to(knew[b:b + 1], (SWA_HEADS, 2 * SWA_HD)))
        vn_rows.append(jnp.broadcast_to(vnew[b:b + 1], (SWA_HEADS, 2 * SWA_HD)))
    s_c = jnp.concatenate(s_c, 0)
    kn_x = jnp.concatenate(kn_rows, 0)
    vn_x = jnp.concatenate(vn_rows, 0)
    slopes = slp_ref[...]
    snk = snk_ref[...]
    dist = (WINDOW - _iota((nr, WINDOW), 1)).astype(F32)
    s_c = s_c - slopes * dist
    s_n = jnp.sum(qx * kn_x, -1, keepdims=True)
    m = jnp.maximum(jnp.maximum(jnp.max(s_c, -1, keepdims=True), s_n), snk)
    p_c = jnp.exp(s_c - m)
    p_n = jnp.exp(s_n - m)
    den = jnp.sum(p_c, -1, keepdims=True) + p_n + jnp.exp(snk - m)
    p_c = p_c / den
    p_n = p_n / den
    half_sel = (_iota((nr, SWA_HD), 0) % SWA_HEADS) < g
    for b in range(bt):
        rs = slice(b * SWA_HEADS, (b + 1) * SWA_HEADS)
        r = _dot(p_c[rs].astype(BF16), vc_ref[b].astype(BF16)) + p_n[rs] * vn_x[rs]
        oc_ref[rs, :] = jnp.where(half_sel[rs], r[:, :SWA_HD], r[:, SWA_HD:]).astype(BF16)
        kc_out[b, 0:WINDOW - 1, :] = kc_ref[b, 1:WINDOW, :]
        kc_out[b, WINDOW - 1:WINDOW, :] = knew[b:b + 1]
        vc_out[b, 0:WINDOW - 1, :] = vc_ref[b, 1:WINDOW, :]
        vc_out[b, WINDOW - 1:WINDOW, :] = vnew[b:b + 1]


def _dec_kernel_aliased(*refs):
    n_in, n_alias = 26, 5
    _dec_kernel(*refs[:n_in], *refs[n_in + n_alias:])


def _dec_mixers(layer, p, cq_r, states, prev_out, conv_w, alog_row, dtb_row, dn_on, w2pad, gla_b, gla_on,
                qn, kn2, snk_col, slp_col):
    n = p.shape[0]
    bt = DEC_TILE
    hw = GLA_HEADS * GLA_DK
    kvw = SWA_KV * SWA_HD
    nr = bt * SWA_HEADS

    def col(width, off):
        return pl.BlockSpec((bt, width), lambda i: (i, off // width))

    def full(shape):
        return pl.BlockSpec(shape, lambda i: tuple(0 for _ in shape))

    def slot(a):
        rest = a.shape[2:]
        return pl.BlockSpec((None, bt) + rest, lambda i: (layer, i) + tuple(0 for _ in rest))

    in_specs = [col(DN_QKV, C_AQKV), col(BRANCH_W, C_AZ), col(LANES, C_SM), col(hw, C_BQ), col(hw, C_BK),
                col(BRANCH_W, C_BV), col(BRANCH_W, C_BR),
                pl.BlockSpec((nr, SWA_HD), lambda i: (i, 0)),
                col(kvw, C_CK), col(kvw, C_CV)]
    in_specs += [slot(a) for a in states]
    in_specs += [full((CONV_W, DN_QKV)), full((1, LANES)), full((1, LANES)), full((1, DN_DV)),
                 full((LANES, hw)), full((1, hw)), full((1, GLA_DV)),
                 full((1, SWA_HD)), full((1, kvw)), full((nr, 1)), full((nr, 1))]
    args = [p, p, p, p, p, p, p, cq_r, p, p, *states,
            conv_w, alog_row, dtb_row, dn_on, w2pad, gla_b, gla_on, qn, kn2, snk_col, slp_col]
    body, aliases = _dec_kernel, {}
    if prev_out is not None:
        body = _dec_kernel_aliased
        aliases = {len(args) + k: 3 + k for k in range(len(prev_out))}
        in_specs += [pl.BlockSpec(memory_space=pl.ANY)] * len(prev_out)
        args += list(prev_out)
    return pl.pallas_call(
        body,
        grid=(n // bt,),
        in_specs=in_specs,
        out_specs=[pl.BlockSpec((bt, BRANCH_W), lambda i: (i, 0)),
                   pl.BlockSpec((bt, BRANCH_W), lambda i: (i, 0)),
                   pl.BlockSpec((nr, SWA_HD), lambda i: (i, 0))] + [slot(a) for a in states],
        out_shape=[jax.ShapeDtypeStruct((n, BRANCH_W), BF16),
                   jax.ShapeDtypeStruct((n, BRANCH_W), BF16),
                   jax.ShapeDtypeStruct((n * SWA_HEADS, SWA_HD), BF16)]
        + [jax.ShapeDtypeStruct(a.shape, F32) for a in states],
        input_output_aliases=aliases,
        compiler_params=_cparams("parallel"),
    )(*args)


def _merge_kernel(ba_ref, bb_ref, bc_ref, gate_ref, x_ref, gt_ref, sc_ref, sh_ref, ln_ref,
                  wb_ref, wo_ref, wr_ref, rb_ref, x1_ref, h2_ref, comb_ref, cnt_ref):
    mix = None
    for n, br in enumerate((ba_ref, bb_ref, bc_ref)):
        up = _dot(br[...], wb_ref[n])
        term = _sigmoid(gate_ref[:, n * D_MODEL:(n + 1) * D_MODEL]) * up
        mix = term if mix is None else mix + term
    x1 = x_ref[...] + gt_ref[0] * _dot(mix.astype(BF16), wo_ref[...])
    x1_ref[...] = x1
    h2 = x1 * lax.rsqrt(jnp.mean(x1 * x1, -1, keepdims=True) + EPS) * ln_ref[...]
    h2 = h2 * (1.0 + sc_ref[0]) + sh_ref[0]
    h2b = h2.astype(BF16)
    h2_ref[...] = h2b

    h2l = (h2 - h2b.astype(F32)).astype(BF16)
    hi_lo = _dot(h2b, wr_ref[...])
    logits = hi_lo[:, :LANES] + hi_lo[:, LANES:] + _dot(h2l, wr_ref[:, :LANES]) + rb_ref[...]
    lane = _iota(logits.shape, 1).astype(F32)
    big = float(LANES)
    lc = jnp.where(lane < N_GROUPS, logits, -jnp.inf)
    mc = jnp.max(lc, -1, keepdims=True)
    pg = 1.0 / jnp.sum(jnp.exp(lc - mc), -1, keepdims=True)
    grp = jnp.min(jnp.where(lc == mc, lane, big), -1, keepdims=True)
    lo = R_EXP + grp * EXP_PER_GROUP
    emask = (lane >= lo) & (lane < lo + EXP_PER_GROUP)
    le = jnp.where(emask, logits, -jnp.inf)
    pe = jnp.exp(le - jnp.max(le, -1, keepdims=True))
    pe = pe / jnp.sum(pe, -1, keepdims=True)
    v1 = jnp.max(pe, -1, keepdims=True)
    i1 = jnp.min(jnp.where(emask & (pe == v1), lane, big), -1, keepdims=True)
    pe2 = jnp.where(emask & (lane != i1), pe, -1.0)
    v2 = jnp.max(pe2, -1, keepdims=True)
    i2 = jnp.min(jnp.where(pe2 == v2, lane, big), -1, keepdims=True)
    tot = v1 + v2
    comb_ref[...] = jnp.where(lane == 0.0, i1 - R_EXP, jnp.where(lane == 1.0, i2 - R_EXP, jnp.where(
        lane == 2.0, pg * v1 / tot, jnp.where(lane == 3.0, pg * v2 / tot, 0.0))))
    picks = jnp.where((lane == i1 - R_EXP) | (lane == i2 - R_EXP), 1.0, 0.0)
    cnt_ref[...] = jnp.broadcast_to(jnp.sum(picks, 0, keepdims=True), cnt_ref.shape)


def _merge(layer, ba, bb, bc, p, x, gt, sc, sh, ln, wb, wo, wr, rb, tm, tokens_per_row):
    t, d = x.shape
    tok = lambda width: pl.BlockSpec((tm, width), lambda i: (i, 0))
    full = lambda shape: pl.BlockSpec((None,) + shape, lambda i: (layer,) + tuple(0 for _ in shape))
    return pl.pallas_call(
        _merge_kernel,
        grid=(t // tm,),
        in_specs=[tok(BRANCH_W), tok(BRANCH_W), tok(BRANCH_W),
                  pl.BlockSpec((tm, N_BRANCH * d), lambda i: (i, C_GATE)),
                  tok(d),
                  _mod_spec(gt, tm, tokens_per_row), _mod_spec(sc, tm, tokens_per_row),
                  _mod_spec(sh, tm, tokens_per_row),
                  full((1, d)), full((N_BRANCH, BRANCH_W, d)), full((d, d)), full((d, 2 * LANES)),
                  full((1, LANES))],
        out_specs=[tok(d), tok(d), tok(LANES), pl.BlockSpec((8, LANES), lambda i: (i, 0))],
        out_shape=[jax.ShapeDtypeStruct((t, d), F32),
                   jax.ShapeDtypeStruct((t, d), BF16),
                   jax.ShapeDtypeStruct((t, LANES), F32),
                   jax.ShapeDtypeStruct((t // tm * 8, LANES), F32)],
        compiler_params=_cparams("parallel"),
    )(ba, bb, bc, p, x, gt, sc, sh, ln, wb, wo, wr, rb)


def _moe_kernel(h_ref, comb_ref, x1_ref, gt_ref, w1_ref, w3_ref, w2_ref, o_ref, acc_ref):
    e = pl.program_id(1)

    @pl.when(e == 0)
    def _():
        acc_ref[...] = jnp.zeros_like(acc_ref)

    h = h_ref[...]
    he = _silu(_dot(h, w1_ref[0].astype(BF16))) * _dot(h, w3_ref[0].astype(BF16))
    ye = _dot(he.astype(BF16), w2_ref[0].astype(BF16))
    comb = comb_ref[...]
    ef = e.astype(F32)
    ce = (jnp.where(comb[:, 0:1] == ef, comb[:, 2:3], 0.0) + jnp.where(comb[:, 1:2] == ef, comb[:, 3:4], 0.0))
    acc_ref[...] += ce * ye

    @pl.when(e == pl.num_programs(1) - 1)
    def _():
        o_ref[...] = x1_ref[...] + gt_ref[0] * acc_ref[...]


def _moe(layer, h2, comb, x1, gt, w1, w3, w2, tm, tokens_per_row):
    t, d = x1.shape
    _, ne, _, de = w1.shape
    return pl.pallas_call(
        _moe_kernel,
        grid=(t // tm, ne),
        in_specs=[pl.BlockSpec((tm, d), lambda i, e: (i, 0)),
                  pl.BlockSpec((tm, LANES), lambda i, e: (i, 0)),
                  pl.BlockSpec((tm, d), lambda i, e: (i, 0)),
                  _mod_spec(gt, tm, tokens_per_row),
                  pl.BlockSpec((None, 1, d, de), lambda i, e: (layer, e, 0, 0)),
                  pl.BlockSpec((None, 1, d, de), lambda i, e: (layer, e, 0, 0)),
                  pl.BlockSpec((None, 1, de, d), lambda i, e: (layer, e, 0, 0))],
        out_specs=pl.BlockSpec((tm, d), lambda i, e: (i, 0)),
        out_shape=jax.ShapeDtypeStruct((t, d), F32),
        scratch_shapes=[pltpu.VMEM((tm, d), F32)],
        compiler_params=_cparams("parallel", "arbitrary"),
    )(h2, comb, x1, gt, w1, w3, w2)


MOE_TK = 512
MOE_TM = 512
ROW_CHUNK = 16
MOE_LB = 2 * MOE_TK + 512
MOE_CH = MOE_LB // ROW_CHUNK
MOE_XW = D_MODEL + LANES
MOE_LB_SHORT = 2 * MOE_TK + 256
assert 2 * MOE_TK + N_EXPERTS * (ROW_CHUNK - 1) <= MOE_LB - ROW_CHUNK


def _moe_sorted_tiles(t):
    rows = 2 * t + (t // MOE_TK) * N_EXPERTS * (ROW_CHUNK - 1) + N_EXPERTS * (MOE_TM - 1)
    return -(-rows // MOE_TM)


def _plan_kernel(r_ref, cnt_ref, lp_ref, dc_ref, te_ref, base_ref):
    t = pl.program_id(0)
    tk = r_ref.shape[0]
    ntp = te_ref.shape[0]
    r = r_ref[...]
    lane = _iota((tk, LANES), 1).astype(F32)
    sel1, sel2 = lane == r[:, 0:1], lane == r[:, 1:2]
    oh = jnp.where(sel1 | sel2, 1.0, 0.0)
    upper = (_iota((LANES, LANES), 0) < _iota((LANES, LANES), 1)).astype(F32)

    def whole_chunks(cnt):
        return jnp.floor((cnt + (ROW_CHUNK - 1)) * (1.0 / ROW_CHUNK)) * ROW_CHUNK

    def excl_cumsum(v):
        return _dot(jnp.broadcast_to(v, (8, LANES)), upper, HIGHEST)[0:1]

    @pl.when(t == 0)
    def _():
        tot = jnp.sum(whole_chunks(cnt_ref[...]), 0, keepdims=True) * 0.125
        gp = jnp.floor((tot + (MOE_TM - 1)) * (1.0 / MOE_TM)) * MOE_TM
        off = excl_cumsum(gp)
        base_ref[...] = off
        end = off + gp
        lane_t = _iota((ntp, LANES), 1).astype(F32)
        start = _iota((ntp, 1), 0).astype(F32) * MOE_TM
        te = jnp.sum(jnp.where((lane_t < N_EXPERTS) & (end <= start), 1.0, 0.0), -1, keepdims=True)
        mine = lane_t == te
        filled = jnp.sum(jnp.where(mine, tot + off, 0.0), -1, keepdims=True)
        tv = jnp.clip(filled - start, 0.0, float(MOE_TM))
        n_used = jnp.sum(jnp.where(lane_t == N_EXPERTS - 1, end, 0.0), -1, keepdims=True) * (1.0 / MOE_TM)
        te_ref[...] = jnp.where(lane_t == 0.0, jnp.minimum(te, N_EXPERTS - 1.0),
                                jnp.where(lane_t == 1.0, tv, jnp.where(lane_t == 2.0, n_used, 0.0))
                                ).astype(jnp.int32)

    c8 = whole_chunks(jnp.sum(oh, 0, keepdims=True))
    base = base_ref[...]
    lo = excl_cumsum(c8)
    below = (_iota((tk, tk), 0) > _iota((tk, tk), 1)).astype(BF16)
    p = _dot(below, oh.astype(BF16)) + lo
    lp1 = jnp.sum(jnp.where(sel1, p, 0.0), -1, keepdims=True)
    lp2 = jnp.sum(jnp.where(sel2, p, 0.0), -1, keepdims=True)
    lp_ref[...] = jnp.where(lane == 0.0, lp1, jnp.where(lane == 1.0, lp2, jnp.where(lane < 4.0, r, 0.0)))
    lane_c = _iota((MOE_CH, LANES), 1).astype(F32)
    cstart = _iota((MOE_CH, 1), 0).astype(F32) * ROW_CHUNK
    ej = jnp.sum(jnp.where((lane_c < N_EXPERTS) & (lo + c8 <= cstart), 1.0, 0.0), -1, keepdims=True)
    dj = jnp.sum(jnp.where(lane_c == ej, base - lo, 0.0), -1, keepdims=True) + cstart
    nrows = jnp.sum(c8, -1, keepdims=True)
    last = _iota((MOE_CH, 1), 0) == MOE_CH - 1
    dcv = jnp.where(last, nrows * (1.0 / ROW_CHUNK), jnp.where(cstart < nrows, dj, 0.0))
    dc_ref[...] = jnp.broadcast_to(dcv, (MOE_CH, LANES)).astype(jnp.int32)
    base_ref[...] = base + c8


def _moe_plan(route, counts):
    t = route.shape[0]
    nt = t // MOE_TK
    ntp = -(-_moe_sorted_tiles(t) // 8) * 8
    assert counts.shape == (nt * 8, LANES)
    return pl.pallas_call(
        _plan_kernel,
        grid=(nt,),
        in_specs=[pl.BlockSpec((MOE_TK, LANES), lambda i: (i, 0)),
                  pl.BlockSpec((nt * 8, LANES), lambda i: (0, 0))],
        out_specs=[pl.BlockSpec((MOE_TK, LANES), lambda i: (i, 0)),
                   pl.BlockSpec((MOE_CH, LANES), lambda i: (i, 0)),
                   pl.BlockSpec((ntp, LANES), lambda i: (0, 0))],
        out_shape=[jax.ShapeDtypeStruct((t, LANES), F32),
                   jax.ShapeDtypeStruct((nt * MOE_CH, LANES), jnp.int32),
                   jax.ShapeDtypeStruct((ntp, LANES), jnp.int32)],
        scratch_shapes=[pltpu.VMEM((1, LANES), F32)],
        compiler_params=_cparams("arbitrary"),
    )(route, counts)


def _pick_onehot(lp, rows):
    pos = _iota((lp.shape[0], rows), 1).astype(F32)
    return pos == lp[:, 0:1], pos == lp[:, 1:2]


def _for_staged_rows(nch, fn):
    few = nch * ROW_CHUNK <= MOE_LB_SHORT

    @pl.when(few)
    def _():
        fn(MOE_LB_SHORT)

    @pl.when(jnp.logical_not(few))
    def _():
        fn(MOE_LB)


def _chunk_loop(n, fn):
    def body(j, carry):
        fn(j)
        return carry

    lax.fori_loop(0, n, body, 0)


def _dispatch_kernel(dc_ref, h_ref, lp_ref, xs_ref, buf_ref, sem, nprev_ref):
    t, nt = pl.program_id(0), pl.num_programs(0)
    slot = t % 2
    tk = h_ref.shape[0]
    lp = lp_ref[...]
    lane = _iota((tk, LANES), 1)

    def split3(w):
        hi = w.astype(BF16).astype(F32)
        mid = (w - hi).astype(BF16).astype(F32)
        low = w - hi - mid
        return jnp.where(lane == 0, hi, jnp.where(lane == 1, mid, jnp.where(lane == 2, low, 0.0))).astype(BF16)

    w1, w2 = split3(lp[:, 2:3]), split3(lp[:, 3:4])
    nch = dc_ref[0, 0, MOE_CH - 1]

    def stage(rows):
        oh1, oh2 = _pick_onehot(lp, rows)
        b1, b2 = jnp.where(oh1, 1.0, 0.0).astype(BF16), jnp.where(oh2, 1.0, 0.0).astype(BF16)
        buf_ref[slot, 0:rows, 0:D_MODEL] = _dot_tn(b1 + b2, h_ref[...]).astype(BF16)
        buf_ref[slot, 0:rows, D_MODEL:] = (_dot_tn(b1, w1) + _dot_tn(b2, w2)).astype(BF16)

    _for_staged_rows(nch, stage)

    def chunk_copy(j, s):
        src = buf_ref.at[s, pl.ds(pl.multiple_of(j * ROW_CHUNK, ROW_CHUNK), ROW_CHUNK), :]
        dst = xs_ref.at[pl.ds(pl.multiple_of(dc_ref[0, 0, j], ROW_CHUNK), ROW_CHUNK), :]
        return pltpu.make_async_copy(src, dst, sem.at[s])

    _chunk_loop(nch, lambda j: chunk_copy(j, slot).start())

    @pl.when(t > 0)
    def _():
        _chunk_loop(nprev_ref[0], lambda j: chunk_copy(0, 1 - slot).wait())

    nprev_ref[0] = nch

    @pl.when(t == nt - 1)
    def _():
        _chunk_loop(nch, lambda j: chunk_copy(0, slot).wait())


def _moe_dispatch(dc3, h2, lp, n_rows):
    t, d = h2.shape
    nt = t // MOE_TK
    return pl.pallas_call(
        _dispatch_kernel,
        grid=(nt,),
        in_specs=[pl.BlockSpec((1, 1, MOE_CH), lambda i: (i, 0, 0), memory_space=pltpu.SMEM),
                  pl.BlockSpec((MOE_TK, d), lambda i: (i, 0)),
                  pl.BlockSpec((MOE_TK, LANES), lambda i: (i, 0))],
        out_specs=pl.BlockSpec(memory_space=pl.ANY),
        out_shape=jax.ShapeDtypeStruct((n_rows, MOE_XW), BF16),
        scratch_shapes=[pltpu.VMEM((2, MOE_LB, MOE_XW), BF16), pltpu.SemaphoreType.DMA((2,)),
                        pltpu.SMEM((1,), jnp.int32)],
        compiler_params=_cparams("arbitrary"),
    )(dc3, h2, lp)


def _experts_kernel(layer, te_ref, tv_ref, nu_ref, first_ref, nxt_ref, par_ref, x_ref, w1_hbm, w3_hbm, w2_hbm,
                    o_ref, w1f, w3f, w2f, w1b, w3b, w2b, sem):
    i = pl.program_id(0)

    def fetch(e, s):
        return (pltpu.make_async_copy(w1_hbm.at[layer, e], w1f.at[s], sem.at[s, 0]),
                pltpu.make_async_copy(w3_hbm.at[layer, e], w3f.at[s], sem.at[s, 1]),
                pltpu.make_async_copy(w2_hbm.at[layer, e], w2f.at[s], sem.at[s, 2]))

    @pl.when(i < nu_ref[0])
    def _():
        @pl.when(first_ref[i] == 1)
        def _():
            s = par_ref[i]

            @pl.when(i == 0)
            def _():
                for c in fetch(te_ref[0], s):
                    c.start()

            for c in fetch(te_ref[i], s):
                c.wait()

            @pl.when(nxt_ref[i] >= 0)
            def _():
                for c in fetch(nxt_ref[i], 1 - s):
                    c.start()

            w1b[...] = w1f[s].astype(BF16)
            w3b[...] = w3f[s].astype(BF16)
            w2b[...] = w2f[s].astype(BF16)

        tm = x_ref.shape[0]
        valid = _iota((tm, 1), 0) < tv_ref[i]
        h = jnp.where(valid, x_ref[:, :D_MODEL], jnp.zeros((), BF16))
        wx = x_ref[:, D_MODEL:].astype(F32)
        wv = jnp.where(valid, wx[:, 0:1] + wx[:, 1:2] + wx[:, 2:3], 0.0)
        he = _silu(_dot(h, w1b[...])) * _dot(h, w3b[...])
        o_ref[...] = (wv * _dot(he.astype(BF16), w2b[...])).astype(BF16)


def _moe_experts(layer, te, tv, nu, xs, w1, w3, w2):
    n_rows = xs.shape[0]
    _, _, d, de = w1.shape
    used = jnp.arange(te.shape[0]) < nu[0]
    first = jnp.where(used, jnp.concatenate([jnp.ones((1,), jnp.int32), (te[1:] != te[:-1]).astype(jnp.int32)]), 0)
    par = ((jnp.cumsum(first) - 1) % 2).astype(jnp.int32)
    te_next = jnp.concatenate([te[1:], te[-1:]])
    first_next = jnp.concatenate([first[1:], jnp.zeros((1,), jnp.int32)])

    def from_right(carry, z):
        e_next, f_next = z
        new = jnp.where(f_next == 1, e_next, carry)
        return new, new

    nxt = lax.scan(from_right, jnp.int32(-1), (te_next[::-1], first_next[::-1]))[1][::-1].astype(jnp.int32)
    cur = lambda i, *refs: jnp.minimum(i, refs[2][0] - 1)
    return pl.pallas_call(
        partial(_experts_kernel, layer),
        grid_spec=pltpu.PrefetchScalarGridSpec(
            num_scalar_prefetch=6,
            grid=(n_rows // MOE_TM,),
            in_specs=[pl.BlockSpec((MOE_TM, MOE_XW), lambda i, *refs: (cur(i, *refs), 0)),
                      pl.BlockSpec(memory_space=pl.ANY), pl.BlockSpec(memory_space=pl.ANY),
                      pl.BlockSpec(memory_space=pl.ANY)],
            out_specs=pl.BlockSpec((MOE_TM, d), lambda i, *refs: (cur(i, *refs), 0)),
            scratch_shapes=[pltpu.VMEM((2, d, de), F32), pltpu.VMEM((2, d, de), F32), pltpu.VMEM((2, de, d), F32),
                            pltpu.VMEM((d, de), BF16), pltpu.VMEM((d, de), BF16), pltpu.VMEM((de, d), BF16),
                            pltpu.SemaphoreType.DMA((2, 3))]),
        out_shape=jax.ShapeDtypeStruct((n_rows, d), BF16),
        compiler_params=_cparams("arbitrary"),
    )(te, tv, nu, first, nxt, par, xs, w1, w3, w2)


def _combine_kernel(dc_ref, dcn_ref, lp_ref, x1_ref, gt_ref, ys_ref, o_ref, buf_ref, sem):
    t, nt = pl.program_id(0), pl.num_programs(0)
    slot = t % 2

    def chunk_copy(tab, j, s):
        src = ys_ref.at[pl.ds(pl.multiple_of(tab[0, 0, j], ROW_CHUNK), ROW_CHUNK), :]
        dst = buf_ref.at[s, pl.ds(pl.multiple_of(j * ROW_CHUNK, ROW_CHUNK), ROW_CHUNK), :]
        return pltpu.make_async_copy(src, dst, sem.at[s])

    nch = dc_ref[0, 0, MOE_CH - 1]

    @pl.when(t == 0)
    def _():
        buf_ref[...] = jnp.zeros_like(buf_ref)
        _chunk_loop(nch, lambda j: chunk_copy(dc_ref, j, slot).start())

    @pl.when(t + 1 < nt)
    def _():
        _chunk_loop(dcn_ref[0, 0, MOE_CH - 1], lambda j: chunk_copy(dcn_ref, j, 1 - slot).start())

    _chunk_loop(nch, lambda j: chunk_copy(dc_ref, 0, slot).wait())
    def unpermute(rows):
        live = _iota((rows, 1), 0) < nch * ROW_CHUNK
        local = jnp.where(live, buf_ref[slot, 0:rows, :], jnp.zeros((), BF16))
        oh1, oh2 = _pick_onehot(lp_ref[...], rows)
        perm = jnp.where(oh1 | oh2, 1.0, 0.0).astype(BF16)
        o_ref[...] = x1_ref[...] + gt_ref[0] * _dot(perm, local)

    _for_staged_rows(nch, unpermute)


def _moe_combine(dc3, lp, x1, gt, ys, tokens_per_row):
    t, d = x1.shape
    nt = t // MOE_TK
    smem = lambda f: pl.BlockSpec((1, 1, MOE_CH), f, memory_space=pltpu.SMEM)
    return pl.pallas_call(
        _combine_kernel,
        grid=(nt,),
        in_specs=[smem(lambda i: (i, 0, 0)), smem(lambda i: (jnp.minimum(i + 1, nt - 1), 0, 0)),
                  pl.BlockSpec((MOE_TK, LANES), lambda i: (i, 0)),
                  pl.BlockSpec((MOE_TK, d), lambda i: (i, 0)),
                  _mod_spec(gt, MOE_TK, tokens_per_row),
                  pl.BlockSpec(memory_space=pl.ANY)],
        out_specs=pl.BlockSpec((MOE_TK, d), lambda i: (i, 0)),
        out_shape=jax.ShapeDtypeStruct((t, d), F32),
        scratch_shapes=[pltpu.VMEM((2, MOE_LB, d), BF16), pltpu.SemaphoreType.DMA((2,))],
        compiler_params=_cparams("arbitrary"),
    )(dc3, dc3, lp, x1, gt, ys)


def _moe_sparse(layer, h2, route, counts, x1, gt, w1, w3, w2, tokens_per_row):
    t = h2.shape[0]
    nt = t // MOE_TK
    lp, dc, tmeta = _moe_plan(route, counts)
    dc3 = dc[:, 0].reshape(nt, 1, MOE_CH)
    n_tiles = _moe_sorted_tiles(t)
    xs = _moe_dispatch(dc3, h2, lp, n_tiles * MOE_TM)
    ys = _moe_experts(layer, tmeta[:n_tiles, 0], tmeta[:n_tiles, 1], tmeta[0:1, 2], xs, w1, w3, w2)
    return _moe_combine(dc3, lp, x1, gt, ys, tokens_per_row)


def _w_in_segments():
    sizes = (DN_QKV, DN_HEADS * DN_DV, DN_HEADS, DN_HEADS,
             GLA_HEADS * GLA_DK, GLA_HEADS * GLA_DK, GLA_HEADS * GLA_DV, GLA_HEADS * GLA_DV, GLA_RANK,
             SWA_HEADS * SWA_HD, SWA_KV * SWA_HD, SWA_KV * SWA_HD, N_BRANCH * D_MODEL)
    offs = [0]
    for s in sizes:
        offs.append(offs[-1] + s)
    (a_qkv, a_z, a_b, a_a, b_q, b_k, b_v, b_r, b_lr, c_q, c_k, c_v, gate) = range(len(sizes))
    dst = {gate: C_GATE, a_qkv: C_AQKV, a_z: C_AZ, b_v: C_BV, b_r: C_BR, c_q: C_CQ, b_q: C_BQ, b_k: C_BK,
           c_k: C_CK, c_v: C_CV, a_b: C_SM + SM_BETA, a_a: C_SM + SM_G, b_lr: C_SM + SM_LR}
    return [(offs[i], dst[i], sizes[i]) for i in range(len(sizes))], offs[-1]


def _w_in_prep_kernel(w_ref, o_ref):
    segments, _ = _w_in_segments()
    n_src = w_ref.shape[-1]
    o_ref[:, C_SM:] = jnp.zeros((o_ref.shape[0], P_PAD - C_SM), BF16)
    for src, dst, width in segments:
        lo = (src // LANES) * LANES
        hi = min(-(-(src + width) // LANES) * LANES, n_src)
        win = w_ref[:, lo:hi]
        o_ref[:, dst:dst + width] = win[:, src - lo:src - lo + width].astype(BF16)


def _permute_w_in(w):
    depth, d, n = w.shape
    assert n == _w_in_segments()[1]
    rows = 256
    return pl.pallas_call(
        _w_in_prep_kernel,
        grid=(depth, d // rows),
        in_specs=[pl.BlockSpec((None, rows, n), lambda l, i: (l, i, 0))],
        out_specs=pl.BlockSpec((None, rows, P_PAD), lambda l, i: (l, i, 0)),
        out_shape=jax.ShapeDtypeStruct((depth, d, P_PAD), BF16),
        compiler_params=_cparams("parallel", "parallel"),
    )(w)


def _lane_row(v, off):
    depth, n = v.shape
    return jnp.zeros((depth, 1, LANES), F32).at[:, 0, off:off + n].set(v.astype(F32))


def kernel(x_prompt, x_sample, c_prompt, c_sample, state_dn, state_dn_conv, state_gla, cache_swa_k, cache_swa_v, ln1_g, ln2_g, ada_w, ada_b, w_in, dn_conv_w, dn_a_log, dn_dt_bias, dn_onorm_g, gla_w2, gla_b, gla_onorm_g, swa_qnorm_g, swa_knorm_g, swa_sinks, w_branch, w_out, rc_w, rc_b, re_w, re_b, w1, w3, w2):
    batch, seq, d = x_prompt.shape
    nb = x_sample.shape[0]
    depth = w_in.shape[0]
    assert x_sample.shape[1] == 1 and d == D_MODEL and seq % CHUNK == 0 and nb % DEC_TILE == 0
    kvw = SWA_KV * SWA_HD

    w_in_p = _permute_w_in(w_in)
    wb_b, wo_b = w_branch.astype(BF16), w_out.astype(BF16)
    wr = jnp.concatenate([rc_w, re_w, jnp.zeros((depth, d, LANES - N_GROUPS - N_EXPERTS), F32)], -1)
    rb = jnp.concatenate([rc_b, re_b, jnp.zeros((depth, LANES - N_GROUPS - N_EXPERTS), F32)], -1)[:, None, :]
    wr_hi = wr.astype(BF16)
    wr2 = jnp.concatenate([wr_hi, (wr - wr_hi.astype(F32)).astype(BF16)], -1)
    ln2_all = ln2_g[:, None, :]
    alog_row = _lane_row(dn_a_log, SM_G)
    dtb_row = _lane_row(dn_dt_bias, SM_G)
    w2pad = jnp.zeros((depth, LANES, GLA_HEADS * GLA_DK), F32).at[:, SM_LR:SM_LR + GLA_RANK].set(gla_w2).astype(BF16)
    kn2 = jnp.concatenate([swa_knorm_g] * SWA_KV, -1)[:, None, :]
    slopes = jnp.exp2(-8.0 * jnp.arange(1, SWA_HEADS + 1, dtype=F32) / SWA_HEADS)
    slp_col = jnp.tile(slopes, DEC_TILE)[:, None]

    pad_rows = (-(batch + nb)) % 8
    c_all = jnp.concatenate([c_prompt, c_sample, jnp.zeros((pad_rows, d), F32)], 0)
    mod = _ada(c_all, ada_w, ada_b)

    xp = x_prompt.reshape(batch * seq, d)
    xs = x_sample.reshape(nb, d)
    tm_p = 2048 if seq % 2048 == 0 else CHUNK
    tm_m = 512 if seq % 512 == 0 else CHUNK

    dec_states = (state_dn,
                  state_dn_conv.reshape(depth, nb, (CONV_W - 1) * DN_QKV),
                  state_gla.reshape(depth, nb, GLA_HEADS // 2, 2 * GLA_DK, GLA_DV),
                  cache_swa_k.reshape(depth, nb, WINDOW, kvw),
                  cache_swa_v.reshape(depth, nb, WINDOW, kvw))
    dec_out = None
    st_p = []
    for l in range(depth):
        mp = [m[:, None, :] for m in jnp.split(mod[l, :batch], 6, -1)]
        ms = [m[None] for m in jnp.split(mod[l, batch:batch + nb], 6, -1)]
        ln1, ln2 = ln1_g[l][None], ln2_g[l][None]
        conv_w = dn_conv_w[l]
        dn_on, gla_on = dn_onorm_g[l][None], gla_onorm_g[l][None]
        glb = gla_b[l][None]
        qn = swa_qnorm_g[l][None]
        snk = swa_sinks[l][None]

        pp = _in_proj(l, xp, mp[1], mp[0], ln1, w_in_p, tm_p, seq)
        pp3 = pp.reshape(batch, seq, P_PAD)
        ba, dn_s = _dn_prompt_mb(pp3, conv_w, alog_row[l], dtb_row[l], dn_on)
        bb, gla_st = _gla_prompt_mb(pp3, w2pad[l], glb, gla_on)
        bc, k_new = _swa_prompt_mb(pp3, jnp.concatenate([qn, qn], -1), kn2[l], snk)
        ba, bb, bc = (z.reshape(batch * seq, BRANCH_W) for z in (ba, bb, bc))
        x1, h2, comb, counts = _merge(l, ba, bb, bc, pp, xp, mp[2], mp[4], mp[3], ln2_all, wb_b, wo_b, wr2, rb,
                              tm_m, seq)
        xp = _moe_sparse(l, h2, comb, counts, x1, mp[5], w1, w3, w2, seq)
        pp3 = pp.reshape(batch, seq, P_PAD)
        st_p.append((dn_s,
                     pp3[:, seq - (CONV_W - 1):, C_AQKV:C_AQKV + DN_QKV],
                     jnp.swapaxes(gla_st, -1, -2),
                     k_new.reshape(batch, WINDOW, SWA_KV, SWA_HD),
                     pp3[:, seq - WINDOW:, C_CV:C_CV + kvw].reshape(batch, WINDOW, SWA_KV, SWA_HD)))

        ps = _in_proj(l, xs, ms[1], ms[0], ln1, w_in_p, nb, nb)
        cq_r = ps[:, C_CQ:C_CQ + SWA_HEADS * SWA_HD].reshape(nb * SWA_HEADS, SWA_HD)
        oa, ob, oc_r, *dec_out = _dec_mixers(
            l, ps, cq_r, dec_states, dec_out, conv_w, alog_row[l], dtb_row[l], dn_on, w2pad[l], glb, gla_on,
            qn, kn2[l], jnp.tile(swa_sinks[l], DEC_TILE)[:, None], slp_col)
        oc = oc_r.reshape(nb, SWA_HEADS * SWA_HD)
        x1, h2, comb, _ = _merge(l, oa, ob, oc, ps, xs, ms[2], ms[4], ms[3], ln2_all, wb_b, wo_b, wr2, rb,
                              nb, nb)
        xs = _moe(l, h2, comb, x1, ms[5], w1, w3, w2, nb, nb)

    dn_p, conv_p, gla_p, k_p, v_p = [jnp.stack(z) for z in zip(*st_p)]
    dn_s = dec_out[0]
    conv_s = dec_out[1].reshape(depth, nb, CONV_W - 1, DN_QKV)
    gla_s = dec_out[2].reshape(depth, nb, GLA_HEADS, GLA_DK, GLA_DV)
    k_s = dec_out[3].reshape(depth, nb, WINDOW, SWA_KV, SWA_HD)
    v_s = dec_out[4].reshape(depth, nb, WINDOW, SWA_KV, SWA_HD)
    return (xp.reshape(batch, seq, d), xs.reshape(nb, 1, d), dn_p, dn_s, conv_p, conv_s, gla_p, gla_s,
            k_p, k_s, v_p, v_s)
```
